```python
import jax, jax.numpy as jnp
from jax import lax
import numpy as np

D_MODEL = 2048
BATCH = 1
SEQ = 8192
DEPTH = 4

CHUNK = 64
N_HEADS = 16
HEAD_DIM = D_MODEL // N_HEADS
D_FF = -(-8 * D_MODEL // (3 * 256)) * 256
N_A = DEPTH // 2
N_B = DEPTH - N_A
Q_BLOCK = 128
EPS = 1e-6

kernel_name = "fox_yoco_stickbreaking_hybrid"


def rmsnorm(x, g):
    xf = x.astype(jnp.float32)
    y = xf * lax.rsqrt(jnp.mean(xf * xf, axis=-1, keepdims=True) + EPS) * g.astype(jnp.float32)
    return y.astype(x.dtype)


def split_heads(t):
    b, s, _ = t.shape
    return t.reshape(b, s, N_HEADS, HEAD_DIM).transpose(0, 2, 1, 3)


def merge_heads(o):
    b, h, s, d = o.shape
    return o.transpose(0, 2, 1, 3).reshape(b, s, h * d)


def to_blocks(t):
    b, h, s, d = t.shape
    return t.reshape(b, h, s // Q_BLOCK, Q_BLOCK, d).transpose(2, 0, 1, 3, 4)


def from_blocks(t):
    nb, b, h, q, d = t.shape
    return t.transpose(1, 2, 0, 3, 4).reshape(b, h, nb * q, d)


def forgetting_attention(q, k, v, log_f):
    b, h, s, _ = q.shape
    nb = s // Q_BLOCK
    c = jnp.cumsum(log_f, axis=-1)
    c_blocks = c.reshape(b, h, nb, Q_BLOCK).transpose(2, 0, 1, 3)
    key_pos = jnp.arange(s)
    scale = HEAD_DIM ** -0.5

    def block(args):
        q_blk, c_blk, i = args
        q_pos = i * Q_BLOCK + jnp.arange(Q_BLOCK)
        logits = jnp.einsum('bhqd,bhkd->bhqk', q_blk, k,
                            preferred_element_type=jnp.float32) * scale
        logits = logits + c_blk[..., :, None] - c[..., None, :]
        mask = key_pos[None, :] <= q_pos[:, None]
        logits = jnp.where(mask, logits, -jnp.inf)
        p = jax.nn.softmax(logits, axis=-1)
        return jnp.einsum('bhqk,bhkd->bhqd', p.astype(v.dtype), v)

    out = lax.map(block, (to_blocks(q), c_blocks, jnp.arange(nb)))
    return from_blocks(out)


def stick_breaking_attention(q, k, v):
    s = q.shape[2]
    nb = s // Q_BLOCK
    key_pos = jnp.arange(s)
    scale = HEAD_DIM ** -0.5

    def block(args):
        q_blk, i = args
        q_pos = i * Q_BLOCK + jnp.arange(Q_BLOCK)
        z = jnp.einsum('bhqd,bhkd->bhqk', q_blk, k,
                       preferred_element_type=jnp.float32) * scale
        mask = key_pos[None, :] < q_pos[:, None]
        log_beta = jax.nn.log_sigmoid(z)
        log_1m = jnp.where(mask, jax.nn.log_sigmoid(-z), 0.0)
        suffix = lax.cumsum(log_1m, axis=3, reverse=True) - log_1m
        weights = jnp.where(mask, jnp.exp(log_beta + suffix), 0.0)
        return jnp.einsum('bhqk,bhkd->bhqd', weights.astype(v.dtype), v)

    out = lax.map(block, (to_blocks(q), jnp.arange(nb)))
    return from_blocks(out)


def swiglu(x, w_gate, w_up, w_down):
    return (jax.nn.silu(x @ w_gate) * (x @ w_up)) @ w_down


def setup_inputs(seed: int = 0) -> dict:
    key = jax.random.key(seed)
    ks = jax.random.split(key, 16)
    d = D_MODEL
    f32 = jnp.float32
    nrm = lambda k, shape, fan_in: jax.random.normal(k, shape, f32) * (fan_in ** -0.5)
    return {
        "x": jax.random.normal(ks[0], (BATCH, SEQ, d), f32),
        "attn_norm": 1.0 + 0.02 * jax.random.normal(ks[1], (DEPTH, d), f32),
        "ffn_norm": 1.0 + 0.02 * jax.random.normal(ks[2], (DEPTH, d), f32),
        "a_w_in": nrm(ks[3], (N_A, d, 3 * d + N_HEADS), d),
        "a_b_f": 3.0 + 0.5 * jax.random.normal(ks[4], (N_A, N_HEADS), f32),
        "a_w_out": nrm(ks[5], (N_A, d, d), d),
        "kv_norm": 1.0 + 0.02 * jax.random.normal(ks[6], (d,), f32),
        "w_kv": nrm(ks[7], (d, 2 * d), d),
        "b_w_q": nrm(ks[8], (N_B, d, d), d),
        "b_w_out": nrm(ks[9], (N_B, d, d), d),
        "ffn_w_gate": nrm(ks[10], (DEPTH, d, D_FF), d),
        "ffn_w_up": nrm(ks[11], (DEPTH, d, D_FF), d),
        "ffn_w_down": nrm(ks[12], (DEPTH, D_FF, d), D_FF),
        "final_norm": 1.0 + 0.02 * jax.random.normal(ks[13], (d,), f32),
    }


def reference(x, attn_norm, ffn_norm, a_w_in, a_b_f, a_w_out, kv_norm, w_kv,
              b_w_q, b_w_out, ffn_w_gate, ffn_w_up, ffn_w_down, final_norm):
    d = D_MODEL
    h = x
    k_shared = None
    v_shared = None
    for layer in range(DEPTH):
        hn = rmsnorm(h, attn_norm[layer])
        if layer < N_A:
            proj = hn @ a_w_in[layer]
            q = split_heads(proj[..., :d])
            k = split_heads(proj[..., d:2 * d])
            v = split_heads(proj[..., 2 * d:3 * d])
            f_logit = proj[..., 3 * d:].astype(jnp.float32) + a_b_f[layer].astype(jnp.float32)
            log_f = jax.nn.log_sigmoid(f_logit).transpose(0, 2, 1)
            o = forgetting_attention(q, k, v, log_f)
            h = h + merge_heads(o) @ a_w_out[layer]
        else:
            j = layer - N_A
            q = split_heads(hn @ b_w_q[j])
            o = stick_breaking_attention(q, k_shared, v_shared)
            h = h + merge_heads(o) @ b_w_out[j]
        h = h + swiglu(rmsnorm(h, ffn_norm[layer]), ffn_w_gate[layer],
                       ffn_w_up[layer], ffn_w_down[layer])
        if layer == N_A - 1:
            kv = rmsnorm(h, kv_norm) @ w_kv
            k_shared = split_heads(kv[..., :d])
            v_shared = split_heads(kv[..., d:])
    return rmsnorm(h, final_norm)
```

```python
import functools

import jax
import jax.numpy as jnp
from jax import lax
from jax.experimental import pallas as pl
from jax.experimental.pallas import tpu as pltpu

N_HEADS = 16
HEAD_DIM = 128
EPS = 1e-6
LANES = 128
VMEM_LIMIT = 56 * 1024 * 1024

F32 = jnp.float32
BF16 = jnp.bfloat16


def _params(*sem):
    return pltpu.CompilerParams(dimension_semantics=sem, vmem_limit_bytes=VMEM_LIMIT)


def _rms_scale(x, g):
    ms = jnp.mean(x * x, axis=-1, keepdims=True)
    return x * lax.rsqrt(ms + EPS) * g


def _norm_matmul_kernel(x_ref, g_ref, w_ref, o_ref, xn_ref):
    @pl.when(pl.program_id(1) == 0)
    def _():
        xn_ref[...] = _rms_scale(x_ref[...], g_ref[...]).astype(BF16)

    o_ref[...] = jnp.dot(xn_ref[...], w_ref[...],
                         preferred_element_type=F32).astype(o_ref.dtype)


def _norm_matmul_gate_kernel(x_ref, g_ref, w_ref, wf_ref, o_ref, f_ref, xn_ref):
    @pl.when(pl.program_id(1) == 0)
    def _():
        xn = _rms_scale(x_ref[...], g_ref[...]).astype(BF16)
        xn_ref[...] = xn
        f_ref[...] = jnp.dot(xn, wf_ref[...], preferred_element_type=F32)

    o_ref[...] = jnp.dot(xn_ref[...], w_ref[...],
                         preferred_element_type=F32).astype(o_ref.dtype)


def norm_matmul(x, g, w, wf=None, *, tm=512, tn=1024):
    s, d = x.shape
    n = w.shape[1]
    grid = (s // tm, n // tn)
    x_spec = pl.BlockSpec((tm, d), lambda i, j: (i, 0))
    g_spec = pl.BlockSpec((1, d), lambda i, j: (0, 0))
    w_spec = pl.BlockSpec((d, tn), lambda i, j: (0, j))
    o_spec = pl.BlockSpec((tm, tn), lambda i, j: (i, j))
    scratch = [pltpu.VMEM((tm, d), BF16)]
    if wf is None:
        return pl.pallas_call(
            _norm_matmul_kernel, grid=grid,
            in_specs=[x_spec, g_spec, w_spec], out_specs=o_spec,
            out_shape=jax.ShapeDtypeStruct((s, n), BF16),
            scratch_shapes=scratch, compiler_params=_params("parallel", "arbitrary"),
            name="norm_matmul")(x, g, w)
    wf_spec = pl.BlockSpec((d, LANES), lambda i, j: (0, 0))
    f_spec = pl.BlockSpec((tm, LANES), lambda i, j: (i, 0))
    return pl.pallas_call(
        _norm_matmul_gate_kernel, grid=grid,
        in_specs=[x_spec, g_spec, w_spec, wf_spec], out_specs=[o_spec, f_spec],
        out_shape=[jax.ShapeDtypeStruct((s, n), BF16),
                   jax.ShapeDtypeStruct((s, LANES), F32)],
        scratch_shapes=scratch, compiler_params=_params("parallel", "arbitrary"),
        name="norm_matmul_gate")(x, g, w, wf)


def _log_sigmoid(z):
    return jnp.minimum(z, 0.0) - jnp.log(1.0 + jnp.exp(-jnp.abs(z)))


def _split3(x):
    hi = x.astype(BF16)
    r = x - hi.astype(F32)
    mid = r.astype(BF16)
    lo = (r - mid.astype(F32)).astype(BF16)
    return hi, mid, lo


def _gate_cumsum_kernel(f_ref, b_ref, ccol_ref, crow_ref, carry_ref, *, tc):
    @pl.when(pl.program_id(0) == 0)
    def _():
        carry_ref[...] = jnp.zeros_like(carry_ref)

    log_f = _log_sigmoid(f_ref[...] + b_ref[...])
    row = lax.broadcasted_iota(jnp.int32, (tc, tc), 0)
    col = lax.broadcasted_iota(jnp.int32, (tc, tc), 1)
    tri = jnp.where(col <= row, 1.0, 0.0).astype(BF16)
    hi, mid, lo = _split3(log_f)
    c = (jnp.dot(tri, hi, preferred_element_type=F32)
         + jnp.dot(tri, mid, preferred_element_type=F32)
         + jnp.dot(tri, lo, preferred_element_type=F32)) + carry_ref[...]
    ccol_ref[...] = c
    crow_ref[...] = c.T
    carry_ref[...] = c[tc - 1:tc, :]


def gate_cumsum(f, b, *, tc=512):
    s = f.shape[0]
    return pl.pallas_call(
        functools.partial(_gate_cumsum_kernel, tc=tc), grid=(s // tc,),
        in_specs=[pl.BlockSpec((tc, LANES), lambda i: (i, 0)),
                  pl.BlockSpec((1, LANES), lambda i: (0, 0))],
        out_specs=[pl.BlockSpec((tc, LANES), lambda i: (i, 0)),
                   pl.BlockSpec((LANES, tc), lambda i: (0, i))],
        out_shape=[jax.ShapeDtypeStruct((s, LANES), F32),
                   jax.ShapeDtypeStruct((LANES, s), F32)],
        scratch_shapes=[pltpu.VMEM((1, LANES), F32)],
        compiler_params=_params("arbitrary"), name="gate_cumsum")(f, b)


def _qk(q, k):
    return lax.dot_general(q, k, (((1,), (1,)), ((), ())), preferred_element_type=F32)


def _fox_kernel(q_ref, k_ref, v_ref, ccol_ref, crow_ref, o_ref, *, t, scale):
    h = pl.program_id(0)
    i = pl.program_id(1)
    q = q_ref[...]
    lane = lax.broadcasted_iota(jnp.int32, (t, LANES), 1)
    c_t = jnp.sum(jnp.where(lane == h, ccol_ref[...], 0.0), axis=-1, keepdims=True)

    def tile(j, carry, diagonal):
        m, l, acc = carry
        start = pl.multiple_of(j * t, t)
        s = _qk(q, k_ref[pl.ds(start, t), :]) * scale
        s = s + c_t - crow_ref[pl.ds(j, 1), :]
        if diagonal:
            row = lax.broadcasted_iota(jnp.int32, (t, t), 0)
            col = lax.broadcasted_iota(jnp.int32, (t, t), 1)
            s = jnp.where(col <= row, s, -jnp.inf)
        m_new = jnp.maximum(m, jnp.max(s, axis=-1, keepdims=True))
        alpha = jnp.exp(m - m_new)
        p = jnp.exp(s - m_new)
        l = alpha * l + jnp.sum(p, axis=-1, keepdims=True)
        acc = alpha * acc + jnp.dot(p.astype(BF16), v_ref[pl.ds(start, t), :],
                                    preferred_element_type=F32)
        return m_new, l, acc

    init = (jnp.full((t, 1), -jnp.inf, F32), jnp.zeros((t, 1), F32),
            jnp.zeros((t, HEAD_DIM), F32))
    carry = lax.fori_loop(0, i, lambda j, c: tile(j, c, False), init)
    _, l, acc = tile(i, carry, True)
    o_ref[...] = (acc / l).astype(o_ref.dtype)


def fox_attention(qkv, ccol, crow, *, t=256):
    s = qkv.shape[0]
    d = qkv.shape[1] // 3
    nh = d // HEAD_DIM
    crow = crow[:nh].reshape(nh, s // t, t)
    return pl.pallas_call(
        functools.partial(_fox_kernel, t=t, scale=HEAD_DIM ** -0.5),
        grid=(nh, s // t),
        in_specs=[pl.BlockSpec((t, HEAD_DIM), lambda h, i: (i, h)),
                  pl.BlockSpec((s, HEAD_DIM), lambda h, i: (0, nh + h)),
                  pl.BlockSpec((s, HEAD_DIM), lambda h, i: (0, 2 * nh + h)),
                  pl.BlockSpec((t, LANES), lambda h, i: (i, 0)),
                  pl.BlockSpec((None, s // t, t), lambda h, i: (h, 0, 0))],
        out_specs=pl.BlockSpec((t, HEAD_DIM), lambda h, i: (i, h)),
        out_shape=jax.ShapeDtypeStruct((s, d), BF16),
        compiler_params=_params("parallel", "arbitrary"),
        name="fox_attention")(qkv, qkv, qkv, ccol, crow)


def _sb_kernel(q_ref, k_ref, v_ref, o_ref, *, t, scale):
    i = pl.program_id(1)
    q = q_ref[...]
    row = lax.broadcasted_iota(jnp.int32, (t, t), 0)
    col = lax.broadcasted_iota(jnp.int32, (t, t), 1)
    after = jnp.where(row > col, 1.0, 0.0).astype(BF16)
    after2 = jnp.concatenate([after, after], axis=0)

    def tile(j, carry, diagonal):
        run, acc = carry
        start = pl.multiple_of(j * t, t)
        z = _qk(q, k_ref[pl.ds(start, t), :]) * scale
        log_beta = _log_sigmoid(z)
        log_1m = log_beta - z
        if diagonal:
            mask = col < row
            log_1m = jnp.where(mask, log_1m, 0.0)
        hi = log_1m.astype(BF16)
        lo = (log_1m - hi.astype(F32)).astype(BF16)
        suffix = jnp.dot(jnp.concatenate([hi, lo], axis=1), after2,
                         preferred_element_type=F32)
        w = jnp.exp(log_beta + suffix + run)
        if diagonal:
            w = jnp.where(mask, w, 0.0)
        acc = acc + jnp.dot(w.astype(BF16), v_ref[pl.ds(start, t), :],
                            preferred_element_type=F32)
        run = run + jnp.sum(log_1m, axis=-1, keepdims=True)
        return run, acc

    carry = tile(i, (jnp.zeros((t, 1), F32), jnp.zeros((t, HEAD_DIM), F32)), True)
    _, acc = lax.fori_loop(0, i, lambda n, c: tile(i - 1 - n, c, False), carry)
    o_ref[...] = acc.astype(o_ref.dtype)


def sb_attention(q, kv, *, t=256):
    s, d = q.shape
    nh = d // HEAD_DIM
    return pl.pallas_call(
        functools.partial(_sb_kernel, t=t, scale=HEAD_DIM ** -0.5),
        grid=(nh, s // t),
        in_specs=[pl.BlockSpec((t, HEAD_DIM), lambda h, i: (i, h)),
                  pl.BlockSpec((s, HEAD_DIM), lambda h, i: (0, h)),
                  pl.BlockSpec((s, HEAD_DIM), lambda h, i: (0, nh + h))],
        out_specs=pl.BlockSpec((t, HEAD_DIM), lambda h, i: (i, h)),
        out_shape=jax.ShapeDtypeStruct((s, d), BF16),
        compiler_params=_params("parallel", "arbitrary"),
        name="sb_attention")(q, kv, kv)


def _matmul_residual_kernel(a_ref, w_ref, h_ref, o_ref):
    o_ref[...] = h_ref[...] + jnp.dot(a_ref[...], w_ref[...], preferred_element_type=F32)


def matmul_residual(a, w, h, *, tm=512, tn=1024):
    s, k = a.shape
    n = w.shape[1]
    return pl.pallas_call(
        _matmul_residual_kernel, grid=(s // tm, n // tn),
        in_specs=[pl.BlockSpec((tm, k), lambda i, j: (i, 0)),
                  pl.BlockSpec((k, tn), lambda i, j: (0, j)),
                  pl.BlockSpec((tm, tn), lambda i, j: (i, j))],
        out_specs=pl.BlockSpec((tm, tn), lambda i, j: (i, j)),
        out_shape=jax.ShapeDtypeStruct((s, n), F32),
        compiler_params=_params("parallel", "arbitrary"),
        name="matmul_residual")(a, w, h)


def _ffn_kernel(h_ref, g_ref, wg_ref, wu_ref, wd_ref, o_ref, xn_ref):
    @pl.when(pl.program_id(1) == 0)
    def _():
        x = h_ref[...]
        xn_ref[...] = _rms_scale(x, g_ref[...]).astype(BF16)
        o_ref[...] = x

    xn = xn_ref[...]
    gate = jnp.dot(xn, wg_ref[...], preferred_element_type=F32)
    up = jnp.dot(xn, wu_ref[...], preferred_element_type=F32)
    act = (gate * jax.nn.sigmoid(gate) * up).astype(BF16)
    o_ref[...] += jnp.dot(act, wd_ref[...], preferred_element_type=F32)


def ffn(h, g, wg, wu, wd, *, tm=512, tf=512):
    s, d = h.shape
    dff = wg.shape[1]
    return pl.pallas_call(
        _ffn_kernel, grid=(s // tm, dff // tf),
        in_specs=[pl.BlockSpec((tm, d), lambda i, f: (i, 0)),
                  pl.BlockSpec((1, d), lambda i, f: (0, 0)),
                  pl.BlockSpec((d, tf), lambda i, f: (0, f)),
                  pl.BlockSpec((d, tf), lambda i, f: (0, f)),
                  pl.BlockSpec((tf, d), lambda i, f: (f, 0))],
        out_specs=pl.BlockSpec((tm, d), lambda i, f: (i, 0)),
        out_shape=jax.ShapeDtypeStruct((s, d), F32),
        scratch_shapes=[pltpu.VMEM((tm, d), BF16)],
        compiler_params=_params("parallel", "arbitrary"),
        name="ffn")(h, g, wg, wu, wd)


def _rmsnorm_kernel(x_ref, g_ref, o_ref):
    o_ref[...] = _rms_scale(x_ref[...], g_ref[...])


def rmsnorm(x, g, *, tm=512):
    s, d = x.shape
    return pl.pallas_call(
        _rmsnorm_kernel, grid=(s // tm,),
        in_specs=[pl.BlockSpec((tm, d), lambda i: (i, 0)),
                  pl.BlockSpec((1, d), lambda i: (0, 0))],
        out_specs=pl.BlockSpec((tm, d), lambda i: (i, 0)),
        out_shape=jax.ShapeDtypeStruct((s, d), F32),
        compiler_params=_params("parallel"), name="rmsnorm")(x, g)


def kernel(x, attn_norm, ffn_norm, a_w_in, a_b_f, a_w_out, kv_norm, w_kv, b_w_q, b_w_out,
           ffn_w_gate, ffn_w_up, ffn_w_down, final_norm):
    b, s, d = x.shape
    assert b == 1 and d == N_HEADS * HEAD_DIM
    depth = attn_norm.shape[0]
    n_a = a_w_in.shape[0]
    h = x.reshape(s, d)
    kv = None
    for layer in range(depth):
        g_attn = attn_norm[layer].reshape(1, d)
        if layer < n_a:
            w_in = a_w_in[layer]
            w_qkv = w_in[:, :3 * d].astype(BF16)
            w_f = jnp.pad(w_in[:, 3 * d:], ((0, 0), (0, LANES - N_HEADS))).astype(BF16)
            b_f = jnp.pad(a_b_f[layer], (0, LANES - N_HEADS)).reshape(1, LANES)
            qkv, f_logit = norm_matmul(h, g_attn, w_qkv, w_f)
            ccol, crow = gate_cumsum(f_logit, b_f)
            o = fox_attention(qkv, ccol, crow)
            h = matmul_residual(o, a_w_out[layer].astype(BF16), h)
        else:
            j = layer - n_a
            q = norm_matmul(h, g_attn, b_w_q[j].astype(BF16))
            o = sb_attention(q, kv)
            h = matmul_residual(o, b_w_out[j].astype(BF16), h)
        h = ffn(h, ffn_norm[layer].reshape(1, d), ffn_w_gate[layer].astype(BF16),
                ffn_w_up[layer].astype(BF16), ffn_w_down[layer].astype(BF16))
        if layer == n_a - 1:
            kv = norm_matmul(h, kv_norm.reshape(1, d), w_kv.astype(BF16))
    return rmsnorm(h, final_norm.reshape(1, d)).reshape(b, s, d)
```

```python
import functools

import jax
import jax.numpy as jnp
from jax import lax
from jax.experimental import pallas as pl
from jax.experimental.pallas import tpu as pltpu

N_HEADS = 16
HEAD_DIM = 128
EPS = 1e-6
LOG2E = 1.4426950408889634
LANES = 128
MXU_WIDTH = 256
VMEM_LIMIT = 56 * 1024 * 1024

KEY_CHUNK = 128
QUERY_CHUNK = MXU_WIDTH
PROB_CHUNK = 64

F32 = jnp.float32
BF16 = jnp.bfloat16


def _params(*sem):
    return pltpu.CompilerParams(dimension_semantics=sem, vmem_limit_bytes=VMEM_LIMIT)


def _rms_scale(x, g):
    ms = jnp.mean(x * x, axis=-1, keepdims=True)
    return x * lax.rsqrt(ms + EPS) * g


def _scaled_tile(acc, n_scaled_tiles, col_scale):
    if n_scaled_tiles == 0:
        return acc
    return acc * jnp.where(pl.program_id(1) < n_scaled_tiles, col_scale, 1.0)


def _norm_matmul_kernel(x_ref, g_ref, w_ref, o_ref, xn_ref, *, n_scaled_tiles, col_scale):
    @pl.when(pl.program_id(1) == 0)
    def _():
        xn_ref[...] = _rms_scale(x_ref[...], g_ref[...]).astype(BF16)

    acc = jnp.dot(xn_ref[...], w_ref[...], preferred_element_type=F32)
    o_ref[...] = _scaled_tile(acc, n_scaled_tiles, col_scale).astype(o_ref.dtype)


def _norm_matmul_gate_kernel(x_ref, g_ref, w_ref, wf_ref, o_ref, f_ref, xn_ref,
                             *, n_scaled_tiles, col_scale):
    @pl.when(pl.program_id(1) == 0)
    def _():
        xn = _rms_scale(x_ref[...], g_ref[...]).astype(BF16)
        xn_ref[...] = xn
        f_ref[...] = jnp.dot(xn, wf_ref[...], preferred_element_type=F32)

    acc = jnp.dot(xn_ref[...], w_ref[...], preferred_element_type=F32)
    o_ref[...] = _scaled_tile(acc, n_scaled_tiles, col_scale).astype(o_ref.dtype)


def norm_matmul(x, g, w, wf=None, *, scaled_cols=0, col_scale=1.0, tm=512, tn=1024):
    s, d = x.shape
    n = w.shape[1]
    assert scaled_cols % tn == 0
    static = dict(n_scaled_tiles=scaled_cols // tn, col_scale=col_scale)
    grid = (s // tm, n // tn)
    x_spec = pl.BlockSpec((tm, d), lambda i, j: (i, 0))
    g_spec = pl.BlockSpec((1, d), lambda i, j: (0, 0))
    w_spec = pl.BlockSpec((d, tn), lambda i, j: (0, j))
    o_spec = pl.BlockSpec((tm, tn), lambda i, j: (i, j))
    scratch = [pltpu.VMEM((tm, d), BF16)]
    if wf is None:
        return pl.pallas_call(
            functools.partial(_norm_matmul_kernel, **static), grid=grid,
            in_specs=[x_spec, g_spec, w_spec], out_specs=o_spec,
            out_shape=jax.ShapeDtypeStruct((s, n), BF16),
            scratch_shapes=scratch, compiler_params=_params("parallel", "arbitrary"),
            name="norm_matmul")(x, g, w)
    wf_spec = pl.BlockSpec((d, LANES), lambda i, j: (0, 0))
    f_spec = pl.BlockSpec((tm, LANES), lambda i, j: (i, 0))
    return pl.pallas_call(
        functools.partial(_norm_matmul_gate_kernel, **static), grid=grid,
        in_specs=[x_spec, g_spec, w_spec, wf_spec], out_specs=[o_spec, f_spec],
        out_shape=[jax.ShapeDtypeStruct((s, n), BF16),
                   jax.ShapeDtypeStruct((s, LANES), F32)],
        scratch_shapes=scratch, compiler_params=_params("parallel", "arbitrary"),
        name="norm_matmul_gate")(x, g, w, wf)


def _log_sigmoid(z):
    return jnp.minimum(z, 0.0) - jnp.log(1.0 + jnp.exp(-jnp.abs(z)))


def _split3(x):
    hi = x.astype(BF16)
    r = x - hi.astype(F32)
    mid = r.astype(BF16)
    lo = (r - mid.astype(F32)).astype(BF16)
    return hi, mid, lo


def _gate_cumsum_kernel(f_ref, b_ref, ccol_ref, carry_ref, *, tc):
    @pl.when(pl.program_id(0) == 0)
    def _():
        carry_ref[...] = jnp.zeros_like(carry_ref)

    log_f = _log_sigmoid(f_ref[...] + b_ref[...])
    row = lax.broadcasted_iota(jnp.int32, (tc, tc), 0)
    col = lax.broadcasted_iota(jnp.int32, (tc, tc), 1)
    tri = jnp.where(col <= row, 1.0, 0.0).astype(BF16)
    hi, mid, lo = _split3(log_f)
    c = (jnp.dot(tri, hi, preferred_element_type=F32)
         + jnp.dot(tri, mid, preferred_element_type=F32)
         + jnp.dot(tri, lo, preferred_element_type=F32)) + carry_ref[...]
    ccol_ref[...] = c
    carry_ref[...] = c[tc - 1:tc, :]


def gate_cumsum(f, b, *, tc=512):
    s = f.shape[0]
    return pl.pallas_call(
        functools.partial(_gate_cumsum_kernel, tc=tc), grid=(s // tc,),
        in_specs=[pl.BlockSpec((tc, LANES), lambda i: (i, 0)),
                  pl.BlockSpec((1, LANES), lambda i: (0, 0))],
        out_specs=pl.BlockSpec((tc, LANES), lambda i: (i, 0)),
        out_shape=jax.ShapeDtypeStruct((s, LANES), F32),
        scratch_shapes=[pltpu.VMEM((1, LANES), F32)],
        compiler_params=_params("arbitrary"), name="gate_cumsum")(f, b)


def _qk(q, k):
    return lax.dot_general(q, k, (((1,), (1,)), ((), ())), preferred_element_type=F32)


def _colmax8(x):
    return jnp.max(x.reshape(x.shape[0] // 8, 8, x.shape[1]), axis=0)


def _colsum8(x):
    return jnp.sum(x.reshape(x.shape[0] // 8, 8, x.shape[1]), axis=0)


def _fox_kernel(q_ref, k_ref, v_ref, ccol_ref, o_ref,
                vt_ref, csb_ref, sa_ref, sb_ref, p_ref, acc_ref, lp_ref, *, t):
    h = pl.program_id(0)
    i = pl.program_id(1)
    n = i + 1

    @pl.when(i == 0)
    def _():
        def stage(c, carry):
            rows = pl.ds(pl.multiple_of(c * t, t), t)
            vt_ref[c] = v_ref[rows, :].astype(F32).T.astype(BF16)
            lane = lax.broadcasted_iota(jnp.int32, (t, LANES), 1)
            cs = jnp.sum(jnp.where(lane == h, ccol_ref[rows, :], 0.0), axis=-1, keepdims=True)
            csb_ref[rows, :] = jnp.broadcast_to(cs * LOG2E, (t, LANES))
            return carry
        lax.fori_loop(0, k_ref.shape[0] // t, stage, None)

    qt = q_ref[...].astype(F32).T.astype(BF16)
    acc_ref[...] = jnp.zeros(acc_ref.shape, F32)
    lp_ref[...] = jnp.zeros(lp_ref.shape, F32)
    diff = (lax.broadcasted_iota(jnp.int32, (KEY_CHUNK, QUERY_CHUNK), 0)
            - lax.broadcasted_iota(jnp.int32, (KEY_CHUNK, QUERY_CHUNK), 1))

    def scores(j, s_ref, diagonal):
        mparts = [jnp.full((8, QUERY_CHUNK), -jnp.inf, F32) for _ in range(t // QUERY_CHUNK)]
        for c in range(t // KEY_CHUNK):
            k0 = pl.multiple_of(j * t + c * KEY_CHUNK, KEY_CHUNK)
            rows = slice(c * KEY_CHUNK, (c + 1) * KEY_CHUNK)
            kc = k_ref[pl.ds(k0, KEY_CHUNK), :]
            bias = csb_ref[pl.ds(k0, KEY_CHUNK), :]
            bias = jnp.concatenate([bias] * (QUERY_CHUNK // LANES), axis=1)
            for g in range(t // QUERY_CHUNK):
                cols = slice(g * QUERY_CHUNK, (g + 1) * QUERY_CHUNK)
                shift = g * QUERY_CHUNK - c * KEY_CHUNK
                if diagonal and shift <= -QUERY_CHUNK:
                    s_ref[rows, cols] = jnp.full((KEY_CHUNK, QUERY_CHUNK), -jnp.inf, F32)
                    continue
                s = jnp.dot(kc, qt[:, cols], preferred_element_type=F32) - bias
                if diagonal and shift < KEY_CHUNK - 1:
                    s = jnp.where(diff <= shift, s, -jnp.inf)
                s_ref[rows, cols] = s
                mparts[g] = jnp.maximum(mparts[g], _colmax8(s))
        return jnp.concatenate(mparts, axis=1)

    def new_max(m_old, mpart):
        m_new = jnp.maximum(m_old, jnp.max(mpart, axis=0, keepdims=True))
        return m_new, jnp.exp2(m_old - m_new)

    def half(step, cur_ref, nxt_ref, carry, last=False):
        m, alpha = carry
        j = jnp.where(step == 0, n - 1, step - 1)
        lsum = jnp.zeros((8, t), F32)
        for c in range(t // PROB_CHUNK):
            rows = slice(c * PROB_CHUNK, (c + 1) * PROB_CHUNK)
            p = jnp.exp2(cur_ref[rows, :] - m)
            p_ref[rows, :] = p.astype(BF16)
            lsum = lsum + _colsum8(p)
        if not last:
            carry = new_max(m, scores(step, nxt_ref, False))
        lp_ref[...] = alpha * lp_ref[...] + lsum
        acc_ref[...] = alpha * acc_ref[...] + jnp.dot(vt_ref[j], p_ref[...],
                                                      preferred_element_type=F32)
        return carry

    carry = new_max(jnp.full((1, t), -jnp.inf, F32), scores(n - 1, sa_ref, True))

    def pair(mm, carry):
        carry = half(2 * mm, sa_ref, sb_ref, carry)
        return half(2 * mm + 1, sb_ref, sa_ref, carry)

    carry = lax.fori_loop(0, (n - 1) // 2, pair, carry)

    @pl.when(n % 2 == 1)
    def _():
        half(n - 1, sa_ref, sb_ref, carry, last=True)

    @pl.when(n % 2 == 0)
    def _():
        half(n - 1, sb_ref, sa_ref, half(n - 2, sa_ref, sb_ref, carry), last=True)

    l = jnp.sum(lp_ref[...], axis=0, keepdims=True)
    o_ref[...] = (acc_ref[...] / l).T.astype(o_ref.dtype)


def fox_attention(qkv, ccol, *, t=512):
    s = qkv.shape[0]
    d = qkv.shape[1] // 3
    nh = d // HEAD_DIM
    return pl.pallas_call(
        functools.partial(_fox_kernel, t=t),
        grid=(nh, s // t),
        in_specs=[pl.BlockSpec((t, HEAD_DIM), lambda h, i: (i, h)),
                  pl.BlockSpec((s, HEAD_DIM), lambda h, i: (0, nh + h)),
                  pl.BlockSpec((s, HEAD_DIM), lambda h, i: (0, 2 * nh + h)),
                  pl.BlockSpec((s, LANES), lambda h, i: (0, 0))],
        out_specs=pl.BlockSpec((t, HEAD_DIM), lambda h, i: (i, h)),
        out_shape=jax.ShapeDtypeStruct((s, d), BF16),
        scratch_shapes=[pltpu.VMEM((s // t, HEAD_DIM, t), BF16),
                        pltpu.VMEM((s, LANES), F32),
                        pltpu.VMEM((t, t), F32), pltpu.VMEM((t, t), F32),
                        pltpu.VMEM((t, t), BF16),
                        pltpu.VMEM((HEAD_DIM, t), F32), pltpu.VMEM((8, t), F32)],
        compiler_params=_params("arbitrary", "arbitrary"),
        name="fox_attention")(qkv, qkv, qkv, ccol)


def _sb_kernel(q_ref, k_ref, v_ref, o_ref, *, t, scale):
    i = pl.program_id(1)
    q = q_ref[...]
    row = lax.broadcasted_iota(jnp.int32, (t, t), 0)
    col = lax.broadcasted_iota(jnp.int32, (t, t), 1)
    after = jnp.where(row > col, 1.0, 0.0).astype(BF16)
    after2 = jnp.concatenate([after, after], axis=0)

    def tile(j, carry, diagonal):
        run, acc = carry
        start = pl.multiple_of(j * t, t)
        z = _qk(q, k_ref[pl.ds(start, t), :]) * scale
        log_beta = _log_sigmoid(z)
        log_1m = log_beta - z
        if diagonal:
            mask = col < row
            log_1m = jnp.where(mask, log_1m, 0.0)
        hi = log_1m.astype(BF16)
        lo = (log_1m - hi.astype(F32)).astype(BF16)
        suffix = jnp.dot(jnp.concatenate([hi, lo], axis=1), after2,
                         preferred_element_type=F32)
        w = jnp.exp(log_beta + suffix + run)
        if diagonal:
            w = jnp.where(mask, w, 0.0)
        acc = acc + jnp.dot(w.astype(BF16), v_ref[pl.ds(start, t), :],
                            preferred_element_type=F32)
        run = run + jnp.sum(log_1m, axis=-1, keepdims=True)
        return run, acc

    carry = tile(i, (jnp.zeros((t, 1), F32), jnp.zeros((t, HEAD_DIM), F32)), True)
    _, acc = lax.fori_loop(0, i, lambda n, c: tile(i - 1 - n, c, False), carry)
    o_ref[...] = acc.astype(o_ref.dtype)


def sb_attention(q, kv, *, t=256):
    s, d = q.shape
    nh = d // HEAD_DIM
    return pl.pallas_call(
        functools.partial(_sb_kernel, t=t, scale=HEAD_DIM ** -0.5),
        grid=(nh, s // t),
        in_specs=[pl.BlockSpec((t, HEAD_DIM), lambda h, i: (i, h)),
                  pl.BlockSpec((s, HEAD_DIM), lambda h, i: (0, h)),
                  pl.BlockSpec((s, HEAD_DIM), lambda h, i: (0, nh + h))],
        out_specs=pl.BlockSpec((t, HEAD_DIM), lambda h, i: (i, h)),
        out_shape=jax.ShapeDtypeStruct((s, d), BF16),
        compiler_params=_params("parallel", "arbitrary"),
        name="sb_attention")(q, kv, kv)


def _matmul_residual_kernel(a_ref, w_ref, h_ref, o_ref):
    o_ref[...] = h_ref[...] + jnp.dot(a_ref[...], w_ref[...], preferred_element_type=F32)


def matmul_residual(a, w, h, *, tm=512, tn=1024):
    s, k = a.shape
    n = w.shape[1]
    return pl.pallas_call(
        _matmul_residual_kernel, grid=(s // tm, n // tn),
        in_specs=[pl.BlockSpec((tm, k), lambda i, j: (i, 0)),
                  pl.BlockSpec((k, tn), lambda i, j: (0, j)),
                  pl.BlockSpec((tm, tn), lambda i, j: (i, j))],
        out_specs=pl.BlockSpec((tm, tn), lambda i, j: (i, j)),
        out_shape=jax.ShapeDtypeStruct((s, n), F32),
        compiler_params=_params("parallel", "arbitrary"),
        name="matmul_residual")(a, w, h)


def _ffn_kernel(h_ref, g_ref, wg_ref, wu_ref, wd_ref, o_ref, xn_ref):
    @pl.when(pl.program_id(1) == 0)
    def _():
        x = h_ref[...]
        xn_ref[...] = _rms_scale(x, g_ref[...]).astype(BF16)
        o_ref[...] = x

    xn = xn_ref[...]
    gate = jnp.dot(xn, wg_ref[...], preferred_element_type=F32)
    up = jnp.dot(xn, wu_ref[...], preferred_element_type=F32)
    act = (gate * jax.nn.sigmoid(gate) * up).astype(BF16)
    o_ref[...] += jnp.dot(act, wd_ref[...], preferred_element_type=F32)


def ffn(h, g, wg, wu, wd, *, tm=512, tf=512):
    s, d = h.shape
    dff = wg.shape[1]
    return pl.pallas_call(
        _ffn_kernel, grid=(s // tm, dff // tf),
        in_specs=[pl.BlockSpec((tm, d), lambda i, f: (i, 0)),
                  pl.BlockSpec((1, d), lambda i, f: (0, 0)),
                  pl.BlockSpec((d, tf), lambda i, f: (0, f)),
                  pl.BlockSpec((d, tf), lambda i, f: (0, f)),
                  pl.BlockSpec((tf, d), lambda i, f: (f, 0))],
        out_specs=pl.BlockSpec((tm, d), lambda i, f: (i, 0)),
        out_shape=jax.ShapeDtypeStruct((s, d), F32),
        scratch_shapes=[pltpu.VMEM((tm, d), BF16)],
        compiler_params=_params("parallel", "arbitrary"),
        name="ffn")(h, g, wg, wu, wd)


def _rmsnorm_kernel(x_ref, g_ref, o_ref):
    o_ref[...] = _rms_scale(x_ref[...], g_ref[...])


def rmsnorm(x, g, *, tm=512):
    s, d = x.shape
    return pl.pallas_call(
        _rmsnorm_kernel, grid=(s // tm,),
        in_specs=[pl.BlockSpec((tm, d), lambda i: (i, 0)),
                  pl.BlockSpec((1, d), lambda i: (0, 0))],
        out_specs=pl.BlockSpec((tm, d), lambda i: (i, 0)),
        out_shape=jax.ShapeDtypeStruct((s, d), F32),
        compiler_params=_params("parallel"), name="rmsnorm")(x, g)


def kernel(x, attn_norm, ffn_norm, a_w_in, a_b_f, a_w_out, kv_norm, w_kv, b_w_q, b_w_out,
           ffn_w_gate, ffn_w_up, ffn_w_down, final_norm):
    b, s, d = x.shape
    assert b == 1 and d == N_HEADS * HEAD_DIM
    depth = attn_norm.shape[0]
    n_a = a_w_in.shape[0]
    h = x.reshape(s, d)
    kv = None
    for layer in range(depth):
        g_attn = attn_norm[layer].reshape(1, d)
        if layer < n_a:
            w_in = a_w_in[layer]
            w_qkv = w_in[:, :3 * d].astype(BF16)
            w_f = jnp.pad(w_in[:, 3 * d:], ((0, 0), (0, LANES - N_HEADS))).astype(BF16)
            b_f = jnp.pad(a_b_f[layer], (0, LANES - N_HEADS)).reshape(1, LANES)
            qkv, f_logit = norm_matmul(h, g_attn, w_qkv, w_f,
                                       scaled_cols=d, col_scale=HEAD_DIM ** -0.5 * LOG2E)
            o = fox_attention(qkv, gate_cumsum(f_logit, b_f))
            h = matmul_residual(o, a_w_out[layer].astype(BF16), h)
        else:
            j = layer - n_a
            q = norm_matmul(h, g_attn, b_w_q[j].astype(BF16))
            o = sb_attention(q, kv)
            h = matmul_residual(o, b_w_out[j].astype(BF16), h)
        h = ffn(h, ffn_norm[layer].reshape(1, d), ffn_w_gate[layer].astype(BF16),
                ffn_w_up[layer].astype(BF16), ffn_w_down[layer].astype(BF16))
        if layer == n_a - 1:
            kv = norm_matmul(h, kv_norm.reshape(1, d), w_kv.astype(BF16))
    return rmsnorm(h, final_norm.reshape(1, d)).reshape(b, s, d)
```

```python
import functools

import jax
import jax.numpy as jnp
from jax import lax
from jax.experimental import pallas as pl
from jax.experimental.pallas import tpu as pltpu

N_HEADS = 16
HEAD_DIM = 128
EPS = 1e-6
LOG2E = 1.4426950408889634
LANES = 128
MXU_WIDTH = 256
VMEM_LIMIT = 56 * 1024 * 1024

KEY_CHUNK = 128
QUERY_CHUNK = MXU_WIDTH
PROB_CHUNK = 64

F32 = jnp.float32
BF16 = jnp.bfloat16


def _params(*sem):
    return pltpu.CompilerParams(dimension_semantics=sem, vmem_limit_bytes=VMEM_LIMIT)


def _rms_scale(x, g):
    ms = jnp.mean(x * x, axis=-1, keepdims=True)
    return x * lax.rsqrt(ms + EPS) * g


def _scaled_tile(acc, n_scaled_tiles, col_scale):
    if n_scaled_tiles == 0:
        return acc
    return acc * jnp.where(pl.program_id(1) < n_scaled_tiles, col_scale, 1.0)


def _norm_matmul_kernel(x_ref, g_ref, w_ref, o_ref, xn_ref, *, n_scaled_tiles, col_scale):
    @pl.when(pl.program_id(1) == 0)
    def _():
        xn_ref[...] = _rms_scale(x_ref[...], g_ref[...]).astype(BF16)

    acc = jnp.dot(xn_ref[...], w_ref[...], preferred_element_type=F32)
    o_ref[...] = _scaled_tile(acc, n_scaled_tiles, col_scale).astype(o_ref.dtype)


def _norm_matmul_gate_kernel(x_ref, g_ref, w_ref, wf_ref, o_ref, f_ref, xn_ref,
                             *, n_scaled_tiles, col_scale):
    @pl.when(pl.program_id(1) == 0)
    def _():
        xn = _rms_scale(x_ref[...], g_ref[...]).astype(BF16)
        xn_ref[...] = xn
        f_ref[...] = jnp.dot(xn, wf_ref[...], preferred_element_type=F32)

    acc = jnp.dot(xn_ref[...], w_ref[...], preferred_element_type=F32)
    o_ref[...] = _scaled_tile(acc, n_scaled_tiles, col_scale).astype(o_ref.dtype)


def norm_matmul(x, g, w, wf=None, *, scaled_cols=0, col_scale=1.0, tm=512, tn=1024):
    s, d = x.shape
    n = w.shape[1]
    assert scaled_cols % tn == 0
    static = dict(n_scaled_tiles=scaled_cols // tn, col_scale=col_scale)
    grid = (s // tm, n // tn)
    x_spec = pl.BlockSpec((tm, d), lambda i, j: (i, 0))
    g_spec = pl.BlockSpec((1, d), lambda i, j: (0, 0))
    w_spec = pl.BlockSpec((d, tn), lambda i, j: (0, j))
    o_spec = pl.BlockSpec((tm, tn), lambda i, j: (i, j))
    scratch = [pltpu.VMEM((tm, d), BF16)]
    if wf is None:
        return pl.pallas_call(
            functools.partial(_norm_matmul_kernel, **static), grid=grid,
            in_specs=[x_spec, g_spec, w_spec], out_specs=o_spec,
            out_shape=jax.ShapeDtypeStruct((s, n), BF16),
            scratch_shapes=scratch, compiler_params=_params("parallel", "arbitrary"),
            name="norm_matmul")(x, g, w)
    wf_spec = pl.BlockSpec((d, LANES), lambda i, j: (0, 0))
    f_spec = pl.BlockSpec((tm, LANES), lambda i, j: (i, 0))
    return pl.pallas_call(
        functools.partial(_norm_matmul_gate_kernel, **static), grid=grid,
        in_specs=[x_spec, g_spec, w_spec, wf_spec], out_specs=[o_spec, f_spec],
        out_shape=[jax.ShapeDtypeStruct((s, n), BF16),
                   jax.ShapeDtypeStruct((s, LANES), F32)],
        scratch_shapes=scratch, compiler_params=_params("parallel", "arbitrary"),
        name="norm_matmul_gate")(x, g, w, wf)


def _log_sigmoid(z):
    return jnp.minimum(z, 0.0) - jnp.log(1.0 + jnp.exp(-jnp.abs(z)))


def _split3(x):
    hi = x.astype(BF16)
    r = x - hi.astype(F32)
    mid = r.astype(BF16)
    lo = (r - mid.astype(F32)).astype(BF16)
    return hi, mid, lo


def _gate_cumsum_kernel(f_ref, b_ref, ccol_ref, carry_ref, *, tc):
    @pl.when(pl.program_id(0) == 0)
    def _():
        carry_ref[...] = jnp.zeros_like(carry_ref)

    log_f = _log_sigmoid(f_ref[...] + b_ref[...])
    row = lax.broadcasted_iota(jnp.int32, (tc, tc), 0)
    col = lax.broadcasted_iota(jnp.int32, (tc, tc), 1)
    tri = jnp.where(col <= row, 1.0, 0.0).astype(BF16)
    hi, mid, lo = _split3(log_f)
    c = (jnp.dot(tri, hi, preferred_element_type=F32)
         + jnp.dot(tri, mid, preferred_element_type=F32)
         + jnp.dot(tri, lo, preferred_element_type=F32)) + carry_ref[...]
    ccol_ref[...] = c
    carry_ref[...] = c[tc - 1:tc, :]


def gate_cumsum(f, b, *, tc=512):
    s = f.shape[0]
    return pl.pallas_call(
        functools.partial(_gate_cumsum_kernel, tc=tc), grid=(s // tc,),
        in_specs=[pl.BlockSpec((tc, LANES), lambda i: (i, 0)),
                  pl.BlockSpec((1, LANES), lambda i: (0, 0))],
        out_specs=pl.BlockSpec((tc, LANES), lambda i: (i, 0)),
        out_shape=jax.ShapeDtypeStruct((s, LANES), F32),
        scratch_shapes=[pltpu.VMEM((1, LANES), F32)],
        compiler_params=_params("arbitrary"), name="gate_cumsum")(f, b)


def _qk(q, k):
    return lax.dot_general(q, k, (((1,), (1,)), ((), ())), preferred_element_type=F32)


def _colmax8(x):
    return jnp.max(x.reshape(x.shape[0] // 8, 8, x.shape[1]), axis=0)


def _neg_abs(x):
    bits = lax.bitcast_convert_type(x, jnp.int32) | jnp.int32(-2 ** 31)
    return lax.bitcast_convert_type(bits, F32)


def _colsum8(x):
    return jnp.sum(x.reshape(x.shape[0] // 8, 8, x.shape[1]), axis=0)


def _fox_kernel(q_ref, k_ref, v_ref, ccol_ref, o_ref,
                vt_ref, csb_ref, sa_ref, sb_ref, p_ref, acc_ref, lp_ref, *, t):
    h = pl.program_id(0)
    i = pl.program_id(1)
    n = i + 1

    @pl.when(i == 0)
    def _():
        def stage(c, carry):
            rows = pl.ds(pl.multiple_of(c * t, t), t)
            vt_ref[c] = v_ref[rows, :].astype(F32).T.astype(BF16)
            lane = lax.broadcasted_iota(jnp.int32, (t, LANES), 1)
            cs = jnp.sum(jnp.where(lane == h, ccol_ref[rows, :], 0.0), axis=-1, keepdims=True)
            csb_ref[rows, :] = jnp.broadcast_to(cs * LOG2E, (t, LANES))
            return carry
        lax.fori_loop(0, k_ref.shape[0] // t, stage, None)

    qt = q_ref[...].astype(F32).T.astype(BF16)
    acc_ref[...] = jnp.zeros(acc_ref.shape, F32)
    lp_ref[...] = jnp.zeros(lp_ref.shape, F32)
    diff = (lax.broadcasted_iota(jnp.int32, (KEY_CHUNK, QUERY_CHUNK), 0)
            - lax.broadcasted_iota(jnp.int32, (KEY_CHUNK, QUERY_CHUNK), 1))

    def scores(j, s_ref, diagonal):
        mparts = [jnp.full((8, QUERY_CHUNK), -jnp.inf, F32) for _ in range(t // QUERY_CHUNK)]
        for c in range(t // KEY_CHUNK):
            k0 = pl.multiple_of(j * t + c * KEY_CHUNK, KEY_CHUNK)
            rows = slice(c * KEY_CHUNK, (c + 1) * KEY_CHUNK)
            kc = k_ref[pl.ds(k0, KEY_CHUNK), :]
            bias = csb_ref[pl.ds(k0, KEY_CHUNK), :]
            bias = jnp.concatenate([bias] * (QUERY_CHUNK // LANES), axis=1)
            for g in range(t // QUERY_CHUNK):
                cols = slice(g * QUERY_CHUNK, (g + 1) * QUERY_CHUNK)
                shift = g * QUERY_CHUNK - c * KEY_CHUNK
                if diagonal and shift <= -QUERY_CHUNK:
                    s_ref[rows, cols] = jnp.full((KEY_CHUNK, QUERY_CHUNK), -jnp.inf, F32)
                    continue
                s = jnp.dot(kc, qt[:, cols], preferred_element_type=F32) - bias
                if diagonal and shift < KEY_CHUNK - 1:
                    s = jnp.where(diff <= shift, s, -jnp.inf)
                s_ref[rows, cols] = s
                mparts[g] = jnp.maximum(mparts[g], _colmax8(s))
        return jnp.concatenate(mparts, axis=1)

    def new_max(m_old, mpart):
        m_new = jnp.maximum(m_old, jnp.max(mpart, axis=0, keepdims=True))
        return m_new, jnp.exp2(m_old - m_new)

    def half(step, cur_ref, nxt_ref, carry, last=False):
        m, alpha = carry
        j = jnp.where(step == 0, n - 1, step - 1)
        lsum = jnp.zeros((8, t), F32)
        for c in range(t // PROB_CHUNK):
            rows = slice(c * PROB_CHUNK, (c + 1) * PROB_CHUNK)
            p = jnp.exp2(cur_ref[rows, :] - m)
            p_ref[rows, :] = p.astype(BF16)
            lsum = lsum + _colsum8(p)
        if not last:
            carry = new_max(m, scores(step, nxt_ref, False))
        lp_ref[...] = alpha * lp_ref[...] + lsum
        acc_ref[...] = alpha * acc_ref[...] + jnp.dot(vt_ref[j], p_ref[...],
                                                      preferred_element_type=F32)
        return carry

    carry = new_max(jnp.full((1, t), -jnp.inf, F32), scores(n - 1, sa_ref, True))

    def pair(mm, carry):
        carry = half(2 * mm, sa_ref, sb_ref, carry)
        return half(2 * mm + 1, sb_ref, sa_ref, carry)

    carry = lax.fori_loop(0, (n - 1) // 2, pair, carry)

    @pl.when(n % 2 == 1)
    def _():
        half(n - 1, sa_ref, sb_ref, carry, last=True)

    @pl.when(n % 2 == 0)
    def _():
        half(n - 1, sb_ref, sa_ref, half(n - 2, sa_ref, sb_ref, carry), last=True)

    l = jnp.sum(lp_ref[...], axis=0, keepdims=True)
    o_ref[...] = (acc_ref[...] / l).T.astype(o_ref.dtype)


def fox_attention(qkv, ccol, *, t=512):
    s = qkv.shape[0]
    d = qkv.shape[1] // 3
    nh = d // HEAD_DIM
    return pl.pallas_call(
        functools.partial(_fox_kernel, t=t),
        grid=(nh, s // t),
        in_specs=[pl.BlockSpec((t, HEAD_DIM), lambda h, i: (i, h)),
                  pl.BlockSpec((s, HEAD_DIM), lambda h, i: (0, nh + h)),
                  pl.BlockSpec((s, HEAD_DIM), lambda h, i: (0, 2 * nh + h)),
                  pl.BlockSpec((s, LANES), lambda h, i: (0, 0))],
        out_specs=pl.BlockSpec((t, HEAD_DIM), lambda h, i: (i, h)),
        out_shape=jax.ShapeDtypeStruct((s, d), BF16),
        scratch_shapes=[pltpu.VMEM((s // t, HEAD_DIM, t), BF16),
                        pltpu.VMEM((s, LANES), F32),
                        pltpu.VMEM((t, t), F32), pltpu.VMEM((t, t), F32),
                        pltpu.VMEM((t, t), BF16),
                        pltpu.VMEM((HEAD_DIM, t), F32), pltpu.VMEM((8, t), F32)],
        compiler_params=_params("arbitrary", "arbitrary"),
        name="fox_attention")(qkv, qkv, qkv, ccol)


def _sb_kernel(q_ref, k_ref, v_ref, o_ref,
               vt_ref, lba_ref, lbb_ref, lma_ref, lmb_ref, w_ref, acc_ref, *, t):
    i = pl.program_id(1)
    n = i + 1
    n_sub = t // MXU_WIDTH

    @pl.when(i == 0)
    def _():
        def stage(c, carry):
            rows = pl.ds(pl.multiple_of(c * t, t), t)
            vt_ref[c] = v_ref[rows, :].astype(F32).T.astype(BF16)
            return carry
        lax.fori_loop(0, k_ref.shape[0] // t, stage, None)

    qt = q_ref[...].astype(F32).T.astype(BF16)
    acc_ref[...] = jnp.zeros(acc_ref.shape, F32)
    diff = (lax.broadcasted_iota(jnp.int32, (KEY_CHUNK, QUERY_CHUNK), 0)
            - lax.broadcasted_iota(jnp.int32, (KEY_CHUNK, QUERY_CHUNK), 1))
    later = jnp.where(lax.broadcasted_iota(jnp.int32, (MXU_WIDTH, MXU_WIDTH), 1)
                      > lax.broadcasted_iota(jnp.int32, (MXU_WIDTH, MXU_WIDTH), 0),
                      1.0, 0.0).astype(BF16)

    def scores(j, lb_ref, lm_ref, diagonal):
        for c in range(t // KEY_CHUNK):
            k0 = pl.multiple_of(j * t + c * KEY_CHUNK, KEY_CHUNK)
            rows = slice(c * KEY_CHUNK, (c + 1) * KEY_CHUNK)
            kc = k_ref[pl.ds(k0, KEY_CHUNK), :]
            for g in range(t // QUERY_CHUNK):
                cols = slice(g * QUERY_CHUNK, (g + 1) * QUERY_CHUNK)
                shift = g * QUERY_CHUNK - c * KEY_CHUNK
                if diagonal and shift <= -(QUERY_CHUNK - 1):
                    lb_ref[rows, cols] = jnp.full((KEY_CHUNK, QUERY_CHUNK), -jnp.inf, F32)
                    lm_ref[rows, cols] = jnp.zeros((KEY_CHUNK, QUERY_CHUNK), BF16)
                    continue
                z2 = jnp.dot(kc, qt[:, cols], preferred_element_type=F32)
                lb2 = jnp.minimum(z2, 0.0) - jnp.log(1.0 + jnp.exp2(_neg_abs(z2))) * LOG2E
                l1m2 = lb2 - z2
                if diagonal and shift < KEY_CHUNK:
                    visible = diff < shift
                    lb2 = jnp.where(visible, lb2, -jnp.inf)
                    l1m2 = jnp.where(visible, l1m2, 0.0)
                lb_ref[rows, cols] = lb2
                lm_ref[rows, cols] = l1m2.astype(BF16)

    def half(step, lb_ref, lm_ref, nlb_ref, nlm_ref, run, last=False):
        j = i - step
        for b in reversed(range(n_sub)):
            rows = slice(b * MXU_WIDTH, (b + 1) * MXU_WIDTH)
            totals = []
            for g in range(t // QUERY_CHUNK):
                cols = slice(g * QUERY_CHUNK, (g + 1) * QUERY_CHUNK)
                lm = lm_ref[rows, cols]
                within = jnp.dot(later, lm, preferred_element_type=F32)
                w = jnp.exp2(lb_ref[rows, cols] + within + run[:, cols])
                w_ref[rows, cols] = w.astype(BF16)
                totals.append(within[0:1, :] + lm[0:1, :].astype(F32))
            run = run + jnp.concatenate(totals, axis=1)
        if not last:
            scores(j - 1, nlb_ref, nlm_ref, False)
        acc_ref[...] += jnp.dot(vt_ref[j], w_ref[...], preferred_element_type=F32)
        return run

    scores(i, lba_ref, lma_ref, True)
    a = (lba_ref, lma_ref)
    b = (lbb_ref, lmb_ref)

    def pair(mm, run):
        run = half(2 * mm, *a, *b, run)
        return half(2 * mm + 1, *b, *a, run)

    run = lax.fori_loop(0, (n - 1) // 2, pair, jnp.zeros((1, t), F32))

    @pl.when(n % 2 == 1)
    def _():
        half(n - 1, *a, *b, run, last=True)

    @pl.when(n % 2 == 0)
    def _():
        half(n - 1, *b, *a, half(n - 2, *a, *b, run), last=True)

    o_ref[...] = acc_ref[...].T.astype(o_ref.dtype)


def sb_attention(q, kv, *, t=512):
    s, d = q.shape
    nh = d // HEAD_DIM
    return pl.pallas_call(
        functools.partial(_sb_kernel, t=t),
        grid=(nh, s // t),
        in_specs=[pl.BlockSpec((t, HEAD_DIM), lambda h, i: (i, h)),
                  pl.BlockSpec((s, HEAD_DIM), lambda h, i: (0, h)),
                  pl.BlockSpec((s, HEAD_DIM), lambda h, i: (0, nh + h))],
        out_specs=pl.BlockSpec((t, HEAD_DIM), lambda h, i: (i, h)),
        out_shape=jax.ShapeDtypeStruct((s, d), BF16),
        scratch_shapes=[pltpu.VMEM((s // t, HEAD_DIM, t), BF16),
                        pltpu.VMEM((t, t), F32), pltpu.VMEM((t, t), F32),
                        pltpu.VMEM((t, t), BF16), pltpu.VMEM((t, t), BF16),
                        pltpu.VMEM((t, t), BF16),
                        pltpu.VMEM((HEAD_DIM, t), F32)],
        compiler_params=_params("arbitrary", "arbitrary"),
        name="sb_attention")(q, kv, kv)


def _matmul_residual_kernel(a_ref, w_ref, h_ref, o_ref):
    o_ref[...] = h_ref[...] + jnp.dot(a_ref[...], w_ref[...], preferred_element_type=F32)


def matmul_residual(a, w, h, *, tm=512, tn=1024):
    s, k = a.shape
    n = w.shape[1]
    return pl.pallas_call(
        _matmul_residual_kernel, grid=(s // tm, n // tn),
        in_specs=[pl.BlockSpec((tm, k), lambda i, j: (i, 0)),
                  pl.BlockSpec((k, tn), lambda i, j: (0, j)),
                  pl.BlockSpec((tm, tn), lambda i, j: (i, j))],
        out_specs=pl.BlockSpec((tm, tn), lambda i, j: (i, j)),
        out_shape=jax.ShapeDtypeStruct((s, n), F32),
        compiler_params=_params("parallel", "arbitrary"),
        name="matmul_residual")(a, w, h)


def _ffn_kernel(h_ref, g_ref, wg_ref, wu_ref, wd_ref, o_ref, xn_ref):
    @pl.when(pl.program_id(1) == 0)
    def _():
        x = h_ref[...]
        xn_ref[...] = _rms_scale(x, g_ref[...]).astype(BF16)
        o_ref[...] = x

    xn = xn_ref[...]
    gate = jnp.dot(xn, wg_ref[...], preferred_element_type=F32)
    up = jnp.dot(xn, wu_ref[...], preferred_element_type=F32)
    act = (gate * jax.nn.sigmoid(gate) * up).astype(BF16)
    o_ref[...] += jnp.dot(act, wd_ref[...], preferred_element_type=F32)


def ffn(h, g, wg, wu, wd, *, tm=512, tf=512):
    s, d = h.shape
    dff = wg.shape[1]
    return pl.pallas_call(
        _ffn_kernel, grid=(s // tm, dff // tf),
        in_specs=[pl.BlockSpec((tm, d), lambda i, f: (i, 0)),
                  pl.BlockSpec((1, d), lambda i, f: (0, 0)),
                  pl.BlockSpec((d, tf), lambda i, f: (0, f)),
                  pl.BlockSpec((d, tf), lambda i, f: (0, f)),
                  pl.BlockSpec((tf, d), lambda i, f: (f, 0))],
        out_specs=pl.BlockSpec((tm, d), lambda i, f: (i, 0)),
        out_shape=jax.ShapeDtypeStruct((s, d), F32),
        scratch_shapes=[pltpu.VMEM((tm, d), BF16)],
        compiler_params=_params("parallel", "arbitrary"),
        name="ffn")(h, g, wg, wu, wd)


def _rmsnorm_kernel(x_ref, g_ref, o_ref):
    o_ref[...] = _rms_scale(x_ref[...], g_ref[...])


def rmsnorm(x, g, *, tm=512):
    s, d = x.shape
    return pl.pallas_call(
        _rmsnorm_kernel, grid=(s // tm,),
        in_specs=[pl.BlockSpec((tm, d), lambda i: (i, 0)),
                  pl.BlockSpec((1, d), lambda i: (0, 0))],
        out_specs=pl.BlockSpec((tm, d), lambda i: (i, 0)),
        out_shape=jax.ShapeDtypeStruct((s, d), F32),
        compiler_params=_params("parallel"), name="rmsnorm")(x, g)


def kernel(x, attn_norm, ffn_norm, a_w_in, a_b_f, a_w_out, kv_norm, w_kv, b_w_q, b_w_out,
           ffn_w_gate, ffn_w_up, ffn_w_down, final_norm):
    b, s, d = x.shape
    assert b == 1 and d == N_HEADS * HEAD_DIM
    depth = attn_norm.shape[0]
    n_a = a_w_in.shape[0]
    h = x.reshape(s, d)
    kv = None
    for layer in range(depth):
        g_attn = attn_norm[layer].reshape(1, d)
        if layer < n_a:
            w_in = a_w_in[layer]
            w_qkv = w_in[:, :3 * d].astype(BF16)
            w_f = jnp.pad(w_in[:, 3 * d:], ((0, 0), (0, LANES - N_HEADS))).astype(BF16)
            b_f = jnp.pad(a_b_f[layer], (0, LANES - N_HEADS)).reshape(1, LANES)
            qkv, f_logit = norm_matmul(h, g_attn, w_qkv, w_f,
                                       scaled_cols=d, col_scale=HEAD_DIM ** -0.5 * LOG2E)
            o = fox_attention(qkv, gate_cumsum(f_logit, b_f))
            h = matmul_residual(o, a_w_out[layer].astype(BF16), h)
        else:
            j = layer - n_a
            q = norm_matmul(h, g_attn, b_w_q[j].astype(BF16),
                            scaled_cols=d, col_scale=HEAD_DIM ** -0.5 * LOG2E)
            o = sb_attention(q, kv)
            h = matmul_residual(o, b_w_out[j].astype(BF16), h)
        h = ffn(h, ffn_norm[layer].reshape(1, d), ffn_w_gate[layer].astype(BF16),
                ffn_w_up[layer].astype(BF16), ffn_w_down[layer].astype(BF16))
        if layer == n_a - 1:
            kv = norm_matmul(h, kv_norm.reshape(1, d), w_kv.astype(BF16))
    return rmsnorm(h, final_norm.reshape(1, d)).reshape(b, s, d)
```

```python
import functools

import jax
import jax.numpy as jnp
from jax import lax
from jax.experimental import pallas as pl
from jax.experimental.pallas import tpu as pltpu

N_HEADS = 16
HEAD_DIM = 128
EPS = 1e-6
LOG2E = 1.4426950408889634
LANES = 128
MXU_WIDTH = 256
VMEM_LIMIT = 56 * 1024 * 1024

KEY_CHUNK = 128
QUERY_CHUNK = MXU_WIDTH
PROB_CHUNK = 64
WEIGHT_FLOOR_LOG2 = -200.0

F32 = jnp.float32
BF16 = jnp.bfloat16


def _params(*sem):
    return pltpu.CompilerParams(dimension_semantics=sem, vmem_limit_bytes=VMEM_LIMIT)


def _rms_scale(x, g):
    ms = jnp.mean(x * x, axis=-1, keepdims=True)
    return x * lax.rsqrt(ms + EPS) * g


def _scaled_tile(acc, n_scaled_tiles, col_scale):
    if n_scaled_tiles == 0:
        return acc
    return acc * jnp.where(pl.program_id(1) < n_scaled_tiles, col_scale, 1.0)


def _norm_matmul_kernel(x_ref, g_ref, w_ref, o_ref, xn_ref, *, n_scaled_tiles, col_scale):
    @pl.when(pl.program_id(1) == 0)
    def _():
        xn_ref[...] = _rms_scale(x_ref[...], g_ref[...]).astype(BF16)

    acc = jnp.dot(xn_ref[...], w_ref[...], preferred_element_type=F32)
    o_ref[...] = _scaled_tile(acc, n_scaled_tiles, col_scale).astype(o_ref.dtype)


def _norm_matmul_gate_kernel(x_ref, g_ref, w_ref, wf_ref, o_ref, f_ref, xn_ref,
                             *, n_scaled_tiles, col_scale):
    @pl.when(pl.program_id(1) == 0)
    def _():
        xn = _rms_scale(x_ref[...], g_ref[...]).astype(BF16)
        xn_ref[...] = xn
        f_ref[...] = jnp.dot(xn, wf_ref[...], preferred_element_type=F32)

    acc = jnp.dot(xn_ref[...], w_ref[...], preferred_element_type=F32)
    o_ref[...] = _scaled_tile(acc, n_scaled_tiles, col_scale).astype(o_ref.dtype)


def norm_matmul(x, g, w, wf=None, *, scaled_cols=0, col_scale=1.0, tm=512, tn=1024):
    s, d = x.shape
    n = w.shape[1]
    assert scaled_cols % tn == 0
    static = dict(n_scaled_tiles=scaled_cols // tn, col_scale=col_scale)
    grid = (s // tm, n // tn)
    x_spec = pl.BlockSpec((tm, d), lambda i, j: (i, 0))
    g_spec = pl.BlockSpec((1, d), lambda i, j: (0, 0))
    w_spec = pl.BlockSpec((d, tn), lambda i, j: (0, j))
    o_spec = pl.BlockSpec((tm, tn), lambda i, j: (i, j))
    scratch = [pltpu.VMEM((tm, d), BF16)]
    if wf is None:
        return pl.pallas_call(
            functools.partial(_norm_matmul_kernel, **static), grid=grid,
            in_specs=[x_spec, g_spec, w_spec], out_specs=o_spec,
            out_shape=jax.ShapeDtypeStruct((s, n), BF16),
            scratch_shapes=scratch, compiler_params=_params("parallel", "arbitrary"),
            name="norm_matmul")(x, g, w)
    wf_spec = pl.BlockSpec((d, LANES), lambda i, j: (0, 0))
    f_spec = pl.BlockSpec((tm, LANES), lambda i, j: (i, 0))
    return pl.pallas_call(
        functools.partial(_norm_matmul_gate_kernel, **static), grid=grid,
        in_specs=[x_spec, g_spec, w_spec, wf_spec], out_specs=[o_spec, f_spec],
        out_shape=[jax.ShapeDtypeStruct((s, n), BF16),
                   jax.ShapeDtypeStruct((s, LANES), F32)],
        scratch_shapes=scratch, compiler_params=_params("parallel", "arbitrary"),
        name="norm_matmul_gate")(x, g, w, wf)


def _log_sigmoid(z):
    return jnp.minimum(z, 0.0) - jnp.log(1.0 + jnp.exp(-jnp.abs(z)))


def _split3(x):
    hi = x.astype(BF16)
    r = x - hi.astype(F32)
    mid = r.astype(BF16)
    lo = (r - mid.astype(F32)).astype(BF16)
    return hi, mid, lo


def _gate_cumsum_kernel(f_ref, b_ref, ccol_ref, carry_ref, *, tc):
    @pl.when(pl.program_id(0) == 0)
    def _():
        carry_ref[...] = jnp.zeros_like(carry_ref)

    log_f = _log_sigmoid(f_ref[...] + b_ref[...])
    row = lax.broadcasted_iota(jnp.int32, (tc, tc), 0)
    col = lax.broadcasted_iota(jnp.int32, (tc, tc), 1)
    tri = jnp.where(col <= row, 1.0, 0.0).astype(BF16)
    hi, mid, lo = _split3(log_f)
    c = (jnp.dot(tri, hi, preferred_element_type=F32)
         + jnp.dot(tri, mid, preferred_element_type=F32)
         + jnp.dot(tri, lo, preferred_element_type=F32)) + carry_ref[...]
    ccol_ref[...] = c
    carry_ref[...] = c[tc - 1:tc, :]


def gate_cumsum(f, b, *, tc=512):
    s = f.shape[0]
    return pl.pallas_call(
        functools.partial(_gate_cumsum_kernel, tc=tc), grid=(s // tc,),
        in_specs=[pl.BlockSpec((tc, LANES), lambda i: (i, 0)),
                  pl.BlockSpec((1, LANES), lambda i: (0, 0))],
        out_specs=pl.BlockSpec((tc, LANES), lambda i: (i, 0)),
        out_shape=jax.ShapeDtypeStruct((s, LANES), F32),
        scratch_shapes=[pltpu.VMEM((1, LANES), F32)],
        compiler_params=_params("arbitrary"), name="gate_cumsum")(f, b)


def _qk(q, k):
    return lax.dot_general(q, k, (((1,), (1,)), ((), ())), preferred_element_type=F32)


def _colmax8(x):
    return jnp.max(x.reshape(x.shape[0] // 8, 8, x.shape[1]), axis=0)


def _neg_abs(x):
    bits = lax.bitcast_convert_type(x, jnp.int32) | jnp.int32(-2 ** 31)
    return lax.bitcast_convert_type(bits, F32)


def _colsum8(x):
    return jnp.sum(x.reshape(x.shape[0] // 8, 8, x.shape[1]), axis=0)


def _fox_kernel(q_ref, k_ref, v_ref, ccol_ref, o_ref,
                vt_ref, csb_ref, sa_ref, sb_ref, p_ref, acc_ref, lp_ref, *, t):
    h = pl.program_id(0)
    i = pl.program_id(1)
    n = i + 1

    @pl.when(i == 0)
    def _():
        def stage(c, carry):
            rows = pl.ds(pl.multiple_of(c * t, t), t)
            vt_ref[c] = v_ref[rows, :].astype(F32).T.astype(BF16)
            lane = lax.broadcasted_iota(jnp.int32, (t, LANES), 1)
            cs = jnp.sum(jnp.where(lane == h, ccol_ref[rows, :], 0.0), axis=-1, keepdims=True)
            csb_ref[rows, :] = jnp.broadcast_to(cs * LOG2E, (t, LANES))
            return carry
        lax.fori_loop(0, k_ref.shape[0] // t, stage, None)

    qt = q_ref[...].astype(F32).T.astype(BF16)
    acc_ref[...] = jnp.zeros(acc_ref.shape, F32)
    lp_ref[...] = jnp.zeros(lp_ref.shape, F32)
    diff = (lax.broadcasted_iota(jnp.int32, (KEY_CHUNK, QUERY_CHUNK), 0)
            - lax.broadcasted_iota(jnp.int32, (KEY_CHUNK, QUERY_CHUNK), 1))

    def scores(j, s_ref, diagonal):
        mparts = [jnp.full((8, QUERY_CHUNK), -jnp.inf, F32) for _ in range(t // QUERY_CHUNK)]
        for c in range(t // KEY_CHUNK):
            k0 = pl.multiple_of(j * t + c * KEY_CHUNK, KEY_CHUNK)
            rows = slice(c * KEY_CHUNK, (c + 1) * KEY_CHUNK)
            kc = k_ref[pl.ds(k0, KEY_CHUNK), :]
            bias = csb_ref[pl.ds(k0, KEY_CHUNK), :]
            bias = jnp.concatenate([bias] * (QUERY_CHUNK // LANES), axis=1)
            for g in range(t // QUERY_CHUNK):
                cols = slice(g * QUERY_CHUNK, (g + 1) * QUERY_CHUNK)
                shift = g * QUERY_CHUNK - c * KEY_CHUNK
                if diagonal and shift <= -QUERY_CHUNK:
                    s_ref[rows, cols] = jnp.full((KEY_CHUNK, QUERY_CHUNK), -jnp.inf, F32)
                    continue
                s = jnp.dot(kc, qt[:, cols], preferred_element_type=F32) - bias
                if diagonal and shift < KEY_CHUNK - 1:
                    s = jnp.where(diff <= shift, s, -jnp.inf)
                s_ref[rows, cols] = s
                mparts[g] = jnp.maximum(mparts[g], _colmax8(s))
        return jnp.concatenate(mparts, axis=1)

    def new_max(m_old, mpart):
        m_new = jnp.maximum(m_old, jnp.max(mpart, axis=0, keepdims=True))
        return m_new, jnp.exp2(m_old - m_new)

    def half(step, cur_ref, nxt_ref, carry, last=False):
        m, alpha = carry
        j = jnp.where(step == 0, n - 1, step - 1)
        lsum = jnp.zeros((8, t), F32)
        for c in range(t // PROB_CHUNK):
            rows = slice(c * PROB_CHUNK, (c + 1) * PROB_CHUNK)
            p = jnp.exp2(cur_ref[rows, :] - m)
            p_ref[rows, :] = p.astype(BF16)
            lsum = lsum + _colsum8(p)
        if not last:
            carry = new_max(m, scores(step, nxt_ref, False))
        lp_ref[...] = alpha * lp_ref[...] + lsum
        acc_ref[...] = alpha * acc_ref[...] + jnp.dot(vt_ref[j], p_ref[...],
                                                      preferred_element_type=F32)
        return carry

    carry = new_max(jnp.full((1, t), -jnp.inf, F32), scores(n - 1, sa_ref, True))

    def pair(mm, carry):
        carry = half(2 * mm, sa_ref, sb_ref, carry)
        return half(2 * mm + 1, sb_ref, sa_ref, carry)

    carry = lax.fori_loop(0, (n - 1) // 2, pair, carry)

    @pl.when(n % 2 == 1)
    def _():
        half(n - 1, sa_ref, sb_ref, carry, last=True)

    @pl.when(n % 2 == 0)
    def _():
        half(n - 1, sb_ref, sa_ref, half(n - 2, sa_ref, sb_ref, carry), last=True)

    l = jnp.sum(lp_ref[...], axis=0, keepdims=True)
    o_ref[...] = (acc_ref[...] / l).T.astype(o_ref.dtype)


def fox_attention(qkv, ccol, *, t=512):
    s = qkv.shape[0]
    d = qkv.shape[1] // 3
    nh = d // HEAD_DIM
    return pl.pallas_call(
        functools.partial(_fox_kernel, t=t),
        grid=(nh, s // t),
        in_specs=[pl.BlockSpec((t, HEAD_DIM), lambda h, i: (i, h)),
                  pl.BlockSpec((s, HEAD_DIM), lambda h, i: (0, nh + h)),
                  pl.BlockSpec((s, HEAD_DIM), lambda h, i: (0, 2 * nh + h)),
                  pl.BlockSpec((s, LANES), lambda h, i: (0, 0))],
        out_specs=pl.BlockSpec((t, HEAD_DIM), lambda h, i: (i, h)),
        out_shape=jax.ShapeDtypeStruct((s, d), BF16),
        scratch_shapes=[pltpu.VMEM((s // t, HEAD_DIM, t), BF16),
                        pltpu.VMEM((s, LANES), F32),
                        pltpu.VMEM((t, t), F32), pltpu.VMEM((t, t), F32),
                        pltpu.VMEM((t, t), BF16),
                        pltpu.VMEM((HEAD_DIM, t), F32), pltpu.VMEM((8, t), F32)],
        compiler_params=_params("arbitrary", "arbitrary"),
        name="fox_attention")(qkv, qkv, qkv, ccol)


def _sb_kernel(q_ref, k_ref, v_ref, o_ref,
               vt_ref, lba_ref, lbb_ref, lma_ref, lmb_ref, w_ref, acc_ref, *, t):
    i = pl.program_id(1)
    n = i + 1
    n_sub = t // MXU_WIDTH

    @pl.when(i == 0)
    def _():
        def stage(c, carry):
            rows = pl.ds(pl.multiple_of(c * t, t), t)
            vt_ref[c] = v_ref[rows, :].astype(F32).T.astype(BF16)
            return carry
        lax.fori_loop(0, k_ref.shape[0] // t, stage, None)

    qt = q_ref[...].astype(F32).T.astype(BF16)
    acc_ref[...] = jnp.zeros(acc_ref.shape, F32)
    diff = (lax.broadcasted_iota(jnp.int32, (KEY_CHUNK, QUERY_CHUNK), 0)
            - lax.broadcasted_iota(jnp.int32, (KEY_CHUNK, QUERY_CHUNK), 1))
    later = jnp.where(lax.broadcasted_iota(jnp.int32, (MXU_WIDTH, MXU_WIDTH), 1)
                      > lax.broadcasted_iota(jnp.int32, (MXU_WIDTH, MXU_WIDTH), 0),
                      1.0, 0.0).astype(BF16)

    def scores(j, lb_ref, lm_ref, diagonal):
        for c in range(t // KEY_CHUNK):
            k0 = pl.multiple_of(j * t + c * KEY_CHUNK, KEY_CHUNK)
            rows = slice(c * KEY_CHUNK, (c + 1) * KEY_CHUNK)
            kc = k_ref[pl.ds(k0, KEY_CHUNK), :]
            for g in range(t // QUERY_CHUNK):
                cols = slice(g * QUERY_CHUNK, (g + 1) * QUERY_CHUNK)
                shift = g * QUERY_CHUNK - c * KEY_CHUNK
                if diagonal and shift <= -(QUERY_CHUNK - 1):
                    lb_ref[rows, cols] = jnp.full((KEY_CHUNK, QUERY_CHUNK), -jnp.inf, F32)
                    lm_ref[rows, cols] = jnp.zeros((KEY_CHUNK, QUERY_CHUNK), BF16)
                    continue
                z2 = jnp.dot(kc, qt[:, cols], preferred_element_type=F32)
                lb2 = jnp.minimum(z2, 0.0) - jnp.log(1.0 + jnp.exp2(_neg_abs(z2))) * LOG2E
                l1m2 = lb2 - z2
                if diagonal and shift < KEY_CHUNK:
                    visible = diff < shift
                    lb2 = jnp.where(visible, lb2, -jnp.inf)
                    l1m2 = jnp.where(visible, l1m2, 0.0)
                lb_ref[rows, cols] = lb2
                lm_ref[rows, cols] = l1m2.astype(BF16)

    def half(step, lb_ref, lm_ref, nlb_ref, nlm_ref, run, last=False):
        j = i - step
        for b in reversed(range(n_sub)):
            rows = slice(b * MXU_WIDTH, (b + 1) * MXU_WIDTH)
            totals = []
            for g in range(t // QUERY_CHUNK):
                cols = slice(g * QUERY_CHUNK, (g + 1) * QUERY_CHUNK)
                lm = lm_ref[rows, cols]
                within = jnp.dot(later, lm, preferred_element_type=F32)
                w = jnp.exp2(lb_ref[rows, cols] + within + run[:, cols])
                w_ref[rows, cols] = w.astype(BF16)
                totals.append(within[0:1, :] + lm[0:1, :].astype(F32))
            run = run + jnp.concatenate(totals, axis=1)
        if not last:
            scores(j - 1, nlb_ref, nlm_ref, False)
        acc_ref[...] += jnp.dot(vt_ref[j], w_ref[...], preferred_element_type=F32)
        return run

    scores(i, lba_ref, lma_ref, True)
    a = (lba_ref, lma_ref)
    b = (lbb_ref, lmb_ref)

    def live(run):
        return jnp.max(run) > WEIGHT_FLOOR_LOG2

    n_pairs = (n - 1) // 2

    def pair(carry):
        mm, run = carry
        run = half(2 * mm, *a, *b, run)
        return mm + 1, half(2 * mm + 1, *b, *a, run)

    mm, run = lax.while_loop(lambda c: (c[0] < n_pairs) & live(c[1]), pair,
                             (jnp.int32(0), jnp.zeros((1, t), F32)))
    rest = (mm == n_pairs) & live(run)

    @pl.when(rest & (n % 2 == 1))
    def _():
        half(n - 1, *a, *b, run, last=True)

    @pl.when(rest & (n % 2 == 0))
    def _():
        run2 = half(n - 2, *a, *b, run)

        @pl.when(live(run2))
        def _():
            half(n - 1, *b, *a, run2, last=True)

    o_ref[...] = acc_ref[...].T.astype(o_ref.dtype)


def sb_attention(q, kv, *, t=512):
    s, d = q.shape
    nh = d // HEAD_DIM
    return pl.pallas_call(
        functools.partial(_sb_kernel, t=t),
        grid=(nh, s // t),
        in_specs=[pl.BlockSpec((t, HEAD_DIM), lambda h, i: (i, h)),
                  pl.BlockSpec((s, HEAD_DIM), lambda h, i: (0, h)),
                  pl.BlockSpec((s, HEAD_DIM), lambda h, i: (0, nh + h))],
        out_specs=pl.BlockSpec((t, HEAD_DIM), lambda h, i: (i, h)),
        out_shape=jax.ShapeDtypeStruct((s, d), BF16),
        scratch_shapes=[pltpu.VMEM((s // t, HEAD_DIM, t), BF16),
                        pltpu.VMEM((t, t), F32), pltpu.VMEM((t, t), F32),
                        pltpu.VMEM((t, t), BF16), pltpu.VMEM((t, t), BF16),
                        pltpu.VMEM((t, t), BF16),
                        pltpu.VMEM((HEAD_DIM, t), F32)],
        compiler_params=_params("arbitrary", "arbitrary"),
        name="sb_attention")(q, kv, kv)


def _matmul_residual_kernel(a_ref, w_ref, h_ref, o_ref):
    o_ref[...] = h_ref[...] + jnp.dot(a_ref[...], w_ref[...], preferred_element_type=F32)


def matmul_residual(a, w, h, *, tm=512, tn=1024):
    s, k = a.shape
    n = w.shape[1]
    return pl.pallas_call(
        _matmul_residual_kernel, grid=(s // tm, n // tn),
        in_specs=[pl.BlockSpec((tm, k), lambda i, j: (i, 0)),
                  pl.BlockSpec((k, tn), lambda i, j: (0, j)),
                  pl.BlockSpec((tm, tn), lambda i, j: (i, j))],
        out_specs=pl.BlockSpec((tm, tn), lambda i, j: (i, j)),
        out_shape=jax.ShapeDtypeStruct((s, n), F32),
        compiler_params=_params("parallel", "arbitrary"),
        name="matmul_residual")(a, w, h)


def _ffn_kernel(h_ref, g_ref, wg_ref, wu_ref, wd_ref, o_ref, xn_ref):
    @pl.when(pl.program_id(1) == 0)
    def _():
        x = h_ref[...]
        xn_ref[...] = _rms_scale(x, g_ref[...]).astype(BF16)
        o_ref[...] = x

    xn = xn_ref[...]
    gate = jnp.dot(xn, wg_ref[...], preferred_element_type=F32)
    up = jnp.dot(xn, wu_ref[...], preferred_element_type=F32)
    act = (gate * jax.nn.sigmoid(gate) * up).astype(BF16)
    o_ref[...] += jnp.dot(act, wd_ref[...], preferred_element_type=F32)


def ffn(h, g, wg, wu, wd, *, tm=512, tf=512):
    s, d = h.shape
    dff = wg.shape[1]
    return pl.pallas_call(
        _ffn_kernel, grid=(s // tm, dff // tf),
        in_specs=[pl.BlockSpec((tm, d), lambda i, f: (i, 0)),
                  pl.BlockSpec((1, d), lambda i, f: (0, 0)),
                  pl.BlockSpec((d, tf), lambda i, f: (0, f)),
                  pl.BlockSpec((d, tf), lambda i, f: (0, f)),
                  pl.BlockSpec((tf, d), lambda i, f: (f, 0))],
        out_specs=pl.BlockSpec((tm, d), lambda i, f: (i, 0)),
        out_shape=jax.ShapeDtypeStruct((s, d), F32),
        scratch_shapes=[pltpu.VMEM((tm, d), BF16)],
        compiler_params=_params("parallel", "arbitrary"),
        name="ffn")(h, g, wg, wu, wd)


def _rmsnorm_kernel(x_ref, g_ref, o_ref):
    o_ref[...] = _rms_scale(x_ref[...], g_ref[...])


def rmsnorm(x, g, *, tm=512):
    s, d = x.shape
    return pl.pallas_call(
        _rmsnorm_kernel, grid=(s // tm,),
        in_specs=[pl.BlockSpec((tm, d), lambda i: (i, 0)),
                  pl.BlockSpec((1, d), lambda i: (0, 0))],
        out_specs=pl.BlockSpec((tm, d), lambda i: (i, 0)),
        out_shape=jax.ShapeDtypeStruct((s, d), F32),
        compiler_params=_params("parallel"), name="rmsnorm")(x, g)


def kernel(x, attn_norm, ffn_norm, a_w_in, a_b_f, a_w_out, kv_norm, w_kv, b_w_q, b_w_out,
           ffn_w_gate, ffn_w_up, ffn_w_down, final_norm):
    b, s, d = x.shape
    assert b == 1 and d == N_HEADS * HEAD_DIM
    depth = attn_norm.shape[0]
    n_a = a_w_in.shape[0]
    h = x.reshape(s, d)
    kv = None
    for layer in range(depth):
        g_attn = attn_norm[layer].reshape(1, d)
        if layer < n_a:
            w_in = a_w_in[layer]
            w_qkv = w_in[:, :3 * d].astype(BF16)
            w_f = jnp.pad(w_in[:, 3 * d:], ((0, 0), (0, LANES - N_HEADS))).astype(BF16)
            b_f = jnp.pad(a_b_f[layer], (0, LANES - N_HEADS)).reshape(1, LANES)
            qkv, f_logit = norm_matmul(h, g_attn, w_qkv, w_f,
                                       scaled_cols=d, col_scale=HEAD_DIM ** -0.5 * LOG2E)
            o = fox_attention(qkv, gate_cumsum(f_logit, b_f))
            h = matmul_residual(o, a_w_out[layer].astype(BF16), h)
        else:
            j = layer - n_a
            q = norm_matmul(h, g_attn, b_w_q[j].astype(BF16),
                            scaled_cols=d, col_scale=HEAD_DIM ** -0.5 * LOG2E)
            o = sb_attention(q, kv)
            h = matmul_residual(o, b_w_out[j].astype(BF16), h)
        h = ffn(h, ffn_norm[layer].reshape(1, d), ffn_w_gate[layer].astype(BF16),
                ffn_w_up[layer].astype(BF16), ffn_w_down[layer].astype(BF16))
        if layer == n_a - 1:
            kv = norm_matmul(h, kv_norm.reshape(1, d), w_kv.astype(BF16))
    return rmsnorm(h, final_norm.reshape(1, d)).reshape(b, s, d)
```

```python
import functools

import jax
import jax.numpy as jnp
from jax import lax
from jax.experimental import pallas as pl
from jax.experimental.pallas import tpu as pltpu

N_HEADS = 16
HEAD_DIM = 128
EPS = 1e-6
LOG2E = 1.4426950408889634
LANES = 128
MXU_WIDTH = 256
VMEM_LIMIT = 56 * 1024 * 1024

KEY_CHUNK = 128
QUERY_CHUNK = MXU_WIDTH
PROB_CHUNK = 64
WEIGHT_FLOOR_LOG2 = -200.0

F32 = jnp.float32
BF16 = jnp.bfloat16


def _params(*sem):
    return pltpu.CompilerParams(dimension_semantics=sem, vmem_limit_bytes=VMEM_LIMIT)


def _rms_scale(x, g):
    ms = jnp.mean(x * x, axis=-1, keepdims=True)
    return x * lax.rsqrt(ms + EPS) * g


def _scaled_tile(acc, n_scaled_tiles, col_scale):
    if n_scaled_tiles == 0:
        return acc
    return acc * jnp.where(pl.program_id(1) < n_scaled_tiles, col_scale, 1.0)


def _norm_matmul_kernel(x_ref, g_ref, w_ref, o_ref, xn_ref, *, n_scaled_tiles, col_scale):
    @pl.when(pl.program_id(1) == 0)
    def _():
        xn_ref[...] = _rms_scale(x_ref[...], g_ref[...]).astype(BF16)

    acc = jnp.dot(xn_ref[...], w_ref[...], preferred_element_type=F32)
    o_ref[...] = _scaled_tile(acc, n_scaled_tiles, col_scale).astype(o_ref.dtype)


def _norm_matmul_gate_kernel(x_ref, g_ref, w_ref, wf_ref, o_ref, f_ref, xn_ref,
                             *, n_scaled_tiles, col_scale):
    @pl.when(pl.program_id(1) == 0)
    def _():
        xn = _rms_scale(x_ref[...], g_ref[...]).astype(BF16)
        xn_ref[...] = xn
        f_ref[...] = jnp.dot(xn, wf_ref[...], preferred_element_type=F32)

    acc = jnp.dot(xn_ref[...], w_ref[...], preferred_element_type=F32)
    o_ref[...] = _scaled_tile(acc, n_scaled_tiles, col_scale).astype(o_ref.dtype)


def norm_matmul(x, g, w, wf=None, *, scaled_cols=0, col_scale=1.0, tm=512, tn=1024):
    s, d = x.shape
    n = w.shape[1]
    assert scaled_cols % tn == 0
    static = dict(n_scaled_tiles=scaled_cols // tn, col_scale=col_scale)
    grid = (s // tm, n // tn)
    x_spec = pl.BlockSpec((tm, d), lambda i, j: (i, 0))
    g_spec = pl.BlockSpec((1, d), lambda i, j: (0, 0))
    w_spec = pl.BlockSpec((d, tn), lambda i, j: (0, j))
    o_spec = pl.BlockSpec((tm, tn), lambda i, j: (i, j))
    scratch = [pltpu.VMEM((tm, d), BF16)]
    if wf is None:
        return pl.pallas_call(
            functools.partial(_norm_matmul_kernel, **static), grid=grid,
            in_specs=[x_spec, g_spec, w_spec], out_specs=o_spec,
            out_shape=jax.ShapeDtypeStruct((s, n), BF16),
            scratch_shapes=scratch, compiler_params=_params("parallel", "arbitrary"),
            name="norm_matmul")(x, g, w)
    wf_spec = pl.BlockSpec((d, LANES), lambda i, j: (0, 0))
    f_spec = pl.BlockSpec((tm, LANES), lambda i, j: (i, 0))
    return pl.pallas_call(
        functools.partial(_norm_matmul_gate_kernel, **static), grid=grid,
        in_specs=[x_spec, g_spec, w_spec, wf_spec], out_specs=[o_spec, f_spec],
        out_shape=[jax.ShapeDtypeStruct((s, n), BF16),
                   jax.ShapeDtypeStruct((s, LANES), F32)],
        scratch_shapes=scratch, compiler_params=_params("parallel", "arbitrary"),
        name="norm_matmul_gate")(x, g, w, wf)


def _log_sigmoid(z):
    return jnp.minimum(z, 0.0) - jnp.log(1.0 + jnp.exp(-jnp.abs(z)))


def _split3(x):
    hi = x.astype(BF16)
    r = x - hi.astype(F32)
    mid = r.astype(BF16)
    lo = (r - mid.astype(F32)).astype(BF16)
    return hi, mid, lo


def _gate_cumsum_kernel(f_ref, b_ref, ccol_ref, carry_ref, *, tc):
    @pl.when(pl.program_id(0) == 0)
    def _():
        carry_ref[...] = jnp.zeros_like(carry_ref)

    log_f = _log_sigmoid(f_ref[...] + b_ref[...])
    row = lax.broadcasted_iota(jnp.int32, (tc, tc), 0)
    col = lax.broadcasted_iota(jnp.int32, (tc, tc), 1)
    tri = jnp.where(col <= row, 1.0, 0.0).astype(BF16)
    hi, mid, lo = _split3(log_f)
    c = (jnp.dot(tri, hi, preferred_element_type=F32)
         + jnp.dot(tri, mid, preferred_element_type=F32)
         + jnp.dot(tri, lo, preferred_element_type=F32)) + carry_ref[...]
    ccol_ref[...] = c
    carry_ref[...] = c[tc - 1:tc, :]


def gate_cumsum(f, b, *, tc=512):
    s = f.shape[0]
    return pl.pallas_call(
        functools.partial(_gate_cumsum_kernel, tc=tc), grid=(s // tc,),
        in_specs=[pl.BlockSpec((tc, LANES), lambda i: (i, 0)),
                  pl.BlockSpec((1, LANES), lambda i: (0, 0))],
        out_specs=pl.BlockSpec((tc, LANES), lambda i: (i, 0)),
        out_shape=jax.ShapeDtypeStruct((s, LANES), F32),
        scratch_shapes=[pltpu.VMEM((1, LANES), F32)],
        compiler_params=_params("arbitrary"), name="gate_cumsum")(f, b)


def _qk(q, k):
    return lax.dot_general(q, k, (((1,), (1,)), ((), ())), preferred_element_type=F32)


def _colmax8(x):
    return jnp.max(x.reshape(x.shape[0] // 8, 8, x.shape[1]), axis=0)


def _neg_abs(x):
    bits = lax.bitcast_convert_type(x, jnp.int32) | jnp.int32(-2 ** 31)
    return lax.bitcast_convert_type(bits, F32)


def _colsum8(x):
    return jnp.sum(x.reshape(x.shape[0] // 8, 8, x.shape[1]), axis=0)


def _fox_kernel(q_ref, k_ref, v_ref, ccol_ref, o_ref,
                vt_ref, csb_ref, sa_ref, sb_ref, p_ref, acc_ref, lp_ref, *, t):
    h = pl.program_id(0)
    i = pl.program_id(1)
    n = i + 1

    @pl.when(i == 0)
    def _():
        def stage(c, carry):
            rows = pl.ds(pl.multiple_of(c * t, t), t)
            vt_ref[c] = v_ref[rows, :].astype(F32).T.astype(BF16)
            lane = lax.broadcasted_iota(jnp.int32, (t, LANES), 1)
            cs = jnp.sum(jnp.where(lane == h, ccol_ref[rows, :], 0.0), axis=-1, keepdims=True)
            csb_ref[rows, :] = jnp.broadcast_to(cs * LOG2E, (t, LANES))
            return carry
        lax.fori_loop(0, k_ref.shape[0] // t, stage, None)

    qt = q_ref[...].astype(F32).T.astype(BF16)
    acc_ref[...] = jnp.zeros(acc_ref.shape, F32)
    lp_ref[...] = jnp.zeros(lp_ref.shape, F32)
    diff = (lax.broadcasted_iota(jnp.int32, (KEY_CHUNK, QUERY_CHUNK), 0)
            - lax.broadcasted_iota(jnp.int32, (KEY_CHUNK, QUERY_CHUNK), 1))

    def scores(j, s_ref, diagonal):
        mparts = [jnp.full((8, QUERY_CHUNK), -jnp.inf, F32) for _ in range(t // QUERY_CHUNK)]
        for c in range(t // KEY_CHUNK):
            k0 = pl.multiple_of(j * t + c * KEY_CHUNK, KEY_CHUNK)
            rows = slice(c * KEY_CHUNK, (c + 1) * KEY_CHUNK)
            kc = k_ref[pl.ds(k0, KEY_CHUNK), :]
            bias = csb_ref[pl.ds(k0, KEY_CHUNK), :]
            bias = jnp.concatenate([bias] * (QUERY_CHUNK // LANES), axis=1)
            for g in range(t // QUERY_CHUNK):
                cols = slice(g * QUERY_CHUNK, (g + 1) * QUERY_CHUNK)
                shift = g * QUERY_CHUNK - c * KEY_CHUNK
                if diagonal and shift <= -QUERY_CHUNK:
                    s_ref[rows, cols] = jnp.full((KEY_CHUNK, QUERY_CHUNK), -jnp.inf, F32)
                    continue
                s = jnp.dot(kc, qt[:, cols], preferred_element_type=F32) - bias
                if diagonal and shift < KEY_CHUNK - 1:
                    s = jnp.where(diff <= shift, s, -jnp.inf)
                s_ref[rows, cols] = s
                mparts[g] = jnp.maximum(mparts[g], _colmax8(s))
        return jnp.concatenate(mparts, axis=1)

    def new_max(m_old, mpart):
        m_new = jnp.maximum(m_old, jnp.max(mpart, axis=0, keepdims=True))
        return m_new, jnp.exp2(m_old - m_new)

    def half(step, cur_ref, nxt_ref, carry, last=False):
        m, alpha = carry
        j = jnp.where(step == 0, n - 1, step - 1)
        lsum = jnp.zeros((8, t), F32)
        for c in range(t // PROB_CHUNK):
            rows = slice(c * PROB_CHUNK, (c + 1) * PROB_CHUNK)
            p = jnp.exp2(cur_ref[rows, :] - m)
            p_ref[rows, :] = p.astype(BF16)
            lsum = lsum + _colsum8(p)
        if not last:
            carry = new_max(m, scores(step, nxt_ref, False))
        lp_ref[...] = alpha * lp_ref[...] + lsum
        acc_ref[...] = alpha * acc_ref[...] + jnp.dot(vt_ref[j], p_ref[...],
                                                      preferred_element_type=F32)
        return carry

    carry = new_max(jnp.full((1, t), -jnp.inf, F32), scores(n - 1, sa_ref, True))

    def pair(mm, carry):
        carry = half(2 * mm, sa_ref, sb_ref, carry)
        return half(2 * mm + 1, sb_ref, sa_ref, carry)

    carry = lax.fori_loop(0, (n - 1) // 2, pair, carry)

    @pl.when(n % 2 == 1)
    def _():
        half(n - 1, sa_ref, sb_ref, carry, last=True)

    @pl.when(n % 2 == 0)
    def _():
        half(n - 1, sb_ref, sa_ref, half(n - 2, sa_ref, sb_ref, carry), last=True)

    l = jnp.sum(lp_ref[...], axis=0, keepdims=True)
    o_ref[...] = (acc_ref[...] / l).T.astype(o_ref.dtype)


def fox_attention(qkv, ccol, *, t=512):
    s = qkv.shape[0]
    d = qkv.shape[1] // 3
    nh = d // HEAD_DIM
    return pl.pallas_call(
        functools.partial(_fox_kernel, t=t),
        grid=(nh, s // t),
        in_specs=[pl.BlockSpec((t, HEAD_DIM), lambda h, i: (i, h)),
                  pl.BlockSpec((s, HEAD_DIM), lambda h, i: (0, nh + h)),
                  pl.BlockSpec((s, HEAD_DIM), lambda h, i: (0, 2 * nh + h)),
                  pl.BlockSpec((s, LANES), lambda h, i: (0, 0))],
        out_specs=pl.BlockSpec((t, HEAD_DIM), lambda h, i: (i, h)),
        out_shape=jax.ShapeDtypeStruct((s, d), BF16),
        scratch_shapes=[pltpu.VMEM((s // t, HEAD_DIM, t), BF16),
                        pltpu.VMEM((s, LANES), F32),
                        pltpu.VMEM((t, t), F32), pltpu.VMEM((t, t), F32),
                        pltpu.VMEM((t, t), BF16),
                        pltpu.VMEM((HEAD_DIM, t), F32), pltpu.VMEM((8, t), F32)],
        compiler_params=_params("arbitrary", "arbitrary"),
        name="fox_attention")(qkv, qkv, qkv, ccol)


def _sb_kernel(q_ref, k_ref, v_ref, o_ref, vt_ref, lb_ref, lm_ref, w_ref, acc_ref, *, t, hg):
    i = pl.program_id(1)
    n = i + 1
    n_sub = t // MXU_WIDTH
    heads = range(hg)

    def head_cols(hh):
        return slice(hh * HEAD_DIM, (hh + 1) * HEAD_DIM)

    @pl.when(i == 0)
    def _():
        def stage(c, carry):
            rows = pl.ds(pl.multiple_of(c * t, t), t)
            for hh in heads:
                vt_ref[hh, c] = v_ref[rows, head_cols(hh)].astype(F32).T.astype(BF16)
            return carry
        lax.fori_loop(0, k_ref.shape[0] // t, stage, None)

    qts = [q_ref[:, head_cols(hh)].astype(F32).T.astype(BF16) for hh in heads]
    acc_ref[...] = jnp.zeros(acc_ref.shape, F32)
    diff = (lax.broadcasted_iota(jnp.int32, (KEY_CHUNK, QUERY_CHUNK), 0)
            - lax.broadcasted_iota(jnp.int32, (KEY_CHUNK, QUERY_CHUNK), 1))
    later = jnp.where(lax.broadcasted_iota(jnp.int32, (MXU_WIDTH, MXU_WIDTH), 1)
                      > lax.broadcasted_iota(jnp.int32, (MXU_WIDTH, MXU_WIDTH), 0),
                      1.0, 0.0).astype(BF16)

    def scores(j, slot, diagonal):
        for c in range(t // KEY_CHUNK):
            k0 = pl.multiple_of(j * t + c * KEY_CHUNK, KEY_CHUNK)
            rows = slice(c * KEY_CHUNK, (c + 1) * KEY_CHUNK)
            for g in range(t // QUERY_CHUNK):
                cols = slice(g * QUERY_CHUNK, (g + 1) * QUERY_CHUNK)
                shift = g * QUERY_CHUNK - c * KEY_CHUNK
                for hh in heads:
                    if diagonal and shift <= -(QUERY_CHUNK - 1):
                        lb_ref[hh, slot, rows, cols] = jnp.full((KEY_CHUNK, QUERY_CHUNK), -jnp.inf, F32)
                        lm_ref[hh, slot, rows, cols] = jnp.zeros((KEY_CHUNK, QUERY_CHUNK), BF16)
                        continue
                    kc = k_ref[pl.ds(k0, KEY_CHUNK), head_cols(hh)]
                    z2 = jnp.dot(kc, qts[hh][:, cols], preferred_element_type=F32)
                    lb2 = jnp.minimum(z2, 0.0) - jnp.log(1.0 + jnp.exp2(_neg_abs(z2))) * LOG2E
                    l1m2 = lb2 - z2
                    if diagonal and shift < KEY_CHUNK:
                        visible = diff < shift
                        lb2 = jnp.where(visible, lb2, -jnp.inf)
                        l1m2 = jnp.where(visible, l1m2, 0.0)
                    lb_ref[hh, slot, rows, cols] = lb2
                    lm_ref[hh, slot, rows, cols] = l1m2.astype(BF16)

    def half(step, slot, runs, last=False):
        j = i - step
        runs = list(runs)
        for b in reversed(range(n_sub)):
            rows = slice(b * MXU_WIDTH, (b + 1) * MXU_WIDTH)
            totals = [[] for _ in heads]
            for g in range(t // QUERY_CHUNK):
                cols = slice(g * QUERY_CHUNK, (g + 1) * QUERY_CHUNK)
                for hh in heads:
                    lm = lm_ref[hh, slot, rows, cols]
                    within = jnp.dot(later, lm, preferred_element_type=F32)
                    w = jnp.exp2(lb_ref[hh, slot, rows, cols] + within + runs[hh][:, cols])
                    w_ref[hh, rows, cols] = w.astype(BF16)
                    totals[hh].append(within[0:1, :] + lm[0:1, :].astype(F32))
            for hh in heads:
                runs[hh] = runs[hh] + jnp.concatenate(totals[hh], axis=1)
        if not last:
            scores(j - 1, 1 - slot, False)
        for hh in heads:
            acc_ref[hh] += jnp.dot(vt_ref[hh, j], w_ref[hh], preferred_element_type=F32)
        return tuple(runs)

    def live(runs):
        return jnp.max(functools.reduce(jnp.maximum, runs)) > WEIGHT_FLOOR_LOG2

    scores(i, 0, True)
    n_pairs = (n - 1) // 2

    def pair(carry):
        mm, runs = carry
        runs = half(2 * mm, 0, runs)
        return mm + 1, half(2 * mm + 1, 1, runs)

    mm, runs = lax.while_loop(lambda c: (c[0] < n_pairs) & live(c[1]), pair,
                              (jnp.int32(0), tuple(jnp.zeros((1, t), F32) for _ in heads)))
    rest = (mm == n_pairs) & live(runs)

    @pl.when(rest & (n % 2 == 1))
    def _():
        half(n - 1, 0, runs, last=True)

    @pl.when(rest & (n % 2 == 0))
    def _():
        runs2 = half(n - 2, 0, runs)

        @pl.when(live(runs2))
        def _():
            half(n - 1, 1, runs2, last=True)

    for hh in heads:
        o_ref[:, head_cols(hh)] = acc_ref[hh].T.astype(o_ref.dtype)


def sb_attention(q, kv, *, t=256, hg=4):
    s, d = q.shape
    ng = d // (hg * HEAD_DIM)
    w = hg * HEAD_DIM
    return pl.pallas_call(
        functools.partial(_sb_kernel, t=t, hg=hg),
        grid=(ng, s // t),
        in_specs=[pl.BlockSpec((t, w), lambda g, i: (i, g)),
                  pl.BlockSpec((s, w), lambda g, i: (0, g)),
                  pl.BlockSpec((s, w), lambda g, i: (0, ng + g))],
        out_specs=pl.BlockSpec((t, w), lambda g, i: (i, g)),
        out_shape=jax.ShapeDtypeStruct((s, d), BF16),
        scratch_shapes=[pltpu.VMEM((hg, s // t, HEAD_DIM, t), BF16),
                        pltpu.VMEM((hg, 2, t, t), F32),
                        pltpu.VMEM((hg, 2, t, t), BF16),
                        pltpu.VMEM((hg, t, t), BF16),
                        pltpu.VMEM((hg, HEAD_DIM, t), F32)],
        compiler_params=_params("arbitrary", "arbitrary"),
        name="sb_attention")(q, kv, kv)


def _matmul_residual_kernel(a_ref, w_ref, h_ref, o_ref):
    o_ref[...] = h_ref[...] + jnp.dot(a_ref[...], w_ref[...], preferred_element_type=F32)


def matmul_residual(a, w, h, *, tm=512, tn=1024):
    s, k = a.shape
    n = w.shape[1]
    return pl.pallas_call(
        _matmul_residual_kernel, grid=(s // tm, n // tn),
        in_specs=[pl.BlockSpec((tm, k), lambda i, j: (i, 0)),
                  pl.BlockSpec((k, tn), lambda i, j: (0, j)),
                  pl.BlockSpec((tm, tn), lambda i, j: (i, j))],
        out_specs=pl.BlockSpec((tm, tn), lambda i, j: (i, j)),
        out_shape=jax.ShapeDtypeStruct((s, n), F32),
        compiler_params=_params("parallel", "arbitrary"),
        name="matmul_residual")(a, w, h)


def _ffn_kernel(h_ref, g_ref, wg_ref, wu_ref, wd_ref, o_ref, xn_ref):
    @pl.when(pl.program_id(1) == 0)
    def _():
        x = h_ref[...]
        xn_ref[...] = _rms_scale(x, g_ref[...]).astype(BF16)
        o_ref[...] = x

    xn = xn_ref[...]
    gate = jnp.dot(xn, wg_ref[...], preferred_element_type=F32)
    up = jnp.dot(xn, wu_ref[...], preferred_element_type=F32)
    act = (gate * jax.nn.sigmoid(gate) * up).astype(BF16)
    o_ref[...] += jnp.dot(act, wd_ref[...], preferred_element_type=F32)


def ffn(h, g, wg, wu, wd, *, tm=512, tf=512):
    s, d = h.shape
    dff = wg.shape[1]
    return pl.pallas_call(
        _ffn_kernel, grid=(s // tm, dff // tf),
        in_specs=[pl.BlockSpec((tm, d), lambda i, f: (i, 0)),
                  pl.BlockSpec((1, d), lambda i, f: (0, 0)),
                  pl.BlockSpec((d, tf), lambda i, f: (0, f)),
                  pl.BlockSpec((d, tf), lambda i, f: (0, f)),
                  pl.BlockSpec((tf, d), lambda i, f: (f, 0))],
        out_specs=pl.BlockSpec((tm, d), lambda i, f: (i, 0)),
        out_shape=jax.ShapeDtypeStruct((s, d), F32),
        scratch_shapes=[pltpu.VMEM((tm, d), BF16)],
        compiler_params=_params("parallel", "arbitrary"),
        name="ffn")(h, g, wg, wu, wd)


def _rmsnorm_kernel(x_ref, g_ref, o_ref):
    o_ref[...] = _rms_scale(x_ref[...], g_ref[...])


def rmsnorm(x, g, *, tm=512):
    s, d = x.shape
    return pl.pallas_call(
        _rmsnorm_kernel, grid=(s // tm,),
        in_specs=[pl.BlockSpec((tm, d), lambda i: (i, 0)),
                  pl.BlockSpec((1, d), lambda i: (0, 0))],
        out_specs=pl.BlockSpec((tm, d), lambda i: (i, 0)),
        out_shape=jax.ShapeDtypeStruct((s, d), F32),
        compiler_params=_params("parallel"), name="rmsnorm")(x, g)


def kernel(x, attn_norm, ffn_norm, a_w_in, a_b_f, a_w_out, kv_norm, w_kv, b_w_q, b_w_out,
           ffn_w_gate, ffn_w_up, ffn_w_down, final_norm):
    b, s, d = x.shape
    assert b == 1 and d == N_HEADS * HEAD_DIM
    depth = attn_norm.shape[0]
    n_a = a_w_in.shape[0]
    h = x.reshape(s, d)
    kv = None
    for layer in range(depth):
        g_attn = attn_norm[layer].reshape(1, d)
        if layer < n_a:
            w_in = a_w_in[layer]
            w_qkv = w_in[:, :3 * d].astype(BF16)
            w_f = jnp.pad(w_in[:, 3 * d:], ((0, 0), (0, LANES - N_HEADS))).astype(BF16)
            b_f = jnp.pad(a_b_f[layer], (0, LANES - N_HEADS)).reshape(1, LANES)
            qkv, f_logit = norm_matmul(h, g_attn, w_qkv, w_f,
                                       scaled_cols=d, col_scale=HEAD_DIM ** -0.5 * LOG2E)
            o = fox_attention(qkv, gate_cumsum(f_logit, b_f))
            h = matmul_residual(o, a_w_out[layer].astype(BF16), h)
        else:
            j = layer - n_a
            q = norm_matmul(h, g_attn, b_w_q[j].astype(BF16),
                            scaled_cols=d, col_scale=HEAD_DIM ** -0.5 * LOG2E)
            o = sb_attention(q, kv)
            h = matmul_residual(o, b_w_out[j].astype(BF16), h)
        h = ffn(h, ffn_norm[layer].reshape(1, d), ffn_w_gate[layer].astype(BF16),
                ffn_w_up[layer].astype(BF16), ffn_w_down[layer].astype(BF16))
        if layer == n_a - 1:
            kv = norm_matmul(h, kv_norm.reshape(1, d), w_kv.astype(BF16))
    return rmsnorm(h, final_norm.reshape(1, d)).reshape(b, s, d)
```

```python
import functools

import jax
import jax.numpy as jnp
from jax import lax
from jax.experimental import pallas as pl
from jax.experimental.pallas import tpu as pltpu

N_HEADS = 16
HEAD_DIM = 128
EPS = 1e-6
LOG2E = 1.4426950408889634
LANES = 128
MXU_WIDTH = 256
VMEM_LIMIT = 56 * 1024 * 1024

KEY_CHUNK = 128
QUERY_CHUNK = MXU_WIDTH
PROB_CHUNK = 32
WEIGHT_FLOOR_LOG2 = -200.0

F32 = jnp.float32
BF16 = jnp.bfloat16


def _params(*sem):
    return pltpu.CompilerParams(dimension_semantics=sem, vmem_limit_bytes=VMEM_LIMIT)


def _rms_scale(x, g):
    ms = jnp.mean(x * x, axis=-1, keepdims=True)
    return x * lax.rsqrt(ms + EPS) * g


def _scaled_tile(acc, n_scaled_tiles, col_scale):
    if n_scaled_tiles == 0:
        return acc
    return acc * jnp.where(pl.program_id(1) < n_scaled_tiles, col_scale, 1.0)


def _norm_matmul_kernel(x_ref, g_ref, w_ref, o_ref, xn_ref, *, n_scaled_tiles, col_scale):
    @pl.when(pl.program_id(1) == 0)
    def _():
        xn_ref[...] = _rms_scale(x_ref[...], g_ref[...]).astype(BF16)

    acc = jnp.dot(xn_ref[...], w_ref[...], preferred_element_type=F32)
    o_ref[...] = _scaled_tile(acc, n_scaled_tiles, col_scale).astype(o_ref.dtype)


def _norm_matmul_gate_kernel(x_ref, g_ref, w_ref, wf_ref, o_ref, f_ref, xn_ref,
                             *, n_scaled_tiles, col_scale):
    @pl.when(pl.program_id(1) == 0)
    def _():
        xn = _rms_scale(x_ref[...], g_ref[...]).astype(BF16)
        xn_ref[...] = xn
        f_ref[...] = jnp.dot(xn, wf_ref[...], preferred_element_type=F32)

    acc = jnp.dot(xn_ref[...], w_ref[...], preferred_element_type=F32)
    o_ref[...] = _scaled_tile(acc, n_scaled_tiles, col_scale).astype(o_ref.dtype)


def norm_matmul(x, g, w, wf=None, *, scaled_cols=0, col_scale=1.0, tm=512, tn=1024):
    s, d = x.shape
    n = w.shape[1]
    assert scaled_cols % tn == 0
    static = dict(n_scaled_tiles=scaled_cols // tn, col_scale=col_scale)
    grid = (s // tm, n // tn)
    x_spec = pl.BlockSpec((tm, d), lambda i, j: (i, 0))
    g_spec = pl.BlockSpec((1, d), lambda i, j: (0, 0))
    w_spec = pl.BlockSpec((d, tn), lambda i, j: (0, j))
    o_spec = pl.BlockSpec((tm, tn), lambda i, j: (i, j))
    scratch = [pltpu.VMEM((tm, d), BF16)]
    if wf is None:
        return pl.pallas_call(
            functools.partial(_norm_matmul_kernel, **static), grid=grid,
            in_specs=[x_spec, g_spec, w_spec], out_specs=o_spec,
            out_shape=jax.ShapeDtypeStruct((s, n), BF16),
            scratch_shapes=scratch, compiler_params=_params("parallel", "arbitrary"),
            name="norm_matmul")(x, g, w)
    wf_spec = pl.BlockSpec((d, LANES), lambda i, j: (0, 0))
    f_spec = pl.BlockSpec((tm, LANES), lambda i, j: (i, 0))
    return pl.pallas_call(
        functools.partial(_norm_matmul_gate_kernel, **static), grid=grid,
        in_specs=[x_spec, g_spec, w_spec, wf_spec], out_specs=[o_spec, f_spec],
        out_shape=[jax.ShapeDtypeStruct((s, n), BF16),
                   jax.ShapeDtypeStruct((s, LANES), F32)],
        scratch_shapes=scratch, compiler_params=_params("parallel", "arbitrary"),
        name="norm_matmul_gate")(x, g, w, wf)


def _log_sigmoid(z):
    return jnp.minimum(z, 0.0) - jnp.log(1.0 + jnp.exp(-jnp.abs(z)))


def _split3(x):
    hi = x.astype(BF16)
    r = x - hi.astype(F32)
    mid = r.astype(BF16)
    lo = (r - mid.astype(F32)).astype(BF16)
    return hi, mid, lo


def _gate_cumsum_kernel(f_ref, b_ref, ccol_ref, carry_ref, *, tc):
    @pl.when(pl.program_id(0) == 0)
    def _():
        carry_ref[...] = jnp.zeros_like(carry_ref)

    log_f = _log_sigmoid(f_ref[...] + b_ref[...])
    row = lax.broadcasted_iota(jnp.int32, (tc, tc), 0)
    col = lax.broadcasted_iota(jnp.int32, (tc, tc), 1)
    tri = jnp.where(col <= row, 1.0, 0.0).astype(BF16)
    hi, mid, lo = _split3(log_f)
    c = (jnp.dot(tri, hi, preferred_element_type=F32)
         + jnp.dot(tri, mid, preferred_element_type=F32)
         + jnp.dot(tri, lo, preferred_element_type=F32)) + carry_ref[...]
    ccol_ref[...] = c
    carry_ref[...] = c[tc - 1:tc, :]


def gate_cumsum(f, b, *, tc=512):
    s = f.shape[0]
    return pl.pallas_call(
        functools.partial(_gate_cumsum_kernel, tc=tc), grid=(s // tc,),
        in_specs=[pl.BlockSpec((tc, LANES), lambda i: (i, 0)),
                  pl.BlockSpec((1, LANES), lambda i: (0, 0))],
        out_specs=pl.BlockSpec((tc, LANES), lambda i: (i, 0)),
        out_shape=jax.ShapeDtypeStruct((s, LANES), F32),
        scratch_shapes=[pltpu.VMEM((1, LANES), F32)],
        compiler_params=_params("arbitrary"), name="gate_cumsum")(f, b)


def _qk(q, k):
    return lax.dot_general(q, k, (((1,), (1,)), ((), ())), preferred_element_type=F32)


def _colmax8(x):
    return jnp.max(x.reshape(x.shape[0] // 8, 8, x.shape[1]), axis=0)


def _neg_abs(x):
    bits = lax.bitcast_convert_type(x, jnp.int32) | jnp.int32(-2 ** 31)
    return lax.bitcast_convert_type(bits, F32)


def _colsum8(x):
    return jnp.sum(x.reshape(x.shape[0] // 8, 8, x.shape[1]), axis=0)


def _fox_kernel(q_ref, k_ref, v_ref, ccol_ref, o_ref,
                vt_ref, csb_ref, kn_ref, s_ref, p_ref, acc_ref, lp_ref, *, t, hg):
    grp = pl.program_id(0)
    i = pl.program_id(1)
    n = i + 1
    heads = range(hg)

    def head_cols(hh):
        return slice(hh * HEAD_DIM, (hh + 1) * HEAD_DIM)

    @pl.when(i == 0)
    def _():
        for hh in heads:
            kn_ref[hh] = jnp.zeros((1, LANES), F32)

        def stage(c, carry):
            rows = pl.ds(pl.multiple_of(c * t, t), t)
            lane = lax.broadcasted_iota(jnp.int32, (t, LANES), 1)
            for hh in heads:
                vt_ref[hh, c] = v_ref[rows, head_cols(hh)].astype(F32).T.astype(BF16)
                cs = jnp.sum(jnp.where(lane == grp * hg + hh, ccol_ref[rows, :], 0.0),
                             axis=-1, keepdims=True)
                csb_ref[hh, rows, :] = jnp.broadcast_to(cs * LOG2E, (t, LANES))
                kf = k_ref[rows, head_cols(hh)].astype(F32)
                kn_ref[hh] = jnp.maximum(kn_ref[hh], jnp.max(jnp.sum(kf * kf, axis=-1, keepdims=True)))
            return carry
        lax.fori_loop(0, k_ref.shape[0] // t, stage, None)

    qts = [q_ref[:, head_cols(hh)].astype(F32).T.astype(BF16) for hh in heads]
    reach = [jnp.sqrt(jnp.sum(qts[hh].astype(F32) ** 2, axis=0, keepdims=True) * kn_ref[hh][:, 0:1])
             for hh in heads]
    acc_ref[...] = jnp.zeros(acc_ref.shape, F32)
    lp_ref[...] = jnp.zeros(lp_ref.shape, F32)
    diff = (lax.broadcasted_iota(jnp.int32, (KEY_CHUNK, QUERY_CHUNK), 0)
            - lax.broadcasted_iota(jnp.int32, (KEY_CHUNK, QUERY_CHUNK), 1))

    def scores(j, slot, diagonal):
        mparts = [[jnp.full((8, QUERY_CHUNK), -jnp.inf, F32) for _ in range(t // QUERY_CHUNK)]
                  for _ in heads]
        for c in range(t // KEY_CHUNK):
            k0 = pl.multiple_of(j * t + c * KEY_CHUNK, KEY_CHUNK)
            rows = slice(c * KEY_CHUNK, (c + 1) * KEY_CHUNK)
            for g in range(t // QUERY_CHUNK):
                cols = slice(g * QUERY_CHUNK, (g + 1) * QUERY_CHUNK)
                shift = g * QUERY_CHUNK - c * KEY_CHUNK
                for hh in heads:
                    if diagonal and shift <= -QUERY_CHUNK:
                        s_ref[hh, slot, rows, cols] = jnp.full((KEY_CHUNK, QUERY_CHUNK), -jnp.inf, F32)
                        continue
                    bias = csb_ref[hh, pl.ds(k0, KEY_CHUNK), :]
                    bias = jnp.concatenate([bias] * (QUERY_CHUNK // LANES), axis=1)
                    kc = k_ref[pl.ds(k0, KEY_CHUNK), head_cols(hh)]
                    s = jnp.dot(kc, qts[hh][:, cols], preferred_element_type=F32) - bias
                    if diagonal and shift < KEY_CHUNK - 1:
                        s = jnp.where(diff <= shift, s, -jnp.inf)
                    s_ref[hh, slot, rows, cols] = s
                    mparts[hh][g] = jnp.maximum(mparts[hh][g], _colmax8(s))
        return [jnp.concatenate(mparts[hh], axis=1) for hh in heads]

    def new_max(m_old, mpart):
        m_new = jnp.maximum(m_old, jnp.max(mpart, axis=0, keepdims=True))
        return m_new, jnp.exp2(m_old - m_new)

    def half(step, slot, carry, last=False):
        ms, alphas = carry
        j = i - step
        lsums = [jnp.zeros((8, t), F32) for _ in heads]
        for c in range(t // PROB_CHUNK):
            rows = slice(c * PROB_CHUNK, (c + 1) * PROB_CHUNK)
            for hh in heads:
                p = jnp.exp2(s_ref[hh, slot, rows, :] - ms[hh])
                p_ref[hh, rows, :] = p.astype(BF16)
                lsums[hh] = lsums[hh] + _colsum8(p)
        if not last:
            mparts = scores(j - 1, 1 - slot, False)
            new = [new_max(ms[hh], mparts[hh]) for hh in heads]
            carry = tuple(m for m, _ in new), tuple(a for _, a in new)
        for hh in heads:
            lp_ref[hh] = alphas[hh] * lp_ref[hh] + lsums[hh]
            acc_ref[hh] = alphas[hh] * acc_ref[hh] + jnp.dot(vt_ref[hh, j], p_ref[hh],
                                                            preferred_element_type=F32)
        return carry

    def tile_live(j, ms):
        last_key = jnp.maximum(j + 1, 1) * t - 1
        gaps = [reach[hh] - csb_ref[hh, pl.ds(last_key, 1), :][:, 0:1] - ms[hh] for hh in heads]
        return jnp.max(functools.reduce(jnp.maximum, gaps)) > WEIGHT_FLOOR_LOG2

    first = [new_max(jnp.full((1, t), -jnp.inf, F32), mp) for mp in scores(i, 0, True)]
    carry = tuple(m for m, _ in first), tuple(a for _, a in first)
    n_pairs = (n - 1) // 2

    def pair(c):
        mm, _, ms, alphas = c
        carry = half(2 * mm, 0, (ms, alphas))
        go = (mm + 1 < n_pairs) & tile_live(i - (2 * mm + 3), carry[0])
        ms, alphas = half(2 * mm + 1, 1, carry)
        return mm + 1, go, ms, alphas

    mm, _, ms, alphas = lax.while_loop(
        lambda c: c[1], pair,
        (jnp.int32(0), (n_pairs > 0) & tile_live(i - 1, carry[0])) + carry)
    step = 2 * mm
    single = (step == n - 1) | jnp.logical_not(tile_live(i - (step + 1), ms))

    @pl.when(single)
    def _():
        half(step, 0, (ms, alphas), last=True)

    @pl.when(jnp.logical_not(single))
    def _():
        half(step + 1, 1, half(step, 0, (ms, alphas)), last=True)

    for hh in heads:
        l = jnp.sum(lp_ref[hh], axis=0, keepdims=True)
        o_ref[:, head_cols(hh)] = (acc_ref[hh] / l).T.astype(o_ref.dtype)


def fox_attention(qkv, ccol, *, t=512, hg=1):
    s = qkv.shape[0]
    d = qkv.shape[1] // 3
    ng = d // (hg * HEAD_DIM)
    w = hg * HEAD_DIM
    return pl.pallas_call(
        functools.partial(_fox_kernel, t=t, hg=hg),
        grid=(ng, s // t),
        in_specs=[pl.BlockSpec((t, w), lambda g, i: (i, g)),
                  pl.BlockSpec((s, w), lambda g, i: (0, ng + g)),
                  pl.BlockSpec((s, w), lambda g, i: (0, 2 * ng + g)),
                  pl.BlockSpec((s, LANES), lambda g, i: (0, 0))],
        out_specs=pl.BlockSpec((t, w), lambda g, i: (i, g)),
        out_shape=jax.ShapeDtypeStruct((s, d), BF16),
        scratch_shapes=[pltpu.VMEM((hg, s // t, HEAD_DIM, t), BF16),
                        pltpu.VMEM((hg, s, LANES), F32),
                        pltpu.VMEM((hg, 1, LANES), F32),
                        pltpu.VMEM((hg, 2, t, t), F32),
                        pltpu.VMEM((hg, t, t), BF16),
                        pltpu.VMEM((hg, HEAD_DIM, t), F32), pltpu.VMEM((hg, 8, t), F32)],
        compiler_params=_params("arbitrary", "arbitrary"),
        name="fox_attention")(qkv, qkv, qkv, ccol)


def _sb_kernel(q_ref, k_ref, v_ref, o_ref, vt_ref, lb_ref, lm_ref, w_ref, acc_ref, *, t, hg):
    i = pl.program_id(1)
    n = i + 1
    n_sub = t // MXU_WIDTH
    heads = range(hg)

    def head_cols(hh):
        return slice(hh * HEAD_DIM, (hh + 1) * HEAD_DIM)

    @pl.when(i == 0)
    def _():
        def stage(c, carry):
            rows = pl.ds(pl.multiple_of(c * t, t), t)
            for hh in heads:
                vt_ref[hh, c] = v_ref[rows, head_cols(hh)].astype(F32).T.astype(BF16)
            return carry
        lax.fori_loop(0, k_ref.shape[0] // t, stage, None)

    qts = [q_ref[:, head_cols(hh)].astype(F32).T.astype(BF16) for hh in heads]
    acc_ref[...] = jnp.zeros(acc_ref.shape, F32)
    diff = (lax.broadcasted_iota(jnp.int32, (KEY_CHUNK, QUERY_CHUNK), 0)
            - lax.broadcasted_iota(jnp.int32, (KEY_CHUNK, QUERY_CHUNK), 1))
    later = jnp.where(lax.broadcasted_iota(jnp.int32, (MXU_WIDTH, MXU_WIDTH), 1)
                      > lax.broadcasted_iota(jnp.int32, (MXU_WIDTH, MXU_WIDTH), 0),
                      1.0, 0.0).astype(BF16)

    def scores(j, slot, diagonal):
        for c in range(t // KEY_CHUNK):
            k0 = pl.multiple_of(j * t + c * KEY_CHUNK, KEY_CHUNK)
            rows = slice(c * KEY_CHUNK, (c + 1) * KEY_CHUNK)
            for g in range(t // QUERY_CHUNK):
                cols = slice(g * QUERY_CHUNK, (g + 1) * QUERY_CHUNK)
                shift = g * QUERY_CHUNK - c * KEY_CHUNK
                for hh in heads:
                    if diagonal and shift <= -(QUERY_CHUNK - 1):
                        lb_ref[hh, slot, rows, cols] = jnp.full((KEY_CHUNK, QUERY_CHUNK), -jnp.inf, F32)
                        lm_ref[hh, slot, rows, cols] = jnp.zeros((KEY_CHUNK, QUERY_CHUNK), BF16)
                        continue
                    kc = k_ref[pl.ds(k0, KEY_CHUNK), head_cols(hh)]
                    z2 = jnp.dot(kc, qts[hh][:, cols], preferred_element_type=F32)
                    lb2 = jnp.minimum(z2, 0.0) - jnp.log(1.0 + jnp.exp2(_neg_abs(z2))) * LOG2E
                    l1m2 = lb2 - z2
                    if diagonal and shift < KEY_CHUNK:
                        visible = diff < shift
                        lb2 = jnp.where(visible, lb2, -jnp.inf)
                        l1m2 = jnp.where(visible, l1m2, 0.0)
                    lb_ref[hh, slot, rows, cols] = lb2
                    lm_ref[hh, slot, rows, cols] = l1m2.astype(BF16)

    def half(step, slot, runs, last=False):
        j = i - step
        runs = list(runs)
        for b in reversed(range(n_sub)):
            rows = slice(b * MXU_WIDTH, (b + 1) * MXU_WIDTH)
            totals = [[] for _ in heads]
            for g in range(t // QUERY_CHUNK):
                cols = slice(g * QUERY_CHUNK, (g + 1) * QUERY_CHUNK)
                for hh in heads:
                    lm = lm_ref[hh, slot, rows, cols]
                    within = jnp.dot(later, lm, preferred_element_type=F32)
                    w = jnp.exp2(lb_ref[hh, slot, rows, cols] + within + runs[hh][:, cols])
                    w_ref[hh, rows, cols] = w.astype(BF16)
                    totals[hh].append(within[0:1, :] + lm[0:1, :].astype(F32))
            for hh in heads:
                runs[hh] = runs[hh] + jnp.concatenate(totals[hh], axis=1)
        if not last:
            scores(j - 1, 1 - slot, False)
        for hh in heads:
            acc_ref[hh] += jnp.dot(vt_ref[hh, j], w_ref[hh], preferred_element_type=F32)
        return tuple(runs)

    def live(runs):
        return jnp.max(functools.reduce(jnp.maximum, runs)) > WEIGHT_FLOOR_LOG2

    scores(i, 0, True)
    n_pairs = (n - 1) // 2

    def pair(carry):
        mm, runs = carry
        runs = half(2 * mm, 0, runs)
        return mm + 1, half(2 * mm + 1, 1, runs)

    mm, runs = lax.while_loop(lambda c: (c[0] < n_pairs) & live(c[1]), pair,
                              (jnp.int32(0), tuple(jnp.zeros((1, t), F32) for _ in heads)))
    rest = (mm == n_pairs) & live(runs)

    @pl.when(rest & (n % 2 == 1))
    def _():
        half(n - 1, 0, runs, last=True)

    @pl.when(rest & (n % 2 == 0))
    def _():
        runs2 = half(n - 2, 0, runs)

        @pl.when(live(runs2))
        def _():
            half(n - 1, 1, runs2, last=True)

    for hh in heads:
        o_ref[:, head_cols(hh)] = acc_ref[hh].T.astype(o_ref.dtype)


def sb_attention(q, kv, *, t=256, hg=4):
    s, d = q.shape
    ng = d // (hg * HEAD_DIM)
    w = hg * HEAD_DIM
    return pl.pallas_call(
        functools.partial(_sb_kernel, t=t, hg=hg),
        grid=(ng, s // t),
        in_specs=[pl.BlockSpec((t, w), lambda g, i: (i, g)),
                  pl.BlockSpec((s, w), lambda g, i: (0, g)),
                  pl.BlockSpec((s, w), lambda g, i: (0, ng + g))],
        out_specs=pl.BlockSpec((t, w), lambda g, i: (i, g)),
        out_shape=jax.ShapeDtypeStruct((s, d), BF16),
        scratch_shapes=[pltpu.VMEM((hg, s // t, HEAD_DIM, t), BF16),
                        pltpu.VMEM((hg, 2, t, t), F32),
                        pltpu.VMEM((hg, 2, t, t), BF16),
                        pltpu.VMEM((hg, t, t), BF16),
                        pltpu.VMEM((hg, HEAD_DIM, t), F32)],
        compiler_params=_params("arbitrary", "arbitrary"),
        name="sb_attention")(q, kv, kv)


def _matmul_residual_kernel(a_ref, w_ref, h_ref, o_ref):
    o_ref[...] = h_ref[...] + jnp.dot(a_ref[...], w_ref[...], preferred_element_type=F32)


def matmul_residual(a, w, h, *, tm=512, tn=1024):
    s, k = a.shape
    n = w.shape[1]
    return pl.pallas_call(
        _matmul_residual_kernel, grid=(s // tm, n // tn),
        in_specs=[pl.BlockSpec((tm, k), lambda i, j: (i, 0)),
                  pl.BlockSpec((k, tn), lambda i, j: (0, j)),
                  pl.BlockSpec((tm, tn), lambda i, j: (i, j))],
        out_specs=pl.BlockSpec((tm, tn), lambda i, j: (i, j)),
        out_shape=jax.ShapeDtypeStruct((s, n), F32),
        compiler_params=_params("parallel", "arbitrary"),
        name="matmul_residual")(a, w, h)


def _ffn_kernel(h_ref, g_ref, wg_ref, wu_ref, wd_ref, o_ref, xn_ref):
    @pl.when(pl.program_id(1) == 0)
    def _():
        x = h_ref[...]
        xn_ref[...] = _rms_scale(x, g_ref[...]).astype(BF16)
        o_ref[...] = x

    xn = xn_ref[...]
    gate = jnp.dot(xn, wg_ref[...], preferred_element_type=F32)
    up = jnp.dot(xn, wu_ref[...], preferred_element_type=F32)
    act = (gate * jax.nn.sigmoid(gate) * up).astype(BF16)
    o_ref[...] += jnp.dot(act, wd_ref[...], preferred_element_type=F32)


def ffn(h, g, wg, wu, wd, *, tm=512, tf=512):
    s, d = h.shape
    dff = wg.shape[1]
    return pl.pallas_call(
        _ffn_kernel, grid=(s // tm, dff // tf),
        in_specs=[pl.BlockSpec((tm, d), lambda i, f: (i, 0)),
                  pl.BlockSpec((1, d), lambda i, f: (0, 0)),
                  pl.BlockSpec((d, tf), lambda i, f: (0, f)),
                  pl.BlockSpec((d, tf), lambda i, f: (0, f)),
                  pl.BlockSpec((tf, d), lambda i, f: (f, 0))],
        out_specs=pl.BlockSpec((tm, d), lambda i, f: (i, 0)),
        out_shape=jax.ShapeDtypeStruct((s, d), F32),
        scratch_shapes=[pltpu.VMEM((tm, d), BF16)],
        compiler_params=_params("parallel", "arbitrary"),
        name="ffn")(h, g, wg, wu, wd)


def _rmsnorm_kernel(x_ref, g_ref, o_ref):
    o_ref[...] = _rms_scale(x_ref[...], g_ref[...])


def rmsnorm(x, g, *, tm=512):
    s, d = x.shape
    return pl.pallas_call(
        _rmsnorm_kernel, grid=(s // tm,),
        in_specs=[pl.BlockSpec((tm, d), lambda i: (i, 0)),
                  pl.BlockSpec((1, d), lambda i: (0, 0))],
        out_specs=pl.BlockSpec((tm, d), lambda i: (i, 0)),
        out_shape=jax.ShapeDtypeStruct((s, d), F32),
        compiler_params=_params("parallel"), name="rmsnorm")(x, g)


def kernel(x, attn_norm, ffn_norm, a_w_in, a_b_f, a_w_out, kv_norm, w_kv, b_w_q, b_w_out,
           ffn_w_gate, ffn_w_up, ffn_w_down, final_norm):
    b, s, d = x.shape
    assert b == 1 and d == N_HEADS * HEAD_DIM
    depth = attn_norm.shape[0]
    n_a = a_w_in.shape[0]
    h = x.reshape(s, d)
    kv = None
    for layer in range(depth):
        g_attn = attn_norm[layer].reshape(1, d)
        if layer < n_a:
            w_in = a_w_in[layer]
            w_qkv = w_in[:, :3 * d].astype(BF16)
            w_f = jnp.pad(w_in[:, 3 * d:], ((0, 0), (0, LANES - N_HEADS))).astype(BF16)
            b_f = jnp.pad(a_b_f[layer], (0, LANES - N_HEADS)).reshape(1, LANES)
            qkv, f_logit = norm_matmul(h, g_attn, w_qkv, w_f,
                                       scaled_cols=d, col_scale=HEAD_DIM ** -0.5 * LOG2E)
            o = fox_attention(qkv, gate_cumsum(f_logit, b_f))
            h = matmul_residual(o, a_w_out[layer].astype(BF16), h)
        else:
            j = layer - n_a
            q = norm_matmul(h, g_attn, b_w_q[j].astype(BF16),
                            scaled_cols=d, col_scale=HEAD_DIM ** -0.5 * LOG2E)
            o = sb_attention(q, kv)
            h = matmul_residual(o, b_w_out[j].astype(BF16), h)
        h = ffn(h, ffn_norm[layer].reshape(1, d), ffn_w_gate[layer].astype(BF16),
                ffn_w_up[layer].astype(BF16), ffn_w_down[layer].astype(BF16))
        if layer == n_a - 1:
            kv = norm_matmul(h, kv_norm.reshape(1, d), w_kv.astype(BF16))
    return rmsnorm(h, final_norm.reshape(1, d)).reshape(b, s, d)
```

```python
import functools

import jax
import jax.numpy as jnp
from jax import lax
from jax.experimental import pallas as pl
from jax.experimental.pallas import tpu as pltpu

N_HEADS = 16
HEAD_DIM = 128
EPS = 1e-6
LOG2E = 1.4426950408889634
LANES = 128
MXU_WIDTH = 256
VMEM_LIMIT = 56 * 1024 * 1024

KEY_CHUNK = 128
QUERY_CHUNK = MXU_WIDTH
PROB_CHUNK = 32
WEIGHT_FLOOR_LOG2 = -200.0

F32 = jnp.float32
BF16 = jnp.bfloat16


def _params(*sem):
    return pltpu.CompilerParams(dimension_semantics=sem, vmem_limit_bytes=VMEM_LIMIT)


def _rms_scale(x, g):
    ms = jnp.mean(x * x, axis=-1, keepdims=True)
    return x * lax.rsqrt(ms + EPS) * g


def _scaled_tile(acc, n_scaled_tiles, col_scale):
    if n_scaled_tiles == 0:
        return acc
    return acc * jnp.where(pl.program_id(1) < n_scaled_tiles, col_scale, 1.0)


def _norm_matmul_kernel(x_ref, g_ref, w_ref, o_ref, xn_ref, *, n_scaled_tiles, col_scale):
    @pl.when(pl.program_id(1) == 0)
    def _():
        xn_ref[...] = _rms_scale(x_ref[...], g_ref[...]).astype(BF16)

    acc = jnp.dot(xn_ref[...], w_ref[...], preferred_element_type=F32)
    o_ref[...] = _scaled_tile(acc, n_scaled_tiles, col_scale).astype(o_ref.dtype)


def _norm_matmul_gate_kernel(x_ref, g_ref, w_ref, wf_ref, o_ref, f_ref, xn_ref,
                             *, n_scaled_tiles, col_scale):
    @pl.when(pl.program_id(1) == 0)
    def _():
        xn = _rms_scale(x_ref[...], g_ref[...]).astype(BF16)
        xn_ref[...] = xn
        f_ref[...] = jnp.dot(xn, wf_ref[...], preferred_element_type=F32)

    acc = jnp.dot(xn_ref[...], w_ref[...], preferred_element_type=F32)
    o_ref[...] = _scaled_tile(acc, n_scaled_tiles, col_scale).astype(o_ref.dtype)


def norm_matmul(x, g, w, wf=None, *, scaled_cols=0, col_scale=1.0, tm=512, tn=1024):
    s, d = x.shape
    n = w.shape[1]
    assert scaled_cols % tn == 0
    static = dict(n_scaled_tiles=scaled_cols // tn, col_scale=col_scale)
    grid = (s // tm, n // tn)
    x_spec = pl.BlockSpec((tm, d), lambda i, j: (i, 0))
    g_spec = pl.BlockSpec((1, d), lambda i, j: (0, 0))
    w_spec = pl.BlockSpec((d, tn), lambda i, j: (0, j))
    o_spec = pl.BlockSpec((tm, tn), lambda i, j: (i, j))
    scratch = [pltpu.VMEM((tm, d), BF16)]
    if wf is None:
        return pl.pallas_call(
            functools.partial(_norm_matmul_kernel, **static), grid=grid,
            in_specs=[x_spec, g_spec, w_spec], out_specs=o_spec,
            out_shape=jax.ShapeDtypeStruct((s, n), BF16),
            scratch_shapes=scratch, compiler_params=_params("parallel", "arbitrary"),
            name="norm_matmul")(x, g, w)
    wf_spec = pl.BlockSpec((d, LANES), lambda i, j: (0, 0))
    f_spec = pl.BlockSpec((tm, LANES), lambda i, j: (i, 0))
    return pl.pallas_call(
        functools.partial(_norm_matmul_gate_kernel, **static), grid=grid,
        in_specs=[x_spec, g_spec, w_spec, wf_spec], out_specs=[o_spec, f_spec],
        out_shape=[jax.ShapeDtypeStruct((s, n), BF16),
                   jax.ShapeDtypeStruct((s, LANES), F32)],
        scratch_shapes=scratch, compiler_params=_params("parallel", "arbitrary"),
        name="norm_matmul_gate")(x, g, w, wf)


def _log_sigmoid(z):
    return jnp.minimum(z, 0.0) - jnp.log(1.0 + jnp.exp(-jnp.abs(z)))


def _split3(x):
    hi = x.astype(BF16)
    r = x - hi.astype(F32)
    mid = r.astype(BF16)
    lo = (r - mid.astype(F32)).astype(BF16)
    return hi, mid, lo


def _gate_cumsum_kernel(f_ref, b_ref, ccol_ref, carry_ref, *, tc):
    @pl.when(pl.program_id(0) == 0)
    def _():
        carry_ref[...] = jnp.zeros_like(carry_ref)

    log_f = _log_sigmoid(f_ref[...] + b_ref[...])
    row = lax.broadcasted_iota(jnp.int32, (tc, tc), 0)
    col = lax.broadcasted_iota(jnp.int32, (tc, tc), 1)
    tri = jnp.where(col <= row, 1.0, 0.0).astype(BF16)
    hi, mid, lo = _split3(log_f)
    c = (jnp.dot(tri, hi, preferred_element_type=F32)
         + jnp.dot(tri, mid, preferred_element_type=F32)
         + jnp.dot(tri, lo, preferred_element_type=F32)) + carry_ref[...]
    ccol_ref[...] = c
    carry_ref[...] = c[tc - 1:tc, :]


def gate_cumsum(f, b, *, tc=512):
    s = f.shape[0]
    return pl.pallas_call(
        functools.partial(_gate_cumsum_kernel, tc=tc), grid=(s // tc,),
        in_specs=[pl.BlockSpec((tc, LANES), lambda i: (i, 0)),
                  pl.BlockSpec((1, LANES), lambda i: (0, 0))],
        out_specs=pl.BlockSpec((tc, LANES), lambda i: (i, 0)),
        out_shape=jax.ShapeDtypeStruct((s, LANES), F32),
        scratch_shapes=[pltpu.VMEM((1, LANES), F32)],
        compiler_params=_params("arbitrary"), name="gate_cumsum")(f, b)


def _qk(q, k):
    return lax.dot_general(q, k, (((1,), (1,)), ((), ())), preferred_element_type=F32)


def _colmax8(x):
    return jnp.max(x.reshape(x.shape[0] // 8, 8, x.shape[1]), axis=0)


def _neg_abs(x):
    bits = lax.bitcast_convert_type(x, jnp.int32) | jnp.int32(-2 ** 31)
    return lax.bitcast_convert_type(bits, F32)


def _colsum8(x):
    return jnp.sum(x.reshape(x.shape[0] // 8, 8, x.shape[1]), axis=0)


def _fox_kernel(q_ref, k_ref, v_ref, ccol_ref, o_ref,
                vt_ref, csb_ref, kn_ref, s_ref, p_ref, acc_ref, lp_ref, *, t, hg):
    grp = pl.program_id(0)
    i = pl.program_id(1)
    n = i + 1
    heads = range(hg)

    def head_cols(hh):
        return slice(hh * HEAD_DIM, (hh + 1) * HEAD_DIM)

    @pl.when(i == 0)
    def _():
        for hh in heads:
            kn_ref[hh] = jnp.zeros((1, LANES), F32)

        def stage(c, carry):
            rows = pl.ds(pl.multiple_of(c * t, t), t)
            lane = lax.broadcasted_iota(jnp.int32, (t, LANES), 1)
            for hh in heads:
                vt_ref[hh, c] = v_ref[rows, head_cols(hh)].astype(F32).T.astype(BF16)
                cs = jnp.sum(jnp.where(lane == grp * hg + hh, ccol_ref[rows, :], 0.0),
                             axis=-1, keepdims=True)
                csb_ref[hh, rows, :] = jnp.broadcast_to(cs * LOG2E, (t, LANES))
                kf = k_ref[rows, head_cols(hh)].astype(F32)
                kn_ref[hh] = jnp.maximum(kn_ref[hh], jnp.max(jnp.sum(kf * kf, axis=-1, keepdims=True)))
            return carry
        lax.fori_loop(0, k_ref.shape[0] // t, stage, None)

    qts = [q_ref[:, head_cols(hh)].astype(F32).T.astype(BF16) for hh in heads]
    reach = [jnp.sqrt(jnp.sum(qts[hh].astype(F32) ** 2, axis=0, keepdims=True) * kn_ref[hh][:, 0:1])
             for hh in heads]
    acc_ref[...] = jnp.zeros(acc_ref.shape, F32)
    lp_ref[...] = jnp.zeros(lp_ref.shape, F32)
    diff = (lax.broadcasted_iota(jnp.int32, (KEY_CHUNK, QUERY_CHUNK), 0)
            - lax.broadcasted_iota(jnp.int32, (KEY_CHUNK, QUERY_CHUNK), 1))

    def scores(j, slot, diagonal):
        mparts = [[jnp.full((8, QUERY_CHUNK), -jnp.inf, F32) for _ in range(t // QUERY_CHUNK)]
                  for _ in heads]
        for c in range(t // KEY_CHUNK):
            k0 = pl.multiple_of(j * t + c * KEY_CHUNK, KEY_CHUNK)
            rows = slice(c * KEY_CHUNK, (c + 1) * KEY_CHUNK)
            for g in range(t // QUERY_CHUNK):
                cols = slice(g * QUERY_CHUNK, (g + 1) * QUERY_CHUNK)
                shift = g * QUERY_CHUNK - c * KEY_CHUNK
                for hh in heads:
                    if diagonal and shift <= -QUERY_CHUNK:
                        s_ref[hh, slot, rows, cols] = jnp.full((KEY_CHUNK, QUERY_CHUNK), -jnp.inf, F32)
                        continue
                    bias = csb_ref[hh, pl.ds(k0, KEY_CHUNK), :]
                    bias = jnp.concatenate([bias] * (QUERY_CHUNK // LANES), axis=1)
                    kc = k_ref[pl.ds(k0, KEY_CHUNK), head_cols(hh)]
                    s = jnp.dot(kc, qts[hh][:, cols], preferred_element_type=F32) - bias
                    if diagonal and shift < KEY_CHUNK - 1:
                        s = jnp.where(diff <= shift, s, -jnp.inf)
                    s_ref[hh, slot, rows, cols] = s
                    mparts[hh][g] = jnp.maximum(mparts[hh][g], _colmax8(s))
        return [jnp.concatenate(mparts[hh], axis=1) for hh in heads]

    def new_max(m_old, mpart):
        m_new = jnp.maximum(m_old, jnp.max(mpart, axis=0, keepdims=True))
        return m_new, jnp.exp2(m_old - m_new)

    def half(step, slot, carry, last=False):
        ms, alphas = carry
        j = i - step
        lsums = [jnp.zeros((8, t), F32) for _ in heads]
        for c in range(t // PROB_CHUNK):
            rows = slice(c * PROB_CHUNK, (c + 1) * PROB_CHUNK)
            for hh in heads:
                p = jnp.exp2(s_ref[hh, slot, rows, :] - ms[hh])
                p_ref[hh, rows, :] = p.astype(BF16)
                lsums[hh] = lsums[hh] + _colsum8(p)
        if not last:
            mparts = scores(j - 1, 1 - slot, False)
            new = [new_max(ms[hh], mparts[hh]) for hh in heads]
            carry = tuple(m for m, _ in new), tuple(a for _, a in new)
        for hh in heads:
            lp_ref[hh] = alphas[hh] * lp_ref[hh] + lsums[hh]
            acc_ref[hh] = alphas[hh] * acc_ref[hh] + jnp.dot(vt_ref[hh, j], p_ref[hh],
                                                            preferred_element_type=F32)
        return carry

    def tile_live(j, ms):
        last_key = jnp.maximum(j + 1, 1) * t - 1
        gaps = [reach[hh] - csb_ref[hh, pl.ds(last_key, 1), :][:, 0:1] - ms[hh] for hh in heads]
        return jnp.max(functools.reduce(jnp.maximum, gaps)) > WEIGHT_FLOOR_LOG2

    first = [new_max(jnp.full((1, t), -jnp.inf, F32), mp) for mp in scores(i, 0, True)]
    carry = tuple(m for m, _ in first), tuple(a for _, a in first)
    n_pairs = (n - 1) // 2

    def pair(c):
        mm, _, ms, alphas = c
        carry = half(2 * mm, 0, (ms, alphas))
        go = (mm + 1 < n_pairs) & tile_live(i - (2 * mm + 3), carry[0])
        ms, alphas = half(2 * mm + 1, 1, carry)
        return mm + 1, go, ms, alphas

    mm, _, ms, alphas = lax.while_loop(
        lambda c: c[1], pair,
        (jnp.int32(0), (n_pairs > 0) & tile_live(i - 1, carry[0])) + carry)
    step = 2 * mm
    single = (step == n - 1) | jnp.logical_not(tile_live(i - (step + 1), ms))

    @pl.when(single)
    def _():
        half(step, 0, (ms, alphas), last=True)

    @pl.when(jnp.logical_not(single))
    def _():
        half(step + 1, 1, half(step, 0, (ms, alphas)), last=True)

    for hh in heads:
        l = jnp.sum(lp_ref[hh], axis=0, keepdims=True)
        o_ref[:, head_cols(hh)] = (acc_ref[hh] / l).T.astype(o_ref.dtype)


def fox_attention(qkv, ccol, *, t=512, hg=1):
    s = qkv.shape[0]
    d = qkv.shape[1] // 3
    ng = d // (hg * HEAD_DIM)
    w = hg * HEAD_DIM
    return pl.pallas_call(
        functools.partial(_fox_kernel, t=t, hg=hg),
        grid=(ng, s // t),
        in_specs=[pl.BlockSpec((t, w), lambda g, i: (i, g)),
                  pl.BlockSpec((s, w), lambda g, i: (0, ng + g)),
                  pl.BlockSpec((s, w), lambda g, i: (0, 2 * ng + g)),
                  pl.BlockSpec((s, LANES), lambda g, i: (0, 0))],
        out_specs=pl.BlockSpec((t, w), lambda g, i: (i, g)),
        out_shape=jax.ShapeDtypeStruct((s, d), BF16),
        scratch_shapes=[pltpu.VMEM((hg, s // t, HEAD_DIM, t), BF16),
                        pltpu.VMEM((hg, s, LANES), F32),
                        pltpu.VMEM((hg, 1, LANES), F32),
                        pltpu.VMEM((hg, 2, t, t), F32),
                        pltpu.VMEM((hg, t, t), BF16),
                        pltpu.VMEM((hg, HEAD_DIM, t), F32), pltpu.VMEM((hg, 8, t), F32)],
        compiler_params=_params("arbitrary", "arbitrary"),
        name="fox_attention")(qkv, qkv, qkv, ccol)


def _sb_kernel(q_ref, k_ref, v_ref, o_ref, vt_ref, lb_ref, lm_ref, w_ref, acc_ref, *, t, hg):
    i = pl.program_id(1)
    n = i + 1
    n_sub = t // MXU_WIDTH
    heads = range(hg)

    def head_cols(hh):
        return slice(hh * HEAD_DIM, (hh + 1) * HEAD_DIM)

    @pl.when(i == 0)
    def _():
        def stage(c, carry):
            rows = pl.ds(pl.multiple_of(c * t, t), t)
            for hh in heads:
                vt_ref[hh, c] = v_ref[rows, head_cols(hh)].astype(F32).T.astype(BF16)
            return carry
        lax.fori_loop(0, k_ref.shape[0] // t, stage, None)

    qts = [q_ref[:, head_cols(hh)].astype(F32).T.astype(BF16) for hh in heads]
    acc_ref[...] = jnp.zeros(acc_ref.shape, F32)
    diff = (lax.broadcasted_iota(jnp.int32, (KEY_CHUNK, QUERY_CHUNK), 0)
            - lax.broadcasted_iota(jnp.int32, (KEY_CHUNK, QUERY_CHUNK), 1))
    later = jnp.where(lax.broadcasted_iota(jnp.int32, (MXU_WIDTH, MXU_WIDTH), 1)
                      > lax.broadcasted_iota(jnp.int32, (MXU_WIDTH, MXU_WIDTH), 0),
                      1.0, 0.0).astype(BF16)

    def scores(j, slot, diagonal):
        for c in range(t // KEY_CHUNK):
            k0 = pl.multiple_of(j * t + c * KEY_CHUNK, KEY_CHUNK)
            rows = slice(c * KEY_CHUNK, (c + 1) * KEY_CHUNK)
            for g in range(t // QUERY_CHUNK):
                cols = slice(g * QUERY_CHUNK, (g + 1) * QUERY_CHUNK)
                shift = g * QUERY_CHUNK - c * KEY_CHUNK
                for hh in heads:
                    if diagonal and shift <= -(QUERY_CHUNK - 1):
                        lb_ref[hh, slot, rows, cols] = jnp.full((KEY_CHUNK, QUERY_CHUNK), -jnp.inf, F32)
                        lm_ref[hh, slot, rows, cols] = jnp.zeros((KEY_CHUNK, QUERY_CHUNK), BF16)
                        continue
                    kc = k_ref[pl.ds(k0, KEY_CHUNK), head_cols(hh)]
                    z2 = jnp.dot(kc, qts[hh][:, cols], preferred_element_type=F32)
                    lb2 = jnp.minimum(z2, 0.0) - jnp.log(1.0 + jnp.exp2(_neg_abs(z2))) * LOG2E
                    l1m2 = lb2 - z2
                    if diagonal and shift < KEY_CHUNK:
                        visible = diff < shift
                        lb2 = jnp.where(visible, lb2, -jnp.inf)
                        l1m2 = jnp.where(visible, l1m2, 0.0)
                    lb_ref[hh, slot, rows, cols] = lb2
                    lm_ref[hh, slot, rows, cols] = l1m2.astype(BF16)

    def half(step, slot, runs, last=False):
        j = i - step
        runs = list(runs)
        for b in reversed(range(n_sub)):
            rows = slice(b * MXU_WIDTH, (b + 1) * MXU_WIDTH)
            totals = [[] for _ in heads]
            for g in range(t // QUERY_CHUNK):
                cols = slice(g * QUERY_CHUNK, (g + 1) * QUERY_CHUNK)
                for hh in heads:
                    lm = lm_ref[hh, slot, rows, cols]
                    within = jnp.dot(later, lm, preferred_element_type=F32)
                    w = jnp.exp2(lb_ref[hh, slot, rows, cols] + within + runs[hh][:, cols])
                    w_ref[hh, rows, cols] = w.astype(BF16)
                    totals[hh].append(within[0:1, :] + lm[0:1, :].astype(F32))
            for hh in heads:
                runs[hh] = runs[hh] + jnp.concatenate(totals[hh], axis=1)
        if not last:
            scores(j - 1, 1 - slot, False)
        for hh in heads:
            acc_ref[hh] += jnp.dot(vt_ref[hh, j], w_ref[hh], preferred_element_type=F32)
        return tuple(runs)

    def live(runs):
        return jnp.max(functools.reduce(jnp.maximum, runs)) > WEIGHT_FLOOR_LOG2

    scores(i, 0, True)
    n_pairs = (n - 1) // 2

    def pair(carry):
        mm, runs = carry
        runs = half(2 * mm, 0, runs)
        return mm + 1, half(2 * mm + 1, 1, runs)

    mm, runs = lax.while_loop(lambda c: (c[0] < n_pairs) & live(c[1]), pair,
                              (jnp.int32(0), tuple(jnp.zeros((1, t), F32) for _ in heads)))
    rest = (mm == n_pairs) & live(runs)

    @pl.when(rest & (n % 2 == 1))
    def _():
        half(n - 1, 0, runs, last=True)

    @pl.when(rest & (n % 2 == 0))
    def _():
        runs2 = half(n - 2, 0, runs)

        @pl.when(live(runs2))
        def _():
            half(n - 1, 1, runs2, last=True)

    for hh in heads:
        o_ref[:, head_cols(hh)] = acc_ref[hh].T.astype(o_ref.dtype)


def sb_attention(q, kv, *, t=256, hg=4):
    s, d = q.shape
    ng = d // (hg * HEAD_DIM)
    w = hg * HEAD_DIM
    return pl.pallas_call(
        functools.partial(_sb_kernel, t=t, hg=hg),
        grid=(ng, s // t),
        in_specs=[pl.BlockSpec((t, w), lambda g, i: (i, g)),
                  pl.BlockSpec((s, w), lambda g, i: (0, g)),
                  pl.BlockSpec((s, w), lambda g, i: (0, ng + g))],
        out_specs=pl.BlockSpec((t, w), lambda g, i: (i, g)),
        out_shape=jax.ShapeDtypeStruct((s, d), BF16),
        scratch_shapes=[pltpu.VMEM((hg, s // t, HEAD_DIM, t), BF16),
                        pltpu.VMEM((hg, 2, t, t), F32),
                        pltpu.VMEM((hg, 2, t, t), BF16),
                        pltpu.VMEM((hg, t, t), BF16),
                        pltpu.VMEM((hg, HEAD_DIM, t), F32)],
        compiler_params=_params("arbitrary", "arbitrary"),
        name="sb_attention")(q, kv, kv)


def _matmul_residual_kernel(a_ref, w_ref, h_ref, o_ref):
    o_ref[...] = h_ref[...] + jnp.dot(a_ref[...], w_ref[...], preferred_element_type=F32)


def matmul_residual(a, w, h, *, tm=512):
    s, k = a.shape
    n = tn = w.shape[1]
    return pl.pallas_call(
        _matmul_residual_kernel, grid=(s // tm, n // tn),
        in_specs=[pl.BlockSpec((tm, k), lambda i, j: (i, 0)),
                  pl.BlockSpec((k, tn), lambda i, j: (0, j)),
                  pl.BlockSpec((tm, tn), lambda i, j: (i, j))],
        out_specs=pl.BlockSpec((tm, tn), lambda i, j: (i, j)),
        out_shape=jax.ShapeDtypeStruct((s, n), F32),
        compiler_params=_params("parallel", "arbitrary"),
        name="matmul_residual")(a, w, h)


def _ffn_kernel(h_ref, g_ref, wg_ref, wu_ref, wd_ref, o_ref, xn_ref):
    @pl.when(pl.program_id(1) == 0)
    def _():
        x = h_ref[...]
        xn_ref[...] = _rms_scale(x, g_ref[...]).astype(BF16)
        o_ref[...] = x

    xn = xn_ref[...]
    gate = jnp.dot(xn, wg_ref[...], preferred_element_type=F32)
    up = jnp.dot(xn, wu_ref[...], preferred_element_type=F32)
    act = (gate * jax.nn.sigmoid(gate) * up).astype(BF16)
    o_ref[...] += jnp.dot(act, wd_ref[...], preferred_element_type=F32)


def ffn(h, g, wg, wu, wd, *, tm=512, tf=512):
    s, d = h.shape
    dff = wg.shape[1]
    return pl.pallas_call(
        _ffn_kernel, grid=(s // tm, dff // tf),
        in_specs=[pl.BlockSpec((tm, d), lambda i, f: (i, 0)),
                  pl.BlockSpec((1, d), lambda i, f: (0, 0)),
                  pl.BlockSpec((d, tf), lambda i, f: (0, f)),
                  pl.BlockSpec((d, tf), lambda i, f: (0, f)),
                  pl.BlockSpec((tf, d), lambda i, f: (f, 0))],
        out_specs=pl.BlockSpec((tm, d), lambda i, f: (i, 0)),
        out_shape=jax.ShapeDtypeStruct((s, d), F32),
        scratch_shapes=[pltpu.VMEM((tm, d), BF16)],
        compiler_params=_params("parallel", "arbitrary"),
        name="ffn")(h, g, wg, wu, wd)


def _rmsnorm_kernel(x_ref, g_ref, o_ref):
    o_ref[...] = _rms_scale(x_ref[...], g_ref[...])


def rmsnorm(x, g, *, tm=512):
    s, d = x.shape
    return pl.pallas_call(
        _rmsnorm_kernel, grid=(s // tm,),
        in_specs=[pl.BlockSpec((tm, d), lambda i: (i, 0)),
                  pl.BlockSpec((1, d), lambda i: (0, 0))],
        out_specs=pl.BlockSpec((tm, d), lambda i: (i, 0)),
        out_shape=jax.ShapeDtypeStruct((s, d), F32),
        compiler_params=_params("parallel"), name="rmsnorm")(x, g)


def kernel(x, attn_norm, ffn_norm, a_w_in, a_b_f, a_w_out, kv_norm, w_kv, b_w_q, b_w_out,
           ffn_w_gate, ffn_w_up, ffn_w_down, final_norm):
    b, s, d = x.shape
    assert b == 1 and d == N_HEADS * HEAD_DIM
    depth = attn_norm.shape[0]
    n_a = a_w_in.shape[0]
    h = x.reshape(s, d)
    kv = None
    for layer in range(depth):
        g_attn = attn_norm[layer].reshape(1, d)
        if layer < n_a:
            w_in = a_w_in[layer]
            w_qkv = w_in[:, :3 * d].astype(BF16)
            w_f = jnp.pad(w_in[:, 3 * d:], ((0, 0), (0, LANES - N_HEADS))).astype(BF16)
            b_f = jnp.pad(a_b_f[layer], (0, LANES - N_HEADS)).reshape(1, LANES)
            qkv, f_logit = norm_matmul(h, g_attn, w_qkv, w_f, tm=1024,
                                       scaled_cols=d, col_scale=HEAD_DIM ** -0.5 * LOG2E)
            o = fox_attention(qkv, gate_cumsum(f_logit, b_f))
            h = matmul_residual(o, a_w_out[layer].astype(BF16), h)
        else:
            j = layer - n_a
            q = norm_matmul(h, g_attn, b_w_q[j].astype(BF16), tn=d,
                            scaled_cols=d, col_scale=HEAD_DIM ** -0.5 * LOG2E)
            o = sb_attention(q, kv)
            h = matmul_residual(o, b_w_out[j].astype(BF16), h)
        h = ffn(h, ffn_norm[layer].reshape(1, d), ffn_w_gate[layer].astype(BF16),
                ffn_w_up[layer].astype(BF16), ffn_w_down[layer].astype(BF16))
        if layer == n_a - 1:
            kv = norm_matmul(h, kv_norm.reshape(1, d), w_kv.astype(BF16), tm=1024)
    return rmsnorm(h, final_norm.reshape(1, d)).reshape(b, s, d)
```

```python
import functools

import jax
import jax.numpy as jnp
from jax import lax
from jax.experimental import pallas as pl
from jax.experimental.pallas import tpu as pltpu

N_HEADS = 16
HEAD_DIM = 128
EPS = 1e-6
LOG2E = 1.4426950408889634
LANES = 128
MXU_WIDTH = 256
VMEM_LIMIT = 56 * 1024 * 1024

KEY_CHUNK = 128
QUERY_CHUNK = MXU_WIDTH
PROB_CHUNK = 32
WEIGHT_FLOOR_LOG2 = -154.0

F32 = jnp.float32
BF16 = jnp.bfloat16


def _params(*sem):
    return pltpu.CompilerParams(dimension_semantics=sem, vmem_limit_bytes=VMEM_LIMIT)


def _rms_scale(x, g):
    ms = jnp.mean(x * x, axis=-1, keepdims=True)
    return x * lax.rsqrt(ms + EPS) * g


def _cast_kernel(x_ref, o_ref):
    o_ref[...] = x_ref[...].astype(o_ref.dtype)


def cast_bf16(w, cols=None, *, tr=256):
    *lead, r, c = w.shape
    cols = c if cols is None else cols
    rows = r
    for n in lead:
        rows *= n
    out = pl.pallas_call(
        _cast_kernel, grid=(rows // tr,),
        in_specs=[pl.BlockSpec((tr, cols), lambda i: (i, 0))],
        out_specs=pl.BlockSpec((tr, cols), lambda i: (i, 0)),
        out_shape=jax.ShapeDtypeStruct((rows, cols), BF16),
        compiler_params=_params("parallel"), name="cast_bf16")(w.reshape(rows, c))
    return out.reshape(*lead, r, cols)


def _scaled_tile(acc, n_scaled_tiles, col_scale):
    if n_scaled_tiles == 0:
        return acc
    return acc * jnp.where(pl.program_id(1) < n_scaled_tiles, col_scale, 1.0)


def _norm_matmul_kernel(x_ref, g_ref, w_ref, o_ref, xn_ref, *, n_scaled_tiles, col_scale):
    @pl.when(pl.program_id(1) == 0)
    def _():
        xn_ref[...] = _rms_scale(x_ref[...], g_ref[...]).astype(BF16)

    acc = jnp.dot(xn_ref[...], w_ref[...], preferred_element_type=F32)
    o_ref[...] = _scaled_tile(acc, n_scaled_tiles, col_scale).astype(o_ref.dtype)


def _norm_matmul_gate_kernel(x_ref, g_ref, w_ref, wf_ref, o_ref, f_ref, xn_ref,
                             *, n_scaled_tiles, col_scale):
    @pl.when(pl.program_id(1) == 0)
    def _():
        xn = _rms_scale(x_ref[...], g_ref[...]).astype(BF16)
        xn_ref[...] = xn
        f_ref[...] = jnp.dot(xn, wf_ref[...], preferred_element_type=F32)

    acc = jnp.dot(xn_ref[...], w_ref[...], preferred_element_type=F32)
    o_ref[...] = _scaled_tile(acc, n_scaled_tiles, col_scale).astype(o_ref.dtype)


def norm_matmul(x, g, w, layer, wf=None, *, scaled_cols=0, col_scale=1.0, tm=512, tn=1024):
    s, d = x.shape
    n = w.shape[2]
    assert scaled_cols % tn == 0
    static = dict(n_scaled_tiles=scaled_cols // tn, col_scale=col_scale)
    grid = (s // tm, n // tn)
    x_spec = pl.BlockSpec((tm, d), lambda i, j: (i, 0))
    g_spec = pl.BlockSpec((1, d), lambda i, j: (0, 0))
    w_spec = pl.BlockSpec((None, d, tn), lambda i, j: (layer, 0, j))
    o_spec = pl.BlockSpec((tm, tn), lambda i, j: (i, j))
    scratch = [pltpu.VMEM((tm, d), BF16)]
    if wf is None:
        return pl.pallas_call(
            functools.partial(_norm_matmul_kernel, **static), grid=grid,
            in_specs=[x_spec, g_spec, w_spec], out_specs=o_spec,
            out_shape=jax.ShapeDtypeStruct((s, n), BF16),
            scratch_shapes=scratch, compiler_params=_params("parallel", "arbitrary"),
            name="norm_matmul")(x, g, w)
    wf_spec = pl.BlockSpec((d, LANES), lambda i, j: (0, 0))
    f_spec = pl.BlockSpec((tm, LANES), lambda i, j: (i, 0))
    return pl.pallas_call(
        functools.partial(_norm_matmul_gate_kernel, **static), grid=grid,
        in_specs=[x_spec, g_spec, w_spec, wf_spec], out_specs=[o_spec, f_spec],
        out_shape=[jax.ShapeDtypeStruct((s, n), BF16),
                   jax.ShapeDtypeStruct((s, LANES), F32)],
        scratch_shapes=scratch, compiler_params=_params("parallel", "arbitrary"),
        name="norm_matmul_gate")(x, g, w, wf)


def _log_sigmoid(z):
    return jnp.minimum(z, 0.0) - jnp.log(1.0 + jnp.exp(-jnp.abs(z)))


def _split3(x):
    hi = x.astype(BF16)
    r = x - hi.astype(F32)
    mid = r.astype(BF16)
    lo = (r - mid.astype(F32)).astype(BF16)
    return hi, mid, lo


def _gate_cumsum_kernel(f_ref, b_ref, ccol_ref, carry_ref, *, tc):
    @pl.when(pl.program_id(0) == 0)
    def _():
        carry_ref[...] = jnp.zeros_like(carry_ref)

    log_f = _log_sigmoid(f_ref[...] + b_ref[...])
    row = lax.broadcasted_iota(jnp.int32, (tc, tc), 0)
    col = lax.broadcasted_iota(jnp.int32, (tc, tc), 1)
    tri = jnp.where(col <= row, 1.0, 0.0).astype(BF16)
    hi, mid, lo = _split3(log_f)
    c = (jnp.dot(tri, hi, preferred_element_type=F32)
         + jnp.dot(tri, mid, preferred_element_type=F32)
         + jnp.dot(tri, lo, preferred_element_type=F32)) + carry_ref[...]
    ccol_ref[...] = c
    carry_ref[...] = c[tc - 1:tc, :]


def gate_cumsum(f, b, *, tc=512):
    s = f.shape[0]
    return pl.pallas_call(
        functools.partial(_gate_cumsum_kernel, tc=tc), grid=(s // tc,),
        in_specs=[pl.BlockSpec((tc, LANES), lambda i: (i, 0)),
                  pl.BlockSpec((1, LANES), lambda i: (0, 0))],
        out_specs=pl.BlockSpec((tc, LANES), lambda i: (i, 0)),
        out_shape=jax.ShapeDtypeStruct((s, LANES), F32),
        scratch_shapes=[pltpu.VMEM((1, LANES), F32)],
        compiler_params=_params("arbitrary"), name="gate_cumsum")(f, b)


def _qk(q, k):
    return lax.dot_general(q, k, (((1,), (1,)), ((), ())), preferred_element_type=F32)


def _colmax8(x):
    return jnp.max(x.reshape(x.shape[0] // 8, 8, x.shape[1]), axis=0)


def _neg_abs(x):
    bits = lax.bitcast_convert_type(x, jnp.int32) | jnp.int32(-2 ** 31)
    return lax.bitcast_convert_type(bits, F32)


def _colsum8(x):
    return jnp.sum(x.reshape(x.shape[0] // 8, 8, x.shape[1]), axis=0)


def _fox_kernel(q_ref, k_ref, v_ref, ccol_ref, o_ref,
                vt_ref, csb_ref, kn_ref, s_ref, p_ref, acc_ref, lp_ref, *, t, hg):
    grp = pl.program_id(0)
    i = pl.program_id(1)
    n = i + 1
    heads = range(hg)

    def head_cols(hh):
        return slice(hh * HEAD_DIM, (hh + 1) * HEAD_DIM)

    @pl.when(i == 0)
    def _():
        for hh in heads:
            kn_ref[hh] = jnp.zeros((1, LANES), F32)

        def stage(c, carry):
            rows = pl.ds(pl.multiple_of(c * t, t), t)
            lane = lax.broadcasted_iota(jnp.int32, (t, LANES), 1)
            for hh in heads:
                vt_ref[hh, c] = v_ref[rows, head_cols(hh)].astype(F32).T.astype(BF16)
                cs = jnp.sum(jnp.where(lane == grp * hg + hh, ccol_ref[rows, :], 0.0),
                             axis=-1, keepdims=True)
                csb_ref[hh, rows, :] = jnp.broadcast_to(cs * LOG2E, (t, LANES))
                kf = k_ref[rows, head_cols(hh)].astype(F32)
                kn_ref[hh] = jnp.maximum(kn_ref[hh], jnp.max(jnp.sum(kf * kf, axis=-1, keepdims=True)))
            return carry
        lax.fori_loop(0, k_ref.shape[0] // t, stage, None)

    qts = [q_ref[:, head_cols(hh)].astype(F32).T.astype(BF16) for hh in heads]
    reach = [jnp.sqrt(jnp.sum(qts[hh].astype(F32) ** 2, axis=0, keepdims=True) * kn_ref[hh][:, 0:1])
             for hh in heads]
    acc_ref[...] = jnp.zeros(acc_ref.shape, F32)
    lp_ref[...] = jnp.zeros(lp_ref.shape, F32)
    diff = (lax.broadcasted_iota(jnp.int32, (KEY_CHUNK, QUERY_CHUNK), 0)
            - lax.broadcasted_iota(jnp.int32, (KEY_CHUNK, QUERY_CHUNK), 1))

    def scores(j, slot, diagonal):
        mparts = [[jnp.full((8, QUERY_CHUNK), -jnp.inf, F32) for _ in range(t // QUERY_CHUNK)]
                  for _ in heads]
        for c in range(t // KEY_CHUNK):
            k0 = pl.multiple_of(j * t + c * KEY_CHUNK, KEY_CHUNK)
            rows = slice(c * KEY_CHUNK, (c + 1) * KEY_CHUNK)
            for g in range(t // QUERY_CHUNK):
                cols = slice(g * QUERY_CHUNK, (g + 1) * QUERY_CHUNK)
                shift = g * QUERY_CHUNK - c * KEY_CHUNK
                for hh in heads:
                    if diagonal and shift <= -QUERY_CHUNK:
                        s_ref[hh, slot, rows, cols] = jnp.full((KEY_CHUNK, QUERY_CHUNK), -jnp.inf, F32)
                        continue
                    bias = csb_ref[hh, pl.ds(k0, KEY_CHUNK), :]
                    bias = jnp.concatenate([bias] * (QUERY_CHUNK // LANES), axis=1)
                    kc = k_ref[pl.ds(k0, KEY_CHUNK), head_cols(hh)]
                    s = jnp.dot(kc, qts[hh][:, cols], preferred_element_type=F32) - bias
                    if diagonal and shift < KEY_CHUNK - 1:
                        s = jnp.where(diff <= shift, s, -jnp.inf)
                    s_ref[hh, slot, rows, cols] = s
                    mparts[hh][g] = jnp.maximum(mparts[hh][g], _colmax8(s))
        return [jnp.concatenate(mparts[hh], axis=1) for hh in heads]

    def new_max(m_old, mpart):
        m_new = jnp.maximum(m_old, jnp.max(mpart, axis=0, keepdims=True))
        return m_new, jnp.exp2(m_old - m_new)

    def half(step, slot, carry, last=False):
        ms, alphas = carry
        j = i - step
        lsums = [jnp.zeros((8, t), F32) for _ in heads]
        for c in range(t // PROB_CHUNK):
            rows = slice(c * PROB_CHUNK, (c + 1) * PROB_CHUNK)
            for hh in heads:
                p = jnp.exp2(s_ref[hh, slot, rows, :] - ms[hh])
                p_ref[hh, rows, :] = p.astype(BF16)
                lsums[hh] = lsums[hh] + _colsum8(p)
        if not last:
            mparts = scores(j - 1, 1 - slot, False)
            new = [new_max(ms[hh], mparts[hh]) for hh in heads]
            carry = tuple(m for m, _ in new), tuple(a for _, a in new)
        for hh in heads:
            lp_ref[hh] = alphas[hh] * lp_ref[hh] + lsums[hh]
            acc_ref[hh] = alphas[hh] * acc_ref[hh] + jnp.dot(vt_ref[hh, j], p_ref[hh],
                                                            preferred_element_type=F32)
        return carry

    def tile_live(j, ms):
        last_key = jnp.maximum(j + 1, 1) * t - 1
        gaps = [reach[hh] - csb_ref[hh, pl.ds(last_key, 1), :][:, 0:1] - ms[hh] for hh in heads]
        return jnp.max(functools.reduce(jnp.maximum, gaps)) > WEIGHT_FLOOR_LOG2

    first = [new_max(jnp.full((1, t), -jnp.inf, F32), mp) for mp in scores(i, 0, True)]
    carry = tuple(m for m, _ in first), tuple(a for _, a in first)
    n_pairs = (n - 1) // 2

    def pair(c):
        mm, _, ms, alphas = c
        carry = half(2 * mm, 0, (ms, alphas))
        go = (mm + 1 < n_pairs) & tile_live(i - (2 * mm + 3), carry[0])
        return (mm + 1, go) + half(2 * mm + 1, 1, carry)

    mm, _, ms, alphas = lax.while_loop(
        lambda c: c[1], pair,
        (jnp.int32(0), (n_pairs > 0) & tile_live(i - 1, carry[0])) + carry)
    step = 2 * mm
    single = (step == n - 1) | jnp.logical_not(tile_live(i - (step + 1), ms))

    @pl.when(single)
    def _():
        half(step, 0, (ms, alphas), last=True)

    @pl.when(jnp.logical_not(single))
    def _():
        half(step + 1, 1, half(step, 0, (ms, alphas)), last=True)

    for hh in heads:
        l = jnp.sum(lp_ref[hh], axis=0, keepdims=True)
        o_ref[:, head_cols(hh)] = (acc_ref[hh] / l).T.astype(o_ref.dtype)


def fox_attention(qkv, ccol, *, t=512, hg=1):
    s = qkv.shape[0]
    d = qkv.shape[1] // 3
    ng = d // (hg * HEAD_DIM)
    w = hg * HEAD_DIM
    return pl.pallas_call(
        functools.partial(_fox_kernel, t=t, hg=hg),
        grid=(ng, s // t),
        in_specs=[pl.BlockSpec((t, w), lambda g, i: (i, g)),
                  pl.BlockSpec((s, w), lambda g, i: (0, ng + g)),
                  pl.BlockSpec((s, w), lambda g, i: (0, 2 * ng + g)),
                  pl.BlockSpec((s, LANES), lambda g, i: (0, 0))],
        out_specs=pl.BlockSpec((t, w), lambda g, i: (i, g)),
        out_shape=jax.ShapeDtypeStruct((s, d), BF16),
        scratch_shapes=[pltpu.VMEM((hg, s // t, HEAD_DIM, t), BF16),
                        pltpu.VMEM((hg, s, LANES), F32),
                        pltpu.VMEM((hg, 1, LANES), F32),
                        pltpu.VMEM((hg, 2, t, t), F32),
                        pltpu.VMEM((hg, t, t), BF16),
                        pltpu.VMEM((hg, HEAD_DIM, t), F32), pltpu.VMEM((hg, 8, t), F32)],
        compiler_params=_params("arbitrary", "arbitrary"),
        name="fox_attention")(qkv, qkv, qkv, ccol)


def _sb_kernel(q_ref, k_ref, v_ref, o_ref, vt_ref, lb_ref, lm_ref, w_ref, acc_ref, *, t, hg):
    i = pl.program_id(1)
    n_sub = t // MXU_WIDTH
    heads = range(hg)

    def head_cols(hh):
        return slice(hh * HEAD_DIM, (hh + 1) * HEAD_DIM)

    @pl.when(i == 0)
    def _():
        def stage(c, carry):
            rows = pl.ds(pl.multiple_of(c * t, t), t)
            for hh in heads:
                vt_ref[hh, c] = v_ref[rows, head_cols(hh)].astype(F32).T.astype(BF16)
            return carry
        lax.fori_loop(0, k_ref.shape[0] // t, stage, None)

    qts = [q_ref[:, head_cols(hh)].astype(F32).T.astype(BF16) for hh in heads]
    acc_ref[...] = jnp.zeros(acc_ref.shape, F32)
    diff = (lax.broadcasted_iota(jnp.int32, (KEY_CHUNK, QUERY_CHUNK), 0)
            - lax.broadcasted_iota(jnp.int32, (KEY_CHUNK, QUERY_CHUNK), 1))
    later = jnp.where(lax.broadcasted_iota(jnp.int32, (MXU_WIDTH, MXU_WIDTH), 1)
                      > lax.broadcasted_iota(jnp.int32, (MXU_WIDTH, MXU_WIDTH), 0),
                      1.0, 0.0).astype(BF16)

    def scores(j, diagonal):
        for c in range(t // KEY_CHUNK):
            k0 = pl.multiple_of(j * t + c * KEY_CHUNK, KEY_CHUNK)
            rows = slice(c * KEY_CHUNK, (c + 1) * KEY_CHUNK)
            for g in range(t // QUERY_CHUNK):
                cols = slice(g * QUERY_CHUNK, (g + 1) * QUERY_CHUNK)
                shift = g * QUERY_CHUNK - c * KEY_CHUNK
                for hh in heads:
                    if diagonal and shift <= -(QUERY_CHUNK - 1):
                        lb_ref[hh, rows, cols] = jnp.full((KEY_CHUNK, QUERY_CHUNK), -jnp.inf, F32)
                        lm_ref[hh, rows, cols] = jnp.zeros((KEY_CHUNK, QUERY_CHUNK), BF16)
                        continue
                    kc = k_ref[pl.ds(k0, KEY_CHUNK), head_cols(hh)]
                    z2 = jnp.dot(kc, qts[hh][:, cols], preferred_element_type=F32)
                    lb2 = jnp.minimum(z2, 0.0) - jnp.log(1.0 + jnp.exp2(_neg_abs(z2))) * LOG2E
                    l1m2 = lb2 - z2
                    if diagonal and shift < KEY_CHUNK:
                        visible = diff < shift
                        lb2 = jnp.where(visible, lb2, -jnp.inf)
                        l1m2 = jnp.where(visible, l1m2, 0.0)
                    lb_ref[hh, rows, cols] = lb2
                    lm_ref[hh, rows, cols] = l1m2.astype(BF16)

    def tile(j, runs, diagonal):
        scores(j, diagonal)
        runs = list(runs)
        for b in reversed(range(n_sub)):
            rows = slice(b * MXU_WIDTH, (b + 1) * MXU_WIDTH)
            totals = [[] for _ in heads]
            for g in range(t // QUERY_CHUNK):
                cols = slice(g * QUERY_CHUNK, (g + 1) * QUERY_CHUNK)
                for hh in heads:
                    lm = lm_ref[hh, rows, cols]
                    within = jnp.dot(later, lm, preferred_element_type=F32)
                    w = jnp.exp2(lb_ref[hh, rows, cols] + within + runs[hh][:, cols])
                    w_ref[hh, rows, cols] = w.astype(BF16)
                    totals[hh].append(within[0:1, :] + lm[0:1, :].astype(F32))
            for hh in heads:
                runs[hh] = runs[hh] + jnp.concatenate(totals[hh], axis=1)
        for hh in heads:
            acc_ref[hh] += jnp.dot(vt_ref[hh, j], w_ref[hh], preferred_element_type=F32)
        return tuple(runs)

    def live(runs):
        return jnp.max(functools.reduce(jnp.maximum, runs)) > WEIGHT_FLOOR_LOG2

    runs = tile(i, tuple(jnp.zeros((1, t), F32) for _ in heads), True)
    lax.while_loop(lambda c: (c[0] >= 0) & live(c[1]),
                   lambda c: (c[0] - 1, tile(c[0], c[1], False)), (i - 1, runs))

    for hh in heads:
        o_ref[:, head_cols(hh)] = acc_ref[hh].T.astype(o_ref.dtype)


def sb_attention(q, kv, *, t=256, hg=4):
    s, d = q.shape
    ng = d // (hg * HEAD_DIM)
    w = hg * HEAD_DIM
    return pl.pallas_call(
        functools.partial(_sb_kernel, t=t, hg=hg),
        grid=(ng, s // t),
        in_specs=[pl.BlockSpec((t, w), lambda g, i: (i, g)),
                  pl.BlockSpec((s, w), lambda g, i: (0, g)),
                  pl.BlockSpec((s, w), lambda g, i: (0, ng + g))],
        out_specs=pl.BlockSpec((t, w), lambda g, i: (i, g)),
        out_shape=jax.ShapeDtypeStruct((s, d), BF16),
        scratch_shapes=[pltpu.VMEM((hg, s // t, HEAD_DIM, t), BF16),
                        pltpu.VMEM((hg, t, t), F32),
                        pltpu.VMEM((hg, t, t), BF16),
                        pltpu.VMEM((hg, t, t), BF16),
                        pltpu.VMEM((hg, HEAD_DIM, t), F32)],
        compiler_params=_params("arbitrary", "arbitrary"),
        name="sb_attention")(q, kv, kv)


def _matmul_residual_kernel(a_ref, w_ref, h_ref, o_ref):
    o_ref[...] = h_ref[...] + jnp.dot(a_ref[...], w_ref[...], preferred_element_type=F32)


def matmul_residual(a, w, layer, h, *, tm=512):
    s, k = a.shape
    n = tn = w.shape[2]
    return pl.pallas_call(
        _matmul_residual_kernel, grid=(s // tm, n // tn),
        in_specs=[pl.BlockSpec((tm, k), lambda i, j: (i, 0)),
                  pl.BlockSpec((None, k, tn), lambda i, j: (layer, 0, j)),
                  pl.BlockSpec((tm, tn), lambda i, j: (i, j))],
        out_specs=pl.BlockSpec((tm, tn), lambda i, j: (i, j)),
        out_shape=jax.ShapeDtypeStruct((s, n), F32),
        compiler_params=_params("parallel", "arbitrary"),
        name="matmul_residual")(a, w, h)


def _ffn_kernel(h_ref, g_ref, wg_ref, wu_ref, wd_ref, o_ref, xn_ref):
    @pl.when(pl.program_id(1) == 0)
    def _():
        x = h_ref[...]
        xn_ref[...] = _rms_scale(x, g_ref[...]).astype(BF16)
        o_ref[...] = x

    xn = xn_ref[...]
    gate = jnp.dot(xn, wg_ref[...], preferred_element_type=F32)
    up = jnp.dot(xn, wu_ref[...], preferred_element_type=F32)
    act = (gate * jax.nn.sigmoid(gate) * up).astype(BF16)
    o_ref[...] += jnp.dot(act, wd_ref[...], preferred_element_type=F32)


def ffn(h, g, wg, wu, wd, layer, *, tm=512, tf=512):
    s, d = h.shape
    dff = wg.shape[2]
    return pl.pallas_call(
        _ffn_kernel, grid=(s // tm, dff // tf),
        in_specs=[pl.BlockSpec((tm, d), lambda i, f: (i, 0)),
                  pl.BlockSpec((1, d), lambda i, f: (0, 0)),
                  pl.BlockSpec((None, d, tf), lambda i, f: (layer, 0, f)),
                  pl.BlockSpec((None, d, tf), lambda i, f: (layer, 0, f)),
                  pl.BlockSpec((None, tf, d), lambda i, f: (layer, f, 0))],
        out_specs=pl.BlockSpec((tm, d), lambda i, f: (i, 0)),
        out_shape=jax.ShapeDtypeStruct((s, d), F32),
        scratch_shapes=[pltpu.VMEM((tm, d), BF16)],
        compiler_params=_params("parallel", "arbitrary"),
        name="ffn")(h, g, wg, wu, wd)


def _rmsnorm_kernel(x_ref, g_ref, o_ref):
    o_ref[...] = _rms_scale(x_ref[...], g_ref[...])


def rmsnorm(x, g, *, tm=512):
    s, d = x.shape
    return pl.pallas_call(
        _rmsnorm_kernel, grid=(s // tm,),
        in_specs=[pl.BlockSpec((tm, d), lambda i: (i, 0)),
                  pl.BlockSpec((1, d), lambda i: (0, 0))],
        out_specs=pl.BlockSpec((tm, d), lambda i: (i, 0)),
        out_shape=jax.ShapeDtypeStruct((s, d), F32),
        compiler_params=_params("parallel"), name="rmsnorm")(x, g)


def kernel(x, attn_norm, ffn_norm, a_w_in, a_b_f, a_w_out, kv_norm, w_kv, b_w_q, b_w_out,
           ffn_w_gate, ffn_w_up, ffn_w_down, final_norm):
    b, s, d = x.shape
    assert b == 1 and d == N_HEADS * HEAD_DIM
    depth = attn_norm.shape[0]
    n_a = a_w_in.shape[0]
    h = x.reshape(s, d)
    w_qkv = cast_bf16(a_w_in, 3 * d)
    w_out_a, w_kv, w_q_b, w_out_b = (cast_bf16(w) for w in (a_w_out, w_kv[None], b_w_q, b_w_out))
    w_gate, w_up, w_down = (cast_bf16(w) for w in (ffn_w_gate, ffn_w_up, ffn_w_down))
    kv = None
    for layer in range(depth):
        g_attn = attn_norm[layer].reshape(1, d)
        if layer < n_a:
            w_f = jnp.pad(a_w_in[layer, :, 3 * d:], ((0, 0), (0, LANES - N_HEADS))).astype(BF16)
            b_f = jnp.pad(a_b_f[layer], (0, LANES - N_HEADS)).reshape(1, LANES)
            qkv, f_logit = norm_matmul(h, g_attn, w_qkv, layer, w_f, tm=1024,
                                       scaled_cols=d, col_scale=HEAD_DIM ** -0.5 * LOG2E)
            o = fox_attention(qkv, gate_cumsum(f_logit, b_f))
            h = matmul_residual(o, w_out_a, layer, h)
        else:
            j = layer - n_a
            q = norm_matmul(h, g_attn, w_q_b, j, tn=d,
                            scaled_cols=d, col_scale=HEAD_DIM ** -0.5 * LOG2E)
            o = sb_attention(q, kv)
            h = matmul_residual(o, w_out_b, j, h)
        h = ffn(h, ffn_norm[layer].reshape(1, d), w_gate, w_up, w_down, layer)
        if layer == n_a - 1:
            kv = norm_matmul(h, kv_norm.reshape(1, d), w_kv, 0, tm=1024)
    return rmsnorm(h, final_norm.reshape(1, d)).reshape(b, s, d)
```

```python
import functools

import jax
import jax.numpy as jnp
from jax import lax
from jax.experimental import pallas as pl
from jax.experimental.pallas import tpu as pltpu

N_HEADS = 16
HEAD_DIM = 128
EPS = 1e-6
LOG2E = 1.4426950408889634
LANES = 128
MXU_WIDTH = 256
VMEM_LIMIT = 56 * 1024 * 1024

KEY_CHUNK = 128
QUERY_CHUNK = MXU_WIDTH
PROB_CHUNK = 32
WEIGHT_FLOOR_LOG2 = -154.0

F32 = jnp.float32
BF16 = jnp.bfloat16


def _params(*sem):
    return pltpu.CompilerParams(dimension_semantics=sem, vmem_limit_bytes=VMEM_LIMIT)


def _rms_scale(x, g):
    ms = jnp.mean(x * x, axis=-1, keepdims=True)
    return x * lax.rsqrt(ms + EPS) * g


def _cast_kernel(x_ref, o_ref):
    o_ref[...] = x_ref[...].astype(o_ref.dtype)


def cast_bf16(w, cols=None, *, tr=256):
    n_layers, r, c = w.shape
    cols = c if cols is None else cols
    return pl.pallas_call(
        _cast_kernel, grid=(n_layers, r // tr),
        in_specs=[pl.BlockSpec((None, tr, cols), lambda l, i: (l, i, 0))],
        out_specs=pl.BlockSpec((None, tr, cols), lambda l, i: (l, i, 0)),
        out_shape=jax.ShapeDtypeStruct((n_layers, r, cols), BF16),
        compiler_params=_params("parallel", "parallel"), name="cast_bf16")(w)


def _scaled_tile(acc, n_scaled_tiles, col_scale):
    if n_scaled_tiles == 0:
        return acc
    return acc * jnp.where(pl.program_id(1) < n_scaled_tiles, col_scale, 1.0)


def _norm_matmul_kernel(x_ref, g_ref, w_ref, o_ref, xn_ref, *, n_scaled_tiles, col_scale):
    @pl.when(pl.program_id(1) == 0)
    def _():
        xn_ref[...] = _rms_scale(x_ref[...], g_ref[...]).astype(BF16)

    acc = jnp.dot(xn_ref[...], w_ref[...], preferred_element_type=F32)
    o_ref[...] = _scaled_tile(acc, n_scaled_tiles, col_scale).astype(o_ref.dtype)


def _norm_matmul_gate_kernel(x_ref, g_ref, w_ref, wf_ref, o_ref, f_ref, xn_ref,
                             *, n_scaled_tiles, col_scale):
    @pl.when(pl.program_id(1) == 0)
    def _():
        xn = _rms_scale(x_ref[...], g_ref[...]).astype(BF16)
        xn_ref[...] = xn
        f_ref[...] = jnp.dot(xn, wf_ref[...].astype(BF16), preferred_element_type=F32)

    acc = jnp.dot(xn_ref[...], w_ref[...], preferred_element_type=F32)
    o_ref[...] = _scaled_tile(acc, n_scaled_tiles, col_scale).astype(o_ref.dtype)


def norm_matmul(x, g, w, layer, wf=None, *, scaled_cols=0, col_scale=1.0, tm=512, tn=1024):
    s, d = x.shape
    n = w.shape[2]
    assert scaled_cols % tn == 0
    static = dict(n_scaled_tiles=scaled_cols // tn, col_scale=col_scale)
    grid = (s // tm, n // tn)
    x_spec = pl.BlockSpec((tm, d), lambda i, j: (i, 0))
    g_spec = pl.BlockSpec((1, d), lambda i, j: (0, 0))
    w_spec = pl.BlockSpec((None, d, tn), lambda i, j: (layer, 0, j))
    o_spec = pl.BlockSpec((tm, tn), lambda i, j: (i, j))
    scratch = [pltpu.VMEM((tm, d), BF16)]
    if wf is None:
        return pl.pallas_call(
            functools.partial(_norm_matmul_kernel, **static), grid=grid,
            in_specs=[x_spec, g_spec, w_spec], out_specs=o_spec,
            out_shape=jax.ShapeDtypeStruct((s, n), BF16),
            scratch_shapes=scratch, compiler_params=_params("parallel", "arbitrary"),
            name="norm_matmul")(x, g, w)
    wf_spec = pl.BlockSpec((d, LANES), lambda i, j: (0, 0))
    f_spec = pl.BlockSpec((tm, LANES), lambda i, j: (i, 0))
    return pl.pallas_call(
        functools.partial(_norm_matmul_gate_kernel, **static), grid=grid,
        in_specs=[x_spec, g_spec, w_spec, wf_spec], out_specs=[o_spec, f_spec],
        out_shape=[jax.ShapeDtypeStruct((s, n), BF16),
                   jax.ShapeDtypeStruct((s, LANES), F32)],
        scratch_shapes=scratch, compiler_params=_params("parallel", "arbitrary"),
        name="norm_matmul_gate")(x, g, w, wf)


def _log_sigmoid(z):
    return jnp.minimum(z, 0.0) - jnp.log(1.0 + jnp.exp(-jnp.abs(z)))


def _split3(x):
    hi = x.astype(BF16)
    r = x - hi.astype(F32)
    mid = r.astype(BF16)
    lo = (r - mid.astype(F32)).astype(BF16)
    return hi, mid, lo


def _gate_cumsum_kernel(f_ref, b_ref, ccol_ref, carry_ref, *, tc):
    @pl.when(pl.program_id(0) == 0)
    def _():
        carry_ref[...] = jnp.zeros_like(carry_ref)

    log_f = _log_sigmoid(f_ref[...] + b_ref[...])
    row = lax.broadcasted_iota(jnp.int32, (tc, tc), 0)
    col = lax.broadcasted_iota(jnp.int32, (tc, tc), 1)
    tri = jnp.where(col <= row, 1.0, 0.0).astype(BF16)
    hi, mid, lo = _split3(log_f)
    c = (jnp.dot(tri, hi, preferred_element_type=F32)
         + jnp.dot(tri, mid, preferred_element_type=F32)
         + jnp.dot(tri, lo, preferred_element_type=F32)) + carry_ref[...]
    ccol_ref[...] = c
    carry_ref[...] = c[tc - 1:tc, :]


def gate_cumsum(f, b, *, tc=512):
    s = f.shape[0]
    return pl.pallas_call(
        functools.partial(_gate_cumsum_kernel, tc=tc), grid=(s // tc,),
        in_specs=[pl.BlockSpec((tc, LANES), lambda i: (i, 0)),
                  pl.BlockSpec((1, LANES), lambda i: (0, 0))],
        out_specs=pl.BlockSpec((tc, LANES), lambda i: (i, 0)),
        out_shape=jax.ShapeDtypeStruct((s, LANES), F32),
        scratch_shapes=[pltpu.VMEM((1, LANES), F32)],
        compiler_params=_params("arbitrary"), name="gate_cumsum")(f, b)


def _qk(q, k):
    return lax.dot_general(q, k, (((1,), (1,)), ((), ())), preferred_element_type=F32)


def _colmax8(x):
    return jnp.max(x.reshape(x.shape[0] // 8, 8, x.shape[1]), axis=0)


def _neg_abs(x):
    bits = lax.bitcast_convert_type(x, jnp.int32) | jnp.int32(-2 ** 31)
    return lax.bitcast_convert_type(bits, F32)


def _colsum8(x):
    return jnp.sum(x.reshape(x.shape[0] // 8, 8, x.shape[1]), axis=0)


def _fox_kernel(q_ref, k_ref, v_ref, ccol_ref, o_ref,
                vt_ref, csb_ref, kn_ref, s_ref, p_ref, acc_ref, lp_ref, *, t, hg):
    grp = pl.program_id(0)
    i = pl.program_id(1)
    n = i + 1
    heads = range(hg)

    def head_cols(hh):
        return slice(hh * HEAD_DIM, (hh + 1) * HEAD_DIM)

    @pl.when(i == 0)
    def _():
        for hh in heads:
            kn_ref[hh] = jnp.zeros((1, LANES), F32)

        def stage(c, carry):
            rows = pl.ds(pl.multiple_of(c * t, t), t)
            lane = lax.broadcasted_iota(jnp.int32, (t, LANES), 1)
            for hh in heads:
                vt_ref[hh, c] = v_ref[rows, head_cols(hh)].astype(F32).T.astype(BF16)
                cs = jnp.sum(jnp.where(lane == grp * hg + hh, ccol_ref[rows, :], 0.0),
                             axis=-1, keepdims=True)
                csb_ref[hh, rows, :] = jnp.broadcast_to(cs * LOG2E, (t, LANES))
                kf = k_ref[rows, head_cols(hh)].astype(F32)
                kn_ref[hh] = jnp.maximum(kn_ref[hh], jnp.max(jnp.sum(kf * kf, axis=-1, keepdims=True)))
            return carry
        lax.fori_loop(0, k_ref.shape[0] // t, stage, None)

    qts = [q_ref[:, head_cols(hh)].astype(F32).T.astype(BF16) for hh in heads]
    reach = [jnp.sqrt(jnp.sum(qts[hh].astype(F32) ** 2, axis=0, keepdims=True) * kn_ref[hh][:, 0:1])
             for hh in heads]
    acc_ref[...] = jnp.zeros(acc_ref.shape, F32)
    lp_ref[...] = jnp.zeros(lp_ref.shape, F32)
    diff = (lax.broadcasted_iota(jnp.int32, (KEY_CHUNK, QUERY_CHUNK), 0)
            - lax.broadcasted_iota(jnp.int32, (KEY_CHUNK, QUERY_CHUNK), 1))

    def scores(j, slot, diagonal):
        mparts = [[jnp.full((8, QUERY_CHUNK), -jnp.inf, F32) for _ in range(t // QUERY_CHUNK)]
                  for _ in heads]
        for c in range(t // KEY_CHUNK):
            k0 = pl.multiple_of(j * t + c * KEY_CHUNK, KEY_CHUNK)
            rows = slice(c * KEY_CHUNK, (c + 1) * KEY_CHUNK)
            for g in range(t // QUERY_CHUNK):
                cols = slice(g * QUERY_CHUNK, (g + 1) * QUERY_CHUNK)
                shift = g * QUERY_CHUNK - c * KEY_CHUNK
                for hh in heads:
                    if diagonal and shift <= -QUERY_CHUNK:
                        s_ref[hh, slot, rows, cols] = jnp.full((KEY_CHUNK, QUERY_CHUNK), -jnp.inf, F32)
                        continue
                    bias = csb_ref[hh, pl.ds(k0, KEY_CHUNK), :]
                    bias = jnp.concatenate([bias] * (QUERY_CHUNK // LANES), axis=1)
                    kc = k_ref[pl.ds(k0, KEY_CHUNK), head_cols(hh)]
                    s = jnp.dot(kc, qts[hh][:, cols], preferred_element_type=F32) - bias
                    if diagonal and shift < KEY_CHUNK - 1:
                        s = jnp.where(diff <= shift, s, -jnp.inf)
                    s_ref[hh, slot, rows, cols] = s
                    mparts[hh][g] = jnp.maximum(mparts[hh][g], _colmax8(s))
        return [jnp.concatenate(mparts[hh], axis=1) for hh in heads]

    def new_max(m_old, mpart):
        m_new = jnp.maximum(m_old, jnp.max(mpart, axis=0, keepdims=True))
        return m_new, jnp.exp2(m_old - m_new)

    def half(step, slot, carry, last=False):
        ms, alphas = carry
        j = i - step
        lsums = [jnp.zeros((8, t), F32) for _ in heads]
        for c in range(t // PROB_CHUNK):
            rows = slice(c * PROB_CHUNK, (c + 1) * PROB_CHUNK)
            for hh in heads:
                p = jnp.exp2(s_ref[hh, slot, rows, :] - ms[hh])
                p_ref[hh, rows, :] = p.astype(BF16)
                lsums[hh] = lsums[hh] + _colsum8(p)
        if not last:
            mparts = scores(j - 1, 1 - slot, False)
            new = [new_max(ms[hh], mparts[hh]) for hh in heads]
            carry = tuple(m for m, _ in new), tuple(a for _, a in new)
        for hh in heads:
            lp_ref[hh] = alphas[hh] * lp_ref[hh] + lsums[hh]
            acc_ref[hh] = alphas[hh] * acc_ref[hh] + jnp.dot(vt_ref[hh, j], p_ref[hh],
                                                            preferred_element_type=F32)
        return carry

    def tile_live(j, ms):
        last_key = jnp.maximum(j + 1, 1) * t - 1
        gaps = [reach[hh] - csb_ref[hh, pl.ds(last_key, 1), :][:, 0:1] - ms[hh] for hh in heads]
        return jnp.max(functools.reduce(jnp.maximum, gaps)) > WEIGHT_FLOOR_LOG2

    first = [new_max(jnp.full((1, t), -jnp.inf, F32), mp) for mp in scores(i, 0, True)]
    carry = tuple(m for m, _ in first), tuple(a for _, a in first)
    n_pairs = (n - 1) // 2

    def pair(c):
        mm, _, ms, alphas = c
        carry = half(2 * mm, 0, (ms, alphas))
        go = (mm + 1 < n_pairs) & tile_live(i - (2 * mm + 3), carry[0])
        return (mm + 1, go) + half(2 * mm + 1, 1, carry)

    mm, _, ms, alphas = lax.while_loop(
        lambda c: c[1], pair,
        (jnp.int32(0), (n_pairs > 0) & tile_live(i - 1, carry[0])) + carry)
    step = 2 * mm
    single = (step == n - 1) | jnp.logical_not(tile_live(i - (step + 1), ms))

    @pl.when(single)
    def _():
        half(step, 0, (ms, alphas), last=True)

    @pl.when(jnp.logical_not(single))
    def _():
        half(step + 1, 1, half(step, 0, (ms, alphas)), last=True)

    for hh in heads:
        l = jnp.sum(lp_ref[hh], axis=0, keepdims=True)
        o_ref[:, head_cols(hh)] = (acc_ref[hh] / l).T.astype(o_ref.dtype)


def fox_attention(qkv, ccol, *, t=512, hg=1):
    s = qkv.shape[0]
    d = qkv.shape[1] // 3
    ng = d // (hg * HEAD_DIM)
    w = hg * HEAD_DIM
    return pl.pallas_call(
        functools.partial(_fox_kernel, t=t, hg=hg),
        grid=(ng, s // t),
        in_specs=[pl.BlockSpec((t, w), lambda g, i: (i, g)),
                  pl.BlockSpec((s, w), lambda g, i: (0, ng + g)),
                  pl.BlockSpec((s, w), lambda g, i: (0, 2 * ng + g)),
                  pl.BlockSpec((s, LANES), lambda g, i: (0, 0))],
        out_specs=pl.BlockSpec((t, w), lambda g, i: (i, g)),
        out_shape=jax.ShapeDtypeStruct((s, d), BF16),
        scratch_shapes=[pltpu.VMEM((hg, s // t, HEAD_DIM, t), BF16),
                        pltpu.VMEM((hg, s, LANES), F32),
                        pltpu.VMEM((hg, 1, LANES), F32),
                        pltpu.VMEM((hg, 2, t, t), F32),
                        pltpu.VMEM((hg, t, t), BF16),
                        pltpu.VMEM((hg, HEAD_DIM, t), F32), pltpu.VMEM((hg, 8, t), F32)],
        compiler_params=_params("arbitrary", "arbitrary"),
        name="fox_attention")(qkv, qkv, qkv, ccol)


def _sb_kernel(q_ref, k_ref, v_ref, o_ref, vt_ref, lb_ref, lm_ref, w_ref, acc_ref, *, t, hg):
    i = pl.program_id(1)
    n_sub = t // MXU_WIDTH
    heads = range(hg)

    def head_cols(hh):
        return slice(hh * HEAD_DIM, (hh + 1) * HEAD_DIM)

    @pl.when(i == 0)
    def _():
        def stage(c, carry):
            rows = pl.ds(pl.multiple_of(c * t, t), t)
            for hh in heads:
                vt_ref[hh, c] = v_ref[rows, head_cols(hh)].astype(F32).T.astype(BF16)
            return carry
        lax.fori_loop(0, k_ref.shape[0] // t, stage, None)

    qts = [q_ref[:, head_cols(hh)].astype(F32).T.astype(BF16) for hh in heads]
    acc_ref[...] = jnp.zeros(acc_ref.shape, F32)
    diff = (lax.broadcasted_iota(jnp.int32, (KEY_CHUNK, QUERY_CHUNK), 0)
            - lax.broadcasted_iota(jnp.int32, (KEY_CHUNK, QUERY_CHUNK), 1))
    later = jnp.where(lax.broadcasted_iota(jnp.int32, (MXU_WIDTH, MXU_WIDTH), 1)
                      > lax.broadcasted_iota(jnp.int32, (MXU_WIDTH, MXU_WIDTH), 0),
                      1.0, 0.0).astype(BF16)

    def scores(j, diagonal):
        for c in range(t // KEY_CHUNK):
            k0 = pl.multiple_of(j * t + c * KEY_CHUNK, KEY_CHUNK)
            rows = slice(c * KEY_CHUNK, (c + 1) * KEY_CHUNK)
            for g in range(t // QUERY_CHUNK):
                cols = slice(g * QUERY_CHUNK, (g + 1) * QUERY_CHUNK)
                shift = g * QUERY_CHUNK - c * KEY_CHUNK
                for hh in heads:
                    if diagonal and shift <= -(QUERY_CHUNK - 1):
                        lb_ref[hh, rows, cols] = jnp.full((KEY_CHUNK, QUERY_CHUNK), -jnp.inf, F32)
                        lm_ref[hh, rows, cols] = jnp.zeros((KEY_CHUNK, QUERY_CHUNK), BF16)
                        continue
                    kc = k_ref[pl.ds(k0, KEY_CHUNK), head_cols(hh)]
                    z2 = jnp.dot(kc, qts[hh][:, cols], preferred_element_type=F32)
                    lb2 = jnp.minimum(z2, 0.0) - jnp.log(1.0 + jnp.exp2(_neg_abs(z2))) * LOG2E
                    l1m2 = lb2 - z2
                    if diagonal and shift < KEY_CHUNK:
                        visible = diff < shift
                        lb2 = jnp.where(visible, lb2, -jnp.inf)
                        l1m2 = jnp.where(visible, l1m2, 0.0)
                    lb_ref[hh, rows, cols] = lb2
                    lm_ref[hh, rows, cols] = l1m2.astype(BF16)

    def tile(j, runs, diagonal):
        scores(j, diagonal)
        runs = list(runs)
        for b in reversed(range(n_sub)):
            rows = slice(b * MXU_WIDTH, (b + 1) * MXU_WIDTH)
            totals = [[] for _ in heads]
            for g in range(t // QUERY_CHUNK):
                cols = slice(g * QUERY_CHUNK, (g + 1) * QUERY_CHUNK)
                for hh in heads:
                    lm = lm_ref[hh, rows, cols]
                    within = jnp.dot(later, lm, preferred_element_type=F32)
                    w = jnp.exp2(lb_ref[hh, rows, cols] + within + runs[hh][:, cols])
                    w_ref[hh, rows, cols] = w.astype(BF16)
                    totals[hh].append(within[0:1, :] + lm[0:1, :].astype(F32))
            for hh in heads:
                runs[hh] = runs[hh] + jnp.concatenate(totals[hh], axis=1)
        for hh in heads:
            acc_ref[hh] += jnp.dot(vt_ref[hh, j], w_ref[hh], preferred_element_type=F32)
        return tuple(runs)

    def live(runs):
        return jnp.max(functools.reduce(jnp.maximum, runs)) > WEIGHT_FLOOR_LOG2

    runs = tile(i, tuple(jnp.zeros((1, t), F32) for _ in heads), True)
    lax.while_loop(lambda c: (c[0] >= 0) & live(c[1]),
                   lambda c: (c[0] - 1, tile(c[0], c[1], False)), (i - 1, runs))

    for hh in heads:
        o_ref[:, head_cols(hh)] = acc_ref[hh].T.astype(o_ref.dtype)


def sb_attention(q, kv, *, t=256, hg=4):
    s, d = q.shape
    ng = d // (hg * HEAD_DIM)
    w = hg * HEAD_DIM
    return pl.pallas_call(
        functools.partial(_sb_kernel, t=t, hg=hg),
        grid=(ng, s // t),
        in_specs=[pl.BlockSpec((t, w), lambda g, i: (i, g)),
                  pl.BlockSpec((s, w), lambda g, i: (0, g)),
                  pl.BlockSpec((s, w), lambda g, i: (0, ng + g))],
        out_specs=pl.BlockSpec((t, w), lambda g, i: (i, g)),
        out_shape=jax.ShapeDtypeStruct((s, d), BF16),
        scratch_shapes=[pltpu.VMEM((hg, s // t, HEAD_DIM, t), BF16),
                        pltpu.VMEM((hg, t, t), F32),
                        pltpu.VMEM((hg, t, t), BF16),
                        pltpu.VMEM((hg, t, t), BF16),
                        pltpu.VMEM((hg, HEAD_DIM, t), F32)],
        compiler_params=_params("arbitrary", "arbitrary"),
        name="sb_attention")(q, kv, kv)


def _matmul_residual_kernel(a_ref, w_ref, h_ref, o_ref):
    o_ref[...] = h_ref[...] + jnp.dot(a_ref[...], w_ref[...], preferred_element_type=F32)


def matmul_residual(a, w, layer, h, *, tm=512):
    s, k = a.shape
    n = tn = w.shape[2]
    return pl.pallas_call(
        _matmul_residual_kernel, grid=(s // tm, n // tn),
        in_specs=[pl.BlockSpec((tm, k), lambda i, j: (i, 0)),
                  pl.BlockSpec((None, k, tn), lambda i, j: (layer, 0, j)),
                  pl.BlockSpec((tm, tn), lambda i, j: (i, j))],
        out_specs=pl.BlockSpec((tm, tn), lambda i, j: (i, j)),
        out_shape=jax.ShapeDtypeStruct((s, n), F32),
        compiler_params=_params("parallel", "arbitrary"),
        name="matmul_residual")(a, w, h)


def _ffn_kernel(h_ref, g_ref, wg_ref, wu_ref, wd_ref, gout_ref, o_ref, xn_ref, *, norm_out):
    @pl.when(pl.program_id(1) == 0)
    def _():
        x = h_ref[...]
        xn_ref[...] = _rms_scale(x, g_ref[...]).astype(BF16)
        o_ref[...] = x

    xn = xn_ref[...]
    gate = jnp.dot(xn, wg_ref[...], preferred_element_type=F32)
    up = jnp.dot(xn, wu_ref[...], preferred_element_type=F32)
    act = (gate * jax.nn.sigmoid(gate) * up).astype(BF16)
    o_ref[...] += jnp.dot(act, wd_ref[...], preferred_element_type=F32)

    if norm_out:
        @pl.when(pl.program_id(1) == pl.num_programs(1) - 1)
        def _():
            o_ref[...] = _rms_scale(o_ref[...], gout_ref[...])


def ffn(h, g, wg, wu, wd, layer, g_out, *, norm_out, tm=512, tf=512):
    s, d = h.shape
    dff = wg.shape[2]
    return pl.pallas_call(
        functools.partial(_ffn_kernel, norm_out=norm_out), grid=(s // tm, dff // tf),
        in_specs=[pl.BlockSpec((tm, d), lambda i, f: (i, 0)),
                  pl.BlockSpec((1, d), lambda i, f: (0, 0)),
                  pl.BlockSpec((None, d, tf), lambda i, f: (layer, 0, f)),
                  pl.BlockSpec((None, d, tf), lambda i, f: (layer, 0, f)),
                  pl.BlockSpec((None, tf, d), lambda i, f: (layer, f, 0)),
                  pl.BlockSpec((1, d), lambda i, f: (0, 0))],
        out_specs=pl.BlockSpec((tm, d), lambda i, f: (i, 0)),
        out_shape=jax.ShapeDtypeStruct((s, d), F32),
        scratch_shapes=[pltpu.VMEM((tm, d), BF16)],
        compiler_params=_params("parallel", "arbitrary"),
        name="ffn")(h, g, wg, wu, wd, g_out)


def kernel(x, attn_norm, ffn_norm, a_w_in, a_b_f, a_w_out, kv_norm, w_kv, b_w_q, b_w_out,
           ffn_w_gate, ffn_w_up, ffn_w_down, final_norm):
    b, s, d = x.shape
    assert b == 1 and d == N_HEADS * HEAD_DIM
    depth = attn_norm.shape[0]
    n_a = a_w_in.shape[0]
    h = x.reshape(s, d)
    w_qkv = cast_bf16(a_w_in, 3 * d)
    w_out_a, w_kv, w_q_b, w_out_b = (cast_bf16(w) for w in (a_w_out, w_kv[None], b_w_q, b_w_out))
    w_gate, w_up, w_down = (cast_bf16(w) for w in (ffn_w_gate, ffn_w_up, ffn_w_down))
    kv = None
    for layer in range(depth):
        g_attn = attn_norm[layer].reshape(1, d)
        if layer < n_a:
            w_f = jnp.pad(a_w_in[layer, :, 3 * d:], ((0, 0), (0, LANES - N_HEADS)))
            b_f = jnp.pad(a_b_f[layer], (0, LANES - N_HEADS)).reshape(1, LANES)
            qkv, f_logit = norm_matmul(h, g_attn, w_qkv, layer, w_f, tm=1024,
                                       scaled_cols=d, col_scale=HEAD_DIM ** -0.5 * LOG2E)
            o = fox_attention(qkv, gate_cumsum(f_logit, b_f))
            h = matmul_residual(o, w_out_a, layer, h)
        else:
            j = layer - n_a
            q = norm_matmul(h, g_attn, w_q_b, j, tn=d,
                            scaled_cols=d, col_scale=HEAD_DIM ** -0.5 * LOG2E)
            o = sb_attention(q, kv)
            h = matmul_residual(o, w_out_b, j, h)
        h = ffn(h, ffn_norm[layer].reshape(1, d), w_gate, w_up, w_down, layer,
                final_norm.reshape(1, d), norm_out=layer == depth - 1)
        if layer == n_a - 1:
            kv = norm_matmul(h, kv_norm.reshape(1, d), w_kv, 0, tm=1024)
    return h.reshape(b, s, d)
```

```python
import functools

import jax
import jax.numpy as jnp
from jax import lax
from jax.experimental import pallas as pl
from jax.experimental.pallas import tpu as pltpu

N_HEADS = 16
HEAD_DIM = 128
EPS = 1e-6
LOG2E = 1.4426950408889634
LANES = 128
MXU_WIDTH = 256
VMEM_LIMIT = 56 * 1024 * 1024

KEY_CHUNK = 128
QUERY_CHUNK = MXU_WIDTH
PROB_CHUNK = 32
WEIGHT_FLOOR_LOG2 = -154.0

F32 = jnp.float32
BF16 = jnp.bfloat16


def _params(*sem):
    return pltpu.CompilerParams(dimension_semantics=sem, vmem_limit_bytes=VMEM_LIMIT)


def _rms_scale(x, g):
    ms = jnp.mean(x * x, axis=-1, keepdims=True)
    return x * lax.rsqrt(ms + EPS) * g


def _cast_kernel(x_ref, o_ref):
    o_ref[...] = x_ref[...].astype(o_ref.dtype)


def cast_bf16(w, cols=None, *, tr=256):
    n_layers, r, c = w.shape
    cols = c if cols is None else cols
    return pl.pallas_call(
        _cast_kernel, grid=(n_layers, r // tr),
        in_specs=[pl.BlockSpec((None, tr, cols), lambda l, i: (l, i, 0))],
        out_specs=pl.BlockSpec((None, tr, cols), lambda l, i: (l, i, 0)),
        out_shape=jax.ShapeDtypeStruct((n_layers, r, cols), BF16),
        compiler_params=_params("parallel", "parallel"), name="cast_bf16")(w)


def _cast_split_kernel(x_ref, o_ref, tail_ref, *, cols):
    x = x_ref[...]
    o_ref[...] = x[:, :cols].astype(o_ref.dtype)
    tail_ref[...] = jnp.zeros(tail_ref.shape, tail_ref.dtype)
    tail_ref[:, :x.shape[1] - cols] = x[:, cols:]


def cast_split(w, cols, *, tr=256):
    n_layers, r, c = w.shape
    assert cols % LANES == 0 and 0 < c - cols <= LANES
    return pl.pallas_call(
        functools.partial(_cast_split_kernel, cols=cols), grid=(n_layers, r // tr),
        in_specs=[pl.BlockSpec((None, tr, c), lambda l, i: (l, i, 0))],
        out_specs=[pl.BlockSpec((None, tr, cols), lambda l, i: (l, i, 0)),
                   pl.BlockSpec((None, tr, LANES), lambda l, i: (l, i, 0))],
        out_shape=[jax.ShapeDtypeStruct((n_layers, r, cols), BF16),
                   jax.ShapeDtypeStruct((n_layers, r, LANES), F32)],
        compiler_params=_params("parallel", "parallel"), name="cast_split")(w)


def _scaled_tile(acc, n_scaled_tiles, col_scale):
    if n_scaled_tiles == 0:
        return acc
    return acc * jnp.where(pl.program_id(1) < n_scaled_tiles, col_scale, 1.0)


def _norm_matmul_kernel(x_ref, g_ref, w_ref, o_ref, xn_ref, *, n_scaled_tiles, col_scale):
    @pl.when(pl.program_id(1) == 0)
    def _():
        xn_ref[...] = _rms_scale(x_ref[...], g_ref[...]).astype(BF16)

    acc = jnp.dot(xn_ref[...], w_ref[...], preferred_element_type=F32)
    o_ref[...] = _scaled_tile(acc, n_scaled_tiles, col_scale).astype(o_ref.dtype)


def _norm_matmul_gate_kernel(x_ref, g_ref, w_ref, wf_ref, o_ref, f_ref, xn_ref,
                             *, n_scaled_tiles, col_scale):
    @pl.when(pl.program_id(1) == 0)
    def _():
        xn = _rms_scale(x_ref[...], g_ref[...]).astype(BF16)
        xn_ref[...] = xn
        f_ref[...] = jnp.dot(xn, wf_ref[...].astype(BF16), preferred_element_type=F32)

    acc = jnp.dot(xn_ref[...], w_ref[...], preferred_element_type=F32)
    o_ref[...] = _scaled_tile(acc, n_scaled_tiles, col_scale).astype(o_ref.dtype)


def norm_matmul(x, g, w, layer, wf=None, *, scaled_cols=0, col_scale=1.0, tm=512, tn=1024):
    s, d = x.shape
    n = w.shape[2]
    assert scaled_cols % tn == 0
    static = dict(n_scaled_tiles=scaled_cols // tn, col_scale=col_scale)
    grid = (s // tm, n // tn)
    x_spec = pl.BlockSpec((tm, d), lambda i, j: (i, 0))
    g_spec = pl.BlockSpec((1, d), lambda i, j: (0, 0))
    w_spec = pl.BlockSpec((None, d, tn), lambda i, j: (layer, 0, j))
    o_spec = pl.BlockSpec((tm, tn), lambda i, j: (i, j))
    scratch = [pltpu.VMEM((tm, d), BF16)]
    if wf is None:
        return pl.pallas_call(
            functools.partial(_norm_matmul_kernel, **static), grid=grid,
            in_specs=[x_spec, g_spec, w_spec], out_specs=o_spec,
            out_shape=jax.ShapeDtypeStruct((s, n), BF16),
            scratch_shapes=scratch, compiler_params=_params("parallel", "arbitrary"),
            name="norm_matmul")(x, g, w)
    wf_spec = pl.BlockSpec((None, d, LANES), lambda i, j: (layer, 0, 0))
    f_spec = pl.BlockSpec((tm, LANES), lambda i, j: (i, 0))
    return pl.pallas_call(
        functools.partial(_norm_matmul_gate_kernel, **static), grid=grid,
        in_specs=[x_spec, g_spec, w_spec, wf_spec], out_specs=[o_spec, f_spec],
        out_shape=[jax.ShapeDtypeStruct((s, n), BF16),
                   jax.ShapeDtypeStruct((s, LANES), F32)],
        scratch_shapes=scratch, compiler_params=_params("parallel", "arbitrary"),
        name="norm_matmul_gate")(x, g, w, wf)


def _log_sigmoid(z):
    return jnp.minimum(z, 0.0) - jnp.log(1.0 + jnp.exp(-jnp.abs(z)))


def _split3(x):
    hi = x.astype(BF16)
    r = x - hi.astype(F32)
    mid = r.astype(BF16)
    lo = (r - mid.astype(F32)).astype(BF16)
    return hi, mid, lo


def _gate_cumsum_kernel(f_ref, b_ref, ccol_ref, carry_ref, *, tc):
    @pl.when(pl.program_id(0) == 0)
    def _():
        carry_ref[...] = jnp.zeros_like(carry_ref)

    log_f = _log_sigmoid(f_ref[...] + b_ref[...])
    row = lax.broadcasted_iota(jnp.int32, (tc, tc), 0)
    col = lax.broadcasted_iota(jnp.int32, (tc, tc), 1)
    tri = jnp.where(col <= row, 1.0, 0.0).astype(BF16)
    hi, mid, lo = _split3(log_f)
    c = (jnp.dot(tri, hi, preferred_element_type=F32)
         + jnp.dot(tri, mid, preferred_element_type=F32)
         + jnp.dot(tri, lo, preferred_element_type=F32)) + carry_ref[...]
    ccol_ref[...] = c
    carry_ref[...] = c[tc - 1:tc, :]


def gate_cumsum(f, b, *, tc=512):
    s = f.shape[0]
    return pl.pallas_call(
        functools.partial(_gate_cumsum_kernel, tc=tc), grid=(s // tc,),
        in_specs=[pl.BlockSpec((tc, LANES), lambda i: (i, 0)),
                  pl.BlockSpec((1, LANES), lambda i: (0, 0))],
        out_specs=pl.BlockSpec((tc, LANES), lambda i: (i, 0)),
        out_shape=jax.ShapeDtypeStruct((s, LANES), F32),
        scratch_shapes=[pltpu.VMEM((1, LANES), F32)],
        compiler_params=_params("arbitrary"), name="gate_cumsum")(f, b)


def _qk(q, k):
    return lax.dot_general(q, k, (((1,), (1,)), ((), ())), preferred_element_type=F32)


def _colmax8(x):
    return jnp.max(x.reshape(x.shape[0] // 8, 8, x.shape[1]), axis=0)


def _neg_abs(x):
    bits = lax.bitcast_convert_type(x, jnp.int32) | jnp.int32(-2 ** 31)
    return lax.bitcast_convert_type(bits, F32)


def _colsum8(x):
    return jnp.sum(x.reshape(x.shape[0] // 8, 8, x.shape[1]), axis=0)


def _fox_kernel(q_ref, k_ref, v_ref, ccol_ref, o_ref,
                vt_ref, csb_ref, kn_ref, s_ref, p_ref, acc_ref, lp_ref, *, t, hg):
    grp = pl.program_id(0)
    i = pl.program_id(1)
    n = i + 1
    heads = range(hg)

    def head_cols(hh):
        return slice(hh * HEAD_DIM, (hh + 1) * HEAD_DIM)

    @pl.when(i == 0)
    def _():
        for hh in heads:
            kn_ref[hh] = jnp.zeros((1, LANES), F32)

        def stage(c, carry):
            rows = pl.ds(pl.multiple_of(c * t, t), t)
            lane = lax.broadcasted_iota(jnp.int32, (t, LANES), 1)
            for hh in heads:
                vt_ref[hh, c] = v_ref[rows, head_cols(hh)].astype(F32).T.astype(BF16)
                cs = jnp.sum(jnp.where(lane == grp * hg + hh, ccol_ref[rows, :], 0.0),
                             axis=-1, keepdims=True)
                csb_ref[hh, rows, :] = jnp.broadcast_to(cs * LOG2E, (t, LANES))
                kf = k_ref[rows, head_cols(hh)].astype(F32)
                kn_ref[hh] = jnp.maximum(kn_ref[hh], jnp.max(jnp.sum(kf * kf, axis=-1, keepdims=True)))
            return carry
        lax.fori_loop(0, k_ref.shape[0] // t, stage, None)

    qts = [q_ref[:, head_cols(hh)].astype(F32).T.astype(BF16) for hh in heads]
    reach = [jnp.sqrt(jnp.sum(qts[hh].astype(F32) ** 2, axis=0, keepdims=True) * kn_ref[hh][:, 0:1])
             for hh in heads]
    acc_ref[...] = jnp.zeros(acc_ref.shape, F32)
    lp_ref[...] = jnp.zeros(lp_ref.shape, F32)
    diff = (lax.broadcasted_iota(jnp.int32, (KEY_CHUNK, QUERY_CHUNK), 0)
            - lax.broadcasted_iota(jnp.int32, (KEY_CHUNK, QUERY_CHUNK), 1))

    def scores(j, slot, diagonal):
        mparts = [[jnp.full((8, QUERY_CHUNK), -jnp.inf, F32) for _ in range(t // QUERY_CHUNK)]
                  for _ in heads]
        for c in range(t // KEY_CHUNK):
            k0 = pl.multiple_of(j * t + c * KEY_CHUNK, KEY_CHUNK)
            rows = slice(c * KEY_CHUNK, (c + 1) * KEY_CHUNK)
            for g in range(t // QUERY_CHUNK):
                cols = slice(g * QUERY_CHUNK, (g + 1) * QUERY_CHUNK)
                shift = g * QUERY_CHUNK - c * KEY_CHUNK
                for hh in heads:
                    if diagonal and shift <= -QUERY_CHUNK:
                        s_ref[hh, slot, rows, cols] = jnp.full((KEY_CHUNK, QUERY_CHUNK), -jnp.inf, F32)
                        continue
                    bias = csb_ref[hh, pl.ds(k0, KEY_CHUNK), :]
                    bias = jnp.concatenate([bias] * (QUERY_CHUNK // LANES), axis=1)
                    kc = k_ref[pl.ds(k0, KEY_CHUNK), head_cols(hh)]
                    s = jnp.dot(kc, qts[hh][:, cols], preferred_element_type=F32) - bias
                    if diagonal and shift < KEY_CHUNK - 1:
                        s = jnp.where(diff <= shift, s, -jnp.inf)
                    s_ref[hh, slot, rows, cols] = s
                    mparts[hh][g] = jnp.maximum(mparts[hh][g], _colmax8(s))
        return [jnp.concatenate(mparts[hh], axis=1) for hh in heads]

    def new_max(m_old, mpart):
        m_new = jnp.maximum(m_old, jnp.max(mpart, axis=0, keepdims=True))
        return m_new, jnp.exp2(m_old - m_new)

    def half(step, slot, carry, last=False):
        ms, alphas = carry
        j = i - step
        lsums = [jnp.zeros((8, t), F32) for _ in heads]
        for c in range(t // PROB_CHUNK):
            rows = slice(c * PROB_CHUNK, (c + 1) * PROB_CHUNK)
            for hh in heads:
                p = jnp.exp2(s_ref[hh, slot, rows, :] - ms[hh])
                p_ref[hh, rows, :] = p.astype(BF16)
                lsums[hh] = lsums[hh] + _colsum8(p)
        if not last:
            mparts = scores(j - 1, 1 - slot, False)
            new = [new_max(ms[hh], mparts[hh]) for hh in heads]
            carry = tuple(m for m, _ in new), tuple(a for _, a in new)
        for hh in heads:
            lp_ref[hh] = alphas[hh] * lp_ref[hh] + lsums[hh]
            acc_ref[hh] = alphas[hh] * acc_ref[hh] + jnp.dot(vt_ref[hh, j], p_ref[hh],
                                                            preferred_element_type=F32)
        return carry

    def tile_live(j, ms):
        last_key = jnp.maximum(j + 1, 1) * t - 1
        gaps = [reach[hh] - csb_ref[hh, pl.ds(last_key, 1), :][:, 0:1] - ms[hh] for hh in heads]
        return jnp.max(functools.reduce(jnp.maximum, gaps)) > WEIGHT_FLOOR_LOG2

    first = [new_max(jnp.full((1, t), -jnp.inf, F32), mp) for mp in scores(i, 0, True)]
    carry = tuple(m for m, _ in first), tuple(a for _, a in first)
    n_pairs = (n - 1) // 2

    def pair(c):
        mm, _, ms, alphas = c
        carry = half(2 * mm, 0, (ms, alphas))
        go = (mm + 1 < n_pairs) & tile_live(i - (2 * mm + 3), carry[0])
        return (mm + 1, go) + half(2 * mm + 1, 1, carry)

    mm, _, ms, alphas = lax.while_loop(
        lambda c: c[1], pair,
        (jnp.int32(0), (n_pairs > 0) & tile_live(i - 1, carry[0])) + carry)
    step = 2 * mm
    single = (step == n - 1) | jnp.logical_not(tile_live(i - (step + 1), ms))

    @pl.when(single)
    def _():
        half(step, 0, (ms, alphas), last=True)

    @pl.when(jnp.logical_not(single))
    def _():
        half(step + 1, 1, half(step, 0, (ms, alphas)), last=True)

    for hh in heads:
        l = jnp.sum(lp_ref[hh], axis=0, keepdims=True)
        o_ref[:, head_cols(hh)] = (acc_ref[hh] / l).T.astype(o_ref.dtype)


def fox_attention(qkv, ccol, *, t=256, hg=4):
    s = qkv.shape[0]
    d = qkv.shape[1] // 3
    ng = d // (hg * HEAD_DIM)
    w = hg * HEAD_DIM
    return pl.pallas_call(
        functools.partial(_fox_kernel, t=t, hg=hg),
        grid=(ng, s // t),
        in_specs=[pl.BlockSpec((t, w), lambda g, i: (i, g)),
                  pl.BlockSpec((s, w), lambda g, i: (0, ng + g), pipeline_mode=pl.Buffered(1)),
                  pl.BlockSpec((s, w), lambda g, i: (0, 2 * ng + g), pipeline_mode=pl.Buffered(1)),
                  pl.BlockSpec((s, LANES), lambda g, i: (0, 0), pipeline_mode=pl.Buffered(1))],
        out_specs=pl.BlockSpec((t, w), lambda g, i: (i, g)),
        out_shape=jax.ShapeDtypeStruct((s, d), BF16),
        scratch_shapes=[pltpu.VMEM((hg, s // t, HEAD_DIM, t), BF16),
                        pltpu.VMEM((hg, s, LANES), F32),
                        pltpu.VMEM((hg, 1, LANES), F32),
                        pltpu.VMEM((hg, 2, t, t), F32),
                        pltpu.VMEM((hg, t, t), BF16),
                        pltpu.VMEM((hg, HEAD_DIM, t), F32), pltpu.VMEM((hg, 8, t), F32)],
        compiler_params=_params("arbitrary", "arbitrary"),
        name="fox_attention")(qkv, qkv, qkv, ccol)


def _sb_kernel(q_ref, k_ref, v_ref, o_ref, vt_ref, lb_ref, lm_ref, w_ref, acc_ref, *, t, hg):
    i = pl.program_id(1)
    n_sub = t // MXU_WIDTH
    heads = range(hg)

    def head_cols(hh):
        return slice(hh * HEAD_DIM, (hh + 1) * HEAD_DIM)

    @pl.when(i == 0)
    def _():
        def stage(c, carry):
            rows = pl.ds(pl.multiple_of(c * t, t), t)
            for hh in heads:
                vt_ref[hh, c] = v_ref[rows, head_cols(hh)].astype(F32).T.astype(BF16)
            return carry
        lax.fori_loop(0, k_ref.shape[0] // t, stage, None)

    qts = [q_ref[:, head_cols(hh)].astype(F32).T.astype(BF16) for hh in heads]
    acc_ref[...] = jnp.zeros(acc_ref.shape, F32)
    diff = (lax.broadcasted_iota(jnp.int32, (KEY_CHUNK, QUERY_CHUNK), 0)
            - lax.broadcasted_iota(jnp.int32, (KEY_CHUNK, QUERY_CHUNK), 1))
    later = jnp.where(lax.broadcasted_iota(jnp.int32, (MXU_WIDTH, MXU_WIDTH), 1)
                      > lax.broadcasted_iota(jnp.int32, (MXU_WIDTH, MXU_WIDTH), 0),
                      1.0, 0.0).astype(BF16)

    def scores(j, diagonal):
        for c in range(t // KEY_CHUNK):
            k0 = pl.multiple_of(j * t + c * KEY_CHUNK, KEY_CHUNK)
            rows = slice(c * KEY_CHUNK, (c + 1) * KEY_CHUNK)
            for g in range(t // QUERY_CHUNK):
                cols = slice(g * QUERY_CHUNK, (g + 1) * QUERY_CHUNK)
                shift = g * QUERY_CHUNK - c * KEY_CHUNK
                for hh in heads:
                    if diagonal and shift <= -(QUERY_CHUNK - 1):
                        lb_ref[hh, rows, cols] = jnp.full((KEY_CHUNK, QUERY_CHUNK), -jnp.inf, F32)
                        lm_ref[hh, rows, cols] = jnp.zeros((KEY_CHUNK, QUERY_CHUNK), BF16)
                        continue
                    kc = k_ref[pl.ds(k0, KEY_CHUNK), head_cols(hh)]
                    z2 = jnp.dot(kc, qts[hh][:, cols], preferred_element_type=F32)
                    lb2 = jnp.minimum(z2, 0.0) - jnp.log(1.0 + jnp.exp2(_neg_abs(z2))) * LOG2E
                    l1m2 = lb2 - z2
                    if diagonal and shift < KEY_CHUNK:
                        visible = diff < shift
                        lb2 = jnp.where(visible, lb2, -jnp.inf)
                        l1m2 = jnp.where(visible, l1m2, 0.0)
                    lb_ref[hh, rows, cols] = lb2
                    lm_ref[hh, rows, cols] = l1m2.astype(BF16)

    def tile(j, runs, diagonal):
        scores(j, diagonal)
        runs = list(runs)
        for b in reversed(range(n_sub)):
            rows = slice(b * MXU_WIDTH, (b + 1) * MXU_WIDTH)
            totals = [[] for _ in heads]
            for g in range(t // QUERY_CHUNK):
                cols = slice(g * QUERY_CHUNK, (g + 1) * QUERY_CHUNK)
                for hh in heads:
                    lm = lm_ref[hh, rows, cols]
                    within = jnp.dot(later, lm, preferred_element_type=F32)
                    w = jnp.exp2(lb_ref[hh, rows, cols] + within + runs[hh][:, cols])
                    w_ref[hh, rows, cols] = w.astype(BF16)
                    totals[hh].append(within[0:1, :] + lm[0:1, :].astype(F32))
            for hh in heads:
                runs[hh] = runs[hh] + jnp.concatenate(totals[hh], axis=1)
        for hh in heads:
            acc_ref[hh] += jnp.dot(vt_ref[hh, j], w_ref[hh], preferred_element_type=F32)
        return tuple(runs)

    def live(runs):
        return jnp.max(functools.reduce(jnp.maximum, runs)) > WEIGHT_FLOOR_LOG2

    runs = tile(i, tuple(jnp.zeros((1, t), F32) for _ in heads), True)
    lax.while_loop(lambda c: (c[0] >= 0) & live(c[1]),
                   lambda c: (c[0] - 1, tile(c[0], c[1], False)), (i - 1, runs))

    for hh in heads:
        o_ref[:, head_cols(hh)] = acc_ref[hh].T.astype(o_ref.dtype)


def sb_attention(q, kv, *, t=256, hg=4):
    s, d = q.shape
    ng = d // (hg * HEAD_DIM)
    w = hg * HEAD_DIM
    return pl.pallas_call(
        functools.partial(_sb_kernel, t=t, hg=hg),
        grid=(ng, s // t),
        in_specs=[pl.BlockSpec((t, w), lambda g, i: (i, g)),
                  pl.BlockSpec((s, w), lambda g, i: (0, g)),
                  pl.BlockSpec((s, w), lambda g, i: (0, ng + g))],
        out_specs=pl.BlockSpec((t, w), lambda g, i: (i, g)),
        out_shape=jax.ShapeDtypeStruct((s, d), BF16),
        scratch_shapes=[pltpu.VMEM((hg, s // t, HEAD_DIM, t), BF16),
                        pltpu.VMEM((hg, t, t), F32),
                        pltpu.VMEM((hg, t, t), BF16),
                        pltpu.VMEM((hg, t, t), BF16),
                        pltpu.VMEM((hg, HEAD_DIM, t), F32)],
        compiler_params=_params("arbitrary", "arbitrary"),
        name="sb_attention")(q, kv, kv)


def _matmul_residual_kernel(a_ref, w_ref, h_ref, o_ref):
    o_ref[...] = h_ref[...] + jnp.dot(a_ref[...], w_ref[...], preferred_element_type=F32)


def matmul_residual(a, w, layer, h, *, tm=512):
    s, k = a.shape
    n = tn = w.shape[2]
    return pl.pallas_call(
        _matmul_residual_kernel, grid=(s // tm, n // tn),
        in_specs=[pl.BlockSpec((tm, k), lambda i, j: (i, 0)),
                  pl.BlockSpec((None, k, tn), lambda i, j: (layer, 0, j)),
                  pl.BlockSpec((tm, tn), lambda i, j: (i, j))],
        out_specs=pl.BlockSpec((tm, tn), lambda i, j: (i, j)),
        out_shape=jax.ShapeDtypeStruct((s, n), F32),
        compiler_params=_params("parallel", "arbitrary"),
        name="matmul_residual")(a, w, h)


def _ffn_kernel(h_ref, g_ref, wg_ref, wu_ref, wd_ref, gout_ref, o_ref, xn_ref, *, norm_out):
    @pl.when(pl.program_id(1) == 0)
    def _():
        x = h_ref[...]
        xn_ref[...] = _rms_scale(x, g_ref[...]).astype(BF16)
        o_ref[...] = x

    xn = xn_ref[...]
    gate = jnp.dot(xn, wg_ref[...], preferred_element_type=F32)
    up = jnp.dot(xn, wu_ref[...], preferred_element_type=F32)
    act = (gate * jax.nn.sigmoid(gate) * up).astype(BF16)
    o_ref[...] += jnp.dot(act, wd_ref[...], preferred_element_type=F32)

    if norm_out:
        @pl.when(pl.program_id(1) == pl.num_programs(1) - 1)
        def _():
            o_ref[...] = _rms_scale(o_ref[...], gout_ref[...])


def ffn(h, g, wg, wu, wd, layer, g_out, *, norm_out, tm=512, tf=512):
    s, d = h.shape
    dff = wg.shape[2]
    return pl.pallas_call(
        functools.partial(_ffn_kernel, norm_out=norm_out), grid=(s // tm, dff // tf),
        in_specs=[pl.BlockSpec((tm, d), lambda i, f: (i, 0)),
                  pl.BlockSpec((1, d), lambda i, f: (0, 0)),
                  pl.BlockSpec((None, d, tf), lambda i, f: (layer, 0, f)),
                  pl.BlockSpec((None, d, tf), lambda i, f: (layer, 0, f)),
                  pl.BlockSpec((None, tf, d), lambda i, f: (layer, f, 0)),
                  pl.BlockSpec((1, d), lambda i, f: (0, 0))],
        out_specs=pl.BlockSpec((tm, d), lambda i, f: (i, 0)),
        out_shape=jax.ShapeDtypeStruct((s, d), F32),
        scratch_shapes=[pltpu.VMEM((tm, d), BF16)],
        compiler_params=_params("parallel", "arbitrary"),
        name="ffn")(h, g, wg, wu, wd, g_out)


def kernel(x, attn_norm, ffn_norm, a_w_in, a_b_f, a_w_out, kv_norm, w_kv, b_w_q, b_w_out,
           ffn_w_gate, ffn_w_up, ffn_w_down, final_norm):
    b, s, d = x.shape
    assert b == 1 and d == N_HEADS * HEAD_DIM
    depth = attn_norm.shape[0]
    n_a = a_w_in.shape[0]
    h = x.reshape(s, d)
    w_qkv, w_f = cast_split(a_w_in, 3 * d)
    w_out_a, w_kv, w_q_b, w_out_b = (cast_bf16(w) for w in (a_w_out, w_kv[None], b_w_q, b_w_out))
    w_gate, w_up, w_down = (cast_bf16(w) for w in (ffn_w_gate, ffn_w_up, ffn_w_down))
    kv = None
    for layer in range(depth):
        g_attn = attn_norm[layer].reshape(1, d)
        if layer < n_a:
            b_f = jnp.pad(a_b_f[layer], (0, LANES - N_HEADS)).reshape(1, LANES)
            qkv, f_logit = norm_matmul(h, g_attn, w_qkv, layer, w_f, tm=1024,
                                       scaled_cols=d, col_scale=HEAD_DIM ** -0.5 * LOG2E)
            o = fox_attention(qkv, gate_cumsum(f_logit, b_f))
            h = matmul_residual(o, w_out_a, layer, h)
        else:
            j = layer - n_a
            q = norm_matmul(h, g_attn, w_q_b, j, tn=d,
                            scaled_cols=d, col_scale=HEAD_DIM ** -0.5 * LOG2E)
            o = sb_attention(q, kv)
            h = matmul_residual(o, w_out_b, j, h)
        h = ffn(h, ffn_norm[layer].reshape(1, d), w_gate, w_up, w_down, layer,
                final_norm.reshape(1, d), norm_out=layer == depth - 1)
        if layer == n_a - 1:
            kv = norm_matmul(h, kv_norm.reshape(1, d), w_kv, 0, tm=1024)
    return h.reshape(b, s, d)
```

```python
import functools

import jax
import jax.numpy as jnp
from jax import lax
from jax.experimental import pallas as pl
from jax.experimental.pallas import tpu as pltpu

N_HEADS = 16
HEAD_DIM = 128
EPS = 1e-6
LOG2E = 1.4426950408889634
LANES = 128
MXU_WIDTH = 256
VMEM_LIMIT = 56 * 1024 * 1024

KEY_CHUNK = 128
QUERY_CHUNK = MXU_WIDTH
PROB_CHUNK = 32
WEIGHT_FLOOR_LOG2 = -154.0

F32 = jnp.float32
BF16 = jnp.bfloat16


def _params(*sem):
    return pltpu.CompilerParams(dimension_semantics=sem, vmem_limit_bytes=VMEM_LIMIT)


def _rms_scale(x, g):
    ms = jnp.mean(x * x, axis=-1, keepdims=True)
    return x * lax.rsqrt(ms + EPS) * g


def _cast_kernel(x_ref, o_ref):
    o_ref[...] = x_ref[...].astype(o_ref.dtype)


def cast_bf16(w, cols=None, *, tr=256):
    n_layers, r, c = w.shape
    cols = c if cols is None else cols
    return pl.pallas_call(
        _cast_kernel, grid=(n_layers, r // tr),
        in_specs=[pl.BlockSpec((None, tr, cols), lambda l, i: (l, i, 0))],
        out_specs=pl.BlockSpec((None, tr, cols), lambda l, i: (l, i, 0)),
        out_shape=jax.ShapeDtypeStruct((n_layers, r, cols), BF16),
        compiler_params=_params("parallel", "parallel"), name="cast_bf16")(w)


def _cast_split_kernel(x_ref, o_ref, tail_ref, *, cols):
    x = x_ref[...]
    o_ref[...] = x[:, :cols].astype(o_ref.dtype)
    tail_ref[...] = jnp.zeros(tail_ref.shape, tail_ref.dtype)
    tail_ref[:, :x.shape[1] - cols] = x[:, cols:]


def cast_split(w, cols, *, tr=256):
    n_layers, r, c = w.shape
    assert cols % LANES == 0 and 0 < c - cols <= LANES
    return pl.pallas_call(
        functools.partial(_cast_split_kernel, cols=cols), grid=(n_layers, r // tr),
        in_specs=[pl.BlockSpec((None, tr, c), lambda l, i: (l, i, 0))],
        out_specs=[pl.BlockSpec((None, tr, cols), lambda l, i: (l, i, 0)),
                   pl.BlockSpec((None, tr, LANES), lambda l, i: (l, i, 0))],
        out_shape=[jax.ShapeDtypeStruct((n_layers, r, cols), BF16),
                   jax.ShapeDtypeStruct((n_layers, r, LANES), F32)],
        compiler_params=_params("parallel", "parallel"), name="cast_split")(w)


def _scaled_tile(acc, n_scaled_tiles, col_scale):
    if n_scaled_tiles == 0:
        return acc
    return acc * jnp.where(pl.program_id(1) < n_scaled_tiles, col_scale, 1.0)


def _norm_matmul_kernel(x_ref, g_ref, w_ref, o_ref, xn_ref, *, n_scaled_tiles, col_scale):
    @pl.when(pl.program_id(1) == 0)
    def _():
        xn_ref[...] = _rms_scale(x_ref[...], g_ref[...]).astype(BF16)

    acc = jnp.dot(xn_ref[...], w_ref[...], preferred_element_type=F32)
    o_ref[...] = _scaled_tile(acc, n_scaled_tiles, col_scale).astype(o_ref.dtype)


def _norm_matmul_gate_kernel(x_ref, g_ref, w_ref, wf_ref, o_ref, f_ref, xn_ref,
                             *, n_scaled_tiles, col_scale):
    @pl.when(pl.program_id(1) == 0)
    def _():
        xn = _rms_scale(x_ref[...], g_ref[...]).astype(BF16)
        xn_ref[...] = xn
        f_ref[...] = jnp.dot(xn, wf_ref[...].astype(BF16), preferred_element_type=F32)

    acc = jnp.dot(xn_ref[...], w_ref[...], preferred_element_type=F32)
    o_ref[...] = _scaled_tile(acc, n_scaled_tiles, col_scale).astype(o_ref.dtype)


def norm_matmul(x, g, w, layer, wf=None, *, scaled_cols=0, col_scale=1.0, tm=512, tn=1024):
    s, d = x.shape
    n = w.shape[2]
    assert scaled_cols % tn == 0
    static = dict(n_scaled_tiles=scaled_cols // tn, col_scale=col_scale)
    grid = (s // tm, n // tn)
    x_spec = pl.BlockSpec((tm, d), lambda i, j: (i, 0))
    g_spec = pl.BlockSpec((1, d), lambda i, j: (0, 0))
    w_spec = pl.BlockSpec((None, d, tn), lambda i, j: (layer, 0, j))
    o_spec = pl.BlockSpec((tm, tn), lambda i, j: (i, j))
    scratch = [pltpu.VMEM((tm, d), BF16)]
    if wf is None:
        return pl.pallas_call(
            functools.partial(_norm_matmul_kernel, **static), grid=grid,
            in_specs=[x_spec, g_spec, w_spec], out_specs=o_spec,
            out_shape=jax.ShapeDtypeStruct((s, n), BF16),
            scratch_shapes=scratch, compiler_params=_params("parallel", "arbitrary"),
            name="norm_matmul")(x, g, w)
    wf_spec = pl.BlockSpec((None, d, LANES), lambda i, j: (layer, 0, 0))
    f_spec = pl.BlockSpec((tm, LANES), lambda i, j: (i, 0))
    return pl.pallas_call(
        functools.partial(_norm_matmul_gate_kernel, **static), grid=grid,
        in_specs=[x_spec, g_spec, w_spec, wf_spec], out_specs=[o_spec, f_spec],
        out_shape=[jax.ShapeDtypeStruct((s, n), BF16),
                   jax.ShapeDtypeStruct((s, LANES), F32)],
        scratch_shapes=scratch, compiler_params=_params("parallel", "arbitrary"),
        name="norm_matmul_gate")(x, g, w, wf)


def _log_sigmoid(z):
    return jnp.minimum(z, 0.0) - jnp.log(1.0 + jnp.exp(-jnp.abs(z)))


def _split3(x):
    hi = x.astype(BF16)
    r = x - hi.astype(F32)
    mid = r.astype(BF16)
    lo = (r - mid.astype(F32)).astype(BF16)
    return hi, mid, lo


def _gate_cumsum_kernel(f_ref, b_ref, ccol_ref, carry_ref, *, tc):
    @pl.when(pl.program_id(0) == 0)
    def _():
        carry_ref[...] = jnp.zeros_like(carry_ref)

    log_f = _log_sigmoid(f_ref[...] + b_ref[...])
    row = lax.broadcasted_iota(jnp.int32, (tc, tc), 0)
    col = lax.broadcasted_iota(jnp.int32, (tc, tc), 1)
    tri = jnp.where(col <= row, 1.0, 0.0).astype(BF16)
    hi, mid, lo = _split3(log_f)
    c = (jnp.dot(tri, hi, preferred_element_type=F32)
         + jnp.dot(tri, mid, preferred_element_type=F32)
         + jnp.dot(tri, lo, preferred_element_type=F32)) + carry_ref[...]
    ccol_ref[...] = c
    carry_ref[...] = c[tc - 1:tc, :]


def gate_cumsum(f, b, *, tc=512):
    s = f.shape[0]
    return pl.pallas_call(
        functools.partial(_gate_cumsum_kernel, tc=tc), grid=(s // tc,),
        in_specs=[pl.BlockSpec((tc, LANES), lambda i: (i, 0)),
                  pl.BlockSpec((1, LANES), lambda i: (0, 0))],
        out_specs=pl.BlockSpec((tc, LANES), lambda i: (i, 0)),
        out_shape=jax.ShapeDtypeStruct((s, LANES), F32),
        scratch_shapes=[pltpu.VMEM((1, LANES), F32)],
        compiler_params=_params("arbitrary"), name="gate_cumsum")(f, b)


def _qk(q, k):
    return lax.dot_general(q, k, (((1,), (1,)), ((), ())), preferred_element_type=F32)


def _colmax8(x):
    return jnp.max(x.reshape(x.shape[0] // 8, 8, x.shape[1]), axis=0)


def _neg_abs(x):
    bits = lax.bitcast_convert_type(x, jnp.int32) | jnp.int32(-2 ** 31)
    return lax.bitcast_convert_type(bits, F32)


def _colsum8(x):
    return jnp.sum(x.reshape(x.shape[0] // 8, 8, x.shape[1]), axis=0)


def _fox_kernel(q_ref, k_ref, v_ref, ccol_ref, o_ref,
                vt_ref, csb_ref, kn_ref, s_ref, p_ref, acc_ref, lp_ref, *, t, hg):
    grp = pl.program_id(0)
    i = pl.program_id(1)
    n = i + 1
    heads = range(hg)

    def head_cols(hh):
        return slice(hh * HEAD_DIM, (hh + 1) * HEAD_DIM)

    @pl.when(i == 0)
    def _():
        for hh in heads:
            kn_ref[hh] = jnp.zeros((1, LANES), F32)

        def stage(c, carry):
            rows = pl.ds(pl.multiple_of(c * t, t), t)
            lane = lax.broadcasted_iota(jnp.int32, (t, LANES), 1)
            for hh in heads:
                vt_ref[hh, c] = v_ref[rows, head_cols(hh)].astype(F32).T.astype(BF16)
                cs = jnp.sum(jnp.where(lane == grp * hg + hh, ccol_ref[rows, :], 0.0),
                             axis=-1, keepdims=True)
                csb_ref[hh, rows, :] = jnp.broadcast_to(cs * LOG2E, (t, LANES))
                kf = k_ref[rows, head_cols(hh)].astype(F32)
                kn_ref[hh] = jnp.maximum(kn_ref[hh], jnp.max(jnp.sum(kf * kf, axis=-1, keepdims=True)))
            return carry
        lax.fori_loop(0, k_ref.shape[0] // t, stage, None)

    qts = [q_ref[:, head_cols(hh)].astype(F32).T.astype(BF16) for hh in heads]
    reach = [jnp.sqrt(jnp.sum(qts[hh].astype(F32) ** 2, axis=0, keepdims=True) * kn_ref[hh][:, 0:1])
             for hh in heads]
    acc_ref[...] = jnp.zeros(acc_ref.shape, F32)
    lp_ref[...] = jnp.zeros(lp_ref.shape, F32)
    diff = (lax.broadcasted_iota(jnp.int32, (KEY_CHUNK, QUERY_CHUNK), 0)
            - lax.broadcasted_iota(jnp.int32, (KEY_CHUNK, QUERY_CHUNK), 1))

    def scores(j, slot, diagonal):
        mparts = [[jnp.full((8, QUERY_CHUNK), -jnp.inf, F32) for _ in range(t // QUERY_CHUNK)]
                  for _ in heads]
        for c in range(t // KEY_CHUNK):
            k0 = pl.multiple_of(j * t + c * KEY_CHUNK, KEY_CHUNK)
            rows = slice(c * KEY_CHUNK, (c + 1) * KEY_CHUNK)
            for g in range(t // QUERY_CHUNK):
                cols = slice(g * QUERY_CHUNK, (g + 1) * QUERY_CHUNK)
                shift = g * QUERY_CHUNK - c * KEY_CHUNK
                for hh in heads:
                    if diagonal and shift <= -QUERY_CHUNK:
                        s_ref[hh, slot, rows, cols] = jnp.full((KEY_CHUNK, QUERY_CHUNK), -jnp.inf, F32)
                        continue
                    bias = csb_ref[hh, pl.ds(k0, KEY_CHUNK), :]
                    bias = jnp.concatenate([bias] * (QUERY_CHUNK // LANES), axis=1)
                    kc = k_ref[pl.ds(k0, KEY_CHUNK), head_cols(hh)]
                    s = jnp.dot(kc, qts[hh][:, cols], preferred_element_type=F32) - bias
                    if diagonal and shift < KEY_CHUNK - 1:
                        s = jnp.where(diff <= shift, s, -jnp.inf)
                    s_ref[hh, slot, rows, cols] = s
                    mparts[hh][g] = jnp.maximum(mparts[hh][g], _colmax8(s))
        return [jnp.concatenate(mparts[hh], axis=1) for hh in heads]

    def new_max(m_old, mpart):
        m_new = jnp.maximum(m_old, jnp.max(mpart, axis=0, keepdims=True))
        return m_new, jnp.exp2(m_old - m_new)

    def half(step, slot, carry, last=False):
        ms, alphas = carry
        j = i - step
        lsums = [jnp.zeros((8, t), F32) for _ in heads]
        for c in range(t // PROB_CHUNK):
            rows = slice(c * PROB_CHUNK, (c + 1) * PROB_CHUNK)
            for hh in heads:
                p = jnp.exp2(s_ref[hh, slot, rows, :] - ms[hh])
                p_ref[hh, rows, :] = p.astype(BF16)
                lsums[hh] = lsums[hh] + _colsum8(p)
        if not last:
            mparts = scores(j - 1, 1 - slot, False)
            new = [new_max(ms[hh], mparts[hh]) for hh in heads]
            carry = tuple(m for m, _ in new), tuple(a for _, a in new)
        for hh in heads:
            lp_ref[hh] = alphas[hh] * lp_ref[hh] + lsums[hh]
            acc_ref[hh] = alphas[hh] * acc_ref[hh] + jnp.dot(vt_ref[hh, j], p_ref[hh],
                                                            preferred_element_type=F32)
        return carry

    def tile_live(j, ms):
        last_key = jnp.maximum(j + 1, 1) * t - 1
        gaps = [reach[hh] - csb_ref[hh, pl.ds(last_key, 1), :][:, 0:1] - ms[hh] for hh in heads]
        return jnp.max(functools.reduce(jnp.maximum, gaps)) > WEIGHT_FLOOR_LOG2

    first = [new_max(jnp.full((1, t), -jnp.inf, F32), mp) for mp in scores(i, 0, True)]
    carry = tuple(m for m, _ in first), tuple(a for _, a in first)
    n_pairs = (n - 1) // 2

    def pair(c):
        mm, _, ms, alphas = c
        carry = half(2 * mm, 0, (ms, alphas))
        alive = tile_live(i - (2 * mm + 3), carry[0])
        return (mm + 1, alive) + half(2 * mm + 1, 1, carry)

    mm, alive, ms, alphas = lax.while_loop(
        lambda c: (c[0] < n_pairs) & c[1], pair, (jnp.int32(0), tile_live(i - 1, carry[0])) + carry)
    step = 2 * mm
    single = (step == n - 1) | jnp.logical_not(alive)

    @pl.when(single)
    def _():
        half(step, 0, (ms, alphas), last=True)

    @pl.when(jnp.logical_not(single))
    def _():
        half(step + 1, 1, half(step, 0, (ms, alphas)), last=True)

    for hh in heads:
        l = jnp.sum(lp_ref[hh], axis=0, keepdims=True)
        o_ref[:, head_cols(hh)] = (acc_ref[hh] / l).T.astype(o_ref.dtype)


def fox_attention(qkv, ccol, *, t=256, hg=4):
    s = qkv.shape[0]
    d = qkv.shape[1] // 3
    ng = d // (hg * HEAD_DIM)
    w = hg * HEAD_DIM
    return pl.pallas_call(
        functools.partial(_fox_kernel, t=t, hg=hg),
        grid=(ng, s // t),
        in_specs=[pl.BlockSpec((t, w), lambda g, i: (i, g)),
                  pl.BlockSpec((s, w), lambda g, i: (0, ng + g), pipeline_mode=pl.Buffered(1)),
                  pl.BlockSpec((s, w), lambda g, i: (0, 2 * ng + g), pipeline_mode=pl.Buffered(1)),
                  pl.BlockSpec((s, LANES), lambda g, i: (0, 0), pipeline_mode=pl.Buffered(1))],
        out_specs=pl.BlockSpec((t, w), lambda g, i: (i, g)),
        out_shape=jax.ShapeDtypeStruct((s, d), BF16),
        scratch_shapes=[pltpu.VMEM((hg, s // t, HEAD_DIM, t), BF16),
                        pltpu.VMEM((hg, s, LANES), F32),
                        pltpu.VMEM((hg, 1, LANES), F32),
                        pltpu.VMEM((hg, 2, t, t), F32),
                        pltpu.VMEM((hg, t, t), BF16),
                        pltpu.VMEM((hg, HEAD_DIM, t), F32), pltpu.VMEM((hg, 8, t), F32)],
        compiler_params=_params("arbitrary", "arbitrary"),
        name="fox_attention")(qkv, qkv, qkv, ccol)


def _sb_kernel(q_ref, k_ref, v_ref, o_ref, vt_ref, lb_ref, lm_ref, w_ref, acc_ref, *, t, hg):
    i = pl.program_id(1)
    n_sub = t // MXU_WIDTH
    heads = range(hg)

    def head_cols(hh):
        return slice(hh * HEAD_DIM, (hh + 1) * HEAD_DIM)

    @pl.when(i == 0)
    def _():
        def stage(c, carry):
            rows = pl.ds(pl.multiple_of(c * t, t), t)
            for hh in heads:
                vt_ref[hh, c] = v_ref[rows, head_cols(hh)].astype(F32).T.astype(BF16)
            return carry
        lax.fori_loop(0, k_ref.shape[0] // t, stage, None)

    qts = [q_ref[:, head_cols(hh)].astype(F32).T.astype(BF16) for hh in heads]
    acc_ref[...] = jnp.zeros(acc_ref.shape, F32)
    diff = (lax.broadcasted_iota(jnp.int32, (KEY_CHUNK, QUERY_CHUNK), 0)
            - lax.broadcasted_iota(jnp.int32, (KEY_CHUNK, QUERY_CHUNK), 1))
    later = jnp.where(lax.broadcasted_iota(jnp.int32, (MXU_WIDTH, MXU_WIDTH), 1)
                      > lax.broadcasted_iota(jnp.int32, (MXU_WIDTH, MXU_WIDTH), 0),
                      1.0, 0.0).astype(BF16)

    def scores(j, diagonal):
        for c in range(t // KEY_CHUNK):
            k0 = pl.multiple_of(j * t + c * KEY_CHUNK, KEY_CHUNK)
            rows = slice(c * KEY_CHUNK, (c + 1) * KEY_CHUNK)
            for g in range(t // QUERY_CHUNK):
                cols = slice(g * QUERY_CHUNK, (g + 1) * QUERY_CHUNK)
                shift = g * QUERY_CHUNK - c * KEY_CHUNK
                for hh in heads:
                    if diagonal and shift <= -(QUERY_CHUNK - 1):
                        lb_ref[hh, rows, cols] = jnp.full((KEY_CHUNK, QUERY_CHUNK), -jnp.inf, F32)
                        lm_ref[hh, rows, cols] = jnp.zeros((KEY_CHUNK, QUERY_CHUNK), BF16)
                        continue
                    kc = k_ref[pl.ds(k0, KEY_CHUNK), head_cols(hh)]
                    z2 = jnp.dot(kc, qts[hh][:, cols], preferred_element_type=F32)
                    lb2 = jnp.minimum(z2, 0.0) - jnp.log(1.0 + jnp.exp2(_neg_abs(z2))) * LOG2E
                    l1m2 = lb2 - z2
                    if diagonal and shift < KEY_CHUNK:
                        visible = diff < shift
                        lb2 = jnp.where(visible, lb2, -jnp.inf)
                        l1m2 = jnp.where(visible, l1m2, 0.0)
                    lb_ref[hh, rows, cols] = lb2
                    lm_ref[hh, rows, cols] = l1m2.astype(BF16)

    def tile(j, runs, diagonal):
        scores(j, diagonal)
        runs = list(runs)
        for b in reversed(range(n_sub)):
            rows = slice(b * MXU_WIDTH, (b + 1) * MXU_WIDTH)
            totals = [[] for _ in heads]
            for g in range(t // QUERY_CHUNK):
                cols = slice(g * QUERY_CHUNK, (g + 1) * QUERY_CHUNK)
                for hh in heads:
                    lm = lm_ref[hh, rows, cols]
                    within = jnp.dot(later, lm, preferred_element_type=F32)
                    w = jnp.exp2(lb_ref[hh, rows, cols] + within + runs[hh][:, cols])
                    w_ref[hh, rows, cols] = w.astype(BF16)
                    totals[hh].append(within[0:1, :] + lm[0:1, :].astype(F32))
            for hh in heads:
                runs[hh] = runs[hh] + jnp.concatenate(totals[hh], axis=1)
        go = live(runs)
        for hh in heads:
            acc_ref[hh] += jnp.dot(vt_ref[hh, j], w_ref[hh], preferred_element_type=F32)
        return tuple(runs), go

    def live(runs):
        return jnp.max(functools.reduce(jnp.maximum, runs)) > WEIGHT_FLOOR_LOG2

    runs, go = tile(i, tuple(jnp.zeros((1, t), F32) for _ in heads), True)
    lax.while_loop(lambda c: (c[0] >= 0) & c[2],
                   lambda c: (c[0] - 1,) + tile(c[0], c[1], False), (i - 1, runs, go))

    for hh in heads:
        o_ref[:, head_cols(hh)] = acc_ref[hh].T.astype(o_ref.dtype)


def sb_attention(q, kv, *, t=256, hg=4):
    s, d = q.shape
    ng = d // (hg * HEAD_DIM)
    w = hg * HEAD_DIM
    return pl.pallas_call(
        functools.partial(_sb_kernel, t=t, hg=hg),
        grid=(ng, s // t),
        in_specs=[pl.BlockSpec((t, w), lambda g, i: (i, g)),
                  pl.BlockSpec((s, w), lambda g, i: (0, g)),
                  pl.BlockSpec((s, w), lambda g, i: (0, ng + g))],
        out_specs=pl.BlockSpec((t, w), lambda g, i: (i, g)),
        out_shape=jax.ShapeDtypeStruct((s, d), BF16),
        scratch_shapes=[pltpu.VMEM((hg, s // t, HEAD_DIM, t), BF16),
                        pltpu.VMEM((hg, t, t), F32),
                        pltpu.VMEM((hg, t, t), BF16),
                        pltpu.VMEM((hg, t, t), BF16),
                        pltpu.VMEM((hg, HEAD_DIM, t), F32)],
        compiler_params=_params("arbitrary", "arbitrary"),
        name="sb_attention")(q, kv, kv)


def _matmul_residual_kernel(a_ref, w_ref, h_ref, o_ref):
    o_ref[...] = h_ref[...] + jnp.dot(a_ref[...], w_ref[...], preferred_element_type=F32)


def matmul_residual(a, w, layer, h, *, tm=512):
    s, k = a.shape
    n = tn = w.shape[2]
    return pl.pallas_call(
        _matmul_residual_kernel, grid=(s // tm, n // tn),
        in_specs=[pl.BlockSpec((tm, k), lambda i, j: (i, 0)),
                  pl.BlockSpec((None, k, tn), lambda i, j: (layer, 0, j)),
                  pl.BlockSpec((tm, tn), lambda i, j: (i, j))],
        out_specs=pl.BlockSpec((tm, tn), lambda i, j: (i, j)),
        out_shape=jax.ShapeDtypeStruct((s, n), F32),
        compiler_params=_params("parallel", "arbitrary"),
        name="matmul_residual")(a, w, h)


def _ffn_kernel(h_ref, g_ref, wg_ref, wu_ref, wd_ref, gout_ref, o_ref, xn_ref, *, norm_out):
    @pl.when(pl.program_id(1) == 0)
    def _():
        x = h_ref[...]
        xn_ref[...] = _rms_scale(x, g_ref[...]).astype(BF16)
        o_ref[...] = x

    xn = xn_ref[...]
    gate = jnp.dot(xn, wg_ref[...], preferred_element_type=F32)
    up = jnp.dot(xn, wu_ref[...], preferred_element_type=F32)
    act = (gate * jax.nn.sigmoid(gate) * up).astype(BF16)
    o_ref[...] += jnp.dot(act, wd_ref[...], preferred_element_type=F32)

    if norm_out:
        @pl.when(pl.program_id(1) == pl.num_programs(1) - 1)
        def _():
            o_ref[...] = _rms_scale(o_ref[...], gout_ref[...])


def ffn(h, g, wg, wu, wd, layer, g_out, *, norm_out, tm=512, tf=512):
    s, d = h.shape
    dff = wg.shape[2]
    return pl.pallas_call(
        functools.partial(_ffn_kernel, norm_out=norm_out), grid=(s // tm, dff // tf),
        in_specs=[pl.BlockSpec((tm, d), lambda i, f: (i, 0)),
                  pl.BlockSpec((1, d), lambda i, f: (0, 0)),
                  pl.BlockSpec((None, d, tf), lambda i, f: (layer, 0, f)),
                  pl.BlockSpec((None, d, tf), lambda i, f: (layer, 0, f)),
                  pl.BlockSpec((None, tf, d), lambda i, f: (layer, f, 0)),
                  pl.BlockSpec((1, d), lambda i, f: (0, 0))],
        out_specs=pl.BlockSpec((tm, d), lambda i, f: (i, 0)),
        out_shape=jax.ShapeDtypeStruct((s, d), F32),
        scratch_shapes=[pltpu.VMEM((tm, d), BF16)],
        compiler_params=_params("parallel", "arbitrary"),
        name="ffn")(h, g, wg, wu, wd, g_out)


def kernel(x, attn_norm, ffn_norm, a_w_in, a_b_f, a_w_out, kv_norm, w_kv, b_w_q, b_w_out,
           ffn_w_gate, ffn_w_up, ffn_w_down, final_norm):
    b, s, d = x.shape
    assert b == 1 and d == N_HEADS * HEAD_DIM
    depth = attn_norm.shape[0]
    n_a = a_w_in.shape[0]
    h = x.reshape(s, d)
    w_qkv, w_f = cast_split(a_w_in, 3 * d)
    w_out_a, w_kv, w_q_b, w_out_b = (cast_bf16(w) for w in (a_w_out, w_kv[None], b_w_q, b_w_out))
    w_gate, w_up, w_down = (cast_bf16(w) for w in (ffn_w_gate, ffn_w_up, ffn_w_down))
    kv = None
    for layer in range(depth):
        g_attn = attn_norm[layer].reshape(1, d)
        if layer < n_a:
            b_f = jnp.pad(a_b_f[layer], (0, LANES - N_HEADS)).reshape(1, LANES)
            qkv, f_logit = norm_matmul(h, g_attn, w_qkv, layer, w_f, tm=1024,
                                       scaled_cols=d, col_scale=HEAD_DIM ** -0.5 * LOG2E)
            o = fox_attention(qkv, gate_cumsum(f_logit, b_f))
            h = matmul_residual(o, w_out_a, layer, h)
        else:
            j = layer - n_a
            q = norm_matmul(h, g_attn, w_q_b, j, tn=d,
                            scaled_cols=d, col_scale=HEAD_DIM ** -0.5 * LOG2E)
            o = sb_attention(q, kv)
            h = matmul_residual(o, w_out_b, j, h)
        h = ffn(h, ffn_norm[layer].reshape(1, d), w_gate, w_up, w_down, layer,
                final_norm.reshape(1, d), norm_out=layer == depth - 1)
        if layer == n_a - 1:
            kv = norm_matmul(h, kv_norm.reshape(1, d), w_kv, 0, tm=1024)
    return h.reshape(b, s, d)
```

```python
import functools

import jax
import jax.numpy as jnp
from jax import lax
from jax.experimental import pallas as pl
from jax.experimental.pallas import tpu as pltpu

N_HEADS = 16
HEAD_DIM = 128
EPS = 1e-6
LOG2E = 1.4426950408889634
LANES = 128
MXU_WIDTH = 256
VMEM_LIMIT = 56 * 1024 * 1024

KEY_CHUNK = 128
QUERY_CHUNK = MXU_WIDTH
PROB_CHUNK = 32
WEIGHT_FLOOR_LOG2 = -154.0

F32 = jnp.float32
BF16 = jnp.bfloat16


def _params(*sem):
    return pltpu.CompilerParams(dimension_semantics=sem, vmem_limit_bytes=VMEM_LIMIT)


def _rms_scale(x, g):
    ms = jnp.mean(x * x, axis=-1, keepdims=True)
    return x * lax.rsqrt(ms + EPS) * g


def _cast_kernel(x_ref, o_ref):
    o_ref[...] = x_ref[...].astype(o_ref.dtype)


def cast_bf16(w, rows=None, *, tr=256):
    n_layers, r, c = w.shape
    rows = r if rows is None else rows
    return pl.pallas_call(
        _cast_kernel, grid=(n_layers, rows // tr),
        in_specs=[pl.BlockSpec((None, tr, c), lambda l, i: (l, i, 0))],
        out_specs=pl.BlockSpec((None, tr, c), lambda l, i: (l, i, 0)),
        out_shape=jax.ShapeDtypeStruct((n_layers, rows, c), BF16),
        compiler_params=_params("parallel", "parallel"), name="cast_bf16")(w)


def _scaled_tile(acc, n_scaled_tiles, col_scale):
    if n_scaled_tiles == 0:
        return acc
    return acc * jnp.where(pl.program_id(1) < n_scaled_tiles, col_scale, 1.0)


def _project(xn, w, transposed):
    if transposed:
        return lax.dot_general(xn, w, (((1,), (1,)), ((), ())), preferred_element_type=F32)
    return jnp.dot(xn, w, preferred_element_type=F32)


def _norm_matmul_kernel(x_ref, g_ref, w_ref, o_ref, xn_ref, *, n_scaled_tiles, col_scale, transposed):
    @pl.when(pl.program_id(1) == 0)
    def _():
        xn_ref[...] = _rms_scale(x_ref[...], g_ref[...]).astype(BF16)

    acc = _project(xn_ref[...], w_ref[...], transposed)
    o_ref[...] = _scaled_tile(acc, n_scaled_tiles, col_scale).astype(o_ref.dtype)


def _norm_matmul_gate_kernel(x_ref, g_ref, w_ref, wf_ref, o_ref, f_ref, xn_ref,
                             *, n_scaled_tiles, col_scale, transposed):
    @pl.when(pl.program_id(1) == 0)
    def _():
        xn = _rms_scale(x_ref[...], g_ref[...]).astype(BF16)
        xn_ref[...] = xn
        f_ref[...] = _project(xn, wf_ref[...].astype(BF16), transposed)

    acc = _project(xn_ref[...], w_ref[...], transposed)
    o_ref[...] = _scaled_tile(acc, n_scaled_tiles, col_scale).astype(o_ref.dtype)


def norm_matmul(x, g, w, layer, wf=None, *, transposed=False, scaled_cols=0, col_scale=1.0,
                tm=512, tn=1024):
    s, d = x.shape
    n = w.shape[1] if transposed else w.shape[2]
    assert scaled_cols % tn == 0
    static = dict(n_scaled_tiles=scaled_cols // tn, col_scale=col_scale, transposed=transposed)
    grid = (s // tm, n // tn)
    x_spec = pl.BlockSpec((tm, d), lambda i, j: (i, 0))
    g_spec = pl.BlockSpec((1, d), lambda i, j: (0, 0))
    if transposed:
        w_spec = pl.BlockSpec((None, tn, d), lambda i, j: (layer, j, 0))
        wf_spec = pl.BlockSpec((None, LANES, d), lambda i, j: (layer, 0, 0))
    else:
        w_spec = pl.BlockSpec((None, d, tn), lambda i, j: (layer, 0, j))
        wf_spec = pl.BlockSpec((None, d, LANES), lambda i, j: (layer, 0, 0))
    o_spec = pl.BlockSpec((tm, tn), lambda i, j: (i, j))
    scratch = [pltpu.VMEM((tm, d), BF16)]
    if wf is None:
        return pl.pallas_call(
            functools.partial(_norm_matmul_kernel, **static), grid=grid,
            in_specs=[x_spec, g_spec, w_spec], out_specs=o_spec,
            out_shape=jax.ShapeDtypeStruct((s, n), BF16),
            scratch_shapes=scratch, compiler_params=_params("parallel", "arbitrary"),
            name="norm_matmul")(x, g, w)
    f_spec = pl.BlockSpec((tm, LANES), lambda i, j: (i, 0))
    return pl.pallas_call(
        functools.partial(_norm_matmul_gate_kernel, **static), grid=grid,
        in_specs=[x_spec, g_spec, w_spec, wf_spec], out_specs=[o_spec, f_spec],
        out_shape=[jax.ShapeDtypeStruct((s, n), BF16),
                   jax.ShapeDtypeStruct((s, LANES), F32)],
        scratch_shapes=scratch, compiler_params=_params("parallel", "arbitrary"),
        name="norm_matmul_gate")(x, g, w, wf)


def _log_sigmoid(z):
    return jnp.minimum(z, 0.0) - jnp.log(1.0 + jnp.exp(-jnp.abs(z)))


def _split3(x):
    hi = x.astype(BF16)
    r = x - hi.astype(F32)
    mid = r.astype(BF16)
    lo = (r - mid.astype(F32)).astype(BF16)
    return hi, mid, lo


def _gate_cumsum_kernel(f_ref, b_ref, ccol_ref, carry_ref, *, tc):
    @pl.when(pl.program_id(0) == 0)
    def _():
        carry_ref[...] = jnp.zeros_like(carry_ref)

    log_f = _log_sigmoid(f_ref[...] + b_ref[...])
    row = lax.broadcasted_iota(jnp.int32, (tc, tc), 0)
    col = lax.broadcasted_iota(jnp.int32, (tc, tc), 1)
    tri = jnp.where(col <= row, 1.0, 0.0).astype(BF16)
    hi, mid, lo = _split3(log_f)
    c = (jnp.dot(tri, hi, preferred_element_type=F32)
         + jnp.dot(tri, mid, preferred_element_type=F32)
         + jnp.dot(tri, lo, preferred_element_type=F32)) + carry_ref[...]
    ccol_ref[...] = c
    carry_ref[...] = c[tc - 1:tc, :]


def gate_cumsum(f, b, *, tc=512):
    s = f.shape[0]
    return pl.pallas_call(
        functools.partial(_gate_cumsum_kernel, tc=tc), grid=(s // tc,),
        in_specs=[pl.BlockSpec((tc, LANES), lambda i: (i, 0)),
                  pl.BlockSpec((1, LANES), lambda i: (0, 0))],
        out_specs=pl.BlockSpec((tc, LANES), lambda i: (i, 0)),
        out_shape=jax.ShapeDtypeStruct((s, LANES), F32),
        scratch_shapes=[pltpu.VMEM((1, LANES), F32)],
        compiler_params=_params("arbitrary"), name="gate_cumsum")(f, b)


def _qk(q, k):
    return lax.dot_general(q, k, (((1,), (1,)), ((), ())), preferred_element_type=F32)


def _colmax8(x):
    return jnp.max(x.reshape(x.shape[0] // 8, 8, x.shape[1]), axis=0)


def _neg_abs(x):
    bits = lax.bitcast_convert_type(x, jnp.int32) | jnp.int32(-2 ** 31)
    return lax.bitcast_convert_type(bits, F32)


def _colsum8(x):
    return jnp.sum(x.reshape(x.shape[0] // 8, 8, x.shape[1]), axis=0)


def _fox_kernel(q_ref, k_ref, v_ref, ccol_ref, o_ref,
                vt_ref, csb_ref, kn_ref, s_ref, p_ref, acc_ref, lp_ref, *, t, hg):
    grp = pl.program_id(0)
    i = pl.program_id(1)
    n = i + 1
    heads = range(hg)

    def head_cols(hh):
        return slice(hh * HEAD_DIM, (hh + 1) * HEAD_DIM)

    @pl.when(i == 0)
    def _():
        for hh in heads:
            kn_ref[hh] = jnp.zeros((1, LANES), F32)

        def stage(c, carry):
            rows = pl.ds(pl.multiple_of(c * t, t), t)
            lane = lax.broadcasted_iota(jnp.int32, (t, LANES), 1)
            for hh in heads:
                vt_ref[hh, c] = v_ref[rows, head_cols(hh)].astype(F32).T.astype(BF16)
                cs = jnp.sum(jnp.where(lane == grp * hg + hh, ccol_ref[rows, :], 0.0),
                             axis=-1, keepdims=True)
                csb_ref[hh, rows, :] = jnp.broadcast_to(cs * LOG2E, (t, LANES))
                kf = k_ref[rows, head_cols(hh)].astype(F32)
                kn_ref[hh] = jnp.maximum(kn_ref[hh], jnp.max(jnp.sum(kf * kf, axis=-1, keepdims=True)))
            return carry
        lax.fori_loop(0, k_ref.shape[0] // t, stage, None)

    qts = [q_ref[:, head_cols(hh)].astype(F32).T.astype(BF16) for hh in heads]
    reach = [jnp.sqrt(jnp.sum(qts[hh].astype(F32) ** 2, axis=0, keepdims=True) * kn_ref[hh][:, 0:1])
             for hh in heads]
    acc_ref[...] = jnp.zeros(acc_ref.shape, F32)
    lp_ref[...] = jnp.zeros(lp_ref.shape, F32)
    diff = (lax.broadcasted_iota(jnp.int32, (KEY_CHUNK, QUERY_CHUNK), 0)
            - lax.broadcasted_iota(jnp.int32, (KEY_CHUNK, QUERY_CHUNK), 1))

    def scores(j, slot, diagonal):
        mparts = [[jnp.full((8, QUERY_CHUNK), -jnp.inf, F32) for _ in range(t // QUERY_CHUNK)]
                  for _ in heads]
        for c in range(t // KEY_CHUNK):
            k0 = pl.multiple_of(j * t + c * KEY_CHUNK, KEY_CHUNK)
            rows = slice(c * KEY_CHUNK, (c + 1) * KEY_CHUNK)
            for g in range(t // QUERY_CHUNK):
                cols = slice(g * QUERY_CHUNK, (g + 1) * QUERY_CHUNK)
                shift = g * QUERY_CHUNK - c * KEY_CHUNK
                for hh in heads:
                    if diagonal and shift <= -QUERY_CHUNK:
                        s_ref[hh, slot, rows, cols] = jnp.full((KEY_CHUNK, QUERY_CHUNK), -jnp.inf, F32)
                        continue
                    bias = csb_ref[hh, pl.ds(k0, KEY_CHUNK), :]
                    bias = jnp.concatenate([bias] * (QUERY_CHUNK // LANES), axis=1)
                    kc = k_ref[pl.ds(k0, KEY_CHUNK), head_cols(hh)]
                    s = jnp.dot(kc, qts[hh][:, cols], preferred_element_type=F32) - bias
                    if diagonal and shift < KEY_CHUNK - 1:
                        s = jnp.where(diff <= shift, s, -jnp.inf)
                    s_ref[hh, slot, rows, cols] = s
                    mparts[hh][g] = jnp.maximum(mparts[hh][g], _colmax8(s))
        return [jnp.concatenate(mparts[hh], axis=1) for hh in heads]

    def new_max(m_old, mpart):
        m_new = jnp.maximum(m_old, jnp.max(mpart, axis=0, keepdims=True))
        return m_new, jnp.exp2(m_old - m_new)

    def half(step, slot, carry, last=False):
        ms, alphas = carry
        j = i - step
        lsums = [jnp.zeros((8, t), F32) for _ in heads]
        for c in range(t // PROB_CHUNK):
            rows = slice(c * PROB_CHUNK, (c + 1) * PROB_CHUNK)
            for hh in heads:
                p = jnp.exp2(s_ref[hh, slot, rows, :] - ms[hh])
                p_ref[hh, rows, :] = p.astype(BF16)
                lsums[hh] = lsums[hh] + _colsum8(p)
        if not last:
            mparts = scores(j - 1, 1 - slot, False)
            new = [new_max(ms[hh], mparts[hh]) for hh in heads]
            carry = tuple(m for m, _ in new), tuple(a for _, a in new)
        for hh in heads:
            lp_ref[hh] = alphas[hh] * lp_ref[hh] + lsums[hh]
            acc_ref[hh] = alphas[hh] * acc_ref[hh] + jnp.dot(vt_ref[hh, j], p_ref[hh],
                                                            preferred_element_type=F32)
        return carry

    def tile_live(j, ms):
        last_key = jnp.maximum(j + 1, 1) * t - 1
        gaps = [reach[hh] - csb_ref[hh, pl.ds(last_key, 1), :][:, 0:1] - ms[hh] for hh in heads]
        return jnp.max(functools.reduce(jnp.maximum, gaps)) > WEIGHT_FLOOR_LOG2

    first = [new_max(jnp.full((1, t), -jnp.inf, F32), mp) for mp in scores(i, 0, True)]
    carry = tuple(m for m, _ in first), tuple(a for _, a in first)
    n_pairs = (n - 1) // 2

    def pair(c):
        mm, _, ms, alphas = c
        carry = half(2 * mm, 0, (ms, alphas))
        alive = tile_live(i - (2 * mm + 3), carry[0])
        return (mm + 1, alive) + half(2 * mm + 1, 1, carry)

    mm, alive, ms, alphas = lax.while_loop(
        lambda c: (c[0] < n_pairs) & c[1], pair, (jnp.int32(0), tile_live(i - 1, carry[0])) + carry)
    step = 2 * mm
    single = (step == n - 1) | jnp.logical_not(alive)

    @pl.when(single)
    def _():
        half(step, 0, (ms, alphas), last=True)

    @pl.when(jnp.logical_not(single))
    def _():
        half(step + 1, 1, half(step, 0, (ms, alphas)), last=True)

    for hh in heads:
        l = jnp.sum(lp_ref[hh], axis=0, keepdims=True)
        o_ref[:, head_cols(hh)] = (acc_ref[hh] / l).T.astype(o_ref.dtype)


def fox_attention(qkv, ccol, *, t=256, hg=4):
    s = qkv.shape[0]
    d = qkv.shape[1] // 3
    ng = d // (hg * HEAD_DIM)
    w = hg * HEAD_DIM
    return pl.pallas_call(
        functools.partial(_fox_kernel, t=t, hg=hg),
        grid=(ng, s // t),
        in_specs=[pl.BlockSpec((t, w), lambda g, i: (i, g)),
                  pl.BlockSpec((s, w), lambda g, i: (0, ng + g), pipeline_mode=pl.Buffered(1)),
                  pl.BlockSpec((s, w), lambda g, i: (0, 2 * ng + g), pipeline_mode=pl.Buffered(1)),
                  pl.BlockSpec((s, LANES), lambda g, i: (0, 0), pipeline_mode=pl.Buffered(1))],
        out_specs=pl.BlockSpec((t, w), lambda g, i: (i, g)),
        out_shape=jax.ShapeDtypeStruct((s, d), BF16),
        scratch_shapes=[pltpu.VMEM((hg, s // t, HEAD_DIM, t), BF16),
                        pltpu.VMEM((hg, s, LANES), F32),
                        pltpu.VMEM((hg, 1, LANES), F32),
                        pltpu.VMEM((hg, 2, t, t), F32),
                        pltpu.VMEM((hg, t, t), BF16),
                        pltpu.VMEM((hg, HEAD_DIM, t), F32), pltpu.VMEM((hg, 8, t), F32)],
        compiler_params=_params("arbitrary", "arbitrary"),
        name="fox_attention")(qkv, qkv, qkv, ccol)


def _sb_kernel(q_ref, k_ref, v_ref, o_ref, vt_ref, lb_ref, lm_ref, w_ref, acc_ref, *, t, hg):
    i = pl.program_id(1)
    n_sub = t // MXU_WIDTH
    heads = range(hg)

    def head_cols(hh):
        return slice(hh * HEAD_DIM, (hh + 1) * HEAD_DIM)

    @pl.when(i == 0)
    def _():
        def stage(c, carry):
            rows = pl.ds(pl.multiple_of(c * t, t), t)
            for hh in heads:
                vt_ref[hh, c] = v_ref[rows, head_cols(hh)].astype(F32).T.astype(BF16)
            return carry
        lax.fori_loop(0, k_ref.shape[0] // t, stage, None)

    qts = [q_ref[:, head_cols(hh)].astype(F32).T.astype(BF16) for hh in heads]
    acc_ref[...] = jnp.zeros(acc_ref.shape, F32)
    diff = (lax.broadcasted_iota(jnp.int32, (KEY_CHUNK, QUERY_CHUNK), 0)
            - lax.broadcasted_iota(jnp.int32, (KEY_CHUNK, QUERY_CHUNK), 1))
    later = jnp.where(lax.broadcasted_iota(jnp.int32, (MXU_WIDTH, MXU_WIDTH), 1)
                      > lax.broadcasted_iota(jnp.int32, (MXU_WIDTH, MXU_WIDTH), 0),
                      1.0, 0.0).astype(BF16)

    def scores(j, diagonal):
        for c in range(t // KEY_CHUNK):
            k0 = pl.multiple_of(j * t + c * KEY_CHUNK, KEY_CHUNK)
            rows = slice(c * KEY_CHUNK, (c + 1) * KEY_CHUNK)
            for g in range(t // QUERY_CHUNK):
                cols = slice(g * QUERY_CHUNK, (g + 1) * QUERY_CHUNK)
                shift = g * QUERY_CHUNK - c * KEY_CHUNK
                for hh in heads:
                    if diagonal and shift <= -(QUERY_CHUNK - 1):
                        lb_ref[hh, rows, cols] = jnp.full((KEY_CHUNK, QUERY_CHUNK), -jnp.inf, F32)
                        lm_ref[hh, rows, cols] = jnp.zeros((KEY_CHUNK, QUERY_CHUNK), BF16)
                        continue
                    kc = k_ref[pl.ds(k0, KEY_CHUNK), head_cols(hh)]
                    z2 = jnp.dot(kc, qts[hh][:, cols], preferred_element_type=F32)
                    lb2 = jnp.minimum(z2, 0.0) - jnp.log(1.0 + jnp.exp2(_neg_abs(z2))) * LOG2E
                    l1m2 = lb2 - z2
                    if diagonal and shift < KEY_CHUNK:
                        visible = diff < shift
                        lb2 = jnp.where(visible, lb2, -jnp.inf)
                        l1m2 = jnp.where(visible, l1m2, 0.0)
                    lb_ref[hh, rows, cols] = lb2
                    lm_ref[hh, rows, cols] = l1m2.astype(BF16)

    def tile(j, runs, diagonal):
        scores(j, diagonal)
        runs = list(runs)
        for b in reversed(range(n_sub)):
            rows = slice(b * MXU_WIDTH, (b + 1) * MXU_WIDTH)
            totals = [[] for _ in heads]
            for g in range(t // QUERY_CHUNK):
                cols = slice(g * QUERY_CHUNK, (g + 1) * QUERY_CHUNK)
                for hh in heads:
                    lm = lm_ref[hh, rows, cols]
                    within = jnp.dot(later, lm, preferred_element_type=F32)
                    w = jnp.exp2(lb_ref[hh, rows, cols] + within + runs[hh][:, cols])
                    w_ref[hh, rows, cols] = w.astype(BF16)
                    totals[hh].append(within[0:1, :] + lm[0:1, :].astype(F32))
            for hh in heads:
                runs[hh] = runs[hh] + jnp.concatenate(totals[hh], axis=1)
        go = live(runs)
        for hh in heads:
            acc_ref[hh] += jnp.dot(vt_ref[hh, j], w_ref[hh], preferred_element_type=F32)
        return tuple(runs), go

    def live(runs):
        return jnp.max(functools.reduce(jnp.maximum, runs)) > WEIGHT_FLOOR_LOG2

    runs, go = tile(i, tuple(jnp.zeros((1, t), F32) for _ in heads), True)
    lax.while_loop(lambda c: (c[0] >= 0) & c[2],
                   lambda c: (c[0] - 1,) + tile(c[0], c[1], False), (i - 1, runs, go))

    for hh in heads:
        o_ref[:, head_cols(hh)] = acc_ref[hh].T.astype(o_ref.dtype)


def sb_attention(q, kv, *, t=256, hg=4):
    s, d = q.shape
    ng = d // (hg * HEAD_DIM)
    w = hg * HEAD_DIM
    return pl.pallas_call(
        functools.partial(_sb_kernel, t=t, hg=hg),
        grid=(ng, s // t),
        in_specs=[pl.BlockSpec((t, w), lambda g, i: (i, g)),
                  pl.BlockSpec((s, w), lambda g, i: (0, g)),
                  pl.BlockSpec((s, w), lambda g, i: (0, ng + g))],
        out_specs=pl.BlockSpec((t, w), lambda g, i: (i, g)),
        out_shape=jax.ShapeDtypeStruct((s, d), BF16),
        scratch_shapes=[pltpu.VMEM((hg, s // t, HEAD_DIM, t), BF16),
                        pltpu.VMEM((hg, t, t), F32),
                        pltpu.VMEM((hg, t, t), BF16),
                        pltpu.VMEM((hg, t, t), BF16),
                        pltpu.VMEM((hg, HEAD_DIM, t), F32)],
        compiler_params=_params("arbitrary", "arbitrary"),
        name="sb_attention")(q, kv, kv)


def _matmul_residual_kernel(a_ref, w_ref, h_ref, o_ref):
    o_ref[...] = h_ref[...] + jnp.dot(a_ref[...], w_ref[...], preferred_element_type=F32)


def matmul_residual(a, w, layer, h, *, tm=512):
    s, k = a.shape
    n = tn = w.shape[2]
    return pl.pallas_call(
        _matmul_residual_kernel, grid=(s // tm, n // tn),
        in_specs=[pl.BlockSpec((tm, k), lambda i, j: (i, 0)),
                  pl.BlockSpec((None, k, tn), lambda i, j: (layer, 0, j)),
                  pl.BlockSpec((tm, tn), lambda i, j: (i, j))],
        out_specs=pl.BlockSpec((tm, tn), lambda i, j: (i, j)),
        out_shape=jax.ShapeDtypeStruct((s, n), F32),
        compiler_params=_params("parallel", "arbitrary"),
        name="matmul_residual")(a, w, h)


def _ffn_kernel(h_ref, g_ref, wg_ref, wu_ref, wd_ref, gout_ref, o_ref, xn_ref, *, norm_out):
    @pl.when(pl.program_id(1) == 0)
    def _():
        x = h_ref[...]
        xn_ref[...] = _rms_scale(x, g_ref[...]).astype(BF16)
        o_ref[...] = x

    xn = xn_ref[...]
    gate = jnp.dot(xn, wg_ref[...], preferred_element_type=F32)
    up = jnp.dot(xn, wu_ref[...], preferred_element_type=F32)
    act = (gate * jax.nn.sigmoid(gate) * up).astype(BF16)
    o_ref[...] += jnp.dot(act, wd_ref[...], preferred_element_type=F32)

    if norm_out:
        @pl.when(pl.program_id(1) == pl.num_programs(1) - 1)
        def _():
            o_ref[...] = _rms_scale(o_ref[...], gout_ref[...])


def ffn(h, g, wg, wu, wd, layer, g_out, *, norm_out, tm=512, tf=512):
    s, d = h.shape
    dff = wg.shape[2]
    return pl.pallas_call(
        functools.partial(_ffn_kernel, norm_out=norm_out), grid=(s // tm, dff // tf),
        in_specs=[pl.BlockSpec((tm, d), lambda i, f: (i, 0)),
                  pl.BlockSpec((1, d), lambda i, f: (0, 0)),
                  pl.BlockSpec((None, d, tf), lambda i, f: (layer, 0, f)),
                  pl.BlockSpec((None, d, tf), lambda i, f: (layer, 0, f)),
                  pl.BlockSpec((None, tf, d), lambda i, f: (layer, f, 0)),
                  pl.BlockSpec((1, d), lambda i, f: (0, 0))],
        out_specs=pl.BlockSpec((tm, d), lambda i, f: (i, 0)),
        out_shape=jax.ShapeDtypeStruct((s, d), F32),
        scratch_shapes=[pltpu.VMEM((tm, d), BF16)],
        compiler_params=_params("parallel", "arbitrary"),
        name="ffn")(h, g, wg, wu, wd, g_out)


def kernel(x, attn_norm, ffn_norm, a_w_in, a_b_f, a_w_out, kv_norm, w_kv, b_w_q, b_w_out,
           ffn_w_gate, ffn_w_up, ffn_w_down, final_norm):
    b, s, d = x.shape
    assert b == 1 and d == N_HEADS * HEAD_DIM
    depth = attn_norm.shape[0]
    n_a = a_w_in.shape[0]
    h = x.reshape(s, d)
    w_in_t = jnp.swapaxes(a_w_in, 1, 2)
    w_qkv = cast_bf16(w_in_t, 3 * d)
    w_f = jnp.pad(w_in_t[:, 3 * d:], ((0, 0), (0, LANES - N_HEADS), (0, 0)))
    w_out_a, w_kv, w_q_b, w_out_b = (cast_bf16(w) for w in (a_w_out, w_kv[None], b_w_q, b_w_out))
    w_gate, w_up, w_down = (cast_bf16(w) for w in (ffn_w_gate, ffn_w_up, ffn_w_down))
    kv = None
    for layer in range(depth):
        g_attn = attn_norm[layer].reshape(1, d)
        if layer < n_a:
            b_f = jnp.pad(a_b_f[layer], (0, LANES - N_HEADS)).reshape(1, LANES)
            qkv, f_logit = norm_matmul(h, g_attn, w_qkv, layer, w_f, transposed=True, tm=1024,
                                       scaled_cols=d, col_scale=HEAD_DIM ** -0.5 * LOG2E)
            o = fox_attention(qkv, gate_cumsum(f_logit, b_f))
            h = matmul_residual(o, w_out_a, layer, h)
        else:
            j = layer - n_a
            q = norm_matmul(h, g_attn, w_q_b, j, tn=d,
                            scaled_cols=d, col_scale=HEAD_DIM ** -0.5 * LOG2E)
            o = sb_attention(q, kv)
            h = matmul_residual(o, w_out_b, j, h)
        h = ffn(h, ffn_norm[layer].reshape(1, d), w_gate, w_up, w_down, layer,
                final_norm.reshape(1, d), norm_out=layer == depth - 1)
        if layer == n_a - 1:
            kv = norm_matmul(h, kv_norm.reshape(1, d), w_kv, 0, tm=1024)
    return h.reshape(b, s, d)
```

```python
import functools

import jax
import jax.numpy as jnp
from jax import lax
from jax.experimental import pallas as pl
from jax.experimental.pallas import tpu as pltpu

N_HEADS = 16
HEAD_DIM = 128
EPS = 1e-6
LOG2E = 1.4426950408889634
LANES = 128
MXU_WIDTH = 256
VMEM_LIMIT = 56 * 1024 * 1024

KEY_CHUNK = 128
QUERY_CHUNK = MXU_WIDTH
PROB_CHUNK = 32
WEIGHT_FLOOR_LOG2 = -154.0

F32 = jnp.float32
BF16 = jnp.bfloat16


def _params(*sem):
    return pltpu.CompilerParams(dimension_semantics=sem, vmem_limit_bytes=VMEM_LIMIT)


def _rms_scale(x, g):
    ms = jnp.mean(x * x, axis=-1, keepdims=True)
    return x * lax.rsqrt(ms + EPS) * g


def _cast_kernel(x_ref, o_ref):
    o_ref[...] = x_ref[...].astype(o_ref.dtype)


def cast_bf16(w, rows=None, *, tr=256):
    n_layers, r, c = w.shape
    rows = r if rows is None else rows
    return pl.pallas_call(
        _cast_kernel, grid=(n_layers, rows // tr),
        in_specs=[pl.BlockSpec((None, tr, c), lambda l, i: (l, i, 0))],
        out_specs=pl.BlockSpec((None, tr, c), lambda l, i: (l, i, 0)),
        out_shape=jax.ShapeDtypeStruct((n_layers, rows, c), BF16),
        compiler_params=_params("parallel", "parallel"), name="cast_bf16")(w)


def _scaled_tile(acc, n_scaled_tiles, col_scale):
    if n_scaled_tiles == 0:
        return acc
    return acc * jnp.where(pl.program_id(1) < n_scaled_tiles, col_scale, 1.0)


def _project(xn, w, transposed):
    if transposed:
        return lax.dot_general(xn, w, (((1,), (1,)), ((), ())), preferred_element_type=F32)
    return jnp.dot(xn, w, preferred_element_type=F32)


def _norm_matmul_kernel(x_ref, g_ref, w_ref, o_ref, xn_ref, *, n_scaled_tiles, col_scale, transposed):
    @pl.when(pl.program_id(1) == 0)
    def _():
        xn_ref[...] = _rms_scale(x_ref[...], g_ref[...]).astype(BF16)

    acc = _project(xn_ref[...], w_ref[...], transposed)
    o_ref[...] = _scaled_tile(acc, n_scaled_tiles, col_scale).astype(o_ref.dtype)


def _norm_matmul_gate_kernel(x_ref, g_ref, w_ref, wf_ref, o_ref, f_ref, xn_ref,
                             *, n_scaled_tiles, col_scale, transposed):
    @pl.when(pl.program_id(1) == 0)
    def _():
        xn = _rms_scale(x_ref[...], g_ref[...]).astype(BF16)
        xn_ref[...] = xn
        f_ref[...] = _project(xn, wf_ref[...].astype(BF16), transposed)

    acc = _project(xn_ref[...], w_ref[...], transposed)
    o_ref[...] = _scaled_tile(acc, n_scaled_tiles, col_scale).astype(o_ref.dtype)


def norm_matmul(x, g, w, layer, wf=None, *, transposed=False, scaled_cols=0, col_scale=1.0,
                tm=512, tn=1024):
    s, d = x.shape
    n = w.shape[1] if transposed else w.shape[2]
    assert scaled_cols % tn == 0
    static = dict(n_scaled_tiles=scaled_cols // tn, col_scale=col_scale, transposed=transposed)
    grid = (s // tm, n // tn)
    x_spec = pl.BlockSpec((tm, d), lambda i, j: (i, 0))
    g_spec = pl.BlockSpec((1, d), lambda i, j: (0, 0))
    if transposed:
        w_spec = pl.BlockSpec((None, tn, d), lambda i, j: (layer, j, 0))
        wf_spec = pl.BlockSpec((None, LANES, d), lambda i, j: (layer, 0, 0))
    else:
        w_spec = pl.BlockSpec((None, d, tn), lambda i, j: (layer, 0, j))
        wf_spec = pl.BlockSpec((None, d, LANES), lambda i, j: (layer, 0, 0))
    o_spec = pl.BlockSpec((tm, tn), lambda i, j: (i, j))
    scratch = [pltpu.VMEM((tm, d), BF16)]
    if wf is None:
        return pl.pallas_call(
            functools.partial(_norm_matmul_kernel, **static), grid=grid,
            in_specs=[x_spec, g_spec, w_spec], out_specs=o_spec,
            out_shape=jax.ShapeDtypeStruct((s, n), BF16),
            scratch_shapes=scratch, compiler_params=_params("parallel", "arbitrary"),
            name="norm_matmul")(x, g, w)
    f_spec = pl.BlockSpec((tm, LANES), lambda i, j: (i, 0))
    return pl.pallas_call(
        functools.partial(_norm_matmul_gate_kernel, **static), grid=grid,
        in_specs=[x_spec, g_spec, w_spec, wf_spec], out_specs=[o_spec, f_spec],
        out_shape=[jax.ShapeDtypeStruct((s, n), BF16),
                   jax.ShapeDtypeStruct((s, LANES), F32)],
        scratch_shapes=scratch, compiler_params=_params("parallel", "arbitrary"),
        name="norm_matmul_gate")(x, g, w, wf)


def _log_sigmoid(z):
    return jnp.minimum(z, 0.0) - jnp.log(1.0 + jnp.exp(-jnp.abs(z)))


def _split3(x):
    hi = x.astype(BF16)
    r = x - hi.astype(F32)
    mid = r.astype(BF16)
    lo = (r - mid.astype(F32)).astype(BF16)
    return hi, mid, lo


def _gate_cumsum_kernel(f_ref, b_ref, ccol_ref, carry_ref, *, tc):
    @pl.when(pl.program_id(0) == 0)
    def _():
        carry_ref[...] = jnp.zeros_like(carry_ref)

    log_f = _log_sigmoid(f_ref[...] + b_ref[...])
    row = lax.broadcasted_iota(jnp.int32, (tc, tc), 0)
    col = lax.broadcasted_iota(jnp.int32, (tc, tc), 1)
    tri = jnp.where(col <= row, 1.0, 0.0).astype(BF16)
    hi, mid, lo = _split3(log_f)
    c = (jnp.dot(tri, hi, preferred_element_type=F32)
         + jnp.dot(tri, mid, preferred_element_type=F32)
         + jnp.dot(tri, lo, preferred_element_type=F32)) + carry_ref[...]
    ccol_ref[...] = c
    carry_ref[...] = c[tc - 1:tc, :]


def gate_cumsum(f, b, *, tc=512):
    s = f.shape[0]
    return pl.pallas_call(
        functools.partial(_gate_cumsum_kernel, tc=tc), grid=(s // tc,),
        in_specs=[pl.BlockSpec((tc, LANES), lambda i: (i, 0)),
                  pl.BlockSpec((1, LANES), lambda i: (0, 0))],
        out_specs=pl.BlockSpec((tc, LANES), lambda i: (i, 0)),
        out_shape=jax.ShapeDtypeStruct((s, LANES), F32),
        scratch_shapes=[pltpu.VMEM((1, LANES), F32)],
        compiler_params=_params("arbitrary"), name="gate_cumsum")(f, b)


def _qk(q, k):
    return lax.dot_general(q, k, (((1,), (1,)), ((), ())), preferred_element_type=F32)


def _colmax8(x):
    return jnp.max(x.reshape(x.shape[0] // 8, 8, x.shape[1]), axis=0)


def _neg_abs(x):
    bits = lax.bitcast_convert_type(x, jnp.int32) | jnp.int32(-2 ** 31)
    return lax.bitcast_convert_type(bits, F32)


def _colsum8(x):
    return jnp.sum(x.reshape(x.shape[0] // 8, 8, x.shape[1]), axis=0)


def _fox_kernel(q_ref, k_ref, v_ref, ccol_ref, o_ref,
                vt_ref, csb_ref, kn_ref, s_ref, p_ref, acc_ref, lp_ref, *, t, hg):
    grp = pl.program_id(0)
    i = pl.program_id(1)
    n = i + 1
    heads = range(hg)

    def head_cols(hh):
        return slice(hh * HEAD_DIM, (hh + 1) * HEAD_DIM)

    @pl.when(i == 0)
    def _():
        for hh in heads:
            kn_ref[hh] = jnp.zeros((1, LANES), F32)

        def stage(c, carry):
            rows = pl.ds(pl.multiple_of(c * t, t), t)
            lane = lax.broadcasted_iota(jnp.int32, (t, LANES), 1)
            for hh in heads:
                vt_ref[hh, c] = v_ref[rows, head_cols(hh)].astype(F32).T.astype(BF16)
                cs = jnp.sum(jnp.where(lane == grp * hg + hh, ccol_ref[rows, :], 0.0),
                             axis=-1, keepdims=True)
                csb_ref[hh, rows, :] = jnp.broadcast_to(cs * LOG2E, (t, LANES))
                kf = k_ref[rows, head_cols(hh)].astype(F32)
                kn_ref[hh] = jnp.maximum(kn_ref[hh], jnp.max(jnp.sum(kf * kf, axis=-1, keepdims=True)))
            return carry
        lax.fori_loop(0, k_ref.shape[0] // t, stage, None)

    qts = [q_ref[:, head_cols(hh)].astype(F32).T.astype(BF16) for hh in heads]
    reach = [jnp.sqrt(jnp.sum(qts[hh].astype(F32) ** 2, axis=0, keepdims=True) * kn_ref[hh][:, 0:1])
             for hh in heads]
    acc_ref[...] = jnp.zeros(acc_ref.shape, F32)
    lp_ref[...] = jnp.zeros(lp_ref.shape, F32)
    diff = (lax.broadcasted_iota(jnp.int32, (KEY_CHUNK, QUERY_CHUNK), 0)
            - lax.broadcasted_iota(jnp.int32, (KEY_CHUNK, QUERY_CHUNK), 1))

    def scores(j, slot, diagonal):
        mparts = [[jnp.full((8, QUERY_CHUNK), -jnp.inf, F32) for _ in range(t // QUERY_CHUNK)]
                  for _ in heads]
        for c in range(t // KEY_CHUNK):
            k0 = pl.multiple_of(j * t + c * KEY_CHUNK, KEY_CHUNK)
            rows = slice(c * KEY_CHUNK, (c + 1) * KEY_CHUNK)
            for g in range(t // QUERY_CHUNK):
                cols = slice(g * QUERY_CHUNK, (g + 1) * QUERY_CHUNK)
                shift = g * QUERY_CHUNK - c * KEY_CHUNK
                for hh in heads:
                    if diagonal and shift <= -QUERY_CHUNK:
                        s_ref[hh, slot, rows, cols] = jnp.full((KEY_CHUNK, QUERY_CHUNK), -jnp.inf, F32)
                        continue
                    bias = csb_ref[hh, pl.ds(k0, KEY_CHUNK), :]
                    bias = jnp.concatenate([bias] * (QUERY_CHUNK // LANES), axis=1)
                    kc = k_ref[pl.ds(k0, KEY_CHUNK), head_cols(hh)]
                    s = jnp.dot(kc, qts[hh][:, cols], preferred_element_type=F32) - bias
                    if diagonal and shift < KEY_CHUNK - 1:
                        s = jnp.where(diff <= shift, s, -jnp.inf)
                    s_ref[hh, slot, rows, cols] = s
                    mparts[hh][g] = jnp.maximum(mparts[hh][g], _colmax8(s))
        return [jnp.concatenate(mparts[hh], axis=1) for hh in heads]

    def new_max(m_old, mpart):
        m_new = jnp.maximum(m_old, jnp.max(mpart, axis=0, keepdims=True))
        return m_new, jnp.exp2(m_old - m_new)

    def half(step, slot, carry, last=False):
        ms, alphas = carry
        j = i - step
        lsums = [jnp.zeros((8, t), F32) for _ in heads]
        for c in range(t // PROB_CHUNK):
            rows = slice(c * PROB_CHUNK, (c + 1) * PROB_CHUNK)
            for hh in heads:
                p = jnp.exp2(s_ref[hh, slot, rows, :] - ms[hh])
                p_ref[hh, rows, :] = p.astype(BF16)
                lsums[hh] = lsums[hh] + _colsum8(p)
        if not last:
            mparts = scores(j - 1, 1 - slot, False)
            new = [new_max(ms[hh], mparts[hh]) for hh in heads]
            carry = tuple(m for m, _ in new), tuple(a for _, a in new)
        for hh in heads:
            lp_ref[hh] = alphas[hh] * lp_ref[hh] + lsums[hh]
            acc_ref[hh] = alphas[hh] * acc_ref[hh] + jnp.dot(vt_ref[hh, j], p_ref[hh],
                                                            preferred_element_type=F32)
        return carry

    def tile_live(j, ms):
        last_key = jnp.maximum(j + 1, 1) * t - 1
        gaps = [reach[hh] - csb_ref[hh, pl.ds(last_key, 1), :][:, 0:1] - ms[hh] for hh in heads]
        return jnp.max(functools.reduce(jnp.maximum, gaps)) > WEIGHT_FLOOR_LOG2

    first = [new_max(jnp.full((1, t), -jnp.inf, F32), mp) for mp in scores(i, 0, True)]
    carry = tuple(m for m, _ in first), tuple(a for _, a in first)
    n_pairs = (n - 1) // 2

    def pair(c):
        mm, _, ms, alphas = c
        carry = half(2 * mm, 0, (ms, alphas))
        alive = tile_live(i - (2 * mm + 3), carry[0])
        return (mm + 1, alive) + half(2 * mm + 1, 1, carry)

    mm, alive, ms, alphas = lax.while_loop(
        lambda c: (c[0] < n_pairs) & c[1], pair, (jnp.int32(0), tile_live(i - 1, carry[0])) + carry)
    step = 2 * mm
    single = (step == n - 1) | jnp.logical_not(alive)

    @pl.when(single)
    def _():
        half(step, 0, (ms, alphas), last=True)

    @pl.when(jnp.logical_not(single))
    def _():
        half(step + 1, 1, half(step, 0, (ms, alphas)), last=True)

    for hh in heads:
        l = jnp.sum(lp_ref[hh], axis=0, keepdims=True)
        o_ref[:, head_cols(hh)] = (acc_ref[hh] / l).T.astype(o_ref.dtype)


def fox_attention(qkv, ccol, *, t=256, hg=4):
    s = qkv.shape[0]
    d = qkv.shape[1] // 3
    ng = d // (hg * HEAD_DIM)
    w = hg * HEAD_DIM
    return pl.pallas_call(
        functools.partial(_fox_kernel, t=t, hg=hg),
        grid=(ng, s // t),
        in_specs=[pl.BlockSpec((t, w), lambda g, i: (i, g)),
                  pl.BlockSpec((s, w), lambda g, i: (0, ng + g), pipeline_mode=pl.Buffered(1)),
                  pl.BlockSpec((s, w), lambda g, i: (0, 2 * ng + g), pipeline_mode=pl.Buffered(1)),
                  pl.BlockSpec((s, LANES), lambda g, i: (0, 0), pipeline_mode=pl.Buffered(1))],
        out_specs=pl.BlockSpec((t, w), lambda g, i: (i, g)),
        out_shape=jax.ShapeDtypeStruct((s, d), BF16),
        scratch_shapes=[pltpu.VMEM((hg, s // t, HEAD_DIM, t), BF16),
                        pltpu.VMEM((hg, s, LANES), F32),
                        pltpu.VMEM((hg, 1, LANES), F32),
                        pltpu.VMEM((hg, 2, t, t), F32),
                        pltpu.VMEM((hg, t, t), BF16),
                        pltpu.VMEM((hg, HEAD_DIM, t), F32), pltpu.VMEM((hg, 8, t), F32)],
        compiler_params=_params("arbitrary", "arbitrary"),
        name="fox_attention")(qkv, qkv, qkv, ccol)


def _sb_kernel(q_ref, k_ref, v_ref, o_ref, vt_ref, lb_ref, lm_ref, w_ref, acc_ref, *, t, hg):
    i = pl.program_id(1)
    n_sub = t // MXU_WIDTH
    heads = range(hg)

    def head_cols(hh):
        return slice(hh * HEAD_DIM, (hh + 1) * HEAD_DIM)

    @pl.when(i == 0)
    def _():
        def stage(c, carry):
            rows = pl.ds(pl.multiple_of(c * t, t), t)
            for hh in heads:
                vt_ref[hh, c] = v_ref[rows, head_cols(hh)].astype(F32).T.astype(BF16)
            return carry
        lax.fori_loop(0, k_ref.shape[0] // t, stage, None)

    qts = [q_ref[:, head_cols(hh)].astype(F32).T.astype(BF16) for hh in heads]
    acc_ref[...] = jnp.zeros(acc_ref.shape, F32)
    diff = (lax.broadcasted_iota(jnp.int32, (KEY_CHUNK, QUERY_CHUNK), 0)
            - lax.broadcasted_iota(jnp.int32, (KEY_CHUNK, QUERY_CHUNK), 1))
    later = jnp.where(lax.broadcasted_iota(jnp.int32, (MXU_WIDTH, MXU_WIDTH), 1)
                      > lax.broadcasted_iota(jnp.int32, (MXU_WIDTH, MXU_WIDTH), 0),
                      1.0, 0.0).astype(BF16)

    def scores(j, slot, diagonal):
        for c in range(t // KEY_CHUNK):
            k0 = pl.multiple_of(j * t + c * KEY_CHUNK, KEY_CHUNK)
            rows = slice(c * KEY_CHUNK, (c + 1) * KEY_CHUNK)
            for g in range(t // QUERY_CHUNK):
                cols = slice(g * QUERY_CHUNK, (g + 1) * QUERY_CHUNK)
                shift = g * QUERY_CHUNK - c * KEY_CHUNK
                for hh in heads:
                    if diagonal and shift <= -(QUERY_CHUNK - 1):
                        lb_ref[hh, slot, rows, cols] = jnp.full((KEY_CHUNK, QUERY_CHUNK), -jnp.inf, F32)
                        lm_ref[hh, slot, rows, cols] = jnp.zeros((KEY_CHUNK, QUERY_CHUNK), BF16)
                        continue
                    kc = k_ref[pl.ds(k0, KEY_CHUNK), head_cols(hh)]
                    z2 = jnp.dot(kc, qts[hh][:, cols], preferred_element_type=F32)
                    lb2 = jnp.minimum(z2, 0.0) - jnp.log(1.0 + jnp.exp2(_neg_abs(z2))) * LOG2E
                    l1m2 = lb2 - z2
                    if diagonal and shift < KEY_CHUNK:
                        visible = diff < shift
                        lb2 = jnp.where(visible, lb2, -jnp.inf)
                        l1m2 = jnp.where(visible, l1m2, 0.0)
                    lb_ref[hh, slot, rows, cols] = lb2
                    lm_ref[hh, slot, rows, cols] = l1m2.astype(BF16)

    def weights(slot, runs):
        runs = list(runs)
        for b in reversed(range(n_sub)):
            rows = slice(b * MXU_WIDTH, (b + 1) * MXU_WIDTH)
            totals = [[] for _ in heads]
            for g in range(t // QUERY_CHUNK):
                cols = slice(g * QUERY_CHUNK, (g + 1) * QUERY_CHUNK)
                for hh in heads:
                    lm = lm_ref[hh, slot, rows, cols]
                    within = jnp.dot(later, lm, preferred_element_type=F32)
                    w = jnp.exp2(lb_ref[hh, slot, rows, cols] + within + runs[hh][:, cols])
                    w_ref[hh, slot, rows, cols] = w.astype(BF16)
                    totals[hh].append(within[0:1, :] + lm[0:1, :].astype(F32))
            for hh in heads:
                runs[hh] = runs[hh] + jnp.concatenate(totals[hh], axis=1)
        return tuple(runs)

    def value_product(j, slot):
        for hh in heads:
            acc_ref[hh] += jnp.dot(vt_ref[hh, j], w_ref[hh, slot], preferred_element_type=F32)

    def live(runs):
        return jnp.max(functools.reduce(jnp.maximum, runs)) > WEIGHT_FLOOR_LOG2

    def tile(j, runs):
        scores(j, 0, False)
        runs = weights(0, runs)
        go = live(runs)
        value_product(j, 0)
        return runs, go

    start = tuple(jnp.zeros((1, t), F32) for _ in heads)

    @pl.when(i == 0)
    def _():
        scores(0, 0, True)
        weights(0, start)
        value_product(0, 0)

    @pl.when(i > 0)
    def _():
        scores(i, 0, True)
        scores(i - 1, 1, False)
        runs = weights(1, weights(0, start))
        go = live(runs)
        value_product(i, 0)
        value_product(i - 1, 1)
        lax.while_loop(lambda c: (c[0] >= 0) & c[2],
                       lambda c: (c[0] - 1,) + tile(c[0], c[1]), (i - 2, runs, go))

    for hh in heads:
        o_ref[:, head_cols(hh)] = acc_ref[hh].T.astype(o_ref.dtype)


def sb_attention(q, kv, *, t=256, hg=4):
    s, d = q.shape
    ng = d // (hg * HEAD_DIM)
    w = hg * HEAD_DIM
    return pl.pallas_call(
        functools.partial(_sb_kernel, t=t, hg=hg),
        grid=(ng, s // t),
        in_specs=[pl.BlockSpec((t, w), lambda g, i: (i, g)),
                  pl.BlockSpec((s, w), lambda g, i: (0, g)),
                  pl.BlockSpec((s, w), lambda g, i: (0, ng + g))],
        out_specs=pl.BlockSpec((t, w), lambda g, i: (i, g)),
        out_shape=jax.ShapeDtypeStruct((s, d), BF16),
        scratch_shapes=[pltpu.VMEM((hg, s // t, HEAD_DIM, t), BF16),
                        pltpu.VMEM((hg, 2, t, t), F32),
                        pltpu.VMEM((hg, 2, t, t), BF16),
                        pltpu.VMEM((hg, 2, t, t), BF16),
                        pltpu.VMEM((hg, HEAD_DIM, t), F32)],
        compiler_params=_params("arbitrary", "arbitrary"),
        name="sb_attention")(q, kv, kv)


def _matmul_residual_kernel(a_ref, w_ref, h_ref, o_ref):
    o_ref[...] = h_ref[...] + jnp.dot(a_ref[...], w_ref[...], preferred_element_type=F32)


def matmul_residual(a, w, layer, h, *, tm=512):
    s, k = a.shape
    n = tn = w.shape[2]
    return pl.pallas_call(
        _matmul_residual_kernel, grid=(s // tm, n // tn),
        in_specs=[pl.BlockSpec((tm, k), lambda i, j: (i, 0)),
                  pl.BlockSpec((None, k, tn), lambda i, j: (layer, 0, j)),
                  pl.BlockSpec((tm, tn), lambda i, j: (i, j))],
        out_specs=pl.BlockSpec((tm, tn), lambda i, j: (i, j)),
        out_shape=jax.ShapeDtypeStruct((s, n), F32),
        compiler_params=_params("parallel", "arbitrary"),
        name="matmul_residual")(a, w, h)


def _ffn_kernel(h_ref, g_ref, wg_ref, wu_ref, wd_ref, gout_ref, o_ref, xn_ref, *, norm_out):
    @pl.when(pl.program_id(1) == 0)
    def _():
        x = h_ref[...]
        xn_ref[...] = _rms_scale(x, g_ref[...]).astype(BF16)
        o_ref[...] = x

    xn = xn_ref[...]
    gate = jnp.dot(xn, wg_ref[...], preferred_element_type=F32)
    up = jnp.dot(xn, wu_ref[...], preferred_element_type=F32)
    act = (gate * jax.nn.sigmoid(gate) * up).astype(BF16)
    o_ref[...] += jnp.dot(act, wd_ref[...], preferred_element_type=F32)

    if norm_out:
        @pl.when(pl.program_id(1) == pl.num_programs(1) - 1)
        def _():
            o_ref[...] = _rms_scale(o_ref[...], gout_ref[...])


def ffn(h, g, wg, wu, wd, layer, g_out, *, norm_out, tm=512, tf=512):
    s, d = h.shape
    dff = wg.shape[2]
    return pl.pallas_call(
        functools.partial(_ffn_kernel, norm_out=norm_out), grid=(s // tm, dff // tf),
        in_specs=[pl.BlockSpec((tm, d), lambda i, f: (i, 0)),
                  pl.BlockSpec((1, d), lambda i, f: (0, 0)),
                  pl.BlockSpec((None, d, tf), lambda i, f: (layer, 0, f)),
                  pl.BlockSpec((None, d, tf), lambda i, f: (layer, 0, f)),
                  pl.BlockSpec((None, tf, d), lambda i, f: (layer, f, 0)),
                  pl.BlockSpec((1, d), lambda i, f: (0, 0))],
        out_specs=pl.BlockSpec((tm, d), lambda i, f: (i, 0)),
        out_shape=jax.ShapeDtypeStruct((s, d), F32),
        scratch_shapes=[pltpu.VMEM((tm, d), BF16)],
        compiler_params=_params("parallel", "arbitrary"),
        name="ffn")(h, g, wg, wu, wd, g_out)


def kernel(x, attn_norm, ffn_norm, a_w_in, a_b_f, a_w_out, kv_norm, w_kv, b_w_q, b_w_out,
           ffn_w_gate, ffn_w_up, ffn_w_down, final_norm):
    b, s, d = x.shape
    assert b == 1 and d == N_HEADS * HEAD_DIM
    depth = attn_norm.shape[0]
    n_a = a_w_in.shape[0]
    h = x.reshape(s, d)
    w_in_t = jnp.swapaxes(a_w_in, 1, 2)
    w_qkv = cast_bf16(w_in_t, 3 * d)
    w_f = jnp.pad(w_in_t[:, 3 * d:], ((0, 0), (0, LANES - N_HEADS), (0, 0)))
    w_out_a, w_kv, w_q_b, w_out_b = (cast_bf16(w) for w in (a_w_out, w_kv[None], b_w_q, b_w_out))
    w_gate, w_up, w_down = (cast_bf16(w) for w in (ffn_w_gate, ffn_w_up, ffn_w_down))
    kv = None
    for layer in range(depth):
        g_attn = attn_norm[layer].reshape(1, d)
        if layer < n_a:
            b_f = jnp.pad(a_b_f[layer], (0, LANES - N_HEADS)).reshape(1, LANES)
            qkv, f_logit = norm_matmul(h, g_attn, w_qkv, layer, w_f, transposed=True, tm=1024,
                                       scaled_cols=d, col_scale=HEAD_DIM ** -0.5 * LOG2E)
            o = fox_attention(qkv, gate_cumsum(f_logit, b_f))
            h = matmul_residual(o, w_out_a, layer, h)
        else:
            j = layer - n_a
            q = norm_matmul(h, g_attn, w_q_b, j, tn=d,
                            scaled_cols=d, col_scale=HEAD_DIM ** -0.5 * LOG2E)
            o = sb_attention(q, kv)
            h = matmul_residual(o, w_out_b, j, h)
        h = ffn(h, ffn_norm[layer].reshape(1, d), w_gate, w_up, w_down, layer,
                final_norm.reshape(1, d), norm_out=layer == depth - 1)
        if layer == n_a - 1:
            kv = norm_matmul(h, kv_norm.reshape(1, d), w_kv, 0, tm=1024)
    return h.reshape(b, s, d)
```

```python
import functools

import jax
import jax.numpy as jnp
from jax import lax
from jax.experimental import pallas as pl
from jax.experimental.pallas import tpu as pltpu

N_HEADS = 16
HEAD_DIM = 128
EPS = 1e-6
LOG2E = 1.4426950408889634
LANES = 128
MXU_WIDTH = 256
VMEM_LIMIT = 56 * 1024 * 1024

KEY_CHUNK = 128
QUERY_CHUNK = MXU_WIDTH
PROB_CHUNK = 32
WEIGHT_FLOOR_LOG2 = -154.0

F32 = jnp.float32
BF16 = jnp.bfloat16


def _params(*sem):
    return pltpu.CompilerParams(dimension_semantics=sem, vmem_limit_bytes=VMEM_LIMIT)


def _rms_scale(x, g):
    ms = jnp.mean(x * x, axis=-1, keepdims=True)
    return x * lax.rsqrt(ms + EPS) * g


def _cast_kernel(x_ref, o_ref):
    o_ref[...] = x_ref[...].astype(o_ref.dtype)


def cast_bf16(w, rows=None, *, tr=256):
    n_layers, r, c = w.shape
    rows = r if rows is None else rows
    return pl.pallas_call(
        _cast_kernel, grid=(n_layers, rows // tr),
        in_specs=[pl.BlockSpec((None, tr, c), lambda l, i: (l, i, 0))],
        out_specs=pl.BlockSpec((None, tr, c), lambda l, i: (l, i, 0)),
        out_shape=jax.ShapeDtypeStruct((n_layers, rows, c), BF16),
        compiler_params=_params("parallel", "parallel"), name="cast_bf16")(w)


def _scaled_tile(acc, n_scaled_tiles, col_scale):
    if n_scaled_tiles == 0:
        return acc
    return acc * jnp.where(pl.program_id(1) < n_scaled_tiles, col_scale, 1.0)


def _project(xn, w, transposed):
    if transposed:
        return lax.dot_general(xn, w, (((1,), (1,)), ((), ())), preferred_element_type=F32)
    return jnp.dot(xn, w, preferred_element_type=F32)


def _norm_matmul_kernel(x_ref, g_ref, w_ref, o_ref, xn_ref, *, n_scaled_tiles, col_scale, transposed):
    @pl.when(pl.program_id(1) == 0)
    def _():
        xn_ref[...] = _rms_scale(x_ref[...], g_ref[...]).astype(BF16)

    acc = _project(xn_ref[...], w_ref[...], transposed)
    o_ref[...] = _scaled_tile(acc, n_scaled_tiles, col_scale).astype(o_ref.dtype)


def _norm_matmul_gate_kernel(x_ref, g_ref, w_ref, wf_ref, o_ref, f_ref, xn_ref,
                             *, n_scaled_tiles, col_scale, transposed):
    @pl.when(pl.program_id(1) == 0)
    def _():
        xn = _rms_scale(x_ref[...], g_ref[...]).astype(BF16)
        xn_ref[...] = xn
        f_ref[...] = _project(xn, wf_ref[...].astype(BF16), transposed)

    acc = _project(xn_ref[...], w_ref[...], transposed)
    o_ref[...] = _scaled_tile(acc, n_scaled_tiles, col_scale).astype(o_ref.dtype)


def norm_matmul(x, g, w, layer, wf=None, *, transposed=False, scaled_cols=0, col_scale=1.0,
                tm=512, tn=1024):
    s, d = x.shape
    n = w.shape[1] if transposed else w.shape[2]
    assert scaled_cols % tn == 0
    static = dict(n_scaled_tiles=scaled_cols // tn, col_scale=col_scale, transposed=transposed)
    grid = (s // tm, n // tn)
    x_spec = pl.BlockSpec((tm, d), lambda i, j: (i, 0))
    g_spec = pl.BlockSpec((1, d), lambda i, j: (0, 0))
    if transposed:
        w_spec = pl.BlockSpec((None, tn, d), lambda i, j: (layer, j, 0))
        wf_spec = pl.BlockSpec((None, LANES, d), lambda i, j: (layer, 0, 0))
    else:
        w_spec = pl.BlockSpec((None, d, tn), lambda i, j: (layer, 0, j))
        wf_spec = pl.BlockSpec((None, d, LANES), lambda i, j: (layer, 0, 0))
    o_spec = pl.BlockSpec((tm, tn), lambda i, j: (i, j))
    scratch = [pltpu.VMEM((tm, d), BF16)]
    if wf is None:
        return pl.pallas_call(
            functools.partial(_norm_matmul_kernel, **static), grid=grid,
            in_specs=[x_spec, g_spec, w_spec], out_specs=o_spec,
            out_shape=jax.ShapeDtypeStruct((s, n), BF16),
            scratch_shapes=scratch, compiler_params=_params("parallel", "arbitrary"),
            name="norm_matmul")(x, g, w)
    f_spec = pl.BlockSpec((tm, LANES), lambda i, j: (i, 0))
    return pl.pallas_call(
        functools.partial(_norm_matmul_gate_kernel, **static), grid=grid,
        in_specs=[x_spec, g_spec, w_spec, wf_spec], out_specs=[o_spec, f_spec],
        out_shape=[jax.ShapeDtypeStruct((s, n), BF16),
                   jax.ShapeDtypeStruct((s, LANES), F32)],
        scratch_shapes=scratch, compiler_params=_params("parallel", "arbitrary"),
        name="norm_matmul_gate")(x, g, w, wf)


def _log_sigmoid(z):
    return jnp.minimum(z, 0.0) - jnp.log(1.0 + jnp.exp(-jnp.abs(z)))


def _split3(x):
    hi = x.astype(BF16)
    r = x - hi.astype(F32)
    mid = r.astype(BF16)
    lo = (r - mid.astype(F32)).astype(BF16)
    return hi, mid, lo


def _gate_cumsum_kernel(f_ref, b_ref, ccol_ref, carry_ref, *, tc):
    @pl.when(pl.program_id(0) == 0)
    def _():
        carry_ref[...] = jnp.zeros_like(carry_ref)

    log_f = _log_sigmoid(f_ref[...] + b_ref[...])
    row = lax.broadcasted_iota(jnp.int32, (tc, tc), 0)
    col = lax.broadcasted_iota(jnp.int32, (tc, tc), 1)
    tri = jnp.where(col <= row, 1.0, 0.0).astype(BF16)
    hi, mid, lo = _split3(log_f)
    c = (jnp.dot(tri, hi, preferred_element_type=F32)
         + jnp.dot(tri, mid, preferred_element_type=F32)
         + jnp.dot(tri, lo, preferred_element_type=F32)) + carry_ref[...]
    ccol_ref[...] = c
    carry_ref[...] = c[tc - 1:tc, :]


def gate_cumsum(f, b, *, tc=512):
    s = f.shape[0]
    return pl.pallas_call(
        functools.partial(_gate_cumsum_kernel, tc=tc), grid=(s // tc,),
        in_specs=[pl.BlockSpec((tc, LANES), lambda i: (i, 0)),
                  pl.BlockSpec((1, LANES), lambda i: (0, 0))],
        out_specs=pl.BlockSpec((tc, LANES), lambda i: (i, 0)),
        out_shape=jax.ShapeDtypeStruct((s, LANES), F32),
        scratch_shapes=[pltpu.VMEM((1, LANES), F32)],
        compiler_params=_params("arbitrary"), name="gate_cumsum")(f, b)


def _qk(q, k):
    return lax.dot_general(q, k, (((1,), (1,)), ((), ())), preferred_element_type=F32)


def _colmax8(x):
    return jnp.max(x.reshape(x.shape[0] // 8, 8, x.shape[1]), axis=0)


def _neg_abs(x):
    bits = lax.bitcast_convert_type(x, jnp.int32) | jnp.int32(-2 ** 31)
    return lax.bitcast_convert_type(bits, F32)


def _colsum8(x):
    return jnp.sum(x.reshape(x.shape[0] // 8, 8, x.shape[1]), axis=0)


def _fox_kernel(q_ref, k_ref, v_ref, ccol_ref, o_ref,
                vt_ref, csb_ref, kn_ref, s_ref, p_ref, acc_ref, lp_ref, *, t, hg):
    grp = pl.program_id(0)
    i = pl.program_id(1)
    n = i + 1
    heads = range(hg)

    def head_cols(hh):
        return slice(hh * HEAD_DIM, (hh + 1) * HEAD_DIM)

    @pl.when(i == 0)
    def _():
        for hh in heads:
            kn_ref[hh] = jnp.zeros((1, LANES), F32)

        def stage(c, carry):
            rows = pl.ds(pl.multiple_of(c * t, t), t)
            lane = lax.broadcasted_iota(jnp.int32, (t, LANES), 1)
            for hh in heads:
                vt_ref[hh, c] = v_ref[rows, head_cols(hh)].astype(F32).T.astype(BF16)
                cs = jnp.sum(jnp.where(lane == grp * hg + hh, ccol_ref[rows, :], 0.0),
                             axis=-1, keepdims=True)
                csb_ref[hh, rows, :] = jnp.broadcast_to(cs * LOG2E, (t, LANES))
                kf = k_ref[rows, head_cols(hh)].astype(F32)
                kn_ref[hh] = jnp.maximum(kn_ref[hh], jnp.max(jnp.sum(kf * kf, axis=-1, keepdims=True)))
            return carry
        lax.fori_loop(0, k_ref.shape[0] // t, stage, None)

    qts = [q_ref[:, head_cols(hh)].astype(F32).T.astype(BF16) for hh in heads]
    reach = [jnp.sqrt(jnp.sum(qts[hh].astype(F32) ** 2, axis=0, keepdims=True) * kn_ref[hh][:, 0:1])
             for hh in heads]
    acc_ref[...] = jnp.zeros(acc_ref.shape, F32)
    lp_ref[...] = jnp.zeros(lp_ref.shape, F32)
    diff = (lax.broadcasted_iota(jnp.int32, (KEY_CHUNK, QUERY_CHUNK), 0)
            - lax.broadcasted_iota(jnp.int32, (KEY_CHUNK, QUERY_CHUNK), 1))

    def scores(j, slot, diagonal):
        mparts = [[jnp.full((8, QUERY_CHUNK), -jnp.inf, F32) for _ in range(t // QUERY_CHUNK)]
                  for _ in heads]
        for c in range(t // KEY_CHUNK):
            k0 = pl.multiple_of(j * t + c * KEY_CHUNK, KEY_CHUNK)
            rows = slice(c * KEY_CHUNK, (c + 1) * KEY_CHUNK)
            for g in range(t // QUERY_CHUNK):
                cols = slice(g * QUERY_CHUNK, (g + 1) * QUERY_CHUNK)
                shift = g * QUERY_CHUNK - c * KEY_CHUNK
                for hh in heads:
                    if diagonal and shift <= -QUERY_CHUNK:
                        s_ref[hh, slot, rows, cols] = jnp.full((KEY_CHUNK, QUERY_CHUNK), -jnp.inf, F32)
                        continue
                    bias = csb_ref[hh, pl.ds(k0, KEY_CHUNK), :]
                    bias = jnp.concatenate([bias] * (QUERY_CHUNK // LANES), axis=1)
                    kc = k_ref[pl.ds(k0, KEY_CHUNK), head_cols(hh)]
                    s = jnp.dot(kc, qts[hh][:, cols], preferred_element_type=F32) - bias
                    if diagonal and shift < KEY_CHUNK - 1:
                        s = jnp.where(diff <= shift, s, -jnp.inf)
                    s_ref[hh, slot, rows, cols] = s
                    mparts[hh][g] = jnp.maximum(mparts[hh][g], _colmax8(s))
        return [jnp.concatenate(mparts[hh], axis=1) for hh in heads]

    def new_max(m_old, mpart):
        m_new = jnp.maximum(m_old, jnp.max(mpart, axis=0, keepdims=True))
        return m_new, jnp.exp2(m_old - m_new)

    def half(step, slot, carry, last=False):
        ms, alphas = carry
        j = i - step
        lsums = [jnp.zeros((8, t), F32) for _ in heads]
        for c in range(t // PROB_CHUNK):
            rows = slice(c * PROB_CHUNK, (c + 1) * PROB_CHUNK)
            for hh in heads:
                p = jnp.exp2(s_ref[hh, slot, rows, :] - ms[hh])
                p_ref[hh, rows, :] = p.astype(BF16)
                lsums[hh] = lsums[hh] + _colsum8(p)
        if not last:
            mparts = scores(j - 1, 1 - slot, False)
            new = [new_max(ms[hh], mparts[hh]) for hh in heads]
            carry = tuple(m for m, _ in new), tuple(a for _, a in new)
        for hh in heads:
            lp_ref[hh] = alphas[hh] * lp_ref[hh] + lsums[hh]
            acc_ref[hh] = alphas[hh] * acc_ref[hh] + jnp.dot(vt_ref[hh, j], p_ref[hh],
                                                            preferred_element_type=F32)
        return carry

    def tile_live(j, ms):
        last_key = jnp.maximum(j + 1, 1) * t - 1
        gaps = [reach[hh] - csb_ref[hh, pl.ds(last_key, 1), :][:, 0:1] - ms[hh] for hh in heads]
        return jnp.max(functools.reduce(jnp.maximum, gaps)) > WEIGHT_FLOOR_LOG2

    first = [new_max(jnp.full((1, t), -jnp.inf, F32), mp) for mp in scores(i, 0, True)]
    carry = tuple(m for m, _ in first), tuple(a for _, a in first)
    n_pairs = (n - 1) // 2

    def pair(c):
        mm, _, ms, alphas = c
        carry = half(2 * mm, 0, (ms, alphas))
        alive = tile_live(i - (2 * mm + 3), carry[0])
        return (mm + 1, alive) + half(2 * mm + 1, 1, carry)

    mm, alive, ms, alphas = lax.while_loop(
        lambda c: (c[0] < n_pairs) & c[1], pair, (jnp.int32(0), tile_live(i - 1, carry[0])) + carry)
    step = 2 * mm
    single = (step == n - 1) | jnp.logical_not(alive)

    @pl.when(single)
    def _():
        half(step, 0, (ms, alphas), last=True)

    @pl.when(jnp.logical_not(single))
    def _():
        half(step + 1, 1, half(step, 0, (ms, alphas)), last=True)

    for hh in heads:
        l = jnp.sum(lp_ref[hh], axis=0, keepdims=True)
        o_ref[:, head_cols(hh)] = (acc_ref[hh] / l).T.astype(o_ref.dtype)


def fox_attention(qkv, ccol, *, t=256, hg=4):
    s = qkv.shape[0]
    d = qkv.shape[1] // 3
    ng = d // (hg * HEAD_DIM)
    w = hg * HEAD_DIM
    return pl.pallas_call(
        functools.partial(_fox_kernel, t=t, hg=hg),
        grid=(ng, s // t),
        in_specs=[pl.BlockSpec((t, w), lambda g, i: (i, g)),
                  pl.BlockSpec((s, w), lambda g, i: (0, ng + g), pipeline_mode=pl.Buffered(1)),
                  pl.BlockSpec((s, w), lambda g, i: (0, 2 * ng + g), pipeline_mode=pl.Buffered(1)),
                  pl.BlockSpec((s, LANES), lambda g, i: (0, 0), pipeline_mode=pl.Buffered(1))],
        out_specs=pl.BlockSpec((t, w), lambda g, i: (i, g)),
        out_shape=jax.ShapeDtypeStruct((s, d), BF16),
        scratch_shapes=[pltpu.VMEM((hg, s // t, HEAD_DIM, t), BF16),
                        pltpu.VMEM((hg, s, LANES), F32),
                        pltpu.VMEM((hg, 1, LANES), F32),
                        pltpu.VMEM((hg, 2, t, t), F32),
                        pltpu.VMEM((hg, t, t), BF16),
                        pltpu.VMEM((hg, HEAD_DIM, t), F32), pltpu.VMEM((hg, 8, t), F32)],
        compiler_params=_params("arbitrary", "arbitrary"),
        name="fox_attention")(qkv, qkv, qkv, ccol)


def _sb_kernel(q_ref, k_ref, v_ref, o_ref, vt_ref, lb_ref, lm_ref, w_ref, acc_ref, *, t, hg):
    i = pl.program_id(1)
    n_sub = t // MXU_WIDTH
    heads = range(hg)

    def head_cols(hh):
        return slice(hh * HEAD_DIM, (hh + 1) * HEAD_DIM)

    @pl.when(i == 0)
    def _():
        def stage(c, carry):
            rows = pl.ds(pl.multiple_of(c * t, t), t)
            for hh in heads:
                vt_ref[hh, c] = v_ref[rows, head_cols(hh)].astype(F32).T.astype(BF16)
            return carry
        lax.fori_loop(0, k_ref.shape[0] // t, stage, None)

    qts = [q_ref[:, head_cols(hh)].astype(F32).T.astype(BF16) for hh in heads]
    acc_ref[...] = jnp.zeros(acc_ref.shape, F32)
    diff = (lax.broadcasted_iota(jnp.int32, (KEY_CHUNK, QUERY_CHUNK), 0)
            - lax.broadcasted_iota(jnp.int32, (KEY_CHUNK, QUERY_CHUNK), 1))
    later = jnp.where(lax.broadcasted_iota(jnp.int32, (MXU_WIDTH, MXU_WIDTH), 1)
                      > lax.broadcasted_iota(jnp.int32, (MXU_WIDTH, MXU_WIDTH), 0),
                      1.0, 0.0).astype(BF16)

    def scores(j, slot, diagonal):
        for c in range(t // KEY_CHUNK):
            k0 = pl.multiple_of(j * t + c * KEY_CHUNK, KEY_CHUNK)
            rows = slice(c * KEY_CHUNK, (c + 1) * KEY_CHUNK)
            for g in range(t // QUERY_CHUNK):
                cols = slice(g * QUERY_CHUNK, (g + 1) * QUERY_CHUNK)
                shift = g * QUERY_CHUNK - c * KEY_CHUNK
                for hh in heads:
                    if diagonal and shift <= -(QUERY_CHUNK - 1):
                        lb_ref[hh, slot, rows, cols] = jnp.full((KEY_CHUNK, QUERY_CHUNK), -jnp.inf, F32)
                        lm_ref[hh, slot, rows, cols] = jnp.zeros((KEY_CHUNK, QUERY_CHUNK), BF16)
                        continue
                    kc = k_ref[pl.ds(k0, KEY_CHUNK), head_cols(hh)]
                    z2 = jnp.dot(kc, qts[hh][:, cols], preferred_element_type=F32)
                    lb2 = jnp.minimum(z2, 0.0) - jnp.log(1.0 + jnp.exp2(_neg_abs(z2))) * LOG2E
                    l1m2 = lb2 - z2
                    if diagonal and shift < KEY_CHUNK:
                        visible = diff < shift
                        lb2 = jnp.where(visible, lb2, -jnp.inf)
                        l1m2 = jnp.where(visible, l1m2, 0.0)
                    lb_ref[hh, slot, rows, cols] = lb2
                    lm_ref[hh, slot, rows, cols] = l1m2.astype(BF16)

    def weights(slot, runs):
        runs = list(runs)
        for b in reversed(range(n_sub)):
            rows = slice(b * MXU_WIDTH, (b + 1) * MXU_WIDTH)
            totals = [[] for _ in heads]
            for g in range(t // QUERY_CHUNK):
                cols = slice(g * QUERY_CHUNK, (g + 1) * QUERY_CHUNK)
                for hh in heads:
                    lm = lm_ref[hh, slot, rows, cols]
                    within = jnp.dot(later, lm, preferred_element_type=F32)
                    w = jnp.exp2(lb_ref[hh, slot, rows, cols] + within + runs[hh][:, cols])
                    w_ref[hh, slot, rows, cols] = w.astype(BF16)
                    totals[hh].append(within[0:1, :] + lm[0:1, :].astype(F32))
            for hh in heads:
                runs[hh] = runs[hh] + jnp.concatenate(totals[hh], axis=1)
        return tuple(runs)

    def value_product(j, slot):
        for hh in heads:
            acc_ref[hh] += jnp.dot(vt_ref[hh, j], w_ref[hh, slot], preferred_element_type=F32)

    def live(runs):
        return jnp.max(functools.reduce(jnp.maximum, runs)) > WEIGHT_FLOOR_LOG2

    def tile(j, runs):
        scores(j, 0, False)
        runs = weights(0, runs)
        go = live(runs)
        value_product(j, 0)
        return runs, go

    start = tuple(jnp.zeros((1, t), F32) for _ in heads)

    @pl.when(i == 0)
    def _():
        scores(0, 0, True)
        weights(0, start)
        value_product(0, 0)

    @pl.when(i > 0)
    def _():
        scores(i, 0, True)
        scores(i - 1, 1, False)
        runs = weights(1, weights(0, start))
        go = live(runs)
        value_product(i, 0)
        value_product(i - 1, 1)
        lax.while_loop(lambda c: (c[0] >= 0) & c[2],
                       lambda c: (c[0] - 1,) + tile(c[0], c[1]), (i - 2, runs, go))

    for hh in heads:
        o_ref[:, head_cols(hh)] = acc_ref[hh].T.astype(o_ref.dtype)


def sb_attention(q, kv, *, t=256, hg=4):
    s, d = q.shape
    ng = d // (hg * HEAD_DIM)
    w = hg * HEAD_DIM
    return pl.pallas_call(
        functools.partial(_sb_kernel, t=t, hg=hg),
        grid=(ng, s // t),
        in_specs=[pl.BlockSpec((t, w), lambda g, i: (i, g)),
                  pl.BlockSpec((s, w), lambda g, i: (0, g)),
                  pl.BlockSpec((s, w), lambda g, i: (0, ng + g))],
        out_specs=pl.BlockSpec((t, w), lambda g, i: (i, g)),
        out_shape=jax.ShapeDtypeStruct((s, d), BF16),
        scratch_shapes=[pltpu.VMEM((hg, s // t, HEAD_DIM, t), BF16),
                        pltpu.VMEM((hg, 2, t, t), F32),
                        pltpu.VMEM((hg, 2, t, t), BF16),
                        pltpu.VMEM((hg, 2, t, t), BF16),
                        pltpu.VMEM((hg, HEAD_DIM, t), F32)],
        compiler_params=_params("arbitrary", "arbitrary"),
        name="sb_attention")(q, kv, kv)


def _matmul_residual_kernel(a_ref, w_ref, h_ref, o_ref):
    o_ref[...] = h_ref[...] + jnp.dot(a_ref[...], w_ref[...], preferred_element_type=F32)


def matmul_residual(a, w, layer, h, *, tm=512):
    s, k = a.shape
    n = tn = w.shape[2]
    return pl.pallas_call(
        _matmul_residual_kernel, grid=(s // tm, n // tn),
        in_specs=[pl.BlockSpec((tm, k), lambda i, j: (i, 0)),
                  pl.BlockSpec((None, k, tn), lambda i, j: (layer, 0, j)),
                  pl.BlockSpec((tm, tn), lambda i, j: (i, j))],
        out_specs=pl.BlockSpec((tm, tn), lambda i, j: (i, j)),
        out_shape=jax.ShapeDtypeStruct((s, n), F32),
        compiler_params=_params("parallel", "arbitrary"),
        name="matmul_residual")(a, w, h)


def _ffn_step(h_ref, g_ref, wg, wu, wd, gout_ref, o_ref, xn_ref, norm_out):
    @pl.when(pl.program_id(1) == 0)
    def _():
        x = h_ref[...]
        xn_ref[...] = _rms_scale(x, g_ref[...]).astype(BF16)
        o_ref[...] = x

    xn = xn_ref[...]
    gate = jnp.dot(xn, wg, preferred_element_type=F32)
    up = jnp.dot(xn, wu, preferred_element_type=F32)
    act = (gate * jax.nn.sigmoid(gate) * up).astype(BF16)
    o_ref[...] += jnp.dot(act, wd, preferred_element_type=F32)

    if norm_out:
        @pl.when(pl.program_id(1) == pl.num_programs(1) - 1)
        def _():
            o_ref[...] = _rms_scale(o_ref[...], gout_ref[...])


def _ffn_first_kernel(h_ref, g_ref, wg_ref, wu_ref, wd_ref, gout_ref,
                      o_ref, wg_out_ref, wu_out_ref, wd_out_ref, xn_ref, *, norm_out):
    wg, wu, wd = (w[...].astype(BF16) for w in (wg_ref, wu_ref, wd_ref))
    wg_out_ref[...] = wg
    wu_out_ref[...] = wu
    wd_out_ref[...] = wd
    _ffn_step(h_ref, g_ref, wg, wu, wd, gout_ref, o_ref, xn_ref, norm_out)


def _ffn_rest_kernel(h_ref, g_ref, wg_ref, wu_ref, wd_ref, gout_ref, first_ref, o_ref, xn_ref, *, norm_out):
    del first_ref
    _ffn_step(h_ref, g_ref, wg_ref[...], wu_ref[...], wd_ref[...], gout_ref, o_ref, xn_ref, norm_out)


def ffn(h, g, wg, wu, wd, layer, g_out, *, norm_out, tm=512, tf=512, tf_first=256):
    s, d = h.shape
    dff = wg.shape[2]
    vec = pl.BlockSpec((1, d), lambda i, f: (0, 0))

    def w_specs(lyr, t_f):
        return [pl.BlockSpec((None, d, t_f), lambda i, f: (lyr, 0, f)),
                pl.BlockSpec((None, d, t_f), lambda i, f: (lyr, 0, f)),
                pl.BlockSpec((None, t_f, d), lambda i, f: (lyr, f, 0))]

    first, wg16, wu16, wd16 = pl.pallas_call(
        functools.partial(_ffn_first_kernel, norm_out=norm_out), grid=(1, dff // tf_first),
        in_specs=[pl.BlockSpec((tm, d), lambda i, f: (0, 0)), vec, *w_specs(layer, tf_first), vec],
        out_specs=[pl.BlockSpec((tm, d), lambda i, f: (0, 0)), *w_specs(0, tf_first)],
        out_shape=[jax.ShapeDtypeStruct((s, d), F32),
                   jax.ShapeDtypeStruct((1, d, dff), BF16), jax.ShapeDtypeStruct((1, d, dff), BF16),
                   jax.ShapeDtypeStruct((1, dff, d), BF16)],
        scratch_shapes=[pltpu.VMEM((tm, d), BF16)],
        compiler_params=_params("arbitrary", "arbitrary"),
        name="ffn_first")(h, g, wg, wu, wd, g_out)
    return pl.pallas_call(
        functools.partial(_ffn_rest_kernel, norm_out=norm_out), grid=(s // tm - 1, dff // tf),
        in_specs=[pl.BlockSpec((tm, d), lambda i, f: (i + 1, 0)), vec, *w_specs(0, tf), vec,
                  pl.BlockSpec(memory_space=pl.ANY)],
        out_specs=pl.BlockSpec((tm, d), lambda i, f: (i + 1, 0)),
        out_shape=jax.ShapeDtypeStruct((s, d), F32),
        input_output_aliases={6: 0},
        scratch_shapes=[pltpu.VMEM((tm, d), BF16)],
        compiler_params=_params("parallel", "arbitrary"),
        name="ffn_rest")(h, g, wg16, wu16, wd16, g_out, first)


def kernel(x, attn_norm, ffn_norm, a_w_in, a_b_f, a_w_out, kv_norm, w_kv, b_w_q, b_w_out,
           ffn_w_gate, ffn_w_up, ffn_w_down, final_norm):
    b, s, d = x.shape
    assert b == 1 and d == N_HEADS * HEAD_DIM
    depth = attn_norm.shape[0]
    n_a = a_w_in.shape[0]
    h = x.reshape(s, d)
    w_in_t = jnp.swapaxes(a_w_in, 1, 2)
    w_qkv = cast_bf16(w_in_t, 3 * d)
    w_f = jnp.pad(w_in_t[:, 3 * d:], ((0, 0), (0, LANES - N_HEADS), (0, 0)))
    w_out_a, w_kv, w_q_b, w_out_b = (cast_bf16(w) for w in (a_w_out, w_kv[None], b_w_q, b_w_out))
    kv = None
    for layer in range(depth):
        g_attn = attn_norm[layer].reshape(1, d)
        if layer < n_a:
            b_f = jnp.pad(a_b_f[layer], (0, LANES - N_HEADS)).reshape(1, LANES)
            qkv, f_logit = norm_matmul(h, g_attn, w_qkv, layer, w_f, transposed=True, tm=1024,
                                       scaled_cols=d, col_scale=HEAD_DIM ** -0.5 * LOG2E)
            o = fox_attention(qkv, gate_cumsum(f_logit, b_f))
            h = matmul_residual(o, w_out_a, layer, h)
        else:
            j = layer - n_a
            q = norm_matmul(h, g_attn, w_q_b, j, tn=d,
                            scaled_cols=d, col_scale=HEAD_DIM ** -0.5 * LOG2E)
            o = sb_attention(q, kv)
            h = matmul_residual(o, w_out_b, j, h)
        h = ffn(h, ffn_norm[layer].reshape(1, d), ffn_w_gate, ffn_w_up, ffn_w_down, layer,
                final_norm.reshape(1, d), norm_out=layer == depth - 1)
        if layer == n_a - 1:
            kv = norm_matmul(h, kv_norm.reshape(1, d), w_kv, 0, tm=1024)
    return h.reshape(b, s, d)
```

```python
import functools

import jax
import jax.numpy as jnp
from jax import lax
from jax.experimental import pallas as pl
from jax.experimental.pallas import tpu as pltpu

N_HEADS = 16
HEAD_DIM = 128
EPS = 1e-6
LOG2E = 1.4426950408889634
LANES = 128
MXU_WIDTH = 256
VMEM_LIMIT = 56 * 1024 * 1024

KEY_CHUNK = 128
QUERY_CHUNK = MXU_WIDTH
PROB_CHUNK = 32
WEIGHT_FLOOR_LOG2 = -154.0

F32 = jnp.float32
BF16 = jnp.bfloat16


def _params(*sem):
    return pltpu.CompilerParams(dimension_semantics=sem, vmem_limit_bytes=VMEM_LIMIT)


def _rms_scale(x, g):
    ms = jnp.mean(x * x, axis=-1, keepdims=True)
    return x * lax.rsqrt(ms + EPS) * g


def _cast_kernel(x_ref, o_ref):
    o_ref[...] = x_ref[...].astype(o_ref.dtype)


def cast_bf16(w, rows=None, *, tr=256):
    n_layers, r, c = w.shape
    rows = r if rows is None else rows
    return pl.pallas_call(
        _cast_kernel, grid=(n_layers, rows // tr),
        in_specs=[pl.BlockSpec((None, tr, c), lambda l, i: (l, i, 0))],
        out_specs=pl.BlockSpec((None, tr, c), lambda l, i: (l, i, 0)),
        out_shape=jax.ShapeDtypeStruct((n_layers, rows, c), BF16),
        compiler_params=_params("parallel", "parallel"), name="cast_bf16")(w)


def _scaled_tile(acc, n_scaled_tiles, col_scale):
    if n_scaled_tiles == 0:
        return acc
    return acc * jnp.where(pl.program_id(1) < n_scaled_tiles, col_scale, 1.0)


def _project(xn, w, transposed):
    if transposed:
        return lax.dot_general(xn, w, (((1,), (1,)), ((), ())), preferred_element_type=F32)
    return jnp.dot(xn, w, preferred_element_type=F32)


def _norm_matmul_kernel(x_ref, g_ref, w_ref, o_ref, xn_ref, *, n_scaled_tiles, col_scale, transposed):
    @pl.when(pl.program_id(1) == 0)
    def _():
        xn_ref[...] = _rms_scale(x_ref[...], g_ref[...]).astype(BF16)

    acc = _project(xn_ref[...], w_ref[...], transposed)
    o_ref[...] = _scaled_tile(acc, n_scaled_tiles, col_scale).astype(o_ref.dtype)


def _norm_matmul_gate_kernel(x_ref, g_ref, w_ref, wf_ref, o_ref, f_ref, xn_ref,
                             *, n_scaled_tiles, col_scale, transposed):
    @pl.when(pl.program_id(1) == 0)
    def _():
        xn = _rms_scale(x_ref[...], g_ref[...]).astype(BF16)
        xn_ref[...] = xn
        f_ref[...] = _project(xn, wf_ref[...].astype(BF16), transposed)

    acc = _project(xn_ref[...], w_ref[...], transposed)
    o_ref[...] = _scaled_tile(acc, n_scaled_tiles, col_scale).astype(o_ref.dtype)


def norm_matmul(x, g, w, layer, wf=None, *, transposed=False, scaled_cols=0, col_scale=1.0,
                tm=512, tn=1024):
    s, d = x.shape
    n = w.shape[1] if transposed else w.shape[2]
    assert scaled_cols % tn == 0
    static = dict(n_scaled_tiles=scaled_cols // tn, col_scale=col_scale, transposed=transposed)
    grid = (s // tm, n // tn)
    x_spec = pl.BlockSpec((tm, d), lambda i, j: (i, 0))
    g_spec = pl.BlockSpec((1, d), lambda i, j: (0, 0))
    if transposed:
        w_spec = pl.BlockSpec((None, tn, d), lambda i, j: (layer, j, 0))
        wf_spec = pl.BlockSpec((None, LANES, d), lambda i, j: (layer, 0, 0))
    else:
        w_spec = pl.BlockSpec((None, d, tn), lambda i, j: (layer, 0, j))
        wf_spec = pl.BlockSpec((None, d, LANES), lambda i, j: (layer, 0, 0))
    o_spec = pl.BlockSpec((tm, tn), lambda i, j: (i, j))
    scratch = [pltpu.VMEM((tm, d), BF16)]
    if wf is None:
        return pl.pallas_call(
            functools.partial(_norm_matmul_kernel, **static), grid=grid,
            in_specs=[x_spec, g_spec, w_spec], out_specs=o_spec,
            out_shape=jax.ShapeDtypeStruct((s, n), BF16),
            scratch_shapes=scratch, compiler_params=_params("parallel", "arbitrary"),
            name="norm_matmul")(x, g, w)
    f_spec = pl.BlockSpec((tm, LANES), lambda i, j: (i, 0))
    return pl.pallas_call(
        functools.partial(_norm_matmul_gate_kernel, **static), grid=grid,
        in_specs=[x_spec, g_spec, w_spec, wf_spec], out_specs=[o_spec, f_spec],
        out_shape=[jax.ShapeDtypeStruct((s, n), BF16),
                   jax.ShapeDtypeStruct((s, LANES), F32)],
        scratch_shapes=scratch, compiler_params=_params("parallel", "arbitrary"),
        name="norm_matmul_gate")(x, g, w, wf)


def _log_sigmoid(z):
    return jnp.minimum(z, 0.0) - jnp.log(1.0 + jnp.exp(-jnp.abs(z)))


def _split3(x):
    hi = x.astype(BF16)
    r = x - hi.astype(F32)
    mid = r.astype(BF16)
    lo = (r - mid.astype(F32)).astype(BF16)
    return hi, mid, lo


def _gate_cumsum_kernel(f_ref, b_ref, ccol_ref, carry_ref, *, tc):
    @pl.when(pl.program_id(0) == 0)
    def _():
        carry_ref[...] = jnp.zeros_like(carry_ref)

    log_f = _log_sigmoid(f_ref[...] + b_ref[...])
    row = lax.broadcasted_iota(jnp.int32, (tc, tc), 0)
    col = lax.broadcasted_iota(jnp.int32, (tc, tc), 1)
    tri = jnp.where(col <= row, 1.0, 0.0).astype(BF16)
    hi, mid, lo = _split3(log_f)
    c = (jnp.dot(tri, hi, preferred_element_type=F32)
         + jnp.dot(tri, mid, preferred_element_type=F32)
         + jnp.dot(tri, lo, preferred_element_type=F32)) + carry_ref[...]
    ccol_ref[...] = c
    carry_ref[...] = c[tc - 1:tc, :]


def gate_cumsum(f, b, *, tc=512):
    s = f.shape[0]
    return pl.pallas_call(
        functools.partial(_gate_cumsum_kernel, tc=tc), grid=(s // tc,),
        in_specs=[pl.BlockSpec((tc, LANES), lambda i: (i, 0)),
                  pl.BlockSpec((1, LANES), lambda i: (0, 0))],
        out_specs=pl.BlockSpec((tc, LANES), lambda i: (i, 0)),
        out_shape=jax.ShapeDtypeStruct((s, LANES), F32),
        scratch_shapes=[pltpu.VMEM((1, LANES), F32)],
        compiler_params=_params("arbitrary"), name="gate_cumsum")(f, b)


def _qk(q, k):
    return lax.dot_general(q, k, (((1,), (1,)), ((), ())), preferred_element_type=F32)


def _colmax8(x):
    return jnp.max(x.reshape(x.shape[0] // 8, 8, x.shape[1]), axis=0)


def _neg_abs(x):
    bits = lax.bitcast_convert_type(x, jnp.int32) | jnp.int32(-2 ** 31)
    return lax.bitcast_convert_type(bits, F32)


def _colsum8(x):
    return jnp.sum(x.reshape(x.shape[0] // 8, 8, x.shape[1]), axis=0)


def _fox_kernel(q_ref, k_ref, v_ref, ccol_ref, o_ref,
                vt_ref, csb_ref, kn_ref, s_ref, p_ref, acc_ref, lp_ref, *, t, hg):
    grp = pl.program_id(0)
    i = pl.program_id(1)
    n = i + 1
    heads = range(hg)

    def head_cols(hh):
        return slice(hh * HEAD_DIM, (hh + 1) * HEAD_DIM)

    @pl.when(i == 0)
    def _():
        for hh in heads:
            kn_ref[hh] = jnp.zeros((1, LANES), F32)

        def stage(c, carry):
            rows = pl.ds(pl.multiple_of(c * t, t), t)
            lane = lax.broadcasted_iota(jnp.int32, (t, LANES), 1)
            for hh in heads:
                vt_ref[hh, c] = v_ref[rows, head_cols(hh)].astype(F32).T.astype(BF16)
                cs = jnp.sum(jnp.where(lane == grp * hg + hh, ccol_ref[rows, :], 0.0),
                             axis=-1, keepdims=True)
                csb_ref[hh, rows, :] = jnp.broadcast_to(cs * LOG2E, (t, LANES))
                kf = k_ref[rows, head_cols(hh)].astype(F32)
                kn_ref[hh] = jnp.maximum(kn_ref[hh], jnp.max(jnp.sum(kf * kf, axis=-1, keepdims=True)))
            return carry
        lax.fori_loop(0, k_ref.shape[0] // t, stage, None)

    qts = [q_ref[:, head_cols(hh)].astype(F32).T.astype(BF16) for hh in heads]
    reach = [jnp.sqrt(jnp.sum(qts[hh].astype(F32) ** 2, axis=0, keepdims=True) * kn_ref[hh][:, 0:1])
             for hh in heads]
    acc_ref[...] = jnp.zeros(acc_ref.shape, F32)
    lp_ref[...] = jnp.zeros(lp_ref.shape, F32)
    diff = (lax.broadcasted_iota(jnp.int32, (KEY_CHUNK, QUERY_CHUNK), 0)
            - lax.broadcasted_iota(jnp.int32, (KEY_CHUNK, QUERY_CHUNK), 1))

    def scores(j, slot, diagonal):
        mparts = [[jnp.full((8, QUERY_CHUNK), -jnp.inf, F32) for _ in range(t // QUERY_CHUNK)]
                  for _ in heads]
        for c in range(t // KEY_CHUNK):
            k0 = pl.multiple_of(j * t + c * KEY_CHUNK, KEY_CHUNK)
            rows = slice(c * KEY_CHUNK, (c + 1) * KEY_CHUNK)
            for g in range(t // QUERY_CHUNK):
                cols = slice(g * QUERY_CHUNK, (g + 1) * QUERY_CHUNK)
                shift = g * QUERY_CHUNK - c * KEY_CHUNK
                for hh in heads:
                    if diagonal and shift <= -QUERY_CHUNK:
                        s_ref[hh, slot, rows, cols] = jnp.full((KEY_CHUNK, QUERY_CHUNK), -jnp.inf, F32)
                        continue
                    bias = csb_ref[hh, pl.ds(k0, KEY_CHUNK), :]
                    bias = jnp.concatenate([bias] * (QUERY_CHUNK // LANES), axis=1)
                    kc = k_ref[pl.ds(k0, KEY_CHUNK), head_cols(hh)]
                    s = jnp.dot(kc, qts[hh][:, cols], preferred_element_type=F32) - bias
                    if diagonal and shift < KEY_CHUNK - 1:
                        s = jnp.where(diff <= shift, s, -jnp.inf)
                    s_ref[hh, slot, rows, cols] = s
                    mparts[hh][g] = jnp.maximum(mparts[hh][g], _colmax8(s))
        return [jnp.concatenate(mparts[hh], axis=1) for hh in heads]

    def new_max(m_old, mpart):
        m_new = jnp.maximum(m_old, jnp.max(mpart, axis=0, keepdims=True))
        return m_new, jnp.exp2(m_old - m_new)

    def half(step, slot, carry, last=False):
        ms, alphas = carry
        j = i - step
        lsums = [jnp.zeros((8, t), F32) for _ in heads]
        for c in range(t // PROB_CHUNK):
            rows = slice(c * PROB_CHUNK, (c + 1) * PROB_CHUNK)
            for hh in heads:
                p = jnp.exp2(s_ref[hh, slot, rows, :] - ms[hh])
                p_ref[hh, rows, :] = p.astype(BF16)
                lsums[hh] = lsums[hh] + _colsum8(p)
        if not last:
            mparts = scores(j - 1, 1 - slot, False)
            new = [new_max(ms[hh], mparts[hh]) for hh in heads]
            carry = tuple(m for m, _ in new), tuple(a for _, a in new)
        for hh in heads:
            lp_ref[hh] = alphas[hh] * lp_ref[hh] + lsums[hh]
            acc_ref[hh] = alphas[hh] * acc_ref[hh] + jnp.dot(vt_ref[hh, j], p_ref[hh],
                                                            preferred_element_type=F32)
        return carry

    def tile_live(j, ms):
        last_key = jnp.maximum(j + 1, 1) * t - 1
        gaps = [reach[hh] - csb_ref[hh, pl.ds(last_key, 1), :][:, 0:1] - ms[hh] for hh in heads]
        return jnp.max(functools.reduce(jnp.maximum, gaps)) > WEIGHT_FLOOR_LOG2

    first = [new_max(jnp.full((1, t), -jnp.inf, F32), mp) for mp in scores(i, 0, True)]
    carry = tuple(m for m, _ in first), tuple(a for _, a in first)
    n_pairs = (n - 1) // 2

    def pair(c):
        mm, _, ms, alphas = c
        carry = half(2 * mm, 0, (ms, alphas))
        alive = tile_live(i - (2 * mm + 3), carry[0])
        return (mm + 1, alive) + half(2 * mm + 1, 1, carry)

    mm, alive, ms, alphas = lax.while_loop(
        lambda c: (c[0] < n_pairs) & c[1], pair, (jnp.int32(0), tile_live(i - 1, carry[0])) + carry)
    step = 2 * mm
    single = (step == n - 1) | jnp.logical_not(alive)

    @pl.when(single)
    def _():
        half(step, 0, (ms, alphas), last=True)

    @pl.when(jnp.logical_not(single))
    def _():
        half(step + 1, 1, half(step, 0, (ms, alphas)), last=True)

    for hh in heads:
        l = jnp.sum(lp_ref[hh], axis=0, keepdims=True)
        o_ref[:, head_cols(hh)] = (acc_ref[hh] / l).T.astype(o_ref.dtype)


def fox_attention(qkv, ccol, *, t=256, hg=4):
    s = qkv.shape[0]
    d = qkv.shape[1] // 3
    ng = d // (hg * HEAD_DIM)
    w = hg * HEAD_DIM
    return pl.pallas_call(
        functools.partial(_fox_kernel, t=t, hg=hg),
        grid=(ng, s // t),
        in_specs=[pl.BlockSpec((t, w), lambda g, i: (i, g)),
                  pl.BlockSpec((s, w), lambda g, i: (0, ng + g), pipeline_mode=pl.Buffered(1)),
                  pl.BlockSpec((s, w), lambda g, i: (0, 2 * ng + g), pipeline_mode=pl.Buffered(1)),
                  pl.BlockSpec((s, LANES), lambda g, i: (0, 0), pipeline_mode=pl.Buffered(1))],
        out_specs=pl.BlockSpec((t, w), lambda g, i: (i, g)),
        out_shape=jax.ShapeDtypeStruct((s, d), BF16),
        scratch_shapes=[pltpu.VMEM((hg, s // t, HEAD_DIM, t), BF16),
                        pltpu.VMEM((hg, s, LANES), F32),
                        pltpu.VMEM((hg, 1, LANES), F32),
                        pltpu.VMEM((hg, 2, t, t), F32),
                        pltpu.VMEM((hg, t, t), BF16),
                        pltpu.VMEM((hg, HEAD_DIM, t), F32), pltpu.VMEM((hg, 8, t), F32)],
        compiler_params=_params("arbitrary", "arbitrary"),
        name="fox_attention")(qkv, qkv, qkv, ccol)


def _sb_kernel(q_ref, k_ref, v_ref, o_ref, vt_ref, lb_ref, lm_ref, w_ref, acc_ref, *, t, hg):
    i = pl.program_id(1)
    n_sub = t // MXU_WIDTH
    heads = range(hg)

    def head_cols(hh):
        return slice(hh * HEAD_DIM, (hh + 1) * HEAD_DIM)

    @pl.when(i == 0)
    def _():
        def stage(c, carry):
            rows = pl.ds(pl.multiple_of(c * t, t), t)
            for hh in heads:
                vt_ref[hh, c] = v_ref[rows, head_cols(hh)].astype(F32).T.astype(BF16)
            return carry
        lax.fori_loop(0, k_ref.shape[0] // t, stage, None)

    qts = [q_ref[:, head_cols(hh)].astype(F32).T.astype(BF16) for hh in heads]
    acc_ref[...] = jnp.zeros(acc_ref.shape, F32)
    diff = (lax.broadcasted_iota(jnp.int32, (KEY_CHUNK, QUERY_CHUNK), 0)
            - lax.broadcasted_iota(jnp.int32, (KEY_CHUNK, QUERY_CHUNK), 1))
    later = jnp.where(lax.broadcasted_iota(jnp.int32, (MXU_WIDTH, MXU_WIDTH), 1)
                      > lax.broadcasted_iota(jnp.int32, (MXU_WIDTH, MXU_WIDTH), 0),
                      1.0, 0.0).astype(BF16)

    def scores(j, slot, diagonal):
        for c in range(t // KEY_CHUNK):
            k0 = pl.multiple_of(j * t + c * KEY_CHUNK, KEY_CHUNK)
            rows = slice(c * KEY_CHUNK, (c + 1) * KEY_CHUNK)
            for g in range(t // QUERY_CHUNK):
                cols = slice(g * QUERY_CHUNK, (g + 1) * QUERY_CHUNK)
                shift = g * QUERY_CHUNK - c * KEY_CHUNK
                for hh in heads:
                    if diagonal and shift <= -(QUERY_CHUNK - 1):
                        lb_ref[hh, slot, rows, cols] = jnp.full((KEY_CHUNK, QUERY_CHUNK), -jnp.inf, F32)
                        lm_ref[hh, slot, rows, cols] = jnp.zeros((KEY_CHUNK, QUERY_CHUNK), BF16)
                        continue
                    kc = k_ref[pl.ds(k0, KEY_CHUNK), head_cols(hh)]
                    z2 = jnp.dot(kc, qts[hh][:, cols], preferred_element_type=F32)
                    lb2 = jnp.minimum(z2, 0.0) - jnp.log(1.0 + jnp.exp2(_neg_abs(z2))) * LOG2E
                    l1m2 = lb2 - z2
                    if diagonal and shift < KEY_CHUNK:
                        visible = diff < shift
                        lb2 = jnp.where(visible, lb2, -jnp.inf)
                        l1m2 = jnp.where(visible, l1m2, 0.0)
                    lb_ref[hh, slot, rows, cols] = lb2
                    lm_ref[hh, slot, rows, cols] = l1m2.astype(BF16)

    def weights(slot, runs):
        runs = list(runs)
        for b in reversed(range(n_sub)):
            rows = slice(b * MXU_WIDTH, (b + 1) * MXU_WIDTH)
            totals = [[] for _ in heads]
            for g in range(t // QUERY_CHUNK):
                cols = slice(g * QUERY_CHUNK, (g + 1) * QUERY_CHUNK)
                for hh in heads:
                    lm = lm_ref[hh, slot, rows, cols]
                    within = jnp.dot(later, lm, preferred_element_type=F32)
                    w = jnp.exp2(lb_ref[hh, slot, rows, cols] + within + runs[hh][:, cols])
                    w_ref[hh, slot, rows, cols] = w.astype(BF16)
                    totals[hh].append(within[0:1, :] + lm[0:1, :].astype(F32))
            for hh in heads:
                runs[hh] = runs[hh] + jnp.concatenate(totals[hh], axis=1)
        return tuple(runs)

    def value_product(j, slot):
        for hh in heads:
            acc_ref[hh] += jnp.dot(vt_ref[hh, j], w_ref[hh, slot], preferred_element_type=F32)

    def live(runs):
        return jnp.max(functools.reduce(jnp.maximum, runs)) > WEIGHT_FLOOR_LOG2

    def tile(j, runs):
        scores(j, 0, False)
        runs = weights(0, runs)
        go = live(runs)
        value_product(j, 0)
        return runs, go

    start = tuple(jnp.zeros((1, t), F32) for _ in heads)

    @pl.when(i == 0)
    def _():
        scores(0, 0, True)
        weights(0, start)
        value_product(0, 0)

    @pl.when(i > 0)
    def _():
        scores(i, 0, True)
        scores(i - 1, 1, False)
        runs = weights(1, weights(0, start))
        go = live(runs)
        value_product(i, 0)
        value_product(i - 1, 1)
        lax.while_loop(lambda c: (c[0] >= 0) & c[2],
                       lambda c: (c[0] - 1,) + tile(c[0], c[1]), (i - 2, runs, go))

    for hh in heads:
        o_ref[:, head_cols(hh)] = acc_ref[hh].T.astype(o_ref.dtype)


def sb_attention(q, kv, *, t=256, hg=4):
    s, d = q.shape
    ng = d // (hg * HEAD_DIM)
    w = hg * HEAD_DIM
    return pl.pallas_call(
        functools.partial(_sb_kernel, t=t, hg=hg),
        grid=(ng, s // t),
        in_specs=[pl.BlockSpec((t, w), lambda g, i: (i, g)),
                  pl.BlockSpec((s, w), lambda g, i: (0, g)),
                  pl.BlockSpec((s, w), lambda g, i: (0, ng + g))],
        out_specs=pl.BlockSpec((t, w), lambda g, i: (i, g)),
        out_shape=jax.ShapeDtypeStruct((s, d), BF16),
        scratch_shapes=[pltpu.VMEM((hg, s // t, HEAD_DIM, t), BF16),
                        pltpu.VMEM((hg, 2, t, t), F32),
                        pltpu.VMEM((hg, 2, t, t), BF16),
                        pltpu.VMEM((hg, 2, t, t), BF16),
                        pltpu.VMEM((hg, HEAD_DIM, t), F32)],
        compiler_params=_params("arbitrary", "arbitrary"),
        name="sb_attention")(q, kv, kv)


def _matmul_residual_kernel(a_ref, w_ref, h_ref, o_ref):
    o_ref[...] = h_ref[...] + jnp.dot(a_ref[...], w_ref[...], preferred_element_type=F32)


def matmul_residual(a, w, layer, h, *, tm=512):
    s, k = a.shape
    n = tn = w.shape[2]
    return pl.pallas_call(
        _matmul_residual_kernel, grid=(s // tm, n // tn),
        in_specs=[pl.BlockSpec((tm, k), lambda i, j: (i, 0)),
                  pl.BlockSpec((None, k, tn), lambda i, j: (layer, 0, j)),
                  pl.BlockSpec((tm, tn), lambda i, j: (i, j))],
        out_specs=pl.BlockSpec((tm, tn), lambda i, j: (i, j)),
        out_shape=jax.ShapeDtypeStruct((s, n), F32),
        compiler_params=_params("parallel", "arbitrary"),
        name="matmul_residual")(a, w, h)


def _ffn_step(h_ref, g_ref, weights, gout_ref, o_ref, xn_ref, norm_out):
    @pl.when(pl.program_id(1) == 0)
    def _():
        x = h_ref[...]
        xn_ref[...] = _rms_scale(x, g_ref[...]).astype(BF16)
        o_ref[...] = x

    wg, wu, wd = weights()
    xn = xn_ref[...]
    gate = jnp.dot(xn, wg, preferred_element_type=F32)
    up = jnp.dot(xn, wu, preferred_element_type=F32)
    act = (gate * jax.nn.sigmoid(gate) * up).astype(BF16)
    o_ref[...] += jnp.dot(act, wd, preferred_element_type=F32)

    if norm_out:
        @pl.when(pl.program_id(1) == pl.num_programs(1) - 1)
        def _():
            o_ref[...] = _rms_scale(o_ref[...], gout_ref[...])


def _ffn_first_kernel(h_ref, g_ref, wg_ref, wu_ref, wd_ref, gout_ref,
                      o_ref, wg_out_ref, wu_out_ref, wd_out_ref, xn_ref, *, norm_out):
    def weights():
        wg, wu, wd = (w[...].astype(BF16) for w in (wg_ref, wu_ref, wd_ref))
        wg_out_ref[...] = wg
        wu_out_ref[...] = wu
        wd_out_ref[...] = wd
        return wg, wu, wd

    _ffn_step(h_ref, g_ref, weights, gout_ref, o_ref, xn_ref, norm_out)


def _ffn_rest_kernel(h_ref, g_ref, wg_ref, wu_ref, wd_ref, gout_ref, first_ref, o_ref, xn_ref, *, norm_out):
    del first_ref
    _ffn_step(h_ref, g_ref, lambda: (wg_ref[...], wu_ref[...], wd_ref[...]), gout_ref, o_ref, xn_ref,
              norm_out)


def ffn(h, g, wg, wu, wd, layer, g_out, *, norm_out, tm=512, tf=512, tf_first=256):
    s, d = h.shape
    dff = wg.shape[2]
    vec = pl.BlockSpec((1, d), lambda i, f: (0, 0))

    def w_specs(lyr, t_f):
        return [pl.BlockSpec((None, d, t_f), lambda i, f: (lyr, 0, f)),
                pl.BlockSpec((None, d, t_f), lambda i, f: (lyr, 0, f)),
                pl.BlockSpec((None, t_f, d), lambda i, f: (lyr, f, 0))]

    first, wg16, wu16, wd16 = pl.pallas_call(
        functools.partial(_ffn_first_kernel, norm_out=norm_out), grid=(1, dff // tf_first),
        in_specs=[pl.BlockSpec((tm, d), lambda i, f: (0, 0)), vec, *w_specs(layer, tf_first), vec],
        out_specs=[pl.BlockSpec((tm, d), lambda i, f: (0, 0)), *w_specs(0, tf_first)],
        out_shape=[jax.ShapeDtypeStruct((s, d), F32),
                   jax.ShapeDtypeStruct((1, d, dff), BF16), jax.ShapeDtypeStruct((1, d, dff), BF16),
                   jax.ShapeDtypeStruct((1, dff, d), BF16)],
        scratch_shapes=[pltpu.VMEM((tm, d), BF16)],
        compiler_params=_params("arbitrary", "arbitrary"),
        name="ffn_first")(h, g, wg, wu, wd, g_out)
    return pl.pallas_call(
        functools.partial(_ffn_rest_kernel, norm_out=norm_out), grid=(s // tm - 1, dff // tf),
        in_specs=[pl.BlockSpec((tm, d), lambda i, f: (i + 1, 0)), vec, *w_specs(0, tf), vec,
                  pl.BlockSpec(memory_space=pl.ANY)],
        out_specs=pl.BlockSpec((tm, d), lambda i, f: (i + 1, 0)),
        out_shape=jax.ShapeDtypeStruct((s, d), F32),
        input_output_aliases={6: 0},
        scratch_shapes=[pltpu.VMEM((tm, d), BF16)],
        compiler_params=_params("parallel", "arbitrary"),
        name="ffn_rest")(h, g, wg16, wu16, wd16, g_out, first)


def kernel(x, attn_norm, ffn_norm, a_w_in, a_b_f, a_w_out, kv_norm, w_kv, b_w_q, b_w_out,
           ffn_w_gate, ffn_w_up, ffn_w_down, final_norm):
    b, s, d = x.shape
    assert b == 1 and d == N_HEADS * HEAD_DIM
    depth = attn_norm.shape[0]
    n_a = a_w_in.shape[0]
    h = x.reshape(s, d)
    w_in_t = jnp.swapaxes(a_w_in, 1, 2)
    w_qkv = cast_bf16(w_in_t, 3 * d)
    w_f = jnp.pad(w_in_t[:, 3 * d:], ((0, 0), (0, LANES - N_HEADS), (0, 0)))
    w_out_a, w_kv, w_q_b, w_out_b = (cast_bf16(w) for w in (a_w_out, w_kv[None], b_w_q, b_w_out))
    kv = None
    for layer in range(depth):
        g_attn = attn_norm[layer].reshape(1, d)
        if layer < n_a:
            b_f = jnp.pad(a_b_f[layer], (0, LANES - N_HEADS)).reshape(1, LANES)
            qkv, f_logit = norm_matmul(h, g_attn, w_qkv, layer, w_f, transposed=True, tm=1024,
                                       scaled_cols=d, col_scale=HEAD_DIM ** -0.5 * LOG2E)
            o = fox_attention(qkv, gate_cumsum(f_logit, b_f))
            h = matmul_residual(o, w_out_a, layer, h)
        else:
            j = layer - n_a
            q = norm_matmul(h, g_attn, w_q_b, j, tn=d,
                            scaled_cols=d, col_scale=HEAD_DIM ** -0.5 * LOG2E)
            o = sb_attention(q, kv)
            h = matmul_residual(o, w_out_b, j, h)
        h = ffn(h, ffn_norm[layer].reshape(1, d), ffn_w_gate, ffn_w_up, ffn_w_down, layer,
                final_norm.reshape(1, d), norm_out=layer == depth - 1)
        if layer == n_a - 1:
            kv = norm_matmul(h, kv_norm.reshape(1, d), w_kv, 0, tm=1024)
    return h.reshape(b, s, d)
```

```python
import functools

import jax
import jax.numpy as jnp
from jax import lax
from jax.experimental import pallas as pl
from jax.experimental.pallas import tpu as pltpu

N_HEADS = 16
HEAD_DIM = 128
EPS = 1e-6
LOG2E = 1.4426950408889634
LANES = 128
MXU_WIDTH = 256
VMEM_LIMIT = 56 * 1024 * 1024

KEY_CHUNK = 128
QUERY_CHUNK = MXU_WIDTH
PROB_CHUNK = 32
WEIGHT_FLOOR_LOG2 = -154.0

F32 = jnp.float32
BF16 = jnp.bfloat16


def _params(*sem):
    return pltpu.CompilerParams(dimension_semantics=sem, vmem_limit_bytes=VMEM_LIMIT)


def _rms_scale(x, g):
    ms = jnp.mean(x * x, axis=-1, keepdims=True)
    return x * lax.rsqrt(ms + EPS) * g


def _cast_kernel(x_ref, o_ref):
    o_ref[...] = x_ref[...].astype(o_ref.dtype)


def cast_bf16(w, rows=None, *, tr=256):
    n_layers, r, c = w.shape
    rows = r if rows is None else rows
    return pl.pallas_call(
        _cast_kernel, grid=(n_layers, rows // tr),
        in_specs=[pl.BlockSpec((None, tr, c), lambda l, i: (l, i, 0))],
        out_specs=pl.BlockSpec((None, tr, c), lambda l, i: (l, i, 0)),
        out_shape=jax.ShapeDtypeStruct((n_layers, rows, c), BF16),
        compiler_params=_params("parallel", "parallel"), name="cast_bf16")(w)


def _scaled_tile(acc, n_scaled_tiles, col_scale):
    if n_scaled_tiles == 0:
        return acc
    return acc * jnp.where(pl.program_id(1) < n_scaled_tiles, col_scale, 1.0)


def _project(xn, w, transposed):
    if transposed:
        return lax.dot_general(xn, w, (((1,), (1,)), ((), ())), preferred_element_type=F32)
    return jnp.dot(xn, w, preferred_element_type=F32)


def _norm_matmul_kernel(x_ref, g_ref, w_ref, o_ref, xn_ref, *, n_scaled_tiles, col_scale, transposed):
    @pl.when(pl.program_id(1) == 0)
    def _():
        xn_ref[...] = _rms_scale(x_ref[...], g_ref[...]).astype(BF16)

    acc = _project(xn_ref[...], w_ref[...], transposed)
    o_ref[...] = _scaled_tile(acc, n_scaled_tiles, col_scale).astype(o_ref.dtype)


def _norm_matmul_gate_kernel(x_ref, g_ref, w_ref, wf_ref, o_ref, f_ref, xn_ref,
                             *, n_scaled_tiles, col_scale, transposed):
    @pl.when(pl.program_id(1) == 0)
    def _():
        xn = _rms_scale(x_ref[...], g_ref[...]).astype(BF16)
        xn_ref[...] = xn
        f_ref[...] = _project(xn, wf_ref[...].astype(BF16), transposed)

    acc = _project(xn_ref[...], w_ref[...], transposed)
    o_ref[...] = _scaled_tile(acc, n_scaled_tiles, col_scale).astype(o_ref.dtype)


def norm_matmul(x, g, w, layer, wf=None, *, transposed=False, scaled_cols=0, col_scale=1.0,
                tm=512, tn=1024):
    s, d = x.shape
    n = w.shape[1] if transposed else w.shape[2]
    assert scaled_cols % tn == 0
    static = dict(n_scaled_tiles=scaled_cols // tn, col_scale=col_scale, transposed=transposed)
    grid = (s // tm, n // tn)
    x_spec = pl.BlockSpec((tm, d), lambda i, j: (i, 0))
    g_spec = pl.BlockSpec((1, d), lambda i, j: (0, 0))
    if transposed:
        w_spec = pl.BlockSpec((None, tn, d), lambda i, j: (layer, j, 0))
        wf_spec = pl.BlockSpec((None, LANES, d), lambda i, j: (layer, 0, 0))
    else:
        w_spec = pl.BlockSpec((None, d, tn), lambda i, j: (layer, 0, j))
        wf_spec = pl.BlockSpec((None, d, LANES), lambda i, j: (layer, 0, 0))
    o_spec = pl.BlockSpec((tm, tn), lambda i, j: (i, j))
    scratch = [pltpu.VMEM((tm, d), BF16)]
    if wf is None:
        return pl.pallas_call(
            functools.partial(_norm_matmul_kernel, **static), grid=grid,
            in_specs=[x_spec, g_spec, w_spec], out_specs=o_spec,
            out_shape=jax.ShapeDtypeStruct((s, n), BF16),
            scratch_shapes=scratch, compiler_params=_params("parallel", "arbitrary"),
            name="norm_matmul")(x, g, w)
    f_spec = pl.BlockSpec((tm, LANES), lambda i, j: (i, 0))
    return pl.pallas_call(
        functools.partial(_norm_matmul_gate_kernel, **static), grid=grid,
        in_specs=[x_spec, g_spec, w_spec, wf_spec], out_specs=[o_spec, f_spec],
        out_shape=[jax.ShapeDtypeStruct((s, n), BF16),
                   jax.ShapeDtypeStruct((s, LANES), F32)],
        scratch_shapes=scratch, compiler_params=_params("parallel", "arbitrary"),
        name="norm_matmul_gate")(x, g, w, wf)


def _log_sigmoid(z):
    return jnp.minimum(z, 0.0) - jnp.log(1.0 + jnp.exp(-jnp.abs(z)))


def _split3(x):
    hi = x.astype(BF16)
    r = x - hi.astype(F32)
    mid = r.astype(BF16)
    lo = (r - mid.astype(F32)).astype(BF16)
    return hi, mid, lo


def _gate_cumsum_kernel(f_ref, b_ref, ccol_ref, carry_ref, *, tc):
    @pl.when(pl.program_id(0) == 0)
    def _():
        carry_ref[...] = jnp.zeros_like(carry_ref)

    log_f = _log_sigmoid(f_ref[...] + b_ref[...])
    row = lax.broadcasted_iota(jnp.int32, (tc, tc), 0)
    col = lax.broadcasted_iota(jnp.int32, (tc, tc), 1)
    tri = jnp.where(col <= row, 1.0, 0.0).astype(BF16)
    hi, mid, lo = _split3(log_f)
    c = (jnp.dot(tri, hi, preferred_element_type=F32)
         + jnp.dot(tri, mid, preferred_element_type=F32)
         + jnp.dot(tri, lo, preferred_element_type=F32)) + carry_ref[...]
    ccol_ref[...] = c
    carry_ref[...] = c[tc - 1:tc, :]


def gate_cumsum(f, b, *, tc=512):
    s = f.shape[0]
    return pl.pallas_call(
        functools.partial(_gate_cumsum_kernel, tc=tc), grid=(s // tc,),
        in_specs=[pl.BlockSpec((tc, LANES), lambda i: (i, 0)),
                  pl.BlockSpec((1, LANES), lambda i: (0, 0))],
        out_specs=pl.BlockSpec((tc, LANES), lambda i: (i, 0)),
        out_shape=jax.ShapeDtypeStruct((s, LANES), F32),
        scratch_shapes=[pltpu.VMEM((1, LANES), F32)],
        compiler_params=_params("arbitrary"), name="gate_cumsum")(f, b)


def _qk(q, k):
    return lax.dot_general(q, k, (((1,), (1,)), ((), ())), preferred_element_type=F32)


def _colmax8(x):
    return jnp.max(x.reshape(x.shape[0] // 8, 8, x.shape[1]), axis=0)


def _neg_abs(x):
    bits = lax.bitcast_convert_type(x, jnp.int32) | jnp.int32(-2 ** 31)
    return lax.bitcast_convert_type(bits, F32)


def _colsum8(x):
    return jnp.sum(x.reshape(x.shape[0] // 8, 8, x.shape[1]), axis=0)


def _fox_kernel(q_ref, k_ref, v_ref, ccol_ref, o_ref,
                vt_ref, csb_ref, kn_ref, s_ref, p_ref, acc_ref, lp_ref, *, t, hg):
    grp = pl.program_id(0)
    i = pl.program_id(1)
    n = i + 1
    heads = range(hg)

    def head_cols(hh):
        return slice(hh * HEAD_DIM, (hh + 1) * HEAD_DIM)

    @pl.when(i == 0)
    def _():
        for hh in heads:
            kn_ref[hh] = jnp.zeros((1, LANES), F32)

        def stage(c, carry):
            rows = pl.ds(pl.multiple_of(c * t, t), t)
            lane = lax.broadcasted_iota(jnp.int32, (t, LANES), 1)
            for hh in heads:
                vt_ref[hh, c] = v_ref[rows, head_cols(hh)].astype(F32).T.astype(BF16)
                cs = jnp.sum(jnp.where(lane == grp * hg + hh, ccol_ref[rows, :], 0.0),
                             axis=-1, keepdims=True)
                csb_ref[hh, rows, :] = jnp.broadcast_to(cs * LOG2E, (t, LANES))
                kf = k_ref[rows, head_cols(hh)].astype(F32)
                kn_ref[hh] = jnp.maximum(kn_ref[hh], jnp.max(jnp.sum(kf * kf, axis=-1, keepdims=True)))
            return carry
        lax.fori_loop(0, k_ref.shape[0] // t, stage, None)

    qts = [q_ref[:, head_cols(hh)].astype(F32).T.astype(BF16) for hh in heads]
    reach = [jnp.sqrt(jnp.sum(qts[hh].astype(F32) ** 2, axis=0, keepdims=True) * kn_ref[hh][:, 0:1])
             for hh in heads]
    acc_ref[...] = jnp.zeros(acc_ref.shape, F32)
    lp_ref[...] = jnp.zeros(lp_ref.shape, F32)
    diff = (lax.broadcasted_iota(jnp.int32, (KEY_CHUNK, QUERY_CHUNK), 0)
            - lax.broadcasted_iota(jnp.int32, (KEY_CHUNK, QUERY_CHUNK), 1))

    def scores(j, slot, diagonal):
        mparts = [[jnp.full((8, QUERY_CHUNK), -jnp.inf, F32) for _ in range(t // QUERY_CHUNK)]
                  for _ in heads]
        for c in range(t // KEY_CHUNK):
            k0 = pl.multiple_of(j * t + c * KEY_CHUNK, KEY_CHUNK)
            rows = slice(c * KEY_CHUNK, (c + 1) * KEY_CHUNK)
            for g in range(t // QUERY_CHUNK):
                cols = slice(g * QUERY_CHUNK, (g + 1) * QUERY_CHUNK)
                shift = g * QUERY_CHUNK - c * KEY_CHUNK
                for hh in heads:
                    if diagonal and shift <= -QUERY_CHUNK:
                        s_ref[hh, slot, rows, cols] = jnp.full((KEY_CHUNK, QUERY_CHUNK), -jnp.inf, F32)
                        continue
                    bias = csb_ref[hh, pl.ds(k0, KEY_CHUNK), :]
                    bias = jnp.concatenate([bias] * (QUERY_CHUNK // LANES), axis=1)
                    kc = k_ref[pl.ds(k0, KEY_CHUNK), head_cols(hh)]
                    s = jnp.dot(kc, qts[hh][:, cols], preferred_element_type=F32) - bias
                    if diagonal and shift < KEY_CHUNK - 1:
                        s = jnp.where(diff <= shift, s, -jnp.inf)
                    s_ref[hh, slot, rows, cols] = s
                    mparts[hh][g] = jnp.maximum(mparts[hh][g], _colmax8(s))
        return [jnp.concatenate(mparts[hh], axis=1) for hh in heads]

    def new_max(m_old, mpart):
        m_new = jnp.maximum(m_old, jnp.max(mpart, axis=0, keepdims=True))
        return m_new, jnp.exp2(m_old - m_new)

    def half(step, slot, carry, last=False):
        ms, alphas = carry
        j = i - step
        lsums = [jnp.zeros((8, t), F32) for _ in heads]
        for c in range(t // PROB_CHUNK):
            rows = slice(c * PROB_CHUNK, (c + 1) * PROB_CHUNK)
            for hh in heads:
                p = jnp.exp2(s_ref[hh, slot, rows, :] - ms[hh])
                p_ref[hh, rows, :] = p.astype(BF16)
                lsums[hh] = lsums[hh] + _colsum8(p)
        if not last:
            mparts = scores(j - 1, 1 - slot, False)
            new = [new_max(ms[hh], mparts[hh]) for hh in heads]
            carry = tuple(m for m, _ in new), tuple(a for _, a in new)
        for hh in heads:
            lp_ref[hh] = alphas[hh] * lp_ref[hh] + lsums[hh]
            acc_ref[hh] = alphas[hh] * acc_ref[hh] + jnp.dot(vt_ref[hh, j], p_ref[hh],
                                                            preferred_element_type=F32)
        return carry

    def tile_live(j, ms):
        last_key = jnp.maximum(j + 1, 1) * t - 1
        gaps = [reach[hh] - csb_ref[hh, pl.ds(last_key, 1), :][:, 0:1] - ms[hh] for hh in heads]
        return jnp.max(functools.reduce(jnp.maximum, gaps)) > WEIGHT_FLOOR_LOG2

    first = [new_max(jnp.full((1, t), -jnp.inf, F32), mp) for mp in scores(i, 0, True)]
    carry = tuple(m for m, _ in first), tuple(a for _, a in first)
    n_pairs = (n - 1) // 2

    def pair(c):
        mm, _, ms, alphas = c
        carry = half(2 * mm, 0, (ms, alphas))
        alive = tile_live(i - (2 * mm + 3), carry[0])
        return (mm + 1, alive) + half(2 * mm + 1, 1, carry)

    mm, alive, ms, alphas = lax.while_loop(
        lambda c: (c[0] < n_pairs) & c[1], pair, (jnp.int32(0), tile_live(i - 1, carry[0])) + carry)
    step = 2 * mm
    single = (step == n - 1) | jnp.logical_not(alive)

    @pl.when(single)
    def _():
        half(step, 0, (ms, alphas), last=True)

    @pl.when(jnp.logical_not(single))
    def _():
        half(step + 1, 1, half(step, 0, (ms, alphas)), last=True)

    for hh in heads:
        l = jnp.sum(lp_ref[hh], axis=0, keepdims=True)
        o_ref[:, head_cols(hh)] = (acc_ref[hh] / l).T.astype(o_ref.dtype)


def fox_attention(qkv, ccol, *, t=256, hg=4):
    s = qkv.shape[0]
    d = qkv.shape[1] // 3
    ng = d // (hg * HEAD_DIM)
    w = hg * HEAD_DIM
    return pl.pallas_call(
        functools.partial(_fox_kernel, t=t, hg=hg),
        grid=(ng, s // t),
        in_specs=[pl.BlockSpec((t, w), lambda g, i: (i, g)),
                  pl.BlockSpec((s, w), lambda g, i: (0, ng + g), pipeline_mode=pl.Buffered(1)),
                  pl.BlockSpec((s, w), lambda g, i: (0, 2 * ng + g), pipeline_mode=pl.Buffered(1)),
                  pl.BlockSpec((s, LANES), lambda g, i: (0, 0), pipeline_mode=pl.Buffered(1))],
        out_specs=pl.BlockSpec((t, w), lambda g, i: (i, g)),
        out_shape=jax.ShapeDtypeStruct((s, d), BF16),
        scratch_shapes=[pltpu.VMEM((hg, s // t, HEAD_DIM, t), BF16),
                        pltpu.VMEM((hg, s, LANES), F32),
                        pltpu.VMEM((hg, 1, LANES), F32),
                        pltpu.VMEM((hg, 2, t, t), F32),
                        pltpu.VMEM((hg, t, t), BF16),
                        pltpu.VMEM((hg, HEAD_DIM, t), F32), pltpu.VMEM((hg, 8, t), F32)],
        compiler_params=_params("arbitrary", "arbitrary"),
        name="fox_attention")(qkv, qkv, qkv, ccol)


def _sb_kernel(q_ref, k_ref, v_ref, o_ref, vt_ref, lb_ref, lm_ref, w_ref, acc_ref, *, t, hg):
    i = pl.program_id(1)
    n_sub = t // MXU_WIDTH
    heads = range(hg)

    def head_cols(hh):
        return slice(hh * HEAD_DIM, (hh + 1) * HEAD_DIM)

    @pl.when(i == 0)
    def _():
        def stage(c, carry):
            rows = pl.ds(pl.multiple_of(c * t, t), t)
            for hh in heads:
                vt_ref[hh, c] = v_ref[rows, head_cols(hh)].astype(F32).T.astype(BF16)
            return carry
        lax.fori_loop(0, k_ref.shape[0] // t, stage, None)

    qts = [q_ref[:, head_cols(hh)].astype(F32).T.astype(BF16) for hh in heads]
    acc_ref[...] = jnp.zeros(acc_ref.shape, F32)
    diff = (lax.broadcasted_iota(jnp.int32, (KEY_CHUNK, QUERY_CHUNK), 0)
            - lax.broadcasted_iota(jnp.int32, (KEY_CHUNK, QUERY_CHUNK), 1))
    later = jnp.where(lax.broadcasted_iota(jnp.int32, (MXU_WIDTH, MXU_WIDTH), 1)
                      > lax.broadcasted_iota(jnp.int32, (MXU_WIDTH, MXU_WIDTH), 0),
                      1.0, 0.0).astype(BF16)

    def scores(j, slot, diagonal):
        for c in range(t // KEY_CHUNK):
            k0 = pl.multiple_of(j * t + c * KEY_CHUNK, KEY_CHUNK)
            rows = slice(c * KEY_CHUNK, (c + 1) * KEY_CHUNK)
            for g in range(t // QUERY_CHUNK):
                cols = slice(g * QUERY_CHUNK, (g + 1) * QUERY_CHUNK)
                shift = g * QUERY_CHUNK - c * KEY_CHUNK
                for hh in heads:
                    if diagonal and shift <= -(QUERY_CHUNK - 1):
                        lb_ref[hh, slot, rows, cols] = jnp.full((KEY_CHUNK, QUERY_CHUNK), -jnp.inf, F32)
                        lm_ref[hh, slot, rows, cols] = jnp.zeros((KEY_CHUNK, QUERY_CHUNK), BF16)
                        continue
                    kc = k_ref[pl.ds(k0, KEY_CHUNK), head_cols(hh)]
                    z2 = jnp.dot(kc, qts[hh][:, cols], preferred_element_type=F32)
                    lb2 = jnp.minimum(z2, 0.0) - jnp.log(1.0 + jnp.exp2(_neg_abs(z2))) * LOG2E
                    l1m2 = lb2 - z2
                    if diagonal and shift < KEY_CHUNK:
                        visible = diff < shift
                        lb2 = jnp.where(visible, lb2, -jnp.inf)
                        l1m2 = jnp.where(visible, l1m2, 0.0)
                    lb_ref[hh, slot, rows, cols] = lb2
                    lm_ref[hh, slot, rows, cols] = l1m2.astype(BF16)

    def weights(slot, runs):
        runs = list(runs)
        for b in reversed(range(n_sub)):
            rows = slice(b * MXU_WIDTH, (b + 1) * MXU_WIDTH)
            totals = [[] for _ in heads]
            for g in range(t // QUERY_CHUNK):
                cols = slice(g * QUERY_CHUNK, (g + 1) * QUERY_CHUNK)
                for hh in heads:
                    lm = lm_ref[hh, slot, rows, cols]
                    within = jnp.dot(later, lm, preferred_element_type=F32)
                    w = jnp.exp2(lb_ref[hh, slot, rows, cols] + within + runs[hh][:, cols])
                    w_ref[hh, slot, rows, cols] = w.astype(BF16)
                    totals[hh].append(within[0:1, :] + lm[0:1, :].astype(F32))
            for hh in heads:
                runs[hh] = runs[hh] + jnp.concatenate(totals[hh], axis=1)
        return tuple(runs)

    def value_product(j, slot):
        for hh in heads:
            acc_ref[hh] += jnp.dot(vt_ref[hh, j], w_ref[hh, slot], preferred_element_type=F32)

    def live(runs):
        return jnp.max(functools.reduce(jnp.maximum, runs)) > WEIGHT_FLOOR_LOG2

    def tile(j, runs):
        scores(j, 0, False)
        runs = weights(0, runs)
        go = live(runs)
        value_product(j, 0)
        return runs, go

    start = tuple(jnp.zeros((1, t), F32) for _ in heads)

    @pl.when(i == 0)
    def _():
        scores(0, 0, True)
        weights(0, start)
        value_product(0, 0)

    @pl.when(i > 0)
    def _():
        scores(i, 0, True)
        scores(i - 1, 1, False)
        runs = weights(1, weights(0, start))
        go = live(runs)
        value_product(i, 0)
        value_product(i - 1, 1)
        lax.while_loop(lambda c: (c[0] >= 0) & c[2],
                       lambda c: (c[0] - 1,) + tile(c[0], c[1]), (i - 2, runs, go))

    for hh in heads:
        o_ref[:, head_cols(hh)] = acc_ref[hh].T.astype(o_ref.dtype)


def sb_attention(q, kv, *, t=256, hg=4):
    s, d = q.shape
    ng = d // (hg * HEAD_DIM)
    w = hg * HEAD_DIM
    return pl.pallas_call(
        functools.partial(_sb_kernel, t=t, hg=hg),
        grid=(ng, s // t),
        in_specs=[pl.BlockSpec((t, w), lambda g, i: (i, g)),
                  pl.BlockSpec((s, w), lambda g, i: (0, g)),
                  pl.BlockSpec((s, w), lambda g, i: (0, ng + g))],
        out_specs=pl.BlockSpec((t, w), lambda g, i: (i, g)),
        out_shape=jax.ShapeDtypeStruct((s, d), BF16),
        scratch_shapes=[pltpu.VMEM((hg, s // t, HEAD_DIM, t), BF16),
                        pltpu.VMEM((hg, 2, t, t), F32),
                        pltpu.VMEM((hg, 2, t, t), BF16),
                        pltpu.VMEM((hg, 2, t, t), BF16),
                        pltpu.VMEM((hg, HEAD_DIM, t), F32)],
        compiler_params=_params("arbitrary", "arbitrary"),
        name="sb_attention")(q, kv, kv)


def _matmul_residual_kernel(a_ref, w_ref, h_ref, o_ref):
    o_ref[...] = h_ref[...] + jnp.dot(a_ref[...], w_ref[...], preferred_element_type=F32)


def matmul_residual(a, w, layer, h, *, tm=512):
    s, k = a.shape
    n = tn = w.shape[2]
    return pl.pallas_call(
        _matmul_residual_kernel, grid=(s // tm, n // tn),
        in_specs=[pl.BlockSpec((tm, k), lambda i, j: (i, 0)),
                  pl.BlockSpec((None, k, tn), lambda i, j: (layer, 0, j)),
                  pl.BlockSpec((tm, tn), lambda i, j: (i, j))],
        out_specs=pl.BlockSpec((tm, tn), lambda i, j: (i, j)),
        out_shape=jax.ShapeDtypeStruct((s, n), F32),
        compiler_params=_params("parallel", "arbitrary"),
        name="matmul_residual")(a, w, h)


def _ffn_step(h_ref, g_ref, weights, gout_ref, o_ref, xn_ref, norm_out):
    @pl.when(pl.program_id(1) == 0)
    def _():
        x = h_ref[...]
        xn_ref[...] = _rms_scale(x, g_ref[...]).astype(BF16)
        o_ref[...] = x

    wg, wu, wd = weights()
    xn = xn_ref[...]
    gate = jnp.dot(xn, wg, preferred_element_type=F32)
    up = jnp.dot(xn, wu, preferred_element_type=F32)
    act = (gate * jax.nn.sigmoid(gate) * up).astype(BF16)
    o_ref[...] += jnp.dot(act, wd, preferred_element_type=F32)

    if norm_out:
        @pl.when(pl.program_id(1) == pl.num_programs(1) - 1)
        def _():
            o_ref[...] = _rms_scale(o_ref[...], gout_ref[...])


def _ffn_first_kernel(h_ref, g_ref, wg_ref, wu_ref, wd_ref, gout_ref,
                      o_ref, wg_out_ref, wu_out_ref, wd_out_ref, xn_ref, *, norm_out):
    def weights():
        wg, wu, wd = (w[...].astype(BF16) for w in (wg_ref, wu_ref, wd_ref))
        wg_out_ref[...] = wg
        wu_out_ref[...] = wu
        wd_out_ref[...] = wd
        return wg, wu, wd

    _ffn_step(h_ref, g_ref, weights, gout_ref, o_ref, xn_ref, norm_out)


def _ffn_rest_kernel(h_ref, g_ref, wg_ref, wu_ref, wd_ref, gout_ref, first_ref, o_ref, xn_ref, *, norm_out):
    del first_ref
    _ffn_step(h_ref, g_ref, lambda: (wg_ref[...], wu_ref[...], wd_ref[...]), gout_ref, o_ref, xn_ref,
              norm_out)


def ffn(h, g, wg, wu, wd, layer, g_out, *, norm_out, tm=512, tf=512, tm_first=1024, tf_first=256):
    s, d = h.shape
    dff = wg.shape[2]
    vec = pl.BlockSpec((1, d), lambda i, f: (0, 0))
    skip = tm_first // tm

    def w_specs(lyr, t_f):
        return [pl.BlockSpec((None, d, t_f), lambda i, f: (lyr, 0, f)),
                pl.BlockSpec((None, d, t_f), lambda i, f: (lyr, 0, f)),
                pl.BlockSpec((None, t_f, d), lambda i, f: (lyr, f, 0))]

    first, wg16, wu16, wd16 = pl.pallas_call(
        functools.partial(_ffn_first_kernel, norm_out=norm_out), grid=(1, dff // tf_first),
        in_specs=[pl.BlockSpec((tm_first, d), lambda i, f: (0, 0), pipeline_mode=pl.Buffered(1)), vec,
                  *w_specs(layer, tf_first), vec],
        out_specs=[pl.BlockSpec((tm_first, d), lambda i, f: (0, 0)), *w_specs(0, tf_first)],
        out_shape=[jax.ShapeDtypeStruct((s, d), F32),
                   jax.ShapeDtypeStruct((1, d, dff), BF16), jax.ShapeDtypeStruct((1, d, dff), BF16),
                   jax.ShapeDtypeStruct((1, dff, d), BF16)],
        scratch_shapes=[pltpu.VMEM((tm_first, d), BF16)],
        compiler_params=_params("arbitrary", "arbitrary"),
        name="ffn_first")(h, g, wg, wu, wd, g_out)
    return pl.pallas_call(
        functools.partial(_ffn_rest_kernel, norm_out=norm_out), grid=((s - tm_first) // tm, dff // tf),
        in_specs=[pl.BlockSpec((tm, d), lambda i, f: (i + skip, 0)), vec, *w_specs(0, tf), vec,
                  pl.BlockSpec(memory_space=pl.ANY)],
        out_specs=pl.BlockSpec((tm, d), lambda i, f: (i + skip, 0)),
        out_shape=jax.ShapeDtypeStruct((s, d), F32),
        input_output_aliases={6: 0},
        scratch_shapes=[pltpu.VMEM((tm, d), BF16)],
        compiler_params=_params("parallel", "arbitrary"),
        name="ffn_rest")(h, g, wg16, wu16, wd16, g_out, first)


def kernel(x, attn_norm, ffn_norm, a_w_in, a_b_f, a_w_out, kv_norm, w_kv, b_w_q, b_w_out,
           ffn_w_gate, ffn_w_up, ffn_w_down, final_norm):
    b, s, d = x.shape
    assert b == 1 and d == N_HEADS * HEAD_DIM
    depth = attn_norm.shape[0]
    n_a = a_w_in.shape[0]
    h = x.reshape(s, d)
    w_in_t = jnp.swapaxes(a_w_in, 1, 2)
    w_qkv = cast_bf16(w_in_t, 3 * d)
    w_f = jnp.pad(w_in_t[:, 3 * d:], ((0, 0), (0, LANES - N_HEADS), (0, 0)))
    w_out_a, w_kv, w_q_b, w_out_b = (cast_bf16(w) for w in (a_w_out, w_kv[None], b_w_q, b_w_out))
    kv = None
    for layer in range(depth):
        g_attn = attn_norm[layer].reshape(1, d)
        if layer < n_a:
            b_f = jnp.pad(a_b_f[layer], (0, LANES - N_HEADS)).reshape(1, LANES)
            qkv, f_logit = norm_matmul(h, g_attn, w_qkv, layer, w_f, transposed=True, tm=1024,
                                       scaled_cols=d, col_scale=HEAD_DIM ** -0.5 * LOG2E)
            o = fox_attention(qkv, gate_cumsum(f_logit, b_f))
            h = matmul_residual(o, w_out_a, layer, h)
        else:
            j = layer - n_a
            q = norm_matmul(h, g_attn, w_q_b, j, tn=d,
                            scaled_cols=d, col_scale=HEAD_DIM ** -0.5 * LOG2E)
            o = sb_attention(q, kv)
            h = matmul_residual(o, w_out_b, j, h)
        h = ffn(h, ffn_norm[layer].reshape(1, d), ffn_w_gate, ffn_w_up, ffn_w_down, layer,
                final_norm.reshape(1, d), norm_out=layer == depth - 1)
        if layer == n_a - 1:
            kv = norm_matmul(h, kv_norm.reshape(1, d), w_kv, 0, tm=1024)
    return h.reshape(b, s, d)
```

```python
import functools

import jax
import jax.numpy as jnp
from jax import lax
from jax.experimental import pallas as pl
from jax.experimental.pallas import tpu as pltpu

N_HEADS = 16
HEAD_DIM = 128
EPS = 1e-6
LOG2E = 1.4426950408889634
LANES = 128
MXU_WIDTH = 256
VMEM_LIMIT = 56 * 1024 * 1024

KEY_CHUNK = 128
QUERY_CHUNK = MXU_WIDTH
PROB_CHUNK = 32
WEIGHT_FLOOR_LOG2 = -154.0

F32 = jnp.float32
BF16 = jnp.bfloat16


def _params(*sem):
    return pltpu.CompilerParams(dimension_semantics=sem, vmem_limit_bytes=VMEM_LIMIT)


def _rms_scale(x, g):
    ms = jnp.mean(x * x, axis=-1, keepdims=True)
    return x * lax.rsqrt(ms + EPS) * g


def _cast_kernel(x_ref, o_ref):
    o_ref[...] = x_ref[...].astype(o_ref.dtype)


def cast_bf16(w, rows=None, *, tr=256):
    n_layers, r, c = w.shape
    rows = r if rows is None else rows
    return pl.pallas_call(
        _cast_kernel, grid=(n_layers, rows // tr),
        in_specs=[pl.BlockSpec((None, tr, c), lambda l, i: (l, i, 0))],
        out_specs=pl.BlockSpec((None, tr, c), lambda l, i: (l, i, 0)),
        out_shape=jax.ShapeDtypeStruct((n_layers, rows, c), BF16),
        compiler_params=_params("parallel", "parallel"), name="cast_bf16")(w)


def _scaled_tile(acc, n_scaled_tiles, col_scale):
    if n_scaled_tiles == 0:
        return acc
    return acc * jnp.where(pl.program_id(1) < n_scaled_tiles, col_scale, 1.0)


def _project(xn, w, transposed):
    if transposed:
        return lax.dot_general(xn, w, (((1,), (1,)), ((), ())), preferred_element_type=F32)
    return jnp.dot(xn, w, preferred_element_type=F32)


def _norm_matmul_kernel(x_ref, g_ref, w_ref, o_ref, xn_ref, *, n_scaled_tiles, col_scale, transposed):
    @pl.when(pl.program_id(1) == 0)
    def _():
        xn_ref[...] = _rms_scale(x_ref[...], g_ref[...]).astype(BF16)

    acc = _project(xn_ref[...], w_ref[...], transposed)
    o_ref[...] = _scaled_tile(acc, n_scaled_tiles, col_scale).astype(o_ref.dtype)


def _norm_matmul_gate_kernel(x_ref, g_ref, w_ref, wf_ref, o_ref, f_ref, xn_ref,
                             *, n_scaled_tiles, col_scale, transposed):
    @pl.when(pl.program_id(1) == 0)
    def _():
        xn = _rms_scale(x_ref[...], g_ref[...]).astype(BF16)
        xn_ref[...] = xn
        f_ref[...] = _project(xn, wf_ref[...].astype(BF16), transposed)

    acc = _project(xn_ref[...], w_ref[...], transposed)
    o_ref[...] = _scaled_tile(acc, n_scaled_tiles, col_scale).astype(o_ref.dtype)


def norm_matmul(x, g, w, layer, wf=None, *, transposed=False, scaled_cols=0, col_scale=1.0,
                tm=512, tn=1024):
    s, d = x.shape
    n = w.shape[1] if transposed else w.shape[2]
    assert scaled_cols % tn == 0
    static = dict(n_scaled_tiles=scaled_cols // tn, col_scale=col_scale, transposed=transposed)
    grid = (s // tm, n // tn)
    x_spec = pl.BlockSpec((tm, d), lambda i, j: (i, 0))
    g_spec = pl.BlockSpec((1, d), lambda i, j: (0, 0))
    if transposed:
        w_spec = pl.BlockSpec((None, tn, d), lambda i, j: (layer, j, 0))
        wf_spec = pl.BlockSpec((None, LANES, d), lambda i, j: (layer, 0, 0))
    else:
        w_spec = pl.BlockSpec((None, d, tn), lambda i, j: (layer, 0, j))
        wf_spec = pl.BlockSpec((None, d, LANES), lambda i, j: (layer, 0, 0))
    o_spec = pl.BlockSpec((tm, tn), lambda i, j: (i, j))
    scratch = [pltpu.VMEM((tm, d), BF16)]
    if wf is None:
        return pl.pallas_call(
            functools.partial(_norm_matmul_kernel, **static), grid=grid,
            in_specs=[x_spec, g_spec, w_spec], out_specs=o_spec,
            out_shape=jax.ShapeDtypeStruct((s, n), BF16),
            scratch_shapes=scratch, compiler_params=_params("parallel", "arbitrary"),
            name="norm_matmul")(x, g, w)
    f_spec = pl.BlockSpec((tm, LANES), lambda i, j: (i, 0))
    return pl.pallas_call(
        functools.partial(_norm_matmul_gate_kernel, **static), grid=grid,
        in_specs=[x_spec, g_spec, w_spec, wf_spec], out_specs=[o_spec, f_spec],
        out_shape=[jax.ShapeDtypeStruct((s, n), BF16),
                   jax.ShapeDtypeStruct((s, LANES), F32)],
        scratch_shapes=scratch, compiler_params=_params("parallel", "arbitrary"),
        name="norm_matmul_gate")(x, g, w, wf)


def _norm_matmul_resident_kernel(x_ref, g_ref, w_ref, o_ref, w16_ref, *, col_scale):
    @pl.when(pl.program_id(0) == 0)
    def _():
        w16_ref[...] = w_ref[...].astype(BF16)

    xn = _rms_scale(x_ref[...], g_ref[...]).astype(BF16)
    acc = jnp.dot(xn, w16_ref[...], preferred_element_type=F32)
    o_ref[...] = (acc * col_scale).astype(o_ref.dtype)


def norm_matmul_resident(x, g, w, layer, *, col_scale, tm=512):
    s, d = x.shape
    n = w.shape[2]
    return pl.pallas_call(
        functools.partial(_norm_matmul_resident_kernel, col_scale=col_scale), grid=(s // tm,),
        in_specs=[pl.BlockSpec((tm, d), lambda i: (i, 0)),
                  pl.BlockSpec((1, d), lambda i: (0, 0)),
                  pl.BlockSpec((None, d, n), lambda i: (layer, 0, 0), pipeline_mode=pl.Buffered(1))],
        out_specs=pl.BlockSpec((tm, n), lambda i: (i, 0)),
        out_shape=jax.ShapeDtypeStruct((s, n), BF16),
        scratch_shapes=[pltpu.VMEM((d, n), BF16)],
        compiler_params=_params("arbitrary"), name="norm_matmul_resident")(x, g, w)


def _log_sigmoid(z):
    return jnp.minimum(z, 0.0) - jnp.log(1.0 + jnp.exp(-jnp.abs(z)))


def _split3(x):
    hi = x.astype(BF16)
    r = x - hi.astype(F32)
    mid = r.astype(BF16)
    lo = (r - mid.astype(F32)).astype(BF16)
    return hi, mid, lo


def _gate_cumsum_kernel(f_ref, b_ref, ccol_ref, carry_ref, *, tc):
    @pl.when(pl.program_id(0) == 0)
    def _():
        carry_ref[...] = jnp.zeros_like(carry_ref)

    log_f = _log_sigmoid(f_ref[...] + b_ref[...])
    row = lax.broadcasted_iota(jnp.int32, (tc, tc), 0)
    col = lax.broadcasted_iota(jnp.int32, (tc, tc), 1)
    tri = jnp.where(col <= row, 1.0, 0.0).astype(BF16)
    hi, mid, lo = _split3(log_f)
    c = (jnp.dot(tri, hi, preferred_element_type=F32)
         + jnp.dot(tri, mid, preferred_element_type=F32)
         + jnp.dot(tri, lo, preferred_element_type=F32)) + carry_ref[...]
    ccol_ref[...] = c
    carry_ref[...] = c[tc - 1:tc, :]


def gate_cumsum(f, b, *, tc=512):
    s = f.shape[0]
    return pl.pallas_call(
        functools.partial(_gate_cumsum_kernel, tc=tc), grid=(s // tc,),
        in_specs=[pl.BlockSpec((tc, LANES), lambda i: (i, 0)),
                  pl.BlockSpec((1, LANES), lambda i: (0, 0))],
        out_specs=pl.BlockSpec((tc, LANES), lambda i: (i, 0)),
        out_shape=jax.ShapeDtypeStruct((s, LANES), F32),
        scratch_shapes=[pltpu.VMEM((1, LANES), F32)],
        compiler_params=_params("arbitrary"), name="gate_cumsum")(f, b)


def _qk(q, k):
    return lax.dot_general(q, k, (((1,), (1,)), ((), ())), preferred_element_type=F32)


def _colmax8(x):
    return jnp.max(x.reshape(x.shape[0] // 8, 8, x.shape[1]), axis=0)


def _neg_abs(x):
    bits = lax.bitcast_convert_type(x, jnp.int32) | jnp.int32(-2 ** 31)
    return lax.bitcast_convert_type(bits, F32)


def _colsum8(x):
    return jnp.sum(x.reshape(x.shape[0] // 8, 8, x.shape[1]), axis=0)


def _fox_kernel(q_ref, k_ref, v_ref, ccol_ref, o_ref,
                vt_ref, csb_ref, kn_ref, s_ref, p_ref, acc_ref, lp_ref, *, t, hg):
    grp = pl.program_id(0)
    i = pl.program_id(1)
    n = i + 1
    heads = range(hg)

    def head_cols(hh):
        return slice(hh * HEAD_DIM, (hh + 1) * HEAD_DIM)

    @pl.when(i == 0)
    def _():
        for hh in heads:
            kn_ref[hh] = jnp.zeros((1, LANES), F32)

        def stage(c, carry):
            rows = pl.ds(pl.multiple_of(c * t, t), t)
            lane = lax.broadcasted_iota(jnp.int32, (t, LANES), 1)
            for hh in heads:
                vt_ref[hh, c] = v_ref[rows, head_cols(hh)].astype(F32).T.astype(BF16)
                cs = jnp.sum(jnp.where(lane == grp * hg + hh, ccol_ref[rows, :], 0.0),
                             axis=-1, keepdims=True)
                csb_ref[hh, rows, :] = jnp.broadcast_to(cs * LOG2E, (t, LANES))
                kf = k_ref[rows, head_cols(hh)].astype(F32)
                kn_ref[hh] = jnp.maximum(kn_ref[hh], jnp.max(jnp.sum(kf * kf, axis=-1, keepdims=True)))
            return carry
        lax.fori_loop(0, k_ref.shape[0] // t, stage, None)

    qts = [q_ref[:, head_cols(hh)].astype(F32).T.astype(BF16) for hh in heads]
    reach = [jnp.sqrt(jnp.sum(qts[hh].astype(F32) ** 2, axis=0, keepdims=True) * kn_ref[hh][:, 0:1])
             for hh in heads]
    acc_ref[...] = jnp.zeros(acc_ref.shape, F32)
    lp_ref[...] = jnp.zeros(lp_ref.shape, F32)
    diff = (lax.broadcasted_iota(jnp.int32, (KEY_CHUNK, QUERY_CHUNK), 0)
            - lax.broadcasted_iota(jnp.int32, (KEY_CHUNK, QUERY_CHUNK), 1))

    def scores(j, slot, diagonal):
        mparts = [[jnp.full((8, QUERY_CHUNK), -jnp.inf, F32) for _ in range(t // QUERY_CHUNK)]
                  for _ in heads]
        for c in range(t // KEY_CHUNK):
            k0 = pl.multiple_of(j * t + c * KEY_CHUNK, KEY_CHUNK)
            rows = slice(c * KEY_CHUNK, (c + 1) * KEY_CHUNK)
            for g in range(t // QUERY_CHUNK):
                cols = slice(g * QUERY_CHUNK, (g + 1) * QUERY_CHUNK)
                shift = g * QUERY_CHUNK - c * KEY_CHUNK
                for hh in heads:
                    if diagonal and shift <= -QUERY_CHUNK:
                        s_ref[hh, slot, rows, cols] = jnp.full((KEY_CHUNK, QUERY_CHUNK), -jnp.inf, F32)
                        continue
                    bias = csb_ref[hh, pl.ds(k0, KEY_CHUNK), :]
                    bias = jnp.concatenate([bias] * (QUERY_CHUNK // LANES), axis=1)
                    kc = k_ref[pl.ds(k0, KEY_CHUNK), head_cols(hh)]
                    s = jnp.dot(kc, qts[hh][:, cols], preferred_element_type=F32) - bias
                    if diagonal and shift < KEY_CHUNK - 1:
                        s = jnp.where(diff <= shift, s, -jnp.inf)
                    s_ref[hh, slot, rows, cols] = s
                    mparts[hh][g] = jnp.maximum(mparts[hh][g], _colmax8(s))
        return [jnp.concatenate(mparts[hh], axis=1) for hh in heads]

    def new_max(m_old, mpart):
        m_new = jnp.maximum(m_old, jnp.max(mpart, axis=0, keepdims=True))
        return m_new, jnp.exp2(m_old - m_new)

    def half(step, slot, carry, last=False):
        ms, alphas = carry
        j = i - step
        lsums = [jnp.zeros((8, t), F32) for _ in heads]
        for c in range(t // PROB_CHUNK):
            rows = slice(c * PROB_CHUNK, (c + 1) * PROB_CHUNK)
            for hh in heads:
                p = jnp.exp2(s_ref[hh, slot, rows, :] - ms[hh])
                p_ref[hh, rows, :] = p.astype(BF16)
                lsums[hh] = lsums[hh] + _colsum8(p)
        if not last:
            mparts = scores(j - 1, 1 - slot, False)
            new = [new_max(ms[hh], mparts[hh]) for hh in heads]
            carry = tuple(m for m, _ in new), tuple(a for _, a in new)
        for hh in heads:
            lp_ref[hh] = alphas[hh] * lp_ref[hh] + lsums[hh]
            acc_ref[hh] = alphas[hh] * acc_ref[hh] + jnp.dot(vt_ref[hh, j], p_ref[hh],
                                                            preferred_element_type=F32)
        return carry

    def tile_live(j, ms):
        last_key = jnp.maximum(j + 1, 1) * t - 1
        gaps = [reach[hh] - csb_ref[hh, pl.ds(last_key, 1), :][:, 0:1] - ms[hh] for hh in heads]
        return jnp.max(functools.reduce(jnp.maximum, gaps)) > WEIGHT_FLOOR_LOG2

    first = [new_max(jnp.full((1, t), -jnp.inf, F32), mp) for mp in scores(i, 0, True)]
    carry = tuple(m for m, _ in first), tuple(a for _, a in first)
    n_pairs = (n - 1) // 2

    def pair(c):
        mm, _, ms, alphas = c
        carry = half(2 * mm, 0, (ms, alphas))
        alive = tile_live(i - (2 * mm + 3), carry[0])
        return (mm + 1, alive) + half(2 * mm + 1, 1, carry)

    mm, alive, ms, alphas = lax.while_loop(
        lambda c: (c[0] < n_pairs) & c[1], pair, (jnp.int32(0), tile_live(i - 1, carry[0])) + carry)
    step = 2 * mm
    single = (step == n - 1) | jnp.logical_not(alive)

    @pl.when(single)
    def _():
        half(step, 0, (ms, alphas), last=True)

    @pl.when(jnp.logical_not(single))
    def _():
        half(step + 1, 1, half(step, 0, (ms, alphas)), last=True)

    for hh in heads:
        l = jnp.sum(lp_ref[hh], axis=0, keepdims=True)
        o_ref[:, head_cols(hh)] = (acc_ref[hh] / l).T.astype(o_ref.dtype)


def fox_attention(qkv, ccol, *, t=256, hg=4):
    s = qkv.shape[0]
    d = qkv.shape[1] // 3
    ng = d // (hg * HEAD_DIM)
    w = hg * HEAD_DIM
    return pl.pallas_call(
        functools.partial(_fox_kernel, t=t, hg=hg),
        grid=(ng, s // t),
        in_specs=[pl.BlockSpec((t, w), lambda g, i: (i, g)),
                  pl.BlockSpec((s, w), lambda g, i: (0, ng + g), pipeline_mode=pl.Buffered(1)),
                  pl.BlockSpec((s, w), lambda g, i: (0, 2 * ng + g), pipeline_mode=pl.Buffered(1)),
                  pl.BlockSpec((s, LANES), lambda g, i: (0, 0), pipeline_mode=pl.Buffered(1))],
        out_specs=pl.BlockSpec((t, w), lambda g, i: (i, g)),
        out_shape=jax.ShapeDtypeStruct((s, d), BF16),
        scratch_shapes=[pltpu.VMEM((hg, s // t, HEAD_DIM, t), BF16),
                        pltpu.VMEM((hg, s, LANES), F32),
                        pltpu.VMEM((hg, 1, LANES), F32),
                        pltpu.VMEM((hg, 2, t, t), F32),
                        pltpu.VMEM((hg, t, t), BF16),
                        pltpu.VMEM((hg, HEAD_DIM, t), F32), pltpu.VMEM((hg, 8, t), F32)],
        compiler_params=_params("arbitrary", "arbitrary"),
        name="fox_attention")(qkv, qkv, qkv, ccol)


def _sb_kernel(q_ref, k_ref, v_ref, o_ref, vt_ref, lb_ref, lm_ref, w_ref, acc_ref, *, t, hg):
    i = pl.program_id(1)
    n_sub = t // MXU_WIDTH
    heads = range(hg)

    def head_cols(hh):
        return slice(hh * HEAD_DIM, (hh + 1) * HEAD_DIM)

    @pl.when(i == 0)
    def _():
        def stage(c, carry):
            rows = pl.ds(pl.multiple_of(c * t, t), t)
            for hh in heads:
                vt_ref[hh, c] = v_ref[rows, head_cols(hh)].astype(F32).T.astype(BF16)
            return carry
        lax.fori_loop(0, k_ref.shape[0] // t, stage, None)

    qts = [q_ref[:, head_cols(hh)].astype(F32).T.astype(BF16) for hh in heads]
    acc_ref[...] = jnp.zeros(acc_ref.shape, F32)
    diff = (lax.broadcasted_iota(jnp.int32, (KEY_CHUNK, QUERY_CHUNK), 0)
            - lax.broadcasted_iota(jnp.int32, (KEY_CHUNK, QUERY_CHUNK), 1))
    later = jnp.where(lax.broadcasted_iota(jnp.int32, (MXU_WIDTH, MXU_WIDTH), 1)
                      > lax.broadcasted_iota(jnp.int32, (MXU_WIDTH, MXU_WIDTH), 0),
                      1.0, 0.0).astype(BF16)

    def scores(j, slot, diagonal):
        for c in range(t // KEY_CHUNK):
            k0 = pl.multiple_of(j * t + c * KEY_CHUNK, KEY_CHUNK)
            rows = slice(c * KEY_CHUNK, (c + 1) * KEY_CHUNK)
            for g in range(t // QUERY_CHUNK):
                cols = slice(g * QUERY_CHUNK, (g + 1) * QUERY_CHUNK)
                shift = g * QUERY_CHUNK - c * KEY_CHUNK
                for hh in heads:
                    if diagonal and shift <= -(QUERY_CHUNK - 1):
                        lb_ref[hh, slot, rows, cols] = jnp.full((KEY_CHUNK, QUERY_CHUNK), -jnp.inf, F32)
                        lm_ref[hh, slot, rows, cols] = jnp.zeros((KEY_CHUNK, QUERY_CHUNK), BF16)
                        continue
                    kc = k_ref[pl.ds(k0, KEY_CHUNK), head_cols(hh)]
                    z2 = jnp.dot(kc, qts[hh][:, cols], preferred_element_type=F32)
                    lb2 = jnp.minimum(z2, 0.0) - jnp.log(1.0 + jnp.exp2(_neg_abs(z2))) * LOG2E
                    l1m2 = lb2 - z2
                    if diagonal and shift < KEY_CHUNK:
                        visible = diff < shift
                        lb2 = jnp.where(visible, lb2, -jnp.inf)
                        l1m2 = jnp.where(visible, l1m2, 0.0)
                    lb_ref[hh, slot, rows, cols] = lb2
                    lm_ref[hh, slot, rows, cols] = l1m2.astype(BF16)

    def weights(slot, runs):
        runs = list(runs)
        for b in reversed(range(n_sub)):
            rows = slice(b * MXU_WIDTH, (b + 1) * MXU_WIDTH)
            totals = [[] for _ in heads]
            for g in range(t // QUERY_CHUNK):
                cols = slice(g * QUERY_CHUNK, (g + 1) * QUERY_CHUNK)
                for hh in heads:
                    lm = lm_ref[hh, slot, rows, cols]
                    within = jnp.dot(later, lm, preferred_element_type=F32)
                    w = jnp.exp2(lb_ref[hh, slot, rows, cols] + within + runs[hh][:, cols])
                    w_ref[hh, slot, rows, cols] = w.astype(BF16)
                    totals[hh].append(within[0:1, :] + lm[0:1, :].astype(F32))
            for hh in heads:
                runs[hh] = runs[hh] + jnp.concatenate(totals[hh], axis=1)
        return tuple(runs)

    def value_product(j, slot):
        for hh in heads:
            acc_ref[hh] += jnp.dot(vt_ref[hh, j], w_ref[hh, slot], preferred_element_type=F32)

    def live(runs):
        return jnp.max(functools.reduce(jnp.maximum, runs)) > WEIGHT_FLOOR_LOG2

    def tile(j, runs):
        scores(j, 0, False)
        runs = weights(0, runs)
        go = live(runs)
        value_product(j, 0)
        return runs, go

    start = tuple(jnp.zeros((1, t), F32) for _ in heads)

    @pl.when(i == 0)
    def _():
        scores(0, 0, True)
        weights(0, start)
        value_product(0, 0)

    @pl.when(i > 0)
    def _():
        scores(i, 0, True)
        scores(i - 1, 1, False)
        runs = weights(1, weights(0, start))
        go = live(runs)
        value_product(i, 0)
        value_product(i - 1, 1)
        lax.while_loop(lambda c: (c[0] >= 0) & c[2],
                       lambda c: (c[0] - 1,) + tile(c[0], c[1]), (i - 2, runs, go))

    for hh in heads:
        o_ref[:, head_cols(hh)] = acc_ref[hh].T.astype(o_ref.dtype)


def sb_attention(q, kv, *, t=256, hg=4):
    s, d = q.shape
    ng = d // (hg * HEAD_DIM)
    w = hg * HEAD_DIM
    return pl.pallas_call(
        functools.partial(_sb_kernel, t=t, hg=hg),
        grid=(ng, s // t),
        in_specs=[pl.BlockSpec((t, w), lambda g, i: (i, g)),
                  pl.BlockSpec((s, w), lambda g, i: (0, g)),
                  pl.BlockSpec((s, w), lambda g, i: (0, ng + g))],
        out_specs=pl.BlockSpec((t, w), lambda g, i: (i, g)),
        out_shape=jax.ShapeDtypeStruct((s, d), BF16),
        scratch_shapes=[pltpu.VMEM((hg, s // t, HEAD_DIM, t), BF16),
                        pltpu.VMEM((hg, 2, t, t), F32),
                        pltpu.VMEM((hg, 2, t, t), BF16),
                        pltpu.VMEM((hg, 2, t, t), BF16),
                        pltpu.VMEM((hg, HEAD_DIM, t), F32)],
        compiler_params=_params("arbitrary", "arbitrary"),
        name="sb_attention")(q, kv, kv)


def _matmul_residual_kernel(a_ref, w_ref, h_ref, o_ref, w16_ref):
    @pl.when(pl.program_id(0) == 0)
    def _():
        w16_ref[...] = w_ref[...].astype(BF16)

    o_ref[...] = h_ref[...] + jnp.dot(a_ref[...], w16_ref[...], preferred_element_type=F32)


def matmul_residual(a, w, layer, h, *, tm=512):
    s, k = a.shape
    n = w.shape[2]
    return pl.pallas_call(
        _matmul_residual_kernel, grid=(s // tm,),
        in_specs=[pl.BlockSpec((tm, k), lambda i: (i, 0)),
                  pl.BlockSpec((None, k, n), lambda i: (layer, 0, 0), pipeline_mode=pl.Buffered(1)),
                  pl.BlockSpec((tm, n), lambda i: (i, 0))],
        out_specs=pl.BlockSpec((tm, n), lambda i: (i, 0)),
        out_shape=jax.ShapeDtypeStruct((s, n), F32),
        scratch_shapes=[pltpu.VMEM((k, n), BF16)],
        compiler_params=_params("arbitrary"),
        name="matmul_residual")(a, w, h)


def _ffn_step(h_ref, g_ref, weights, gout_ref, o_ref, xn_ref, norm_out):
    @pl.when(pl.program_id(1) == 0)
    def _():
        x = h_ref[...]
        xn_ref[...] = _rms_scale(x, g_ref[...]).astype(BF16)
        o_ref[...] = x

    wg, wu, wd = weights()
    xn = xn_ref[...]
    gate = jnp.dot(xn, wg, preferred_element_type=F32)
    up = jnp.dot(xn, wu, preferred_element_type=F32)
    act = (gate * jax.nn.sigmoid(gate) * up).astype(BF16)
    o_ref[...] += jnp.dot(act, wd, preferred_element_type=F32)

    if norm_out:
        @pl.when(pl.program_id(1) == pl.num_programs(1) - 1)
        def _():
            o_ref[...] = _rms_scale(o_ref[...], gout_ref[...])


def _ffn_first_kernel(h_ref, g_ref, wg_ref, wu_ref, wd_ref, gout_ref,
                      o_ref, wg_out_ref, wu_out_ref, wd_out_ref, xn_ref, *, norm_out):
    def weights():
        wg, wu, wd = (w[...].astype(BF16) for w in (wg_ref, wu_ref, wd_ref))
        wg_out_ref[...] = wg
        wu_out_ref[...] = wu
        wd_out_ref[...] = wd
        return wg, wu, wd

    _ffn_step(h_ref, g_ref, weights, gout_ref, o_ref, xn_ref, norm_out)


def _ffn_rest_kernel(h_ref, g_ref, wg_ref, wu_ref, wd_ref, gout_ref, first_ref, o_ref, xn_ref, *, norm_out):
    del first_ref
    _ffn_step(h_ref, g_ref, lambda: (wg_ref[...], wu_ref[...], wd_ref[...]), gout_ref, o_ref, xn_ref,
              norm_out)


def ffn(h, g, wg, wu, wd, layer, g_out, *, norm_out, tm=512, tf=512, tm_first=1024, tf_first=256):
    s, d = h.shape
    dff = wg.shape[2]
    vec = pl.BlockSpec((1, d), lambda i, f: (0, 0))
    skip = tm_first // tm

    def w_specs(lyr, t_f):
        return [pl.BlockSpec((None, d, t_f), lambda i, f: (lyr, 0, f)),
                pl.BlockSpec((None, d, t_f), lambda i, f: (lyr, 0, f)),
                pl.BlockSpec((None, t_f, d), lambda i, f: (lyr, f, 0))]

    first, wg16, wu16, wd16 = pl.pallas_call(
        functools.partial(_ffn_first_kernel, norm_out=norm_out), grid=(1, dff // tf_first),
        in_specs=[pl.BlockSpec((tm_first, d), lambda i, f: (0, 0), pipeline_mode=pl.Buffered(1)), vec,
                  *w_specs(layer, tf_first), vec],
        out_specs=[pl.BlockSpec((tm_first, d), lambda i, f: (0, 0)), *w_specs(0, tf_first)],
        out_shape=[jax.ShapeDtypeStruct((s, d), F32),
                   jax.ShapeDtypeStruct((1, d, dff), BF16), jax.ShapeDtypeStruct((1, d, dff), BF16),
                   jax.ShapeDtypeStruct((1, dff, d), BF16)],
        scratch_shapes=[pltpu.VMEM((tm_first, d), BF16)],
        compiler_params=_params("arbitrary", "arbitrary"),
        name="ffn_first")(h, g, wg, wu, wd, g_out)
    return pl.pallas_call(
        functools.partial(_ffn_rest_kernel, norm_out=norm_out), grid=((s - tm_first) // tm, dff // tf),
        in_specs=[pl.BlockSpec((tm, d), lambda i, f: (i + skip, 0)), vec, *w_specs(0, tf), vec,
                  pl.BlockSpec(memory_space=pl.ANY)],
        out_specs=pl.BlockSpec((tm, d), lambda i, f: (i + skip, 0)),
        out_shape=jax.ShapeDtypeStruct((s, d), F32),
        input_output_aliases={6: 0},
        scratch_shapes=[pltpu.VMEM((tm, d), BF16)],
        compiler_params=_params("parallel", "arbitrary"),
        name="ffn_rest")(h, g, wg16, wu16, wd16, g_out, first)


def kernel(x, attn_norm, ffn_norm, a_w_in, a_b_f, a_w_out, kv_norm, w_kv, b_w_q, b_w_out,
           ffn_w_gate, ffn_w_up, ffn_w_down, final_norm):
    b, s, d = x.shape
    assert b == 1 and d == N_HEADS * HEAD_DIM
    depth = attn_norm.shape[0]
    n_a = a_w_in.shape[0]
    h = x.reshape(s, d)
    w_in_t = jnp.swapaxes(a_w_in, 1, 2)
    w_qkv = cast_bf16(w_in_t, 3 * d)
    w_f = jnp.pad(w_in_t[:, 3 * d:], ((0, 0), (0, LANES - N_HEADS), (0, 0)))
    w_kv = cast_bf16(w_kv[None])
    kv = None
    for layer in range(depth):
        g_attn = attn_norm[layer].reshape(1, d)
        if layer < n_a:
            b_f = jnp.pad(a_b_f[layer], (0, LANES - N_HEADS)).reshape(1, LANES)
            qkv, f_logit = norm_matmul(h, g_attn, w_qkv, layer, w_f, transposed=True, tm=1024,
                                       scaled_cols=d, col_scale=HEAD_DIM ** -0.5 * LOG2E)
            o = fox_attention(qkv, gate_cumsum(f_logit, b_f))
            h = matmul_residual(o, a_w_out, layer, h)
        else:
            j = layer - n_a
            q = norm_matmul_resident(h, g_attn, b_w_q, j, col_scale=HEAD_DIM ** -0.5 * LOG2E)
            o = sb_attention(q, kv)
            h = matmul_residual(o, b_w_out, j, h)
        h = ffn(h, ffn_norm[layer].reshape(1, d), ffn_w_gate, ffn_w_up, ffn_w_down, layer,
                final_norm.reshape(1, d), norm_out=layer == depth - 1)
        if layer == n_a - 1:
            kv = norm_matmul(h, kv_norm.reshape(1, d), w_kv, 0, tm=1024)
    return h.reshape(b, s, d)
```

```python
import functools

import jax
import jax.numpy as jnp
from jax import lax
from jax.experimental import pallas as pl
from jax.experimental.pallas import tpu as pltpu

N_HEADS = 16
HEAD_DIM = 128
EPS = 1e-6
LOG2E = 1.4426950408889634
LANES = 128
MXU_WIDTH = 256
VMEM_LIMIT = 56 * 1024 * 1024

KEY_CHUNK = 128
QUERY_CHUNK = MXU_WIDTH
PROB_CHUNK = 32
WEIGHT_FLOOR_LOG2 = -154.0

F32 = jnp.float32
BF16 = jnp.bfloat16


def _params(*sem):
    return pltpu.CompilerParams(dimension_semantics=sem, vmem_limit_bytes=VMEM_LIMIT)


def _rms_scale(x, g):
    ms = jnp.mean(x * x, axis=-1, keepdims=True)
    return x * lax.rsqrt(ms + EPS) * g


def _cast_kernel(x_ref, o_ref):
    o_ref[...] = x_ref[...].astype(o_ref.dtype)


def cast_bf16(w, rows=None, *, tr=256):
    n_layers, r, c = w.shape
    rows = r if rows is None else rows
    return pl.pallas_call(
        _cast_kernel, grid=(n_layers, rows // tr),
        in_specs=[pl.BlockSpec((None, tr, c), lambda l, i: (l, i, 0))],
        out_specs=pl.BlockSpec((None, tr, c), lambda l, i: (l, i, 0)),
        out_shape=jax.ShapeDtypeStruct((n_layers, rows, c), BF16),
        compiler_params=_params("parallel", "parallel"), name="cast_bf16")(w)


def _scaled_tile(acc, n_scaled_tiles, col_scale):
    if n_scaled_tiles == 0:
        return acc
    return acc * jnp.where(pl.program_id(1) < n_scaled_tiles, col_scale, 1.0)


def _project(xn, w, transposed):
    if transposed:
        return lax.dot_general(xn, w, (((1,), (1,)), ((), ())), preferred_element_type=F32)
    return jnp.dot(xn, w, preferred_element_type=F32)


def _norm_matmul_kernel(x_ref, g_ref, w_ref, o_ref, xn_ref, *, n_scaled_tiles, col_scale, transposed):
    @pl.when(pl.program_id(1) == 0)
    def _():
        xn_ref[...] = _rms_scale(x_ref[...], g_ref[...]).astype(BF16)

    acc = _project(xn_ref[...], w_ref[...], transposed)
    o_ref[...] = _scaled_tile(acc, n_scaled_tiles, col_scale).astype(o_ref.dtype)


def _norm_matmul_gate_kernel(x_ref, g_ref, w_ref, wf_ref, o_ref, f_ref, xn_ref,
                             *, n_scaled_tiles, col_scale, transposed):
    @pl.when(pl.program_id(1) == 0)
    def _():
        xn = _rms_scale(x_ref[...], g_ref[...]).astype(BF16)
        xn_ref[...] = xn
        f_ref[...] = _project(xn, wf_ref[...].astype(BF16), transposed)

    acc = _project(xn_ref[...], w_ref[...], transposed)
    o_ref[...] = _scaled_tile(acc, n_scaled_tiles, col_scale).astype(o_ref.dtype)


def norm_matmul(x, g, w, layer, wf=None, *, transposed=False, scaled_cols=0, col_scale=1.0,
                tm=1024, tn=1024):
    s, d = x.shape
    n = w.shape[1] if transposed else w.shape[2]
    assert scaled_cols % tn == 0
    static = dict(n_scaled_tiles=scaled_cols // tn, col_scale=col_scale, transposed=transposed)
    grid = (s // tm, n // tn)
    x_spec = pl.BlockSpec((tm, d), lambda i, j: (i, 0))
    g_spec = pl.BlockSpec((1, d), lambda i, j: (0, 0))
    if transposed:
        w_spec = pl.BlockSpec((None, tn, d), lambda i, j: (layer, j, 0))
        wf_spec = pl.BlockSpec((None, LANES, d), lambda i, j: (layer, 0, 0))
    else:
        w_spec = pl.BlockSpec((None, d, tn), lambda i, j: (layer, 0, j))
        wf_spec = pl.BlockSpec((None, d, LANES), lambda i, j: (layer, 0, 0))
    o_spec = pl.BlockSpec((tm, tn), lambda i, j: (i, j))
    scratch = [pltpu.VMEM((tm, d), BF16)]
    if wf is None:
        return pl.pallas_call(
            functools.partial(_norm_matmul_kernel, **static), grid=grid,
            in_specs=[x_spec, g_spec, w_spec], out_specs=o_spec,
            out_shape=jax.ShapeDtypeStruct((s, n), BF16),
            scratch_shapes=scratch, compiler_params=_params("parallel", "arbitrary"),
            name="norm_matmul")(x, g, w)
    f_spec = pl.BlockSpec((tm, LANES), lambda i, j: (i, 0))
    return pl.pallas_call(
        functools.partial(_norm_matmul_gate_kernel, **static), grid=grid,
        in_specs=[x_spec, g_spec, w_spec, wf_spec], out_specs=[o_spec, f_spec],
        out_shape=[jax.ShapeDtypeStruct((s, n), BF16),
                   jax.ShapeDtypeStruct((s, LANES), F32)],
        scratch_shapes=scratch, compiler_params=_params("parallel", "arbitrary"),
        name="norm_matmul_gate")(x, g, w, wf)


def _norm_matmul_resident_kernel(x_ref, g_ref, w_ref, o_ref, w16_ref, *, col_scale):
    @pl.when(pl.program_id(0) == 0)
    def _():
        w16_ref[...] = w_ref[...].astype(BF16)

    xn = _rms_scale(x_ref[...], g_ref[...]).astype(BF16)
    acc = jnp.dot(xn, w16_ref[...], preferred_element_type=F32)
    o_ref[...] = (acc * col_scale).astype(o_ref.dtype)


def norm_matmul_resident(x, g, w, layer, *, col_scale, tm=512):
    s, d = x.shape
    n = w.shape[2]
    return pl.pallas_call(
        functools.partial(_norm_matmul_resident_kernel, col_scale=col_scale), grid=(s // tm,),
        in_specs=[pl.BlockSpec((tm, d), lambda i: (i, 0)),
                  pl.BlockSpec((1, d), lambda i: (0, 0)),
                  pl.BlockSpec((None, d, n), lambda i: (layer, 0, 0), pipeline_mode=pl.Buffered(1))],
        out_specs=pl.BlockSpec((tm, n), lambda i: (i, 0)),
        out_shape=jax.ShapeDtypeStruct((s, n), BF16),
        scratch_shapes=[pltpu.VMEM((d, n), BF16)],
        compiler_params=_params("arbitrary"), name="norm_matmul_resident")(x, g, w)


def _log_sigmoid(z):
    return jnp.minimum(z, 0.0) - jnp.log(1.0 + jnp.exp(-jnp.abs(z)))


def _split3(x):
    hi = x.astype(BF16)
    r = x - hi.astype(F32)
    mid = r.astype(BF16)
    lo = (r - mid.astype(F32)).astype(BF16)
    return hi, mid, lo


def _gate_cumsum_kernel(f_ref, b_ref, ccol_ref, carry_ref, *, tc):
    @pl.when(pl.program_id(0) == 0)
    def _():
        carry_ref[...] = jnp.zeros_like(carry_ref)

    log_f = _log_sigmoid(f_ref[...] + b_ref[...])
    row = lax.broadcasted_iota(jnp.int32, (tc, tc), 0)
    col = lax.broadcasted_iota(jnp.int32, (tc, tc), 1)
    tri = jnp.where(col <= row, 1.0, 0.0).astype(BF16)
    hi, mid, lo = _split3(log_f)
    c = (jnp.dot(tri, hi, preferred_element_type=F32)
         + jnp.dot(tri, mid, preferred_element_type=F32)
         + jnp.dot(tri, lo, preferred_element_type=F32)) + carry_ref[...]
    ccol_ref[...] = c
    carry_ref[...] = c[tc - 1:tc, :]


def gate_cumsum(f, b, *, tc=512):
    s = f.shape[0]
    return pl.pallas_call(
        functools.partial(_gate_cumsum_kernel, tc=tc), grid=(s // tc,),
        in_specs=[pl.BlockSpec((tc, LANES), lambda i: (i, 0)),
                  pl.BlockSpec((1, LANES), lambda i: (0, 0))],
        out_specs=pl.BlockSpec((tc, LANES), lambda i: (i, 0)),
        out_shape=jax.ShapeDtypeStruct((s, LANES), F32),
        scratch_shapes=[pltpu.VMEM((1, LANES), F32)],
        compiler_params=_params("arbitrary"), name="gate_cumsum")(f, b)


def _colmax8(x):
    return jnp.max(x.reshape(x.shape[0] // 8, 8, x.shape[1]), axis=0)


def _neg_abs(x):
    bits = lax.bitcast_convert_type(x, jnp.int32) | jnp.int32(-2 ** 31)
    return lax.bitcast_convert_type(bits, F32)


def _colsum8(x):
    return jnp.sum(x.reshape(x.shape[0] // 8, 8, x.shape[1]), axis=0)


def _fox_kernel(q_ref, k_ref, v_ref, ccol_ref, o_ref,
                vt_ref, csb_ref, kn_ref, s_ref, p_ref, acc_ref, lp_ref, *, t, hg):
    grp = pl.program_id(0)
    i = pl.program_id(1)
    n = i + 1
    heads = range(hg)

    def head_cols(hh):
        return slice(hh * HEAD_DIM, (hh + 1) * HEAD_DIM)

    @pl.when(i == 0)
    def _():
        for hh in heads:
            kn_ref[hh] = jnp.zeros((1, LANES), F32)

        def stage(c, carry):
            rows = pl.ds(pl.multiple_of(c * t, t), t)
            lane = lax.broadcasted_iota(jnp.int32, (t, LANES), 1)
            for hh in heads:
                vt_ref[hh, c] = v_ref[rows, head_cols(hh)].astype(F32).T.astype(BF16)
                cs = jnp.sum(jnp.where(lane == grp * hg + hh, ccol_ref[rows, :], 0.0),
                             axis=-1, keepdims=True)
                csb_ref[hh, rows, :] = jnp.broadcast_to(cs * LOG2E, (t, LANES))
                kf = k_ref[rows, head_cols(hh)].astype(F32)
                kn_ref[hh] = jnp.maximum(kn_ref[hh], jnp.max(jnp.sum(kf * kf, axis=-1, keepdims=True)))
            return carry
        lax.fori_loop(0, k_ref.shape[0] // t, stage, None)

    qts = [q_ref[:, head_cols(hh)].astype(F32).T.astype(BF16) for hh in heads]
    reach = [jnp.sqrt(jnp.sum(qts[hh].astype(F32) ** 2, axis=0, keepdims=True) * kn_ref[hh][:, 0:1])
             for hh in heads]
    acc_ref[...] = jnp.zeros(acc_ref.shape, F32)
    lp_ref[...] = jnp.zeros(lp_ref.shape, F32)
    diff = (lax.broadcasted_iota(jnp.int32, (KEY_CHUNK, QUERY_CHUNK), 0)
            - lax.broadcasted_iota(jnp.int32, (KEY_CHUNK, QUERY_CHUNK), 1))

    def scores(j, slot, diagonal):
        mparts = [[jnp.full((8, QUERY_CHUNK), -jnp.inf, F32) for _ in range(t // QUERY_CHUNK)]
                  for _ in heads]
        for c in range(t // KEY_CHUNK):
            k0 = pl.multiple_of(j * t + c * KEY_CHUNK, KEY_CHUNK)
            rows = slice(c * KEY_CHUNK, (c + 1) * KEY_CHUNK)
            for g in range(t // QUERY_CHUNK):
                cols = slice(g * QUERY_CHUNK, (g + 1) * QUERY_CHUNK)
                shift = g * QUERY_CHUNK - c * KEY_CHUNK
                for hh in heads:
                    if diagonal and shift <= -QUERY_CHUNK:
                        s_ref[hh, slot, rows, cols] = jnp.full((KEY_CHUNK, QUERY_CHUNK), -jnp.inf, F32)
                        continue
                    bias = csb_ref[hh, pl.ds(k0, KEY_CHUNK), :]
                    bias = jnp.concatenate([bias] * (QUERY_CHUNK // LANES), axis=1)
                    kc = k_ref[pl.ds(k0, KEY_CHUNK), head_cols(hh)]
                    s = jnp.dot(kc, qts[hh][:, cols], preferred_element_type=F32) - bias
                    if diagonal and shift < KEY_CHUNK - 1:
                        s = jnp.where(diff <= shift, s, -jnp.inf)
                    s_ref[hh, slot, rows, cols] = s
                    mparts[hh][g] = jnp.maximum(mparts[hh][g], _colmax8(s))
        return [jnp.concatenate(mparts[hh], axis=1) for hh in heads]

    def new_max(m_old, mpart):
        m_new = jnp.maximum(m_old, jnp.max(mpart, axis=0, keepdims=True))
        return m_new, jnp.exp2(m_old - m_new)

    def half(step, slot, carry, last=False):
        ms, alphas = carry
        j = i - step
        lsums = [jnp.zeros((8, t), F32) for _ in heads]
        for c in range(t // PROB_CHUNK):
            rows = slice(c * PROB_CHUNK, (c + 1) * PROB_CHUNK)
            for hh in heads:
                p = jnp.exp2(s_ref[hh, slot, rows, :] - ms[hh])
                p_ref[hh, rows, :] = p.astype(BF16)
                lsums[hh] = lsums[hh] + _colsum8(p)
        if not last:
            mparts = scores(j - 1, 1 - slot, False)
            new = [new_max(ms[hh], mparts[hh]) for hh in heads]
            carry = tuple(m for m, _ in new), tuple(a for _, a in new)
        for hh in heads:
            lp_ref[hh] = alphas[hh] * lp_ref[hh] + lsums[hh]
            acc_ref[hh] = alphas[hh] * acc_ref[hh] + jnp.dot(vt_ref[hh, j], p_ref[hh],
                                                            preferred_element_type=F32)
        return carry

    def tile_live(j, ms):
        last_key = jnp.maximum(j + 1, 1) * t - 1
        gaps = [reach[hh] - csb_ref[hh, pl.ds(last_key, 1), :][:, 0:1] - ms[hh] for hh in heads]
        return jnp.max(functools.reduce(jnp.maximum, gaps)) > WEIGHT_FLOOR_LOG2

    first = [new_max(jnp.full((1, t), -jnp.inf, F32), mp) for mp in scores(i, 0, True)]
    carry = tuple(m for m, _ in first), tuple(a for _, a in first)
    n_pairs = (n - 1) // 2

    def pair(c):
        mm, _, ms, alphas = c
        carry = half(2 * mm, 0, (ms, alphas))
        alive = tile_live(i - (2 * mm + 3), carry[0])
        return (mm + 1, alive) + half(2 * mm + 1, 1, carry)

    mm, alive, ms, alphas = lax.while_loop(
        lambda c: (c[0] < n_pairs) & c[1], pair, (jnp.int32(0), tile_live(i - 1, carry[0])) + carry)
    step = 2 * mm
    single = (step == n - 1) | jnp.logical_not(alive)

    @pl.when(single)
    def _():
        half(step, 0, (ms, alphas), last=True)

    @pl.when(jnp.logical_not(single))
    def _():
        half(step + 1, 1, half(step, 0, (ms, alphas)), last=True)

    for hh in heads:
        l = jnp.sum(lp_ref[hh], axis=0, keepdims=True)
        o_ref[:, head_cols(hh)] = (acc_ref[hh] / l).T.astype(o_ref.dtype)


def fox_attention(qkv, ccol, *, t=256, hg=4):
    s = qkv.shape[0]
    d = qkv.shape[1] // 3
    ng = d // (hg * HEAD_DIM)
    w = hg * HEAD_DIM
    return pl.pallas_call(
        functools.partial(_fox_kernel, t=t, hg=hg),
        grid=(ng, s // t),
        in_specs=[pl.BlockSpec((t, w), lambda g, i: (i, g)),
                  pl.BlockSpec((s, w), lambda g, i: (0, ng + g), pipeline_mode=pl.Buffered(1)),
                  pl.BlockSpec((s, w), lambda g, i: (0, 2 * ng + g), pipeline_mode=pl.Buffered(1)),
                  pl.BlockSpec((s, LANES), lambda g, i: (0, 0), pipeline_mode=pl.Buffered(1))],
        out_specs=pl.BlockSpec((t, w), lambda g, i: (i, g)),
        out_shape=jax.ShapeDtypeStruct((s, d), BF16),
        scratch_shapes=[pltpu.VMEM((hg, s // t, HEAD_DIM, t), BF16),
                        pltpu.VMEM((hg, s, LANES), F32),
                        pltpu.VMEM((hg, 1, LANES), F32),
                        pltpu.VMEM((hg, 2, t, t), F32),
                        pltpu.VMEM((hg, t, t), BF16),
                        pltpu.VMEM((hg, HEAD_DIM, t), F32), pltpu.VMEM((hg, 8, t), F32)],
        compiler_params=_params("arbitrary", "arbitrary"),
        name="fox_attention")(qkv, qkv, qkv, ccol)


def _sb_kernel(q_ref, k_ref, v_ref, o_ref, vt_ref, lb_ref, lm_ref, w_ref, acc_ref, *, t, hg):
    i = pl.program_id(1)
    n_sub = t // MXU_WIDTH
    heads = range(hg)

    def head_cols(hh):
        return slice(hh * HEAD_DIM, (hh + 1) * HEAD_DIM)

    @pl.when(i == 0)
    def _():
        def stage(c, carry):
            rows = pl.ds(pl.multiple_of(c * t, t), t)
            for hh in heads:
                vt_ref[hh, c] = v_ref[rows, head_cols(hh)].astype(F32).T.astype(BF16)
            return carry
        lax.fori_loop(0, k_ref.shape[0] // t, stage, None)

    qts = [q_ref[:, head_cols(hh)].astype(F32).T.astype(BF16) for hh in heads]
    acc_ref[...] = jnp.zeros(acc_ref.shape, F32)
    diff = (lax.broadcasted_iota(jnp.int32, (KEY_CHUNK, QUERY_CHUNK), 0)
            - lax.broadcasted_iota(jnp.int32, (KEY_CHUNK, QUERY_CHUNK), 1))
    later = jnp.where(lax.broadcasted_iota(jnp.int32, (MXU_WIDTH, MXU_WIDTH), 1)
                      > lax.broadcasted_iota(jnp.int32, (MXU_WIDTH, MXU_WIDTH), 0),
                      1.0, 0.0).astype(BF16)

    def scores(j, slot, diagonal):
        for c in range(t // KEY_CHUNK):
            k0 = pl.multiple_of(j * t + c * KEY_CHUNK, KEY_CHUNK)
            rows = slice(c * KEY_CHUNK, (c + 1) * KEY_CHUNK)
            for g in range(t // QUERY_CHUNK):
                cols = slice(g * QUERY_CHUNK, (g + 1) * QUERY_CHUNK)
                shift = g * QUERY_CHUNK - c * KEY_CHUNK
                for hh in heads:
                    if diagonal and shift <= -(QUERY_CHUNK - 1):
                        lb_ref[hh, slot, rows, cols] = jnp.full((KEY_CHUNK, QUERY_CHUNK), -jnp.inf, F32)
                        lm_ref[hh, slot, rows, cols] = jnp.zeros((KEY_CHUNK, QUERY_CHUNK), BF16)
                        continue
                    kc = k_ref[pl.ds(k0, KEY_CHUNK), head_cols(hh)]
                    z2 = jnp.dot(kc, qts[hh][:, cols], preferred_element_type=F32)
                    lb2 = jnp.minimum(z2, 0.0) - jnp.log(1.0 + jnp.exp2(_neg_abs(z2))) * LOG2E
                    l1m2 = lb2 - z2
                    if diagonal and shift < KEY_CHUNK:
                        visible = diff < shift
                        lb2 = jnp.where(visible, lb2, -jnp.inf)
                        l1m2 = jnp.where(visible, l1m2, 0.0)
                    lb_ref[hh, slot, rows, cols] = lb2
                    lm_ref[hh, slot, rows, cols] = l1m2.astype(BF16)

    def weights(slot, runs):
        runs = list(runs)
        for b in reversed(range(n_sub)):
            rows = slice(b * MXU_WIDTH, (b + 1) * MXU_WIDTH)
            totals = [[] for _ in heads]
            for g in range(t // QUERY_CHUNK):
                cols = slice(g * QUERY_CHUNK, (g + 1) * QUERY_CHUNK)
                for hh in heads:
                    lm = lm_ref[hh, slot, rows, cols]
                    within = jnp.dot(later, lm, preferred_element_type=F32)
                    w = jnp.exp2(lb_ref[hh, slot, rows, cols] + within + runs[hh][:, cols])
                    w_ref[hh, slot, rows, cols] = w.astype(BF16)
                    totals[hh].append(within[0:1, :] + lm[0:1, :].astype(F32))
            for hh in heads:
                runs[hh] = runs[hh] + jnp.concatenate(totals[hh], axis=1)
        return tuple(runs)

    def value_product(j, slot):
        for hh in heads:
            acc_ref[hh] += jnp.dot(vt_ref[hh, j], w_ref[hh, slot], preferred_element_type=F32)

    def live(runs):
        return jnp.max(functools.reduce(jnp.maximum, runs)) > WEIGHT_FLOOR_LOG2

    def tile(j, runs):
        scores(j, 0, False)
        runs = weights(0, runs)
        go = live(runs)
        value_product(j, 0)
        return runs, go

    start = tuple(jnp.zeros((1, t), F32) for _ in heads)

    @pl.when(i == 0)
    def _():
        scores(0, 0, True)
        weights(0, start)
        value_product(0, 0)

    @pl.when(i > 0)
    def _():
        scores(i, 0, True)
        scores(i - 1, 1, False)
        runs = weights(1, weights(0, start))
        go = live(runs)
        value_product(i, 0)
        value_product(i - 1, 1)
        lax.while_loop(lambda c: (c[0] >= 0) & c[2],
                       lambda c: (c[0] - 1,) + tile(c[0], c[1]), (i - 2, runs, go))

    for hh in heads:
        o_ref[:, head_cols(hh)] = acc_ref[hh].T.astype(o_ref.dtype)


def sb_attention(q, kv, *, t=256, hg=4):
    s, d = q.shape
    ng = d // (hg * HEAD_DIM)
    w = hg * HEAD_DIM
    return pl.pallas_call(
        functools.partial(_sb_kernel, t=t, hg=hg),
        grid=(ng, s // t),
        in_specs=[pl.BlockSpec((t, w), lambda g, i: (i, g)),
                  pl.BlockSpec((s, w), lambda g, i: (0, g)),
                  pl.BlockSpec((s, w), lambda g, i: (0, ng + g))],
        out_specs=pl.BlockSpec((t, w), lambda g, i: (i, g)),
        out_shape=jax.ShapeDtypeStruct((s, d), BF16),
        scratch_shapes=[pltpu.VMEM((hg, s // t, HEAD_DIM, t), BF16),
                        pltpu.VMEM((hg, 2, t, t), F32),
                        pltpu.VMEM((hg, 2, t, t), BF16),
                        pltpu.VMEM((hg, 2, t, t), BF16),
                        pltpu.VMEM((hg, HEAD_DIM, t), F32)],
        compiler_params=_params("arbitrary", "arbitrary"),
        name="sb_attention")(q, kv, kv)


def _matmul_residual_kernel(a_ref, w_ref, h_ref, o_ref, w16_ref):
    @pl.when(pl.program_id(0) == 0)
    def _():
        w16_ref[...] = w_ref[...].astype(BF16)

    o_ref[...] = h_ref[...] + jnp.dot(a_ref[...], w16_ref[...], preferred_element_type=F32)


def matmul_residual(a, w, layer, h, *, tm=512):
    s, k = a.shape
    n = w.shape[2]
    return pl.pallas_call(
        _matmul_residual_kernel, grid=(s // tm,),
        in_specs=[pl.BlockSpec((tm, k), lambda i: (i, 0)),
                  pl.BlockSpec((None, k, n), lambda i: (layer, 0, 0), pipeline_mode=pl.Buffered(1)),
                  pl.BlockSpec((tm, n), lambda i: (i, 0))],
        out_specs=pl.BlockSpec((tm, n), lambda i: (i, 0)),
        out_shape=jax.ShapeDtypeStruct((s, n), F32),
        scratch_shapes=[pltpu.VMEM((k, n), BF16)],
        compiler_params=_params("arbitrary"),
        name="matmul_residual")(a, w, h)


def _ffn_step(h_ref, g_ref, weights, gout_ref, o_ref, xn_ref, norm_out):
    @pl.when(pl.program_id(1) == 0)
    def _():
        x = h_ref[...]
        xn_ref[...] = _rms_scale(x, g_ref[...]).astype(BF16)
        o_ref[...] = x

    wg, wu, wd = weights()
    xn = xn_ref[...]
    gate = jnp.dot(xn, wg, preferred_element_type=F32)
    up = jnp.dot(xn, wu, preferred_element_type=F32)
    act = (gate * jax.nn.sigmoid(gate) * up).astype(BF16)
    o_ref[...] += jnp.dot(act, wd, preferred_element_type=F32)

    if norm_out:
        @pl.when(pl.program_id(1) == pl.num_programs(1) - 1)
        def _():
            o_ref[...] = _rms_scale(o_ref[...], gout_ref[...])


def _ffn_first_kernel(h_ref, g_ref, wg_ref, wu_ref, wd_ref, gout_ref,
                      o_ref, wg_out_ref, wu_out_ref, wd_out_ref, xn_ref, *, norm_out):
    def weights():
        wg, wu, wd = (w[...].astype(BF16) for w in (wg_ref, wu_ref, wd_ref))
        wg_out_ref[...] = wg
        wu_out_ref[...] = wu
        wd_out_ref[...] = wd
        return wg, wu, wd

    _ffn_step(h_ref, g_ref, weights, gout_ref, o_ref, xn_ref, norm_out)


def _ffn_rest_kernel(h_ref, g_ref, wg_ref, wu_ref, wd_ref, gout_ref, first_ref, o_ref, xn_ref, *, norm_out):
    del first_ref
    _ffn_step(h_ref, g_ref, lambda: (wg_ref[...], wu_ref[...], wd_ref[...]), gout_ref, o_ref, xn_ref,
              norm_out)


def ffn(h, g, wg, wu, wd, layer, g_out, *, norm_out, tm=512, tf=512, tm_first=1024, tf_first=256):
    s, d = h.shape
    dff = wg.shape[2]
    vec = pl.BlockSpec((1, d), lambda i, f: (0, 0))
    skip = tm_first // tm

    def w_specs(lyr, t_f):
        return [pl.BlockSpec((None, d, t_f), lambda i, f: (lyr, 0, f)),
                pl.BlockSpec((None, d, t_f), lambda i, f: (lyr, 0, f)),
                pl.BlockSpec((None, t_f, d), lambda i, f: (lyr, f, 0))]

    first, wg16, wu16, wd16 = pl.pallas_call(
        functools.partial(_ffn_first_kernel, norm_out=norm_out), grid=(1, dff // tf_first),
        in_specs=[pl.BlockSpec((tm_first, d), lambda i, f: (0, 0), pipeline_mode=pl.Buffered(1)), vec,
                  *w_specs(layer, tf_first), vec],
        out_specs=[pl.BlockSpec((tm_first, d), lambda i, f: (0, 0)), *w_specs(0, tf_first)],
        out_shape=[jax.ShapeDtypeStruct((s, d), F32),
                   jax.ShapeDtypeStruct((1, d, dff), BF16), jax.ShapeDtypeStruct((1, d, dff), BF16),
                   jax.ShapeDtypeStruct((1, dff, d), BF16)],
        scratch_shapes=[pltpu.VMEM((tm_first, d), BF16)],
        compiler_params=_params("arbitrary", "arbitrary"),
        name="ffn_first")(h, g, wg, wu, wd, g_out)
    return pl.pallas_call(
        functools.partial(_ffn_rest_kernel, norm_out=norm_out), grid=((s - tm_first) // tm, dff // tf),
        in_specs=[pl.BlockSpec((tm, d), lambda i, f: (i + skip, 0)), vec, *w_specs(0, tf), vec,
                  pl.BlockSpec(memory_space=pl.ANY)],
        out_specs=pl.BlockSpec((tm, d), lambda i, f: (i + skip, 0)),
        out_shape=jax.ShapeDtypeStruct((s, d), F32),
        input_output_aliases={6: 0},
        scratch_shapes=[pltpu.VMEM((tm, d), BF16)],
        compiler_params=_params("parallel", "arbitrary"),
        name="ffn_rest")(h, g, wg16, wu16, wd16, g_out, first)


def kernel(x, attn_norm, ffn_norm, a_w_in, a_b_f, a_w_out, kv_norm, w_kv, b_w_q, b_w_out,
           ffn_w_gate, ffn_w_up, ffn_w_down, final_norm):
    b, s, d = x.shape
    assert b == 1 and d == N_HEADS * HEAD_DIM
    depth = attn_norm.shape[0]
    n_a = a_w_in.shape[0]
    h = x.reshape(s, d)
    w_in_t = jnp.swapaxes(a_w_in, 1, 2)
    w_qkv = cast_bf16(w_in_t, 3 * d)
    w_f = jnp.pad(w_in_t[:, 3 * d:], ((0, 0), (0, LANES - N_HEADS), (0, 0)))
    w_kv = cast_bf16(w_kv[None])
    kv = None
    for layer in range(depth):
        g_attn = attn_norm[layer].reshape(1, d)
        if layer < n_a:
            b_f = jnp.pad(a_b_f[layer], (0, LANES - N_HEADS)).reshape(1, LANES)
            qkv, f_logit = norm_matmul(h, g_attn, w_qkv, layer, w_f, transposed=True,
                                       scaled_cols=d, col_scale=HEAD_DIM ** -0.5 * LOG2E)
            o = fox_attention(qkv, gate_cumsum(f_logit, b_f))
            h = matmul_residual(o, a_w_out, layer, h)
        else:
            j = layer - n_a
            q = norm_matmul_resident(h, g_attn, b_w_q, j, col_scale=HEAD_DIM ** -0.5 * LOG2E)
            o = sb_attention(q, kv)
            h = matmul_residual(o, b_w_out, j, h)
        h = ffn(h, ffn_norm[layer].reshape(1, d), ffn_w_gate, ffn_w_up, ffn_w_down, layer,
                final_norm.reshape(1, d), norm_out=layer == depth - 1)
        if layer == n_a - 1:
            kv = norm_matmul(h, kv_norm.reshape(1, d), w_kv, 0)
    return h.reshape(b, s, d)
```

```python
import functools

import jax
import jax.numpy as jnp
from jax import lax
from jax.experimental import pallas as pl
from jax.experimental.pallas import tpu as pltpu

N_HEADS = 16
HEAD_DIM = 128
EPS = 1e-6
LOG2E = 1.4426950408889634
LANES = 128
MXU_WIDTH = 256
VMEM_LIMIT = 56 * 1024 * 1024

KEY_CHUNK = 128
QUERY_CHUNK = MXU_WIDTH
PROB_CHUNK = 32
WEIGHT_FLOOR_LOG2 = -154.0

F32 = jnp.float32
BF16 = jnp.bfloat16


def _params(*sem):
    return pltpu.CompilerParams(dimension_semantics=sem, vmem_limit_bytes=VMEM_LIMIT)


def _rms_scale(x, g):
    ms = jnp.mean(x * x, axis=-1, keepdims=True)
    return x * lax.rsqrt(ms + EPS) * g


def _cast_kernel(x_ref, o_ref):
    o_ref[...] = x_ref[...].astype(o_ref.dtype)


def cast_bf16(w, rows=None, *, tr=256):
    n_layers, r, c = w.shape
    rows = r if rows is None else rows
    return pl.pallas_call(
        _cast_kernel, grid=(n_layers, rows // tr),
        in_specs=[pl.BlockSpec((None, tr, c), lambda l, i: (l, i, 0))],
        out_specs=pl.BlockSpec((None, tr, c), lambda l, i: (l, i, 0)),
        out_shape=jax.ShapeDtypeStruct((n_layers, rows, c), BF16),
        compiler_params=_params("parallel", "parallel"), name="cast_bf16")(w)


def _cast_heads_kernel(order_ref, x_ref, o_ref):
    del order_ref
    o_ref[...] = x_ref[...].astype(o_ref.dtype)


def cast_bf16_heads(w, orders, n_sections):
    n_layers, _, c = w.shape
    n_heads = orders.shape[1]

    def source(l, i, order):
        return l, (i // n_heads) * n_heads + order[l * n_heads + i % n_heads], 0

    return pl.pallas_call(
        _cast_heads_kernel,
        grid_spec=pltpu.PrefetchScalarGridSpec(
            num_scalar_prefetch=1, grid=(n_layers, n_sections * n_heads),
            in_specs=[pl.BlockSpec((None, HEAD_DIM, c), source)],
            out_specs=pl.BlockSpec((None, HEAD_DIM, c), lambda l, i, order: (l, i, 0))),
        out_shape=jax.ShapeDtypeStruct((n_layers, n_sections * n_heads * HEAD_DIM, c), BF16),
        compiler_params=_params("parallel", "parallel"),
        name="cast_bf16_heads")(orders.reshape(-1), w)


def _scaled_tile(acc, n_scaled_tiles, col_scale):
    if n_scaled_tiles == 0:
        return acc
    return acc * jnp.where(pl.program_id(1) < n_scaled_tiles, col_scale, 1.0)


def _project(xn, w, transposed):
    if transposed:
        return lax.dot_general(xn, w, (((1,), (1,)), ((), ())), preferred_element_type=F32)
    return jnp.dot(xn, w, preferred_element_type=F32)


def _norm_matmul_kernel(x_ref, g_ref, w_ref, o_ref, xn_ref, *, n_scaled_tiles, col_scale, transposed):
    @pl.when(pl.program_id(1) == 0)
    def _():
        xn_ref[...] = _rms_scale(x_ref[...], g_ref[...]).astype(BF16)

    acc = _project(xn_ref[...], w_ref[...], transposed)
    o_ref[...] = _scaled_tile(acc, n_scaled_tiles, col_scale).astype(o_ref.dtype)


def _norm_matmul_gate_kernel(x_ref, g_ref, w_ref, wf_ref, o_ref, f_ref, xn_ref,
                             *, n_scaled_tiles, col_scale, transposed):
    @pl.when(pl.program_id(1) == 0)
    def _():
        xn = _rms_scale(x_ref[...], g_ref[...]).astype(BF16)
        xn_ref[...] = xn
        f_ref[...] = _project(xn, wf_ref[...].astype(BF16), transposed)

    acc = _project(xn_ref[...], w_ref[...], transposed)
    o_ref[...] = _scaled_tile(acc, n_scaled_tiles, col_scale).astype(o_ref.dtype)


def norm_matmul(x, g, w, layer, wf=None, *, transposed=False, scaled_cols=0, col_scale=1.0,
                tm=1024, tn=1024):
    s, d = x.shape
    n = w.shape[1] if transposed else w.shape[2]
    assert scaled_cols % tn == 0
    static = dict(n_scaled_tiles=scaled_cols // tn, col_scale=col_scale, transposed=transposed)
    grid = (s // tm, n // tn)
    x_spec = pl.BlockSpec((tm, d), lambda i, j: (i, 0))
    g_spec = pl.BlockSpec((1, d), lambda i, j: (0, 0))
    if transposed:
        w_spec = pl.BlockSpec((None, tn, d), lambda i, j: (layer, j, 0))
        wf_spec = pl.BlockSpec((None, LANES, d), lambda i, j: (layer, 0, 0))
    else:
        w_spec = pl.BlockSpec((None, d, tn), lambda i, j: (layer, 0, j))
        wf_spec = pl.BlockSpec((None, d, LANES), lambda i, j: (layer, 0, 0))
    o_spec = pl.BlockSpec((tm, tn), lambda i, j: (i, j))
    scratch = [pltpu.VMEM((tm, d), BF16)]
    if wf is None:
        return pl.pallas_call(
            functools.partial(_norm_matmul_kernel, **static), grid=grid,
            in_specs=[x_spec, g_spec, w_spec], out_specs=o_spec,
            out_shape=jax.ShapeDtypeStruct((s, n), BF16),
            scratch_shapes=scratch, compiler_params=_params("parallel", "arbitrary"),
            name="norm_matmul")(x, g, w)
    f_spec = pl.BlockSpec((tm, LANES), lambda i, j: (i, 0))
    return pl.pallas_call(
        functools.partial(_norm_matmul_gate_kernel, **static), grid=grid,
        in_specs=[x_spec, g_spec, w_spec, wf_spec], out_specs=[o_spec, f_spec],
        out_shape=[jax.ShapeDtypeStruct((s, n), BF16),
                   jax.ShapeDtypeStruct((s, LANES), F32)],
        scratch_shapes=scratch, compiler_params=_params("parallel", "arbitrary"),
        name="norm_matmul_gate")(x, g, w, wf)


def _norm_matmul_resident_kernel(x_ref, g_ref, w_ref, o_ref, w16_ref, *, col_scale):
    @pl.when(pl.program_id(0) == 0)
    def _():
        w16_ref[...] = w_ref[...].astype(BF16)

    xn = _rms_scale(x_ref[...], g_ref[...]).astype(BF16)
    acc = jnp.dot(xn, w16_ref[...], preferred_element_type=F32)
    o_ref[...] = (acc * col_scale).astype(o_ref.dtype)


def norm_matmul_resident(x, g, w, layer, *, col_scale, tm=512):
    s, d = x.shape
    n = w.shape[2]
    return pl.pallas_call(
        functools.partial(_norm_matmul_resident_kernel, col_scale=col_scale), grid=(s // tm,),
        in_specs=[pl.BlockSpec((tm, d), lambda i: (i, 0)),
                  pl.BlockSpec((1, d), lambda i: (0, 0)),
                  pl.BlockSpec((None, d, n), lambda i: (layer, 0, 0), pipeline_mode=pl.Buffered(1))],
        out_specs=pl.BlockSpec((tm, n), lambda i: (i, 0)),
        out_shape=jax.ShapeDtypeStruct((s, n), BF16),
        scratch_shapes=[pltpu.VMEM((d, n), BF16)],
        compiler_params=_params("arbitrary"), name="norm_matmul_resident")(x, g, w)


def _log_sigmoid(z):
    return jnp.minimum(z, 0.0) - jnp.log(1.0 + jnp.exp(-jnp.abs(z)))


def _split3(x):
    hi = x.astype(BF16)
    r = x - hi.astype(F32)
    mid = r.astype(BF16)
    lo = (r - mid.astype(F32)).astype(BF16)
    return hi, mid, lo


def _gate_cumsum_kernel(f_ref, b_ref, ccol_ref, carry_ref, *, tc):
    @pl.when(pl.program_id(0) == 0)
    def _():
        carry_ref[...] = jnp.zeros_like(carry_ref)

    log_f = _log_sigmoid(f_ref[...] + b_ref[...])
    row = lax.broadcasted_iota(jnp.int32, (tc, tc), 0)
    col = lax.broadcasted_iota(jnp.int32, (tc, tc), 1)
    tri = jnp.where(col <= row, 1.0, 0.0).astype(BF16)
    hi, mid, lo = _split3(log_f)
    c = (jnp.dot(tri, hi, preferred_element_type=F32)
         + jnp.dot(tri, mid, preferred_element_type=F32)
         + jnp.dot(tri, lo, preferred_element_type=F32)) + carry_ref[...]
    ccol_ref[...] = c
    carry_ref[...] = c[tc - 1:tc, :]


def gate_cumsum(f, b, *, tc=512):
    s = f.shape[0]
    return pl.pallas_call(
        functools.partial(_gate_cumsum_kernel, tc=tc), grid=(s // tc,),
        in_specs=[pl.BlockSpec((tc, LANES), lambda i: (i, 0)),
                  pl.BlockSpec((1, LANES), lambda i: (0, 0))],
        out_specs=pl.BlockSpec((tc, LANES), lambda i: (i, 0)),
        out_shape=jax.ShapeDtypeStruct((s, LANES), F32),
        scratch_shapes=[pltpu.VMEM((1, LANES), F32)],
        compiler_params=_params("arbitrary"), name="gate_cumsum")(f, b)


def _colmax8(x):
    return jnp.max(x.reshape(x.shape[0] // 8, 8, x.shape[1]), axis=0)


def _neg_abs(x):
    bits = lax.bitcast_convert_type(x, jnp.int32) | jnp.int32(-2 ** 31)
    return lax.bitcast_convert_type(bits, F32)


def _colsum8(x):
    return jnp.sum(x.reshape(x.shape[0] // 8, 8, x.shape[1]), axis=0)


def _fox_kernel(q_ref, k_ref, v_ref, ccol_ref, o_ref,
                vt_ref, csb_ref, kn_ref, s_ref, p_ref, acc_ref, lp_ref, *, t, hg):
    grp = pl.program_id(0)
    i = pl.program_id(1)
    n = i + 1
    heads = range(hg)

    def head_cols(hh):
        return slice(hh * HEAD_DIM, (hh + 1) * HEAD_DIM)

    @pl.when(i == 0)
    def _():
        for hh in heads:
            kn_ref[hh] = jnp.zeros((1, LANES), F32)

        def stage(c, carry):
            rows = pl.ds(pl.multiple_of(c * t, t), t)
            lane = lax.broadcasted_iota(jnp.int32, (t, LANES), 1)
            for hh in heads:
                vt_ref[hh, c] = v_ref[rows, head_cols(hh)].astype(F32).T.astype(BF16)
                cs = jnp.sum(jnp.where(lane == grp * hg + hh, ccol_ref[rows, :], 0.0),
                             axis=-1, keepdims=True)
                csb_ref[hh, rows, :] = jnp.broadcast_to(cs * LOG2E, (t, LANES))
                kf = k_ref[rows, head_cols(hh)].astype(F32)
                kn_ref[hh] = jnp.maximum(kn_ref[hh], jnp.max(jnp.sum(kf * kf, axis=-1, keepdims=True)))
            return carry
        lax.fori_loop(0, k_ref.shape[0] // t, stage, None)

    qts = [q_ref[:, head_cols(hh)].astype(F32).T.astype(BF16) for hh in heads]
    reach = [jnp.sqrt(jnp.sum(qts[hh].astype(F32) ** 2, axis=0, keepdims=True) * kn_ref[hh][:, 0:1])
             for hh in heads]
    acc_ref[...] = jnp.zeros(acc_ref.shape, F32)
    lp_ref[...] = jnp.zeros(lp_ref.shape, F32)
    diff = (lax.broadcasted_iota(jnp.int32, (KEY_CHUNK, QUERY_CHUNK), 0)
            - lax.broadcasted_iota(jnp.int32, (KEY_CHUNK, QUERY_CHUNK), 1))

    def scores(j, slot, diagonal):
        mparts = [[jnp.full((8, QUERY_CHUNK), -jnp.inf, F32) for _ in range(t // QUERY_CHUNK)]
                  for _ in heads]
        for c in range(t // KEY_CHUNK):
            k0 = pl.multiple_of(j * t + c * KEY_CHUNK, KEY_CHUNK)
            rows = slice(c * KEY_CHUNK, (c + 1) * KEY_CHUNK)
            for g in range(t // QUERY_CHUNK):
                cols = slice(g * QUERY_CHUNK, (g + 1) * QUERY_CHUNK)
                shift = g * QUERY_CHUNK - c * KEY_CHUNK
                for hh in heads:
                    if diagonal and shift <= -QUERY_CHUNK:
                        s_ref[hh, slot, rows, cols] = jnp.full((KEY_CHUNK, QUERY_CHUNK), -jnp.inf, F32)
                        continue
                    bias = csb_ref[hh, pl.ds(k0, KEY_CHUNK), :]
                    bias = jnp.concatenate([bias] * (QUERY_CHUNK // LANES), axis=1)
                    kc = k_ref[pl.ds(k0, KEY_CHUNK), head_cols(hh)]
                    s = jnp.dot(kc, qts[hh][:, cols], preferred_element_type=F32) - bias
                    if diagonal and shift < KEY_CHUNK - 1:
                        s = jnp.where(diff <= shift, s, -jnp.inf)
                    s_ref[hh, slot, rows, cols] = s
                    mparts[hh][g] = jnp.maximum(mparts[hh][g], _colmax8(s))
        return [jnp.concatenate(mparts[hh], axis=1) for hh in heads]

    def new_max(m_old, mpart):
        m_new = jnp.maximum(m_old, jnp.max(mpart, axis=0, keepdims=True))
        return m_new, jnp.exp2(m_old - m_new)

    def half(step, slot, carry, last=False):
        ms, alphas = carry
        j = i - step
        lsums = [jnp.zeros((8, t), F32) for _ in heads]
        for c in range(t // PROB_CHUNK):
            rows = slice(c * PROB_CHUNK, (c + 1) * PROB_CHUNK)
            for hh in heads:
                p = jnp.exp2(s_ref[hh, slot, rows, :] - ms[hh])
                p_ref[hh, rows, :] = p.astype(BF16)
                lsums[hh] = lsums[hh] + _colsum8(p)
        if not last:
            mparts = scores(j - 1, 1 - slot, False)
            new = [new_max(ms[hh], mparts[hh]) for hh in heads]
            carry = tuple(m for m, _ in new), tuple(a for _, a in new)
        for hh in heads:
            lp_ref[hh] = alphas[hh] * lp_ref[hh] + lsums[hh]
            acc_ref[hh] = alphas[hh] * acc_ref[hh] + jnp.dot(vt_ref[hh, j], p_ref[hh],
                                                            preferred_element_type=F32)
        return carry

    def tile_live(j, ms):
        last_key = jnp.maximum(j + 1, 1) * t - 1
        gaps = [reach[hh] - csb_ref[hh, pl.ds(last_key, 1), :][:, 0:1] - ms[hh] for hh in heads]
        return jnp.max(functools.reduce(jnp.maximum, gaps)) > WEIGHT_FLOOR_LOG2

    first = [new_max(jnp.full((1, t), -jnp.inf, F32), mp) for mp in scores(i, 0, True)]
    carry = tuple(m for m, _ in first), tuple(a for _, a in first)
    n_pairs = (n - 1) // 2

    def pair(c):
        mm, _, ms, alphas = c
        carry = half(2 * mm, 0, (ms, alphas))
        alive = tile_live(i - (2 * mm + 3), carry[0])
        return (mm + 1, alive) + half(2 * mm + 1, 1, carry)

    mm, alive, ms, alphas = lax.while_loop(
        lambda c: (c[0] < n_pairs) & c[1], pair, (jnp.int32(0), tile_live(i - 1, carry[0])) + carry)
    step = 2 * mm
    single = (step == n - 1) | jnp.logical_not(alive)

    @pl.when(single)
    def _():
        half(step, 0, (ms, alphas), last=True)

    @pl.when(jnp.logical_not(single))
    def _():
        half(step + 1, 1, half(step, 0, (ms, alphas)), last=True)

    for hh in heads:
        l = jnp.sum(lp_ref[hh], axis=0, keepdims=True)
        o_ref[:, head_cols(hh)] = (acc_ref[hh] / l).T.astype(o_ref.dtype)


def fox_attention(qkv, ccol, *, t=256, hg=4):
    s = qkv.shape[0]
    d = qkv.shape[1] // 3
    ng = d // (hg * HEAD_DIM)
    w = hg * HEAD_DIM
    return pl.pallas_call(
        functools.partial(_fox_kernel, t=t, hg=hg),
        grid=(ng, s // t),
        in_specs=[pl.BlockSpec((t, w), lambda g, i: (i, g)),
                  pl.BlockSpec((s, w), lambda g, i: (0, ng + g), pipeline_mode=pl.Buffered(1)),
                  pl.BlockSpec((s, w), lambda g, i: (0, 2 * ng + g), pipeline_mode=pl.Buffered(1)),
                  pl.BlockSpec((s, LANES), lambda g, i: (0, 0), pipeline_mode=pl.Buffered(1))],
        out_specs=pl.BlockSpec((t, w), lambda g, i: (i, g)),
        out_shape=jax.ShapeDtypeStruct((s, d), BF16),
        scratch_shapes=[pltpu.VMEM((hg, s // t, HEAD_DIM, t), BF16),
                        pltpu.VMEM((hg, s, LANES), F32),
                        pltpu.VMEM((hg, 1, LANES), F32),
                        pltpu.VMEM((hg, 2, t, t), F32),
                        pltpu.VMEM((hg, t, t), BF16),
                        pltpu.VMEM((hg, HEAD_DIM, t), F32), pltpu.VMEM((hg, 8, t), F32)],
        compiler_params=_params("arbitrary", "arbitrary"),
        name="fox_attention")(qkv, qkv, qkv, ccol)


def _sb_kernel(q_ref, k_ref, v_ref, o_ref, vt_ref, lb_ref, lm_ref, w_ref, acc_ref, *, t, hg):
    i = pl.program_id(1)
    n_sub = t // MXU_WIDTH
    heads = range(hg)

    def head_cols(hh):
        return slice(hh * HEAD_DIM, (hh + 1) * HEAD_DIM)

    @pl.when(i == 0)
    def _():
        def stage(c, carry):
            rows = pl.ds(pl.multiple_of(c * t, t), t)
            for hh in heads:
                vt_ref[hh, c] = v_ref[rows, head_cols(hh)].astype(F32).T.astype(BF16)
            return carry
        lax.fori_loop(0, k_ref.shape[0] // t, stage, None)

    qts = [q_ref[:, head_cols(hh)].astype(F32).T.astype(BF16) for hh in heads]
    acc_ref[...] = jnp.zeros(acc_ref.shape, F32)
    diff = (lax.broadcasted_iota(jnp.int32, (KEY_CHUNK, QUERY_CHUNK), 0)
            - lax.broadcasted_iota(jnp.int32, (KEY_CHUNK, QUERY_CHUNK), 1))
    later = jnp.where(lax.broadcasted_iota(jnp.int32, (MXU_WIDTH, MXU_WIDTH), 1)
                      > lax.broadcasted_iota(jnp.int32, (MXU_WIDTH, MXU_WIDTH), 0),
                      1.0, 0.0).astype(BF16)

    def scores(j, slot, diagonal):
        for c in range(t // KEY_CHUNK):
            k0 = pl.multiple_of(j * t + c * KEY_CHUNK, KEY_CHUNK)
            rows = slice(c * KEY_CHUNK, (c + 1) * KEY_CHUNK)
            for g in range(t // QUERY_CHUNK):
                cols = slice(g * QUERY_CHUNK, (g + 1) * QUERY_CHUNK)
                shift = g * QUERY_CHUNK - c * KEY_CHUNK
                for hh in heads:
                    if diagonal and shift <= -(QUERY_CHUNK - 1):
                        lb_ref[hh, slot, rows, cols] = jnp.full((KEY_CHUNK, QUERY_CHUNK), -jnp.inf, F32)
                        lm_ref[hh, slot, rows, cols] = jnp.zeros((KEY_CHUNK, QUERY_CHUNK), BF16)
                        continue
                    kc = k_ref[pl.ds(k0, KEY_CHUNK), head_cols(hh)]
                    z2 = jnp.dot(kc, qts[hh][:, cols], preferred_element_type=F32)
                    lb2 = jnp.minimum(z2, 0.0) - jnp.log(1.0 + jnp.exp2(_neg_abs(z2))) * LOG2E
                    l1m2 = lb2 - z2
                    if diagonal and shift < KEY_CHUNK:
                        visible = diff < shift
                        lb2 = jnp.where(visible, lb2, -jnp.inf)
                        l1m2 = jnp.where(visible, l1m2, 0.0)
                    lb_ref[hh, slot, rows, cols] = lb2
                    lm_ref[hh, slot, rows, cols] = l1m2.astype(BF16)

    def weights(slot, runs):
        runs = list(runs)
        for b in reversed(range(n_sub)):
            rows = slice(b * MXU_WIDTH, (b + 1) * MXU_WIDTH)
            totals = [[] for _ in heads]
            for g in range(t // QUERY_CHUNK):
                cols = slice(g * QUERY_CHUNK, (g + 1) * QUERY_CHUNK)
                for hh in heads:
                    lm = lm_ref[hh, slot, rows, cols]
                    within = jnp.dot(later, lm, preferred_element_type=F32)
                    w = jnp.exp2(lb_ref[hh, slot, rows, cols] + within + runs[hh][:, cols])
                    w_ref[hh, slot, rows, cols] = w.astype(BF16)
                    totals[hh].append(within[0:1, :] + lm[0:1, :].astype(F32))
            for hh in heads:
                runs[hh] = runs[hh] + jnp.concatenate(totals[hh], axis=1)
        return tuple(runs)

    def value_product(j, slot):
        for hh in heads:
            acc_ref[hh] += jnp.dot(vt_ref[hh, j], w_ref[hh, slot], preferred_element_type=F32)

    def live(runs):
        return jnp.max(functools.reduce(jnp.maximum, runs)) > WEIGHT_FLOOR_LOG2

    def tile(j, runs):
        scores(j, 0, False)
        runs = weights(0, runs)
        go = live(runs)
        value_product(j, 0)
        return runs, go

    start = tuple(jnp.zeros((1, t), F32) for _ in heads)

    @pl.when(i == 0)
    def _():
        scores(0, 0, True)
        weights(0, start)
        value_product(0, 0)

    @pl.when(i > 0)
    def _():
        scores(i, 0, True)
        scores(i - 1, 1, False)
        runs = weights(1, weights(0, start))
        go = live(runs)
        value_product(i, 0)
        value_product(i - 1, 1)
        lax.while_loop(lambda c: (c[0] >= 0) & c[2],
                       lambda c: (c[0] - 1,) + tile(c[0], c[1]), (i - 2, runs, go))

    for hh in heads:
        o_ref[:, head_cols(hh)] = acc_ref[hh].T.astype(o_ref.dtype)


def sb_attention(q, kv, *, t=256, hg=4):
    s, d = q.shape
    ng = d // (hg * HEAD_DIM)
    w = hg * HEAD_DIM
    return pl.pallas_call(
        functools.partial(_sb_kernel, t=t, hg=hg),
        grid=(ng, s // t),
        in_specs=[pl.BlockSpec((t, w), lambda g, i: (i, g)),
                  pl.BlockSpec((s, w), lambda g, i: (0, g)),
                  pl.BlockSpec((s, w), lambda g, i: (0, ng + g))],
        out_specs=pl.BlockSpec((t, w), lambda g, i: (i, g)),
        out_shape=jax.ShapeDtypeStruct((s, d), BF16),
        scratch_shapes=[pltpu.VMEM((hg, s // t, HEAD_DIM, t), BF16),
                        pltpu.VMEM((hg, 2, t, t), F32),
                        pltpu.VMEM((hg, 2, t, t), BF16),
                        pltpu.VMEM((hg, 2, t, t), BF16),
                        pltpu.VMEM((hg, HEAD_DIM, t), F32)],
        compiler_params=_params("arbitrary", "arbitrary"),
        name="sb_attention")(q, kv, kv)


def _matmul_residual_kernel(order_ref, a_ref, w_ref, h_ref, o_ref, w16_ref):
    @pl.when(pl.program_id(0) == 0)
    def _():
        for p in range(w_ref.shape[0] // HEAD_DIM):
            src = pl.multiple_of(order_ref[p] * HEAD_DIM, HEAD_DIM)
            w16_ref[p * HEAD_DIM:(p + 1) * HEAD_DIM, :] = w_ref[pl.ds(src, HEAD_DIM), :].astype(BF16)

    o_ref[...] = h_ref[...] + jnp.dot(a_ref[...], w16_ref[...], preferred_element_type=F32)


def matmul_residual(a, w, layer, h, order, *, tm=512):
    s, k = a.shape
    n = w.shape[2]
    return pl.pallas_call(
        _matmul_residual_kernel,
        grid_spec=pltpu.PrefetchScalarGridSpec(
            num_scalar_prefetch=1, grid=(s // tm,),
            in_specs=[pl.BlockSpec((tm, k), lambda i, order: (i, 0)),
                      pl.BlockSpec((None, k, n), lambda i, order: (layer, 0, 0),
                                   pipeline_mode=pl.Buffered(1)),
                      pl.BlockSpec((tm, n), lambda i, order: (i, 0))],
            out_specs=pl.BlockSpec((tm, n), lambda i, order: (i, 0)),
            scratch_shapes=[pltpu.VMEM((k, n), BF16)]),
        out_shape=jax.ShapeDtypeStruct((s, n), F32),
        compiler_params=_params("arbitrary"),
        name="matmul_residual")(order, a, w, h)


def _ffn_step(h_ref, g_ref, weights, gout_ref, o_ref, xn_ref, norm_out):
    @pl.when(pl.program_id(1) == 0)
    def _():
        x = h_ref[...]
        xn_ref[...] = _rms_scale(x, g_ref[...]).astype(BF16)
        o_ref[...] = x

    wg, wu, wd = weights()
    xn = xn_ref[...]
    gate = jnp.dot(xn, wg, preferred_element_type=F32)
    up = jnp.dot(xn, wu, preferred_element_type=F32)
    act = (gate * jax.nn.sigmoid(gate) * up).astype(BF16)
    o_ref[...] += jnp.dot(act, wd, preferred_element_type=F32)

    if norm_out:
        @pl.when(pl.program_id(1) == pl.num_programs(1) - 1)
        def _():
            o_ref[...] = _rms_scale(o_ref[...], gout_ref[...])


def _ffn_first_kernel(h_ref, g_ref, wg_ref, wu_ref, wd_ref, gout_ref,
                      o_ref, wg_out_ref, wu_out_ref, wd_out_ref, xn_ref, *, norm_out):
    def weights():
        wg, wu, wd = (w[...].astype(BF16) for w in (wg_ref, wu_ref, wd_ref))
        wg_out_ref[...] = wg
        wu_out_ref[...] = wu
        wd_out_ref[...] = wd
        return wg, wu, wd

    _ffn_step(h_ref, g_ref, weights, gout_ref, o_ref, xn_ref, norm_out)


def _ffn_rest_kernel(h_ref, g_ref, wg_ref, wu_ref, wd_ref, gout_ref, first_ref, o_ref, xn_ref, *, norm_out):
    del first_ref
    _ffn_step(h_ref, g_ref, lambda: (wg_ref[...], wu_ref[...], wd_ref[...]), gout_ref, o_ref, xn_ref,
              norm_out)


def ffn(h, g, wg, wu, wd, layer, g_out, *, norm_out, tm=512, tf=512, tm_first=1024, tf_first=256):
    s, d = h.shape
    dff = wg.shape[2]
    vec = pl.BlockSpec((1, d), lambda i, f: (0, 0))
    skip = tm_first // tm

    def w_specs(lyr, t_f):
        return [pl.BlockSpec((None, d, t_f), lambda i, f: (lyr, 0, f)),
                pl.BlockSpec((None, d, t_f), lambda i, f: (lyr, 0, f)),
                pl.BlockSpec((None, t_f, d), lambda i, f: (lyr, f, 0))]

    first, wg16, wu16, wd16 = pl.pallas_call(
        functools.partial(_ffn_first_kernel, norm_out=norm_out), grid=(1, dff // tf_first),
        in_specs=[pl.BlockSpec((tm_first, d), lambda i, f: (0, 0), pipeline_mode=pl.Buffered(1)), vec,
                  *w_specs(layer, tf_first), vec],
        out_specs=[pl.BlockSpec((tm_first, d), lambda i, f: (0, 0)), *w_specs(0, tf_first)],
        out_shape=[jax.ShapeDtypeStruct((s, d), F32),
                   jax.ShapeDtypeStruct((1, d, dff), BF16), jax.ShapeDtypeStruct((1, d, dff), BF16),
                   jax.ShapeDtypeStruct((1, dff, d), BF16)],
        scratch_shapes=[pltpu.VMEM((tm_first, d), BF16)],
        compiler_params=_params("arbitrary", "arbitrary"),
        name="ffn_first")(h, g, wg, wu, wd, g_out)
    return pl.pallas_call(
        functools.partial(_ffn_rest_kernel, norm_out=norm_out), grid=((s - tm_first) // tm, dff // tf),
        in_specs=[pl.BlockSpec((tm, d), lambda i, f: (i + skip, 0)), vec, *w_specs(0, tf), vec,
                  pl.BlockSpec(memory_space=pl.ANY)],
        out_specs=pl.BlockSpec((tm, d), lambda i, f: (i + skip, 0)),
        out_shape=jax.ShapeDtypeStruct((s, d), F32),
        input_output_aliases={6: 0},
        scratch_shapes=[pltpu.VMEM((tm, d), BF16)],
        compiler_params=_params("parallel", "arbitrary"),
        name="ffn_rest")(h, g, wg16, wu16, wd16, g_out, first)


def kernel(x, attn_norm, ffn_norm, a_w_in, a_b_f, a_w_out, kv_norm, w_kv, b_w_q, b_w_out,
           ffn_w_gate, ffn_w_up, ffn_w_down, final_norm):
    b, s, d = x.shape
    assert b == 1 and d == N_HEADS * HEAD_DIM
    depth = attn_norm.shape[0]
    n_a = a_w_in.shape[0]
    h = x.reshape(s, d)
    w_in_t = jnp.swapaxes(a_w_in, 1, 2)
    fox_order = jnp.argsort(a_b_f, axis=1).astype(jnp.int32)
    w_qkv = cast_bf16_heads(w_in_t, fox_order, 3)
    w_f = jnp.take_along_axis(w_in_t[:, 3 * d:], fox_order[:, :, None], axis=1)
    w_f = jnp.pad(w_f, ((0, 0), (0, LANES - N_HEADS), (0, 0)))
    b_f = jnp.pad(jnp.take_along_axis(a_b_f, fox_order, axis=1), ((0, 0), (0, LANES - N_HEADS)))
    natural = jnp.arange(N_HEADS, dtype=jnp.int32)
    w_kv = cast_bf16(w_kv[None])
    kv = None
    for layer in range(depth):
        g_attn = attn_norm[layer].reshape(1, d)
        if layer < n_a:
            qkv, f_logit = norm_matmul(h, g_attn, w_qkv, layer, w_f, transposed=True,
                                       scaled_cols=d, col_scale=HEAD_DIM ** -0.5 * LOG2E)
            o = fox_attention(qkv, gate_cumsum(f_logit, b_f[layer:layer + 1]))
            h = matmul_residual(o, a_w_out, layer, h, fox_order[layer])
        else:
            j = layer - n_a
            q = norm_matmul_resident(h, g_attn, b_w_q, j, col_scale=HEAD_DIM ** -0.5 * LOG2E)
            o = sb_attention(q, kv)
            h = matmul_residual(o, b_w_out, j, h, natural)
        h = ffn(h, ffn_norm[layer].reshape(1, d), ffn_w_gate, ffn_w_up, ffn_w_down, layer,
                final_norm.reshape(1, d), norm_out=layer == depth - 1)
        if layer == n_a - 1:
            kv = norm_matmul(h, kv_norm.reshape(1, d), w_kv, 0)
    return h.reshape(b, s, d)
```

```python
import functools

import jax
import jax.numpy as jnp
from jax import lax
from jax.experimental import pallas as pl
from jax.experimental.pallas import tpu as pltpu

N_HEADS = 16
HEAD_DIM = 128
EPS = 1e-6
LOG2E = 1.4426950408889634
LANES = 128
MXU_WIDTH = 256
VMEM_LIMIT = 56 * 1024 * 1024

KEY_CHUNK = 128
QUERY_CHUNK = MXU_WIDTH
PROB_CHUNK = 32
WEIGHT_FLOOR_LOG2 = -154.0

F32 = jnp.float32
BF16 = jnp.bfloat16


def _params(*sem):
    return pltpu.CompilerParams(dimension_semantics=sem, vmem_limit_bytes=VMEM_LIMIT)


def _rms_scale(x, g):
    ms = jnp.mean(x * x, axis=-1, keepdims=True)
    return x * lax.rsqrt(ms + EPS) * g


def _cast_kernel(x_ref, o_ref):
    o_ref[...] = x_ref[...].astype(o_ref.dtype)


def cast_bf16(w, rows=None, *, tr=256):
    n_layers, r, c = w.shape
    rows = r if rows is None else rows
    return pl.pallas_call(
        _cast_kernel, grid=(n_layers, rows // tr),
        in_specs=[pl.BlockSpec((None, tr, c), lambda l, i: (l, i, 0))],
        out_specs=pl.BlockSpec((None, tr, c), lambda l, i: (l, i, 0)),
        out_shape=jax.ShapeDtypeStruct((n_layers, rows, c), BF16),
        compiler_params=_params("parallel", "parallel"), name="cast_bf16")(w)


def _cast_heads_kernel(order_ref, x_ref, o_ref):
    del order_ref
    o_ref[...] = x_ref[...].astype(o_ref.dtype)


def cast_bf16_heads(w, orders, n_sections):
    n_layers, _, c = w.shape
    n_heads = orders.shape[1]

    def source(l, i, order):
        return l, (i // n_heads) * n_heads + order[l * n_heads + i % n_heads], 0

    return pl.pallas_call(
        _cast_heads_kernel,
        grid_spec=pltpu.PrefetchScalarGridSpec(
            num_scalar_prefetch=1, grid=(n_layers, n_sections * n_heads),
            in_specs=[pl.BlockSpec((None, HEAD_DIM, c), source)],
            out_specs=pl.BlockSpec((None, HEAD_DIM, c), lambda l, i, order: (l, i, 0))),
        out_shape=jax.ShapeDtypeStruct((n_layers, n_sections * n_heads * HEAD_DIM, c), BF16),
        compiler_params=_params("parallel", "parallel"),
        name="cast_bf16_heads")(orders.reshape(-1), w)


def _scaled_tile(acc, n_scaled_tiles, col_scale):
    if n_scaled_tiles == 0:
        return acc
    return acc * jnp.where(pl.program_id(1) < n_scaled_tiles, col_scale, 1.0)


def _project(xn, w, transposed):
    if transposed:
        return lax.dot_general(xn, w, (((1,), (1,)), ((), ())), preferred_element_type=F32)
    return jnp.dot(xn, w, preferred_element_type=F32)


def _norm_matmul_kernel(x_ref, g_ref, w_ref, o_ref, xn_ref, *, n_scaled_tiles, col_scale, transposed):
    @pl.when(pl.program_id(1) == 0)
    def _():
        xn_ref[...] = _rms_scale(x_ref[...], g_ref[...]).astype(BF16)

    acc = _project(xn_ref[...], w_ref[...], transposed)
    o_ref[...] = _scaled_tile(acc, n_scaled_tiles, col_scale).astype(o_ref.dtype)


def _norm_matmul_gate_kernel(x_ref, g_ref, w_ref, wf_ref, o_ref, f_ref, xn_ref,
                             *, n_scaled_tiles, col_scale, transposed):
    @pl.when(pl.program_id(1) == 0)
    def _():
        xn = _rms_scale(x_ref[...], g_ref[...]).astype(BF16)
        xn_ref[...] = xn
        f_ref[...] = _project(xn, wf_ref[...].astype(BF16), transposed)

    acc = _project(xn_ref[...], w_ref[...], transposed)
    o_ref[...] = _scaled_tile(acc, n_scaled_tiles, col_scale).astype(o_ref.dtype)


def norm_matmul(x, g, w, layer, wf=None, *, transposed=False, scaled_cols=0, col_scale=1.0,
                tm=1024, tn=1024):
    s, d = x.shape
    n = w.shape[1] if transposed else w.shape[2]
    assert scaled_cols % tn == 0
    static = dict(n_scaled_tiles=scaled_cols // tn, col_scale=col_scale, transposed=transposed)
    grid = (s // tm, n // tn)
    x_spec = pl.BlockSpec((tm, d), lambda i, j: (i, 0))
    g_spec = pl.BlockSpec((1, d), lambda i, j: (0, 0))
    if transposed:
        w_spec = pl.BlockSpec((None, tn, d), lambda i, j: (layer, j, 0))
        wf_spec = pl.BlockSpec((None, LANES, d), lambda i, j: (layer, 0, 0))
    else:
        w_spec = pl.BlockSpec((None, d, tn), lambda i, j: (layer, 0, j))
        wf_spec = pl.BlockSpec((None, d, LANES), lambda i, j: (layer, 0, 0))
    o_spec = pl.BlockSpec((tm, tn), lambda i, j: (i, j))
    scratch = [pltpu.VMEM((tm, d), BF16)]
    if wf is None:
        return pl.pallas_call(
            functools.partial(_norm_matmul_kernel, **static), grid=grid,
            in_specs=[x_spec, g_spec, w_spec], out_specs=o_spec,
            out_shape=jax.ShapeDtypeStruct((s, n), BF16),
            scratch_shapes=scratch, compiler_params=_params("parallel", "arbitrary"),
            name="norm_matmul")(x, g, w)
    f_spec = pl.BlockSpec((tm, LANES), lambda i, j: (i, 0))
    return pl.pallas_call(
        functools.partial(_norm_matmul_gate_kernel, **static), grid=grid,
        in_specs=[x_spec, g_spec, w_spec, wf_spec], out_specs=[o_spec, f_spec],
        out_shape=[jax.ShapeDtypeStruct((s, n), BF16),
                   jax.ShapeDtypeStruct((s, LANES), F32)],
        scratch_shapes=scratch, compiler_params=_params("parallel", "arbitrary"),
        name="norm_matmul_gate")(x, g, w, wf)


def _norm_matmul_resident_kernel(x_ref, g_ref, w_ref, o_ref, w16_ref, *, col_scale):
    @pl.when(pl.program_id(0) == 0)
    def _():
        w16_ref[...] = w_ref[...].astype(BF16)

    xn = _rms_scale(x_ref[...], g_ref[...]).astype(BF16)
    acc = jnp.dot(xn, w16_ref[...], preferred_element_type=F32)
    o_ref[...] = (acc * col_scale).astype(o_ref.dtype)


def norm_matmul_resident(x, g, w, layer, *, col_scale, tm=512):
    s, d = x.shape
    n = w.shape[2]
    return pl.pallas_call(
        functools.partial(_norm_matmul_resident_kernel, col_scale=col_scale), grid=(s // tm,),
        in_specs=[pl.BlockSpec((tm, d), lambda i: (i, 0)),
                  pl.BlockSpec((1, d), lambda i: (0, 0)),
                  pl.BlockSpec((None, d, n), lambda i: (layer, 0, 0), pipeline_mode=pl.Buffered(1))],
        out_specs=pl.BlockSpec((tm, n), lambda i: (i, 0)),
        out_shape=jax.ShapeDtypeStruct((s, n), BF16),
        scratch_shapes=[pltpu.VMEM((d, n), BF16)],
        compiler_params=_params("arbitrary"), name="norm_matmul_resident")(x, g, w)


def _log_sigmoid(z):
    return jnp.minimum(z, 0.0) - jnp.log(1.0 + jnp.exp(-jnp.abs(z)))


def _split3(x):
    hi = x.astype(BF16)
    r = x - hi.astype(F32)
    mid = r.astype(BF16)
    lo = (r - mid.astype(F32)).astype(BF16)
    return hi, mid, lo


def _gate_cumsum_kernel(f_ref, b_ref, ccol_ref, carry_ref, *, tc):
    @pl.when(pl.program_id(0) == 0)
    def _():
        carry_ref[...] = jnp.zeros_like(carry_ref)

    log_f = _log_sigmoid(f_ref[...] + b_ref[...])
    row = lax.broadcasted_iota(jnp.int32, (tc, tc), 0)
    col = lax.broadcasted_iota(jnp.int32, (tc, tc), 1)
    tri = jnp.where(col <= row, 1.0, 0.0).astype(BF16)
    hi, mid, lo = _split3(log_f)
    c = (jnp.dot(tri, hi, preferred_element_type=F32)
         + jnp.dot(tri, mid, preferred_element_type=F32)
         + jnp.dot(tri, lo, preferred_element_type=F32)) + carry_ref[...]
    ccol_ref[...] = c
    carry_ref[...] = c[tc - 1:tc, :]


def gate_cumsum(f, b, *, tc=512):
    s = f.shape[0]
    return pl.pallas_call(
        functools.partial(_gate_cumsum_kernel, tc=tc), grid=(s // tc,),
        in_specs=[pl.BlockSpec((tc, LANES), lambda i: (i, 0)),
                  pl.BlockSpec((1, LANES), lambda i: (0, 0))],
        out_specs=pl.BlockSpec((tc, LANES), lambda i: (i, 0)),
        out_shape=jax.ShapeDtypeStruct((s, LANES), F32),
        scratch_shapes=[pltpu.VMEM((1, LANES), F32)],
        compiler_params=_params("arbitrary"), name="gate_cumsum")(f, b)


def _colmax8(x):
    return jnp.max(x.reshape(x.shape[0] // 8, 8, x.shape[1]), axis=0)


def _neg_abs(x):
    bits = lax.bitcast_convert_type(x, jnp.int32) | jnp.int32(-2 ** 31)
    return lax.bitcast_convert_type(bits, F32)


def _colsum8(x):
    return jnp.sum(x.reshape(x.shape[0] // 8, 8, x.shape[1]), axis=0)


def _fox_kernel(q_ref, k_ref, v_ref, ccol_ref, o_ref,
                vt_ref, csb_ref, kn_ref, s_ref, p_ref, acc_ref, lp_ref, *, t, hg):
    grp = pl.program_id(0)
    i = pl.program_id(1)
    n = i + 1
    heads = range(hg)

    def head_cols(hh):
        return slice(hh * HEAD_DIM, (hh + 1) * HEAD_DIM)

    @pl.when(i == 0)
    def _():
        for hh in heads:
            kn_ref[hh] = jnp.zeros((1, LANES), F32)

        def stage(c, carry):
            rows = pl.ds(pl.multiple_of(c * t, t), t)
            lane = lax.broadcasted_iota(jnp.int32, (t, LANES), 1)
            for hh in heads:
                vt_ref[hh, c] = v_ref[rows, head_cols(hh)].astype(F32).T.astype(BF16)
                cs = jnp.sum(jnp.where(lane == grp * hg + hh, ccol_ref[rows, :], 0.0),
                             axis=-1, keepdims=True)
                csb_ref[hh, rows, :] = jnp.broadcast_to(cs * LOG2E, (t, LANES))
                kf = k_ref[rows, head_cols(hh)].astype(F32)
                kn_ref[hh] = jnp.maximum(kn_ref[hh], jnp.max(jnp.sum(kf * kf, axis=-1, keepdims=True)))
            return carry
        lax.fori_loop(0, k_ref.shape[0] // t, stage, None)

    qts = [q_ref[:, head_cols(hh)].astype(F32).T.astype(BF16) for hh in heads]
    reach = [jnp.sqrt(jnp.sum(qts[hh].astype(F32) ** 2, axis=0, keepdims=True) * kn_ref[hh][:, 0:1])
             for hh in heads]
    acc_ref[...] = jnp.zeros(acc_ref.shape, F32)
    lp_ref[...] = jnp.zeros(lp_ref.shape, F32)
    diff = (lax.broadcasted_iota(jnp.int32, (KEY_CHUNK, QUERY_CHUNK), 0)
            - lax.broadcasted_iota(jnp.int32, (KEY_CHUNK, QUERY_CHUNK), 1))

    def scores(j, slot, diagonal):
        mparts = [[jnp.full((8, QUERY_CHUNK), -jnp.inf, F32) for _ in range(t // QUERY_CHUNK)]
                  for _ in heads]
        for c in range(t // KEY_CHUNK):
            k0 = pl.multiple_of(j * t + c * KEY_CHUNK, KEY_CHUNK)
            rows = slice(c * KEY_CHUNK, (c + 1) * KEY_CHUNK)
            for g in range(t // QUERY_CHUNK):
                cols = slice(g * QUERY_CHUNK, (g + 1) * QUERY_CHUNK)
                shift = g * QUERY_CHUNK - c * KEY_CHUNK
                for hh in heads:
                    if diagonal and shift <= -QUERY_CHUNK:
                        s_ref[hh, slot, rows, cols] = jnp.full((KEY_CHUNK, QUERY_CHUNK), -jnp.inf, F32)
                        continue
                    bias = csb_ref[hh, pl.ds(k0, KEY_CHUNK), :]
                    bias = jnp.concatenate([bias] * (QUERY_CHUNK // LANES), axis=1)
                    kc = k_ref[pl.ds(k0, KEY_CHUNK), head_cols(hh)]
                    s = jnp.dot(kc, qts[hh][:, cols], preferred_element_type=F32) - bias
                    if diagonal and shift < KEY_CHUNK - 1:
                        s = jnp.where(diff <= shift, s, -jnp.inf)
                    s_ref[hh, slot, rows, cols] = s
                    mparts[hh][g] = jnp.maximum(mparts[hh][g], _colmax8(s))
        return [jnp.concatenate(mparts[hh], axis=1) for hh in heads]

    def new_max(m_old, mpart):
        m_new = jnp.maximum(m_old, jnp.max(mpart, axis=0, keepdims=True))
        return m_new, jnp.exp2(m_old - m_new)

    def half(step, slot, carry, last=False):
        ms, alphas = carry
        j = i - step
        lsums = [jnp.zeros((8, t), F32) for _ in heads]
        for c in range(t // PROB_CHUNK):
            rows = slice(c * PROB_CHUNK, (c + 1) * PROB_CHUNK)
            for hh in heads:
                p = jnp.exp2(s_ref[hh, slot, rows, :] - ms[hh])
                p_ref[hh, rows, :] = p.astype(BF16)
                lsums[hh] = lsums[hh] + _colsum8(p)
        if not last:
            mparts = scores(j - 1, 1 - slot, False)
            new = [new_max(ms[hh], mparts[hh]) for hh in heads]
            carry = tuple(m for m, _ in new), tuple(a for _, a in new)
        for hh in heads:
            lp_ref[hh] = alphas[hh] * lp_ref[hh] + lsums[hh]
            acc_ref[hh] = alphas[hh] * acc_ref[hh] + jnp.dot(vt_ref[hh, j], p_ref[hh],
                                                            preferred_element_type=F32)
        return carry

    def tile_live(j, ms):
        last_key = jnp.maximum(j + 1, 1) * t - 1
        gaps = [reach[hh] - csb_ref[hh, pl.ds(last_key, 1), :][:, 0:1] - ms[hh] for hh in heads]
        return jnp.max(functools.reduce(jnp.maximum, gaps)) > WEIGHT_FLOOR_LOG2

    first = [new_max(jnp.full((1, t), -jnp.inf, F32), mp) for mp in scores(i, 0, True)]
    carry = tuple(m for m, _ in first), tuple(a for _, a in first)
    n_pairs = (n - 1) // 2

    def pair(c):
        mm, _, ms, alphas = c
        carry = half(2 * mm, 0, (ms, alphas))
        alive = tile_live(i - (2 * mm + 3), carry[0])
        return (mm + 1, alive) + half(2 * mm + 1, 1, carry)

    mm, alive, ms, alphas = lax.while_loop(
        lambda c: (c[0] < n_pairs) & c[1], pair, (jnp.int32(0), tile_live(i - 1, carry[0])) + carry)
    step = 2 * mm
    single = (step == n - 1) | jnp.logical_not(alive)

    @pl.when(single)
    def _():
        half(step, 0, (ms, alphas), last=True)

    @pl.when(jnp.logical_not(single))
    def _():
        half(step + 1, 1, half(step, 0, (ms, alphas)), last=True)

    for hh in heads:
        l = jnp.sum(lp_ref[hh], axis=0, keepdims=True)
        o_ref[:, head_cols(hh)] = (acc_ref[hh] / l).T.astype(o_ref.dtype)


def fox_attention(qkv, ccol, *, t=512, hg=2):
    s = qkv.shape[0]
    d = qkv.shape[1] // 3
    ng = d // (hg * HEAD_DIM)
    w = hg * HEAD_DIM
    return pl.pallas_call(
        functools.partial(_fox_kernel, t=t, hg=hg),
        grid=(ng, s // t),
        in_specs=[pl.BlockSpec((t, w), lambda g, i: (i, g)),
                  pl.BlockSpec((s, w), lambda g, i: (0, ng + g), pipeline_mode=pl.Buffered(1)),
                  pl.BlockSpec((s, w), lambda g, i: (0, 2 * ng + g), pipeline_mode=pl.Buffered(1)),
                  pl.BlockSpec((s, LANES), lambda g, i: (0, 0), pipeline_mode=pl.Buffered(1))],
        out_specs=pl.BlockSpec((t, w), lambda g, i: (i, g)),
        out_shape=jax.ShapeDtypeStruct((s, d), BF16),
        scratch_shapes=[pltpu.VMEM((hg, s // t, HEAD_DIM, t), BF16),
                        pltpu.VMEM((hg, s, LANES), F32),
                        pltpu.VMEM((hg, 1, LANES), F32),
                        pltpu.VMEM((hg, 2, t, t), F32),
                        pltpu.VMEM((hg, t, t), BF16),
                        pltpu.VMEM((hg, HEAD_DIM, t), F32), pltpu.VMEM((hg, 8, t), F32)],
        compiler_params=_params("arbitrary", "arbitrary"),
        name="fox_attention")(qkv, qkv, qkv, ccol)


def _sb_kernel(q_ref, k_ref, v_ref, o_ref, vt_ref, lb_ref, lm_ref, w_ref, acc_ref, *, t, hg):
    i = pl.program_id(1)
    n_sub = t // MXU_WIDTH
    heads = range(hg)

    def head_cols(hh):
        return slice(hh * HEAD_DIM, (hh + 1) * HEAD_DIM)

    @pl.when(i == 0)
    def _():
        def stage(c, carry):
            rows = pl.ds(pl.multiple_of(c * t, t), t)
            for hh in heads:
                vt_ref[hh, c] = v_ref[rows, head_cols(hh)].astype(F32).T.astype(BF16)
            return carry
        lax.fori_loop(0, k_ref.shape[0] // t, stage, None)

    qts = [q_ref[:, head_cols(hh)].astype(F32).T.astype(BF16) for hh in heads]
    acc_ref[...] = jnp.zeros(acc_ref.shape, F32)
    diff = (lax.broadcasted_iota(jnp.int32, (KEY_CHUNK, QUERY_CHUNK), 0)
            - lax.broadcasted_iota(jnp.int32, (KEY_CHUNK, QUERY_CHUNK), 1))
    later = jnp.where(lax.broadcasted_iota(jnp.int32, (MXU_WIDTH, MXU_WIDTH), 1)
                      > lax.broadcasted_iota(jnp.int32, (MXU_WIDTH, MXU_WIDTH), 0),
                      1.0, 0.0).astype(BF16)

    def scores(j, slot, diagonal):
        for c in range(t // KEY_CHUNK):
            k0 = pl.multiple_of(j * t + c * KEY_CHUNK, KEY_CHUNK)
            rows = slice(c * KEY_CHUNK, (c + 1) * KEY_CHUNK)
            for g in range(t // QUERY_CHUNK):
                cols = slice(g * QUERY_CHUNK, (g + 1) * QUERY_CHUNK)
                shift = g * QUERY_CHUNK - c * KEY_CHUNK
                for hh in heads:
                    if diagonal and shift <= -(QUERY_CHUNK - 1):
                        lb_ref[hh, slot, rows, cols] = jnp.full((KEY_CHUNK, QUERY_CHUNK), -jnp.inf, F32)
                        lm_ref[hh, slot, rows, cols] = jnp.zeros((KEY_CHUNK, QUERY_CHUNK), BF16)
                        continue
                    kc = k_ref[pl.ds(k0, KEY_CHUNK), head_cols(hh)]
                    z2 = jnp.dot(kc, qts[hh][:, cols], preferred_element_type=F32)
                    lb2 = jnp.minimum(z2, 0.0) - jnp.log(1.0 + jnp.exp2(_neg_abs(z2))) * LOG2E
                    l1m2 = lb2 - z2
                    if diagonal and shift < KEY_CHUNK:
                        visible = diff < shift
                        lb2 = jnp.where(visible, lb2, -jnp.inf)
                        l1m2 = jnp.where(visible, l1m2, 0.0)
                    lb_ref[hh, slot, rows, cols] = lb2
                    lm_ref[hh, slot, rows, cols] = l1m2.astype(BF16)

    def weights(slot, runs):
        runs = list(runs)
        for b in reversed(range(n_sub)):
            rows = slice(b * MXU_WIDTH, (b + 1) * MXU_WIDTH)
            totals = [[] for _ in heads]
            for g in range(t // QUERY_CHUNK):
                cols = slice(g * QUERY_CHUNK, (g + 1) * QUERY_CHUNK)
                for hh in heads:
                    lm = lm_ref[hh, slot, rows, cols]
                    within = jnp.dot(later, lm, preferred_element_type=F32)
                    w = jnp.exp2(lb_ref[hh, slot, rows, cols] + within + runs[hh][:, cols])
                    w_ref[hh, slot, rows, cols] = w.astype(BF16)
                    totals[hh].append(within[0:1, :] + lm[0:1, :].astype(F32))
            for hh in heads:
                runs[hh] = runs[hh] + jnp.concatenate(totals[hh], axis=1)
        return tuple(runs)

    def value_product(j, slot):
        for hh in heads:
            acc_ref[hh] += jnp.dot(vt_ref[hh, j], w_ref[hh, slot], preferred_element_type=F32)

    def live(runs):
        return jnp.max(functools.reduce(jnp.maximum, runs)) > WEIGHT_FLOOR_LOG2

    def tile(j, runs):
        scores(j, 0, False)
        runs = weights(0, runs)
        go = live(runs)
        value_product(j, 0)
        return runs, go

    start = tuple(jnp.zeros((1, t), F32) for _ in heads)

    @pl.when(i == 0)
    def _():
        scores(0, 0, True)
        weights(0, start)
        value_product(0, 0)

    @pl.when(i > 0)
    def _():
        scores(i, 0, True)
        scores(i - 1, 1, False)
        runs = weights(1, weights(0, start))
        go = live(runs)
        value_product(i, 0)
        value_product(i - 1, 1)
        lax.while_loop(lambda c: (c[0] >= 0) & c[2],
                       lambda c: (c[0] - 1,) + tile(c[0], c[1]), (i - 2, runs, go))

    for hh in heads:
        o_ref[:, head_cols(hh)] = acc_ref[hh].T.astype(o_ref.dtype)


def sb_attention(q, kv, *, t=256, hg=4):
    s, d = q.shape
    ng = d // (hg * HEAD_DIM)
    w = hg * HEAD_DIM
    return pl.pallas_call(
        functools.partial(_sb_kernel, t=t, hg=hg),
        grid=(ng, s // t),
        in_specs=[pl.BlockSpec((t, w), lambda g, i: (i, g)),
                  pl.BlockSpec((s, w), lambda g, i: (0, g)),
                  pl.BlockSpec((s, w), lambda g, i: (0, ng + g))],
        out_specs=pl.BlockSpec((t, w), lambda g, i: (i, g)),
        out_shape=jax.ShapeDtypeStruct((s, d), BF16),
        scratch_shapes=[pltpu.VMEM((hg, s // t, HEAD_DIM, t), BF16),
                        pltpu.VMEM((hg, 2, t, t), F32),
                        pltpu.VMEM((hg, 2, t, t), BF16),
                        pltpu.VMEM((hg, 2, t, t), BF16),
                        pltpu.VMEM((hg, HEAD_DIM, t), F32)],
        compiler_params=_params("arbitrary", "arbitrary"),
        name="sb_attention")(q, kv, kv)


def _matmul_residual_kernel(order_ref, a_ref, w_ref, h_ref, o_ref, w16_ref):
    @pl.when(pl.program_id(0) == 0)
    def _():
        for p in range(w_ref.shape[0] // HEAD_DIM):
            src = pl.multiple_of(order_ref[p] * HEAD_DIM, HEAD_DIM)
            w16_ref[p * HEAD_DIM:(p + 1) * HEAD_DIM, :] = w_ref[pl.ds(src, HEAD_DIM), :].astype(BF16)

    o_ref[...] = h_ref[...] + jnp.dot(a_ref[...], w16_ref[...], preferred_element_type=F32)


def matmul_residual(a, w, layer, h, order, *, tm=512):
    s, k = a.shape
    n = w.shape[2]
    return pl.pallas_call(
        _matmul_residual_kernel,
        grid_spec=pltpu.PrefetchScalarGridSpec(
            num_scalar_prefetch=1, grid=(s // tm,),
            in_specs=[pl.BlockSpec((tm, k), lambda i, order: (i, 0)),
                      pl.BlockSpec((None, k, n), lambda i, order: (layer, 0, 0),
                                   pipeline_mode=pl.Buffered(1)),
                      pl.BlockSpec((tm, n), lambda i, order: (i, 0))],
            out_specs=pl.BlockSpec((tm, n), lambda i, order: (i, 0)),
            scratch_shapes=[pltpu.VMEM((k, n), BF16)]),
        out_shape=jax.ShapeDtypeStruct((s, n), F32),
        compiler_params=_params("arbitrary"),
        name="matmul_residual")(order, a, w, h)


def _ffn_step(h_ref, g_ref, weights, gout_ref, o_ref, xn_ref, norm_out):
    @pl.when(pl.program_id(1) == 0)
    def _():
        x = h_ref[...]
        xn_ref[...] = _rms_scale(x, g_ref[...]).astype(BF16)
        o_ref[...] = x

    wg, wu, wd = weights()
    xn = xn_ref[...]
    gate = jnp.dot(xn, wg, preferred_element_type=F32)
    up = jnp.dot(xn, wu, preferred_element_type=F32)
    act = (gate * jax.nn.sigmoid(gate) * up).astype(BF16)
    o_ref[...] += jnp.dot(act, wd, preferred_element_type=F32)

    if norm_out:
        @pl.when(pl.program_id(1) == pl.num_programs(1) - 1)
        def _():
            o_ref[...] = _rms_scale(o_ref[...], gout_ref[...])


def _ffn_first_kernel(h_ref, g_ref, wg_ref, wu_ref, wd_ref, gout_ref,
                      o_ref, wg_out_ref, wu_out_ref, wd_out_ref, xn_ref, *, norm_out):
    def weights():
        wg, wu, wd = (w[...].astype(BF16) for w in (wg_ref, wu_ref, wd_ref))
        wg_out_ref[...] = wg
        wu_out_ref[...] = wu
        wd_out_ref[...] = wd
        return wg, wu, wd

    _ffn_step(h_ref, g_ref, weights, gout_ref, o_ref, xn_ref, norm_out)


def _ffn_rest_kernel(h_ref, g_ref, wg_ref, wu_ref, wd_ref, gout_ref, first_ref, o_ref, xn_ref, *, norm_out):
    del first_ref
    _ffn_step(h_ref, g_ref, lambda: (wg_ref[...], wu_ref[...], wd_ref[...]), gout_ref, o_ref, xn_ref,
              norm_out)


def ffn(h, g, wg, wu, wd, layer, g_out, *, norm_out, tm=512, tf=512, tm_first=1024, tf_first=256):
    s, d = h.shape
    dff = wg.shape[2]
    vec = pl.BlockSpec((1, d), lambda i, f: (0, 0))
    skip = tm_first // tm

    def w_specs(lyr, t_f):
        return [pl.BlockSpec((None, d, t_f), lambda i, f: (lyr, 0, f)),
                pl.BlockSpec((None, d, t_f), lambda i, f: (lyr, 0, f)),
                pl.BlockSpec((None, t_f, d), lambda i, f: (lyr, f, 0))]

    first, wg16, wu16, wd16 = pl.pallas_call(
        functools.partial(_ffn_first_kernel, norm_out=norm_out), grid=(1, dff // tf_first),
        in_specs=[pl.BlockSpec((tm_first, d), lambda i, f: (0, 0), pipeline_mode=pl.Buffered(1)), vec,
                  *w_specs(layer, tf_first), vec],
        out_specs=[pl.BlockSpec((tm_first, d), lambda i, f: (0, 0)), *w_specs(0, tf_first)],
        out_shape=[jax.ShapeDtypeStruct((s, d), F32),
                   jax.ShapeDtypeStruct((1, d, dff), BF16), jax.ShapeDtypeStruct((1, d, dff), BF16),
                   jax.ShapeDtypeStruct((1, dff, d), BF16)],
        scratch_shapes=[pltpu.VMEM((tm_first, d), BF16)],
        compiler_params=_params("arbitrary", "arbitrary"),
        name="ffn_first")(h, g, wg, wu, wd, g_out)
    return pl.pallas_call(
        functools.partial(_ffn_rest_kernel, norm_out=norm_out), grid=((s - tm_first) // tm, dff // tf),
        in_specs=[pl.BlockSpec((tm, d), lambda i, f: (i + skip, 0)), vec, *w_specs(0, tf), vec,
                  pl.BlockSpec(memory_space=pl.ANY)],
        out_specs=pl.BlockSpec((tm, d), lambda i, f: (i + skip, 0)),
        out_shape=jax.ShapeDtypeStruct((s, d), F32),
        input_output_aliases={6: 0},
        scratch_shapes=[pltpu.VMEM((tm, d), BF16)],
        compiler_params=_params("parallel", "arbitrary"),
        name="ffn_rest")(h, g, wg16, wu16, wd16, g_out, first)


def kernel(x, attn_norm, ffn_norm, a_w_in, a_b_f, a_w_out, kv_norm, w_kv, b_w_q, b_w_out,
           ffn_w_gate, ffn_w_up, ffn_w_down, final_norm):
    b, s, d = x.shape
    assert b == 1 and d == N_HEADS * HEAD_DIM
    depth = attn_norm.shape[0]
    n_a = a_w_in.shape[0]
    h = x.reshape(s, d)
    w_in_t = jnp.swapaxes(a_w_in, 1, 2)
    fox_order = jnp.argsort(a_b_f, axis=1).astype(jnp.int32)
    w_qkv = cast_bf16_heads(w_in_t, fox_order, 3)
    w_f = jnp.take_along_axis(w_in_t[:, 3 * d:], fox_order[:, :, None], axis=1)
    w_f = jnp.pad(w_f, ((0, 0), (0, LANES - N_HEADS), (0, 0)))
    b_f = jnp.pad(jnp.take_along_axis(a_b_f, fox_order, axis=1), ((0, 0), (0, LANES - N_HEADS)))
    natural = jnp.arange(N_HEADS, dtype=jnp.int32)
    w_kv = cast_bf16(w_kv[None])
    kv = None
    for layer in range(depth):
        g_attn = attn_norm[layer].reshape(1, d)
        if layer < n_a:
            qkv, f_logit = norm_matmul(h, g_attn, w_qkv, layer, w_f, transposed=True,
                                       scaled_cols=d, col_scale=HEAD_DIM ** -0.5 * LOG2E)
            o = fox_attention(qkv, gate_cumsum(f_logit, b_f[layer:layer + 1]))
            h = matmul_residual(o, a_w_out, layer, h, fox_order[layer])
        else:
            j = layer - n_a
            q = norm_matmul_resident(h, g_attn, b_w_q, j, col_scale=HEAD_DIM ** -0.5 * LOG2E)
            o = sb_attention(q, kv)
            h = matmul_residual(o, b_w_out, j, h, natural)
        h = ffn(h, ffn_norm[layer].reshape(1, d), ffn_w_gate, ffn_w_up, ffn_w_down, layer,
                final_norm.reshape(1, d), norm_out=layer == depth - 1)
        if layer == n_a - 1:
            kv = norm_matmul(h, kv_norm.reshape(1, d), w_kv, 0)
    return h.reshape(b, s, d)
```

```python
import functools

import jax
import jax.numpy as jnp
from jax import lax
from jax.experimental import pallas as pl
from jax.experimental.pallas import tpu as pltpu

N_HEADS = 16
HEAD_DIM = 128
EPS = 1e-6
LOG2E = 1.4426950408889634
LANES = 128
MXU_WIDTH = 256
VMEM_LIMIT = 56 * 1024 * 1024

KEY_CHUNK = 128
QUERY_CHUNK = MXU_WIDTH
PROB_CHUNK = 32
WEIGHT_FLOOR_LOG2 = -154.0

F32 = jnp.float32
BF16 = jnp.bfloat16


def _params(*sem):
    return pltpu.CompilerParams(dimension_semantics=sem, vmem_limit_bytes=VMEM_LIMIT)


def _rms_scale(x, g):
    ms = jnp.mean(x * x, axis=-1, keepdims=True)
    return x * lax.rsqrt(ms + EPS) * g


def _cast_heads_kernel(order_ref, *refs):
    del order_ref
    *x_refs, o_ref = refs
    for u, x_ref in enumerate(x_refs):
        o_ref[u * HEAD_DIM:(u + 1) * HEAD_DIM, :] = x_ref[...].astype(o_ref.dtype)


def cast_bf16_heads(w, orders, n_sections, *, per_step=4):
    n_layers, _, c = w.shape
    n_heads = orders.shape[1]

    def source(u):
        def index(l, i, order):
            slot = i * per_step + u
            return l, (slot // n_heads) * n_heads + order[l * n_heads + slot % n_heads], 0
        return index

    return pl.pallas_call(
        _cast_heads_kernel,
        grid_spec=pltpu.PrefetchScalarGridSpec(
            num_scalar_prefetch=1, grid=(n_layers, n_sections * n_heads // per_step),
            in_specs=[pl.BlockSpec((None, HEAD_DIM, c), source(u)) for u in range(per_step)],
            out_specs=pl.BlockSpec((None, per_step * HEAD_DIM, c), lambda l, i, order: (l, i, 0))),
        out_shape=jax.ShapeDtypeStruct((n_layers, n_sections * n_heads * HEAD_DIM, c), BF16),
        compiler_params=_params("parallel", "parallel"),
        name="cast_bf16_heads")(orders.reshape(-1), *([w] * per_step))


def _scaled_tile(acc, n_scaled_tiles, col_scale):
    if n_scaled_tiles == 0:
        return acc
    return acc * jnp.where(pl.program_id(1) < n_scaled_tiles, col_scale, 1.0)


def _project(xn, w, transposed):
    if transposed:
        return lax.dot_general(xn, w, (((1,), (1,)), ((), ())), preferred_element_type=F32)
    return jnp.dot(xn, w, preferred_element_type=F32)


def _norm_matmul_kernel(x_ref, g_ref, w_ref, o_ref, xn_ref, *, n_scaled_tiles, col_scale, transposed):
    @pl.when(pl.program_id(1) == 0)
    def _():
        xn_ref[...] = _rms_scale(x_ref[...], g_ref[...]).astype(BF16)

    acc = _project(xn_ref[...], w_ref[...].astype(BF16), transposed)
    o_ref[...] = _scaled_tile(acc, n_scaled_tiles, col_scale).astype(o_ref.dtype)


def _norm_matmul_gate_kernel(x_ref, g_ref, w_ref, wf_ref, o_ref, f_ref, xn_ref,
                             *, n_scaled_tiles, col_scale, transposed):
    @pl.when(pl.program_id(1) == 0)
    def _():
        xn = _rms_scale(x_ref[...], g_ref[...]).astype(BF16)
        xn_ref[...] = xn
        f_ref[...] = _project(xn, wf_ref[...].astype(BF16), transposed)

    acc = _project(xn_ref[...], w_ref[...].astype(BF16), transposed)
    o_ref[...] = _scaled_tile(acc, n_scaled_tiles, col_scale).astype(o_ref.dtype)


def norm_matmul(x, g, w, layer, wf=None, *, transposed=False, scaled_cols=0, col_scale=1.0,
                tm=1024, tn=1024):
    s, d = x.shape
    n = w.shape[1] if transposed else w.shape[2]
    assert scaled_cols % tn == 0
    static = dict(n_scaled_tiles=scaled_cols // tn, col_scale=col_scale, transposed=transposed)
    grid = (s // tm, n // tn)
    x_spec = pl.BlockSpec((tm, d), lambda i, j: (i, 0))
    g_spec = pl.BlockSpec((1, d), lambda i, j: (0, 0))
    if transposed:
        w_spec = pl.BlockSpec((None, tn, d), lambda i, j: (layer, j, 0))
        wf_spec = pl.BlockSpec((None, LANES, d), lambda i, j: (layer, 0, 0))
    else:
        w_spec = pl.BlockSpec((None, d, tn), lambda i, j: (layer, 0, j))
        wf_spec = pl.BlockSpec((None, d, LANES), lambda i, j: (layer, 0, 0))
    o_spec = pl.BlockSpec((tm, tn), lambda i, j: (i, j))
    scratch = [pltpu.VMEM((tm, d), BF16)]
    if wf is None:
        return pl.pallas_call(
            functools.partial(_norm_matmul_kernel, **static), grid=grid,
            in_specs=[x_spec, g_spec, w_spec], out_specs=o_spec,
            out_shape=jax.ShapeDtypeStruct((s, n), BF16),
            scratch_shapes=scratch, compiler_params=_params("parallel", "arbitrary"),
            name="norm_matmul")(x, g, w)
    f_spec = pl.BlockSpec((tm, LANES), lambda i, j: (i, 0))
    return pl.pallas_call(
        functools.partial(_norm_matmul_gate_kernel, **static), grid=grid,
        in_specs=[x_spec, g_spec, w_spec, wf_spec], out_specs=[o_spec, f_spec],
        out_shape=[jax.ShapeDtypeStruct((s, n), BF16),
                   jax.ShapeDtypeStruct((s, LANES), F32)],
        scratch_shapes=scratch, compiler_params=_params("parallel", "arbitrary"),
        name="norm_matmul_gate")(x, g, w, wf)


def _norm_matmul_resident_kernel(x_ref, g_ref, w_ref, o_ref, w16_ref, *, col_scale):
    @pl.when(pl.program_id(0) == 0)
    def _():
        w16_ref[...] = w_ref[...].astype(BF16)

    xn = _rms_scale(x_ref[...], g_ref[...]).astype(BF16)
    acc = jnp.dot(xn, w16_ref[...], preferred_element_type=F32)
    o_ref[...] = (acc * col_scale).astype(o_ref.dtype)


def norm_matmul_resident(x, g, w, layer, *, col_scale, tm=512):
    s, d = x.shape
    n = w.shape[2]
    return pl.pallas_call(
        functools.partial(_norm_matmul_resident_kernel, col_scale=col_scale), grid=(s // tm,),
        in_specs=[pl.BlockSpec((tm, d), lambda i: (i, 0)),
                  pl.BlockSpec((1, d), lambda i: (0, 0)),
                  pl.BlockSpec((None, d, n), lambda i: (layer, 0, 0), pipeline_mode=pl.Buffered(1))],
        out_specs=pl.BlockSpec((tm, n), lambda i: (i, 0)),
        out_shape=jax.ShapeDtypeStruct((s, n), BF16),
        scratch_shapes=[pltpu.VMEM((d, n), BF16)],
        compiler_params=_params("arbitrary"), name="norm_matmul_resident")(x, g, w)


def _log_sigmoid(z):
    return jnp.minimum(z, 0.0) - jnp.log(1.0 + jnp.exp(-jnp.abs(z)))


def _split3(x):
    hi = x.astype(BF16)
    r = x - hi.astype(F32)
    mid = r.astype(BF16)
    lo = (r - mid.astype(F32)).astype(BF16)
    return hi, mid, lo


def _gate_cumsum_kernel(f_ref, b_ref, ccol_ref, carry_ref, *, tc):
    @pl.when(pl.program_id(0) == 0)
    def _():
        carry_ref[...] = jnp.zeros_like(carry_ref)

    log_f = _log_sigmoid(f_ref[...] + b_ref[...])
    row = lax.broadcasted_iota(jnp.int32, (tc, tc), 0)
    col = lax.broadcasted_iota(jnp.int32, (tc, tc), 1)
    tri = jnp.where(col <= row, 1.0, 0.0).astype(BF16)
    hi, mid, lo = _split3(log_f)
    c = (jnp.dot(tri, hi, preferred_element_type=F32)
         + jnp.dot(tri, mid, preferred_element_type=F32)
         + jnp.dot(tri, lo, preferred_element_type=F32)) + carry_ref[...]
    ccol_ref[...] = c
    carry_ref[...] = c[tc - 1:tc, :]


def gate_cumsum(f, b, *, tc=512):
    s = f.shape[0]
    return pl.pallas_call(
        functools.partial(_gate_cumsum_kernel, tc=tc), grid=(s // tc,),
        in_specs=[pl.BlockSpec((tc, LANES), lambda i: (i, 0)),
                  pl.BlockSpec((1, LANES), lambda i: (0, 0))],
        out_specs=pl.BlockSpec((tc, LANES), lambda i: (i, 0)),
        out_shape=jax.ShapeDtypeStruct((s, LANES), F32),
        scratch_shapes=[pltpu.VMEM((1, LANES), F32)],
        compiler_params=_params("arbitrary"), name="gate_cumsum")(f, b)


def _colmax8(x):
    return jnp.max(x.reshape(x.shape[0] // 8, 8, x.shape[1]), axis=0)


def _neg_abs(x):
    bits = lax.bitcast_convert_type(x, jnp.int32) | jnp.int32(-2 ** 31)
    return lax.bitcast_convert_type(bits, F32)


def _colsum8(x):
    return jnp.sum(x.reshape(x.shape[0] // 8, 8, x.shape[1]), axis=0)


def _fox_kernel(q_ref, k_ref, v_ref, ccol_ref, o_ref,
                vt_ref, csb_ref, kn_ref, s_ref, p_ref, acc_ref, lp_ref, *, t, hg):
    grp = pl.program_id(0)
    i = pl.program_id(1)
    n = i + 1
    heads = range(hg)

    def head_cols(hh):
        return slice(hh * HEAD_DIM, (hh + 1) * HEAD_DIM)

    @pl.when(i == 0)
    def _():
        for hh in heads:
            kn_ref[hh] = jnp.zeros((1, LANES), F32)

        def stage(c, carry):
            rows = pl.ds(pl.multiple_of(c * t, t), t)
            lane = lax.broadcasted_iota(jnp.int32, (t, LANES), 1)
            for hh in heads:
                vt_ref[hh, c] = v_ref[rows, head_cols(hh)].astype(F32).T.astype(BF16)
                cs = jnp.sum(jnp.where(lane == grp * hg + hh, ccol_ref[rows, :], 0.0),
                             axis=-1, keepdims=True)
                csb_ref[hh, rows, :] = jnp.broadcast_to(cs * LOG2E, (t, LANES))
                kf = k_ref[rows, head_cols(hh)].astype(F32)
                kn_ref[hh] = jnp.maximum(kn_ref[hh], jnp.max(jnp.sum(kf * kf, axis=-1, keepdims=True)))
            return carry
        lax.fori_loop(0, k_ref.shape[0] // t, stage, None)

    qts = [q_ref[:, head_cols(hh)].astype(F32).T.astype(BF16) for hh in heads]
    reach = [jnp.sqrt(jnp.sum(qts[hh].astype(F32) ** 2, axis=0, keepdims=True) * kn_ref[hh][:, 0:1])
             for hh in heads]
    acc_ref[...] = jnp.zeros(acc_ref.shape, F32)
    lp_ref[...] = jnp.zeros(lp_ref.shape, F32)
    diff = (lax.broadcasted_iota(jnp.int32, (KEY_CHUNK, QUERY_CHUNK), 0)
            - lax.broadcasted_iota(jnp.int32, (KEY_CHUNK, QUERY_CHUNK), 1))

    def scores(j, slot, diagonal):
        mparts = [[jnp.full((8, QUERY_CHUNK), -jnp.inf, F32) for _ in range(t // QUERY_CHUNK)]
                  for _ in heads]
        for c in range(t // KEY_CHUNK):
            k0 = pl.multiple_of(j * t + c * KEY_CHUNK, KEY_CHUNK)
            rows = slice(c * KEY_CHUNK, (c + 1) * KEY_CHUNK)
            for g in range(t // QUERY_CHUNK):
                cols = slice(g * QUERY_CHUNK, (g + 1) * QUERY_CHUNK)
                shift = g * QUERY_CHUNK - c * KEY_CHUNK
                for hh in heads:
                    if diagonal and shift <= -QUERY_CHUNK:
                        s_ref[hh, slot, rows, cols] = jnp.full((KEY_CHUNK, QUERY_CHUNK), -jnp.inf, F32)
                        continue
                    bias = csb_ref[hh, pl.ds(k0, KEY_CHUNK), :]
                    bias = jnp.concatenate([bias] * (QUERY_CHUNK // LANES), axis=1)
                    kc = k_ref[pl.ds(k0, KEY_CHUNK), head_cols(hh)]
                    s = jnp.dot(kc, qts[hh][:, cols], preferred_element_type=F32) - bias
                    if diagonal and shift < KEY_CHUNK - 1:
                        s = jnp.where(diff <= shift, s, -jnp.inf)
                    s_ref[hh, slot, rows, cols] = s
                    mparts[hh][g] = jnp.maximum(mparts[hh][g], _colmax8(s))
        return [jnp.concatenate(mparts[hh], axis=1) for hh in heads]

    def new_max(m_old, mpart):
        m_new = jnp.maximum(m_old, jnp.max(mpart, axis=0, keepdims=True))
        return m_new, jnp.exp2(m_old - m_new)

    def half(step, slot, carry, last=False):
        ms, alphas = carry
        j = i - step
        lsums = [jnp.zeros((8, t), F32) for _ in heads]
        for c in range(t // PROB_CHUNK):
            rows = slice(c * PROB_CHUNK, (c + 1) * PROB_CHUNK)
            for hh in heads:
                p = jnp.exp2(s_ref[hh, slot, rows, :] - ms[hh])
                p_ref[hh, rows, :] = p.astype(BF16)
                lsums[hh] = lsums[hh] + _colsum8(p)
        if not last:
            mparts = scores(j - 1, 1 - slot, False)
            new = [new_max(ms[hh], mparts[hh]) for hh in heads]
            carry = tuple(m for m, _ in new), tuple(a for _, a in new)
        for hh in heads:
            lp_ref[hh] = alphas[hh] * lp_ref[hh] + lsums[hh]
            acc_ref[hh] = alphas[hh] * acc_ref[hh] + jnp.dot(vt_ref[hh, j], p_ref[hh],
                                                            preferred_element_type=F32)
        return carry

    def tile_live(j, ms):
        last_key = jnp.maximum(j + 1, 1) * t - 1
        gaps = [reach[hh] - csb_ref[hh, pl.ds(last_key, 1), :][:, 0:1] - ms[hh] for hh in heads]
        return jnp.max(functools.reduce(jnp.maximum, gaps)) > WEIGHT_FLOOR_LOG2

    first = [new_max(jnp.full((1, t), -jnp.inf, F32), mp) for mp in scores(i, 0, True)]
    carry = tuple(m for m, _ in first), tuple(a for _, a in first)
    n_pairs = (n - 1) // 2

    def pair(c):
        mm, _, ms, alphas = c
        carry = half(2 * mm, 0, (ms, alphas))
        alive = tile_live(i - (2 * mm + 3), carry[0])
        return (mm + 1, alive) + half(2 * mm + 1, 1, carry)

    mm, alive, ms, alphas = lax.while_loop(
        lambda c: (c[0] < n_pairs) & c[1], pair, (jnp.int32(0), tile_live(i - 1, carry[0])) + carry)
    step = 2 * mm
    single = (step == n - 1) | jnp.logical_not(alive)

    @pl.when(single)
    def _():
        half(step, 0, (ms, alphas), last=True)

    @pl.when(jnp.logical_not(single))
    def _():
        half(step + 1, 1, half(step, 0, (ms, alphas)), last=True)

    for hh in heads:
        l = jnp.sum(lp_ref[hh], axis=0, keepdims=True)
        o_ref[:, head_cols(hh)] = (acc_ref[hh] / l).T.astype(o_ref.dtype)


def fox_attention(qkv, ccol, *, t=256, hg=4):
    s = qkv.shape[0]
    d = qkv.shape[1] // 3
    ng = d // (hg * HEAD_DIM)
    w = hg * HEAD_DIM
    return pl.pallas_call(
        functools.partial(_fox_kernel, t=t, hg=hg),
        grid=(ng, s // t),
        in_specs=[pl.BlockSpec((t, w), lambda g, i: (i, g)),
                  pl.BlockSpec((s, w), lambda g, i: (0, ng + g), pipeline_mode=pl.Buffered(1)),
                  pl.BlockSpec((s, w), lambda g, i: (0, 2 * ng + g), pipeline_mode=pl.Buffered(1)),
                  pl.BlockSpec((s, LANES), lambda g, i: (0, 0), pipeline_mode=pl.Buffered(1))],
        out_specs=pl.BlockSpec((t, w), lambda g, i: (i, g)),
        out_shape=jax.ShapeDtypeStruct((s, d), BF16),
        scratch_shapes=[pltpu.VMEM((hg, s // t, HEAD_DIM, t), BF16),
                        pltpu.VMEM((hg, s, LANES), F32),
                        pltpu.VMEM((hg, 1, LANES), F32),
                        pltpu.VMEM((hg, 2, t, t), F32),
                        pltpu.VMEM((hg, t, t), BF16),
                        pltpu.VMEM((hg, HEAD_DIM, t), F32), pltpu.VMEM((hg, 8, t), F32)],
        compiler_params=_params("arbitrary", "arbitrary"),
        name="fox_attention")(qkv, qkv, qkv, ccol)


def _sb_kernel(q_ref, k_ref, v_ref, o_ref, vt_ref, lb_ref, lm_ref, w_ref, acc_ref, *, t, hg):
    i = pl.program_id(1)
    n_sub = t // MXU_WIDTH
    heads = range(hg)

    def head_cols(hh):
        return slice(hh * HEAD_DIM, (hh + 1) * HEAD_DIM)

    @pl.when(i == 0)
    def _():
        def stage(c, carry):
            rows = pl.ds(pl.multiple_of(c * t, t), t)
            for hh in heads:
                vt_ref[hh, c] = v_ref[rows, head_cols(hh)].astype(F32).T.astype(BF16)
            return carry
        lax.fori_loop(0, k_ref.shape[0] // t, stage, None)

    qts = [q_ref[:, head_cols(hh)].astype(F32).T.astype(BF16) for hh in heads]
    acc_ref[...] = jnp.zeros(acc_ref.shape, F32)
    diff = (lax.broadcasted_iota(jnp.int32, (KEY_CHUNK, QUERY_CHUNK), 0)
            - lax.broadcasted_iota(jnp.int32, (KEY_CHUNK, QUERY_CHUNK), 1))
    later = jnp.where(lax.broadcasted_iota(jnp.int32, (MXU_WIDTH, MXU_WIDTH), 1)
                      > lax.broadcasted_iota(jnp.int32, (MXU_WIDTH, MXU_WIDTH), 0),
                      1.0, 0.0).astype(BF16)

    def scores(j, slot, diagonal):
        for c in range(t // KEY_CHUNK):
            k0 = pl.multiple_of(j * t + c * KEY_CHUNK, KEY_CHUNK)
            rows = slice(c * KEY_CHUNK, (c + 1) * KEY_CHUNK)
            for g in range(t // QUERY_CHUNK):
                cols = slice(g * QUERY_CHUNK, (g + 1) * QUERY_CHUNK)
                shift = g * QUERY_CHUNK - c * KEY_CHUNK
                for hh in heads:
                    if diagonal and shift <= -(QUERY_CHUNK - 1):
                        lb_ref[hh, slot, rows, cols] = jnp.full((KEY_CHUNK, QUERY_CHUNK), -jnp.inf, F32)
                        lm_ref[hh, slot, rows, cols] = jnp.zeros((KEY_CHUNK, QUERY_CHUNK), BF16)
                        continue
                    kc = k_ref[pl.ds(k0, KEY_CHUNK), head_cols(hh)]
                    z2 = jnp.dot(kc, qts[hh][:, cols], preferred_element_type=F32)
                    lb2 = jnp.minimum(z2, 0.0) - jnp.log(1.0 + jnp.exp2(_neg_abs(z2))) * LOG2E
                    l1m2 = lb2 - z2
                    if diagonal and shift < KEY_CHUNK:
                        visible = diff < shift
                        lb2 = jnp.where(visible, lb2, -jnp.inf)
                        l1m2 = jnp.where(visible, l1m2, 0.0)
                    lb_ref[hh, slot, rows, cols] = lb2
                    lm_ref[hh, slot, rows, cols] = l1m2.astype(BF16)

    def weights(slot, runs):
        runs = list(runs)
        for b in reversed(range(n_sub)):
            rows = slice(b * MXU_WIDTH, (b + 1) * MXU_WIDTH)
            totals = [[] for _ in heads]
            for g in range(t // QUERY_CHUNK):
                cols = slice(g * QUERY_CHUNK, (g + 1) * QUERY_CHUNK)
                for hh in heads:
                    lm = lm_ref[hh, slot, rows, cols]
                    within = jnp.dot(later, lm, preferred_element_type=F32)
                    w = jnp.exp2(lb_ref[hh, slot, rows, cols] + within + runs[hh][:, cols])
                    w_ref[hh, slot, rows, cols] = w.astype(BF16)
                    totals[hh].append(within[0:1, :] + lm[0:1, :].astype(F32))
            for hh in heads:
                runs[hh] = runs[hh] + jnp.concatenate(totals[hh], axis=1)
        return tuple(runs)

    def value_product(j, slot):
        for hh in heads:
            acc_ref[hh] += jnp.dot(vt_ref[hh, j], w_ref[hh, slot], preferred_element_type=F32)

    def live(runs):
        return jnp.max(functools.reduce(jnp.maximum, runs)) > WEIGHT_FLOOR_LOG2

    def tile(j, runs):
        scores(j, 0, False)
        runs = weights(0, runs)
        go = live(runs)
        value_product(j, 0)
        return runs, go

    start = tuple(jnp.zeros((1, t), F32) for _ in heads)

    @pl.when(i == 0)
    def _():
        scores(0, 0, True)
        weights(0, start)
        value_product(0, 0)

    @pl.when(i > 0)
    def _():
        scores(i, 0, True)
        scores(i - 1, 1, False)
        runs = weights(1, weights(0, start))
        go = live(runs)
        value_product(i, 0)
        value_product(i - 1, 1)
        lax.while_loop(lambda c: (c[0] >= 0) & c[2],
                       lambda c: (c[0] - 1,) + tile(c[0], c[1]), (i - 2, runs, go))

    for hh in heads:
        o_ref[:, head_cols(hh)] = acc_ref[hh].T.astype(o_ref.dtype)


def sb_attention(q, kv, *, t=256, hg=4):
    s, d = q.shape
    ng = d // (hg * HEAD_DIM)
    w = hg * HEAD_DIM
    return pl.pallas_call(
        functools.partial(_sb_kernel, t=t, hg=hg),
        grid=(ng, s // t),
        in_specs=[pl.BlockSpec((t, w), lambda g, i: (i, g)),
                  pl.BlockSpec((s, w), lambda g, i: (0, g)),
                  pl.BlockSpec((s, w), lambda g, i: (0, ng + g))],
        out_specs=pl.BlockSpec((t, w), lambda g, i: (i, g)),
        out_shape=jax.ShapeDtypeStruct((s, d), BF16),
        scratch_shapes=[pltpu.VMEM((hg, s // t, HEAD_DIM, t), BF16),
                        pltpu.VMEM((hg, 2, t, t), F32),
                        pltpu.VMEM((hg, 2, t, t), BF16),
                        pltpu.VMEM((hg, 2, t, t), BF16),
                        pltpu.VMEM((hg, HEAD_DIM, t), F32)],
        compiler_params=_params("arbitrary", "arbitrary"),
        name="sb_attention")(q, kv, kv)


def _matmul_residual_kernel(order_ref, a_ref, w_ref, h_ref, o_ref, w16_ref):
    @pl.when(pl.program_id(0) == 0)
    def _():
        for p in range(w_ref.shape[0] // HEAD_DIM):
            src = pl.multiple_of(order_ref[p] * HEAD_DIM, HEAD_DIM)
            w16_ref[p * HEAD_DIM:(p + 1) * HEAD_DIM, :] = w_ref[pl.ds(src, HEAD_DIM), :].astype(BF16)

    o_ref[...] = h_ref[...] + jnp.dot(a_ref[...], w16_ref[...], preferred_element_type=F32)


def matmul_residual(a, w, layer, h, order, *, tm=512):
    s, k = a.shape
    n = w.shape[2]
    return pl.pallas_call(
        _matmul_residual_kernel,
        grid_spec=pltpu.PrefetchScalarGridSpec(
            num_scalar_prefetch=1, grid=(s // tm,),
            in_specs=[pl.BlockSpec((tm, k), lambda i, order: (i, 0)),
                      pl.BlockSpec((None, k, n), lambda i, order: (layer, 0, 0),
                                   pipeline_mode=pl.Buffered(1)),
                      pl.BlockSpec((tm, n), lambda i, order: (i, 0))],
            out_specs=pl.BlockSpec((tm, n), lambda i, order: (i, 0)),
            scratch_shapes=[pltpu.VMEM((k, n), BF16)]),
        out_shape=jax.ShapeDtypeStruct((s, n), F32),
        compiler_params=_params("arbitrary"),
        name="matmul_residual")(order, a, w, h)


def _ffn_step(h_ref, g_ref, weights, gout_ref, o_ref, xn_ref, norm_out):
    @pl.when(pl.program_id(1) == 0)
    def _():
        x = h_ref[...]
        xn_ref[...] = _rms_scale(x, g_ref[...]).astype(BF16)
        o_ref[...] = x

    wg, wu, wd = weights()
    xn = xn_ref[...]
    gate = jnp.dot(xn, wg, preferred_element_type=F32)
    up = jnp.dot(xn, wu, preferred_element_type=F32)
    act = (gate * jax.nn.sigmoid(gate) * up).astype(BF16)
    o_ref[...] += jnp.dot(act, wd, preferred_element_type=F32)

    if norm_out:
        @pl.when(pl.program_id(1) == pl.num_programs(1) - 1)
        def _():
            o_ref[...] = _rms_scale(o_ref[...], gout_ref[...])


def _ffn_first_kernel(h_ref, g_ref, wg_ref, wu_ref, wd_ref, gout_ref,
                      o_ref, wg_out_ref, wu_out_ref, wd_out_ref, xn_ref, *, norm_out):
    def weights():
        wg, wu, wd = (w[...].astype(BF16) for w in (wg_ref, wu_ref, wd_ref))
        wg_out_ref[...] = wg
        wu_out_ref[...] = wu
        wd_out_ref[...] = wd
        return wg, wu, wd

    _ffn_step(h_ref, g_ref, weights, gout_ref, o_ref, xn_ref, norm_out)


def _ffn_rest_kernel(h_ref, g_ref, wg_ref, wu_ref, wd_ref, gout_ref, first_ref, o_ref, xn_ref, *, norm_out):
    del first_ref
    _ffn_step(h_ref, g_ref, lambda: (wg_ref[...], wu_ref[...], wd_ref[...]), gout_ref, o_ref, xn_ref,
              norm_out)


def ffn(h, g, wg, wu, wd, layer, g_out, *, norm_out, tm=512, tf=512, tm_first=1024, tf_first=256):
    s, d = h.shape
    dff = wg.shape[2]
    vec = pl.BlockSpec((1, d), lambda i, f: (0, 0))
    skip = tm_first // tm

    def w_specs(lyr, t_f):
        return [pl.BlockSpec((None, d, t_f), lambda i, f: (lyr, 0, f)),
                pl.BlockSpec((None, d, t_f), lambda i, f: (lyr, 0, f)),
                pl.BlockSpec((None, t_f, d), lambda i, f: (lyr, f, 0))]

    first, wg16, wu16, wd16 = pl.pallas_call(
        functools.partial(_ffn_first_kernel, norm_out=norm_out), grid=(1, dff // tf_first),
        in_specs=[pl.BlockSpec((tm_first, d), lambda i, f: (0, 0), pipeline_mode=pl.Buffered(1)), vec,
                  *w_specs(layer, tf_first), vec],
        out_specs=[pl.BlockSpec((tm_first, d), lambda i, f: (0, 0)), *w_specs(0, tf_first)],
        out_shape=[jax.ShapeDtypeStruct((s, d), F32),
                   jax.ShapeDtypeStruct((1, d, dff), BF16), jax.ShapeDtypeStruct((1, d, dff), BF16),
                   jax.ShapeDtypeStruct((1, dff, d), BF16)],
        scratch_shapes=[pltpu.VMEM((tm_first, d), BF16)],
        compiler_params=_params("arbitrary", "arbitrary"),
        name="ffn_first")(h, g, wg, wu, wd, g_out)
    return pl.pallas_call(
        functools.partial(_ffn_rest_kernel, norm_out=norm_out), grid=((s - tm_first) // tm, dff // tf),
        in_specs=[pl.BlockSpec((tm, d), lambda i, f: (i + skip, 0)), vec, *w_specs(0, tf), vec,
                  pl.BlockSpec(memory_space=pl.ANY)],
        out_specs=pl.BlockSpec((tm, d), lambda i, f: (i + skip, 0)),
        out_shape=jax.ShapeDtypeStruct((s, d), F32),
        input_output_aliases={6: 0},
        scratch_shapes=[pltpu.VMEM((tm, d), BF16)],
        compiler_params=_params("parallel", "arbitrary"),
        name="ffn_rest")(h, g, wg16, wu16, wd16, g_out, first)


def kernel(x, attn_norm, ffn_norm, a_w_in, a_b_f, a_w_out, kv_norm, w_kv, b_w_q, b_w_out,
           ffn_w_gate, ffn_w_up, ffn_w_down, final_norm):
    b, s, d = x.shape
    assert b == 1 and d == N_HEADS * HEAD_DIM
    depth = attn_norm.shape[0]
    n_a = a_w_in.shape[0]
    h = x.reshape(s, d)
    w_in_t = jnp.swapaxes(a_w_in, 1, 2)
    fox_order = jnp.argsort(a_b_f, axis=1).astype(jnp.int32)
    w_qkv = cast_bf16_heads(w_in_t, fox_order, 3)
    w_f = jnp.take_along_axis(w_in_t[:, 3 * d:], fox_order[:, :, None], axis=1)
    w_f = jnp.pad(w_f, ((0, 0), (0, LANES - N_HEADS), (0, 0)))
    b_f = jnp.pad(jnp.take_along_axis(a_b_f, fox_order, axis=1), ((0, 0), (0, LANES - N_HEADS)))
    natural = jnp.arange(N_HEADS, dtype=jnp.int32)
    w_kv = w_kv[None]
    kv = None
    for layer in range(depth):
        g_attn = attn_norm[layer].reshape(1, d)
        if layer < n_a:
            qkv, f_logit = norm_matmul(h, g_attn, w_qkv, layer, w_f, transposed=True,
                                       scaled_cols=d, col_scale=HEAD_DIM ** -0.5 * LOG2E)
            o = fox_attention(qkv, gate_cumsum(f_logit, b_f[layer:layer + 1]))
            h = matmul_residual(o, a_w_out, layer, h, fox_order[layer])
        else:
            j = layer - n_a
            q = norm_matmul_resident(h, g_attn, b_w_q, j, col_scale=HEAD_DIM ** -0.5 * LOG2E)
            o = sb_attention(q, kv)
            h = matmul_residual(o, b_w_out, j, h, natural)
        h = ffn(h, ffn_norm[layer].reshape(1, d), ffn_w_gate, ffn_w_up, ffn_w_down, layer,
                final_norm.reshape(1, d), norm_out=layer == depth - 1)
        if layer == n_a - 1:
            kv = norm_matmul(h, kv_norm.reshape(1, d), w_kv, 0)
    return h.reshape(b, s, d)
```

```python
import functools

import jax
import jax.numpy as jnp
from jax import lax
from jax.experimental import pallas as pl
from jax.experimental.pallas import tpu as pltpu

N_HEADS = 16
HEAD_DIM = 128
EPS = 1e-6
LOG2E = 1.4426950408889634
LANES = 128
MXU_WIDTH = 256
VMEM_LIMIT = 56 * 1024 * 1024

KEY_CHUNK = 128
QUERY_CHUNK = MXU_WIDTH
PROB_CHUNK = 32
WEIGHT_FLOOR_LOG2 = -154.0

F32 = jnp.float32
BF16 = jnp.bfloat16


def _params(*sem):
    return pltpu.CompilerParams(dimension_semantics=sem, vmem_limit_bytes=VMEM_LIMIT)


def _rms_scale(x, g):
    ms = jnp.mean(x * x, axis=-1, keepdims=True)
    return x * lax.rsqrt(ms + EPS) * g


def _cast_heads_kernel(order_ref, *refs):
    del order_ref
    *x_refs, o_ref = refs
    for u, x_ref in enumerate(x_refs):
        o_ref[u * HEAD_DIM:(u + 1) * HEAD_DIM, :] = x_ref[...].astype(o_ref.dtype)


def cast_bf16_heads(w, orders, n_sections, *, per_step=4):
    n_layers, _, c = w.shape
    n_heads = orders.shape[1]

    def source(u):
        def index(l, i, order):
            slot = i * per_step + u
            return l, (slot // n_heads) * n_heads + order[l * n_heads + slot % n_heads], 0
        return index

    return pl.pallas_call(
        _cast_heads_kernel,
        grid_spec=pltpu.PrefetchScalarGridSpec(
            num_scalar_prefetch=1, grid=(n_layers, n_sections * n_heads // per_step),
            in_specs=[pl.BlockSpec((None, HEAD_DIM, c), source(u)) for u in range(per_step)],
            out_specs=pl.BlockSpec((None, per_step * HEAD_DIM, c), lambda l, i, order: (l, i, 0))),
        out_shape=jax.ShapeDtypeStruct((n_layers, n_sections * n_heads * HEAD_DIM, c), BF16),
        compiler_params=_params("parallel", "parallel"),
        name="cast_bf16_heads")(orders.reshape(-1), *([w] * per_step))


def _scaled_tile(acc, n_scaled_tiles, col_scale):
    if n_scaled_tiles == 0:
        return acc
    return acc * jnp.where(pl.program_id(1) < n_scaled_tiles, col_scale, 1.0)


def _project(xn, w, transposed):
    if transposed:
        return lax.dot_general(xn, w, (((1,), (1,)), ((), ())), preferred_element_type=F32)
    return jnp.dot(xn, w, preferred_element_type=F32)


def _norm_matmul_kernel(x_ref, g_ref, w_ref, o_ref, xn_ref, *, n_scaled_tiles, col_scale, transposed):
    @pl.when(pl.program_id(1) == 0)
    def _():
        xn_ref[...] = _rms_scale(x_ref[...], g_ref[...]).astype(BF16)

    acc = _project(xn_ref[...], w_ref[...].astype(BF16), transposed)
    o_ref[...] = _scaled_tile(acc, n_scaled_tiles, col_scale).astype(o_ref.dtype)


def _norm_matmul_gate_kernel(x_ref, g_ref, w_ref, wf_ref, o_ref, f_ref, xn_ref,
                             *, n_scaled_tiles, col_scale, transposed):
    @pl.when(pl.program_id(1) == 0)
    def _():
        xn = _rms_scale(x_ref[...], g_ref[...]).astype(BF16)
        xn_ref[...] = xn
        f_ref[...] = _project(xn, wf_ref[...].astype(BF16), transposed)

    acc = _project(xn_ref[...], w_ref[...].astype(BF16), transposed)
    o_ref[...] = _scaled_tile(acc, n_scaled_tiles, col_scale).astype(o_ref.dtype)


def norm_matmul(x, g, w, layer, wf=None, *, transposed=False, scaled_cols=0, col_scale=1.0,
                tm=1024, tn=1024):
    s, d = x.shape
    n = w.shape[1] if transposed else w.shape[2]
    assert scaled_cols % tn == 0
    static = dict(n_scaled_tiles=scaled_cols // tn, col_scale=col_scale, transposed=transposed)
    grid = (s // tm, n // tn)
    x_spec = pl.BlockSpec((tm, d), lambda i, j: (i, 0))
    g_spec = pl.BlockSpec((1, d), lambda i, j: (0, 0))
    if transposed:
        w_spec = pl.BlockSpec((None, tn, d), lambda i, j: (layer, j, 0))
        wf_spec = pl.BlockSpec((None, LANES, d), lambda i, j: (layer, 0, 0))
    else:
        w_spec = pl.BlockSpec((None, d, tn), lambda i, j: (layer, 0, j))
        wf_spec = pl.BlockSpec((None, d, LANES), lambda i, j: (layer, 0, 0))
    o_spec = pl.BlockSpec((tm, tn), lambda i, j: (i, j))
    scratch = [pltpu.VMEM((tm, d), BF16)]
    if wf is None:
        return pl.pallas_call(
            functools.partial(_norm_matmul_kernel, **static), grid=grid,
            in_specs=[x_spec, g_spec, w_spec], out_specs=o_spec,
            out_shape=jax.ShapeDtypeStruct((s, n), BF16),
            scratch_shapes=scratch, compiler_params=_params("parallel", "arbitrary"),
            name="norm_matmul")(x, g, w)
    f_spec = pl.BlockSpec((tm, LANES), lambda i, j: (i, 0))
    return pl.pallas_call(
        functools.partial(_norm_matmul_gate_kernel, **static), grid=grid,
        in_specs=[x_spec, g_spec, w_spec, wf_spec], out_specs=[o_spec, f_spec],
        out_shape=[jax.ShapeDtypeStruct((s, n), BF16),
                   jax.ShapeDtypeStruct((s, LANES), F32)],
        scratch_shapes=scratch, compiler_params=_params("parallel", "arbitrary"),
        name="norm_matmul_gate")(x, g, w, wf)


def _norm_matmul_resident_kernel(x_ref, g_ref, w_ref, o_ref, w16_ref, *, col_scale):
    @pl.when(pl.program_id(0) == 0)
    def _():
        w16_ref[...] = w_ref[...].astype(BF16)

    xn = _rms_scale(x_ref[...], g_ref[...]).astype(BF16)
    acc = jnp.dot(xn, w16_ref[...], preferred_element_type=F32)
    o_ref[...] = (acc * col_scale).astype(o_ref.dtype)


def norm_matmul_resident(x, g, w, layer, *, col_scale, tm=512):
    s, d = x.shape
    n = w.shape[2]
    return pl.pallas_call(
        functools.partial(_norm_matmul_resident_kernel, col_scale=col_scale), grid=(s // tm,),
        in_specs=[pl.BlockSpec((tm, d), lambda i: (i, 0)),
                  pl.BlockSpec((1, d), lambda i: (0, 0)),
                  pl.BlockSpec((None, d, n), lambda i: (layer, 0, 0), pipeline_mode=pl.Buffered(1))],
        out_specs=pl.BlockSpec((tm, n), lambda i: (i, 0)),
        out_shape=jax.ShapeDtypeStruct((s, n), BF16),
        scratch_shapes=[pltpu.VMEM((d, n), BF16)],
        compiler_params=_params("arbitrary"), name="norm_matmul_resident")(x, g, w)


def _log_sigmoid(z):
    return jnp.minimum(z, 0.0) - jnp.log(1.0 + jnp.exp(-jnp.abs(z)))


def _split3(x):
    hi = x.astype(BF16)
    r = x - hi.astype(F32)
    mid = r.astype(BF16)
    lo = (r - mid.astype(F32)).astype(BF16)
    return hi, mid, lo


def _gate_cumsum_kernel(f_ref, b_ref, ccol_ref, carry_ref, *, tc):
    @pl.when(pl.program_id(0) == 0)
    def _():
        carry_ref[...] = jnp.zeros_like(carry_ref)

    log_f = _log_sigmoid(f_ref[...] + b_ref[...])
    row = lax.broadcasted_iota(jnp.int32, (tc, tc), 0)
    col = lax.broadcasted_iota(jnp.int32, (tc, tc), 1)
    tri = jnp.where(col <= row, 1.0, 0.0).astype(BF16)
    hi, mid, lo = _split3(log_f)
    c = (jnp.dot(tri, hi, preferred_element_type=F32)
         + jnp.dot(tri, mid, preferred_element_type=F32)
         + jnp.dot(tri, lo, preferred_element_type=F32)) + carry_ref[...]
    ccol_ref[...] = c
    carry_ref[...] = c[tc - 1:tc, :]


def gate_cumsum(f, b, *, tc=512):
    s = f.shape[0]
    return pl.pallas_call(
        functools.partial(_gate_cumsum_kernel, tc=tc), grid=(s // tc,),
        in_specs=[pl.BlockSpec((tc, LANES), lambda i: (i, 0)),
                  pl.BlockSpec((1, LANES), lambda i: (0, 0))],
        out_specs=pl.BlockSpec((tc, LANES), lambda i: (i, 0)),
        out_shape=jax.ShapeDtypeStruct((s, LANES), F32),
        scratch_shapes=[pltpu.VMEM((1, LANES), F32)],
        compiler_params=_params("arbitrary"), name="gate_cumsum")(f, b)


def _colmax8(x):
    return jnp.max(x.reshape(x.shape[0] // 8, 8, x.shape[1]), axis=0)


def _neg_abs(x):
    bits = lax.bitcast_convert_type(x, jnp.int32) | jnp.int32(-2 ** 31)
    return lax.bitcast_convert_type(bits, F32)


def _colsum8(x):
    return jnp.sum(x.reshape(x.shape[0] // 8, 8, x.shape[1]), axis=0)


def _fox_kernel(q_ref, k_ref, v_ref, ccol_ref, o_ref,
                vt_ref, csb_ref, kn_ref, s_ref, p_ref, acc_ref, lp_ref, *, t, hg):
    grp = pl.program_id(0)
    i = pl.program_id(1)
    n = i + 1
    heads = range(hg)

    def head_cols(hh):
        return slice(hh * HEAD_DIM, (hh + 1) * HEAD_DIM)

    @pl.when(i == 0)
    def _():
        for hh in heads:
            kn_ref[hh] = jnp.zeros((1, LANES), F32)

        def stage(c, carry):
            rows = pl.ds(pl.multiple_of(c * t, t), t)
            lane = lax.broadcasted_iota(jnp.int32, (t, LANES), 1)
            for hh in heads:
                vt_ref[hh, c] = v_ref[rows, head_cols(hh)].astype(F32).T.astype(BF16)
                cs = jnp.sum(jnp.where(lane == grp * hg + hh, ccol_ref[rows, :], 0.0),
                             axis=-1, keepdims=True)
                csb_ref[hh, rows, :] = jnp.broadcast_to(cs * LOG2E, (t, LANES))
                kf = k_ref[rows, head_cols(hh)].astype(F32)
                kn_ref[hh] = jnp.maximum(kn_ref[hh], jnp.max(jnp.sum(kf * kf, axis=-1, keepdims=True)))
            return carry
        lax.fori_loop(0, k_ref.shape[0] // t, stage, None)

    qts = [q_ref[:, head_cols(hh)].astype(F32).T.astype(BF16) for hh in heads]
    reach = [jnp.sqrt(jnp.sum(qts[hh].astype(F32) ** 2, axis=0, keepdims=True) * kn_ref[hh][:, 0:1])
             for hh in heads]
    acc_ref[...] = jnp.zeros(acc_ref.shape, F32)
    lp_ref[...] = jnp.zeros(lp_ref.shape, F32)
    diff = (lax.broadcasted_iota(jnp.int32, (KEY_CHUNK, QUERY_CHUNK), 0)
            - lax.broadcasted_iota(jnp.int32, (KEY_CHUNK, QUERY_CHUNK), 1))

    def scores(j, slot, diagonal):
        mparts = [[jnp.full((8, QUERY_CHUNK), -jnp.inf, F32) for _ in range(t // QUERY_CHUNK)]
                  for _ in heads]
        for c in range(t // KEY_CHUNK):
            k0 = pl.multiple_of(j * t + c * KEY_CHUNK, KEY_CHUNK)
            rows = slice(c * KEY_CHUNK, (c + 1) * KEY_CHUNK)
            for g in range(t // QUERY_CHUNK):
                cols = slice(g * QUERY_CHUNK, (g + 1) * QUERY_CHUNK)
                shift = g * QUERY_CHUNK - c * KEY_CHUNK
                for hh in heads:
                    if diagonal and shift <= -QUERY_CHUNK:
                        s_ref[hh, slot, rows, cols] = jnp.full((KEY_CHUNK, QUERY_CHUNK), -jnp.inf, F32)
                        continue
                    bias = csb_ref[hh, pl.ds(k0, KEY_CHUNK), :]
                    bias = jnp.concatenate([bias] * (QUERY_CHUNK // LANES), axis=1)
                    kc = k_ref[pl.ds(k0, KEY_CHUNK), head_cols(hh)]
                    s = jnp.dot(kc, qts[hh][:, cols], preferred_element_type=F32) - bias
                    if diagonal and shift < KEY_CHUNK - 1:
                        s = jnp.where(diff <= shift, s, -jnp.inf)
                    s_ref[hh, slot, rows, cols] = s
                    mparts[hh][g] = jnp.maximum(mparts[hh][g], _colmax8(s))
        return [jnp.concatenate(mparts[hh], axis=1) for hh in heads]

    def new_max(m_old, mpart):
        m_new = jnp.maximum(m_old, jnp.max(mpart, axis=0, keepdims=True))
        return m_new, jnp.exp2(m_old - m_new)

    def half(step, slot, carry, last=False):
        ms, alphas = carry
        j = i - step
        lsums = [jnp.zeros((8, t), F32) for _ in heads]
        for c in range(t // PROB_CHUNK):
            rows = slice(c * PROB_CHUNK, (c + 1) * PROB_CHUNK)
            for hh in heads:
                p = jnp.exp2(s_ref[hh, slot, rows, :] - ms[hh])
                p_ref[hh, rows, :] = p.astype(BF16)
                lsums[hh] = lsums[hh] + _colsum8(p)
        if not last:
            mparts = scores(j - 1, 1 - slot, False)
            new = [new_max(ms[hh], mparts[hh]) for hh in heads]
            carry = tuple(m for m, _ in new), tuple(a for _, a in new)
        for hh in heads:
            lp_ref[hh] = alphas[hh] * lp_ref[hh] + lsums[hh]
            acc_ref[hh] = alphas[hh] * acc_ref[hh] + jnp.dot(vt_ref[hh, j], p_ref[hh],
                                                            preferred_element_type=F32)
        return carry

    def tile_live(j, ms):
        last_key = jnp.maximum(j + 1, 1) * t - 1
        gaps = [reach[hh] - csb_ref[hh, pl.ds(last_key, 1), :][:, 0:1] - ms[hh] for hh in heads]
        return jnp.max(functools.reduce(jnp.maximum, gaps)) > WEIGHT_FLOOR_LOG2

    first = [new_max(jnp.full((1, t), -jnp.inf, F32), mp) for mp in scores(i, 0, True)]
    carry = tuple(m for m, _ in first), tuple(a for _, a in first)
    n_pairs = (n - 1) // 2

    def pair(c):
        mm, _, ms, alphas = c
        carry = half(2 * mm, 0, (ms, alphas))
        alive = tile_live(i - (2 * mm + 3), carry[0])
        return (mm + 1, alive) + half(2 * mm + 1, 1, carry)

    mm, alive, ms, alphas = lax.while_loop(
        lambda c: (c[0] < n_pairs) & c[1], pair, (jnp.int32(0), tile_live(i - 1, carry[0])) + carry)
    step = 2 * mm
    single = (step == n - 1) | jnp.logical_not(alive)

    @pl.when(single)
    def _():
        half(step, 0, (ms, alphas), last=True)

    @pl.when(jnp.logical_not(single))
    def _():
        half(step + 1, 1, half(step, 0, (ms, alphas)), last=True)

    for hh in heads:
        l = jnp.sum(lp_ref[hh], axis=0, keepdims=True)
        o_ref[:, head_cols(hh)] = (acc_ref[hh] / l).T.astype(o_ref.dtype)


def fox_attention(qkv, ccol, *, t=256, hg=4):
    s = qkv.shape[0]
    d = qkv.shape[1] // 3
    ng = d // (hg * HEAD_DIM)
    w = hg * HEAD_DIM
    return pl.pallas_call(
        functools.partial(_fox_kernel, t=t, hg=hg),
        grid=(ng, s // t),
        in_specs=[pl.BlockSpec((t, w), lambda g, i: (i, g)),
                  pl.BlockSpec((s, w), lambda g, i: (0, ng + g), pipeline_mode=pl.Buffered(1)),
                  pl.BlockSpec((s, w), lambda g, i: (0, 2 * ng + g), pipeline_mode=pl.Buffered(1)),
                  pl.BlockSpec((s, LANES), lambda g, i: (0, 0), pipeline_mode=pl.Buffered(1))],
        out_specs=pl.BlockSpec((t, w), lambda g, i: (i, g)),
        out_shape=jax.ShapeDtypeStruct((s, d), BF16),
        scratch_shapes=[pltpu.VMEM((hg, s // t, HEAD_DIM, t), BF16),
                        pltpu.VMEM((hg, s, LANES), F32),
                        pltpu.VMEM((hg, 1, LANES), F32),
                        pltpu.VMEM((hg, 2, t, t), F32),
                        pltpu.VMEM((hg, t, t), BF16),
                        pltpu.VMEM((hg, HEAD_DIM, t), F32), pltpu.VMEM((hg, 8, t), F32)],
        compiler_params=_params("arbitrary", "arbitrary"),
        name="fox_attention")(qkv, qkv, qkv, ccol)


def _sb_kernel(q_ref, k_ref, v_ref, o_ref, vt_ref, lb_ref, lm_ref, w_ref, acc_ref, *, t, hg):
    i = pl.program_id(1)
    n_sub = t // MXU_WIDTH
    heads = range(hg)

    def head_cols(hh):
        return slice(hh * HEAD_DIM, (hh + 1) * HEAD_DIM)

    @pl.when(i == 0)
    def _():
        def stage(c, carry):
            rows = pl.ds(pl.multiple_of(c * t, t), t)
            for hh in heads:
                vt_ref[hh, c] = v_ref[rows, head_cols(hh)].astype(F32).T.astype(BF16)
            return carry
        lax.fori_loop(0, k_ref.shape[0] // t, stage, None)

    qts = [q_ref[:, head_cols(hh)].astype(F32).T.astype(BF16) for hh in heads]
    acc_ref[...] = jnp.zeros(acc_ref.shape, F32)
    diff = (lax.broadcasted_iota(jnp.int32, (KEY_CHUNK, QUERY_CHUNK), 0)
            - lax.broadcasted_iota(jnp.int32, (KEY_CHUNK, QUERY_CHUNK), 1))
    later = jnp.where(lax.broadcasted_iota(jnp.int32, (MXU_WIDTH, MXU_WIDTH), 1)
                      > lax.broadcasted_iota(jnp.int32, (MXU_WIDTH, MXU_WIDTH), 0),
                      1.0, 0.0).astype(BF16)

    def scores(j, slot, diagonal):
        for c in range(t // KEY_CHUNK):
            k0 = pl.multiple_of(j * t + c * KEY_CHUNK, KEY_CHUNK)
            rows = slice(c * KEY_CHUNK, (c + 1) * KEY_CHUNK)
            for g in range(t // QUERY_CHUNK):
                cols = slice(g * QUERY_CHUNK, (g + 1) * QUERY_CHUNK)
                shift = g * QUERY_CHUNK - c * KEY_CHUNK
                for hh in heads:
                    if diagonal and shift <= -(QUERY_CHUNK - 1):
                        lb_ref[hh, slot, rows, cols] = jnp.full((KEY_CHUNK, QUERY_CHUNK), -jnp.inf, F32)
                        lm_ref[hh, slot, rows, cols] = jnp.zeros((KEY_CHUNK, QUERY_CHUNK), BF16)
                        continue
                    kc = k_ref[pl.ds(k0, KEY_CHUNK), head_cols(hh)]
                    z2 = jnp.dot(kc, qts[hh][:, cols], preferred_element_type=F32)
                    lb2 = jnp.minimum(z2, 0.0) - jnp.log(1.0 + jnp.exp2(_neg_abs(z2))) * LOG2E
                    l1m2 = lb2 - z2
                    if diagonal and shift < KEY_CHUNK:
                        visible = diff < shift
                        lb2 = jnp.where(visible, lb2, -jnp.inf)
                        l1m2 = jnp.where(visible, l1m2, 0.0)
                    lb_ref[hh, slot, rows, cols] = lb2
                    lm_ref[hh, slot, rows, cols] = l1m2.astype(BF16)

    def weights(slot, runs):
        runs = list(runs)
        for b in reversed(range(n_sub)):
            rows = slice(b * MXU_WIDTH, (b + 1) * MXU_WIDTH)
            totals = [[] for _ in heads]
            for g in range(t // QUERY_CHUNK):
                cols = slice(g * QUERY_CHUNK, (g + 1) * QUERY_CHUNK)
                for hh in heads:
                    lm = lm_ref[hh, slot, rows, cols]
                    within = jnp.dot(later, lm, preferred_element_type=F32)
                    w = jnp.exp2(lb_ref[hh, slot, rows, cols] + within + runs[hh][:, cols])
                    w_ref[hh, slot, rows, cols] = w.astype(BF16)
                    totals[hh].append(within[0:1, :] + lm[0:1, :].astype(F32))
            for hh in heads:
                runs[hh] = runs[hh] + jnp.concatenate(totals[hh], axis=1)
        return tuple(runs)

    def value_product(j, slot):
        for hh in heads:
            acc_ref[hh] += jnp.dot(vt_ref[hh, j], w_ref[hh, slot], preferred_element_type=F32)

    def live(runs):
        return jnp.max(functools.reduce(jnp.maximum, runs)) > WEIGHT_FLOOR_LOG2

    def tile(j, runs):
        scores(j, 0, False)
        runs = weights(0, runs)
        go = live(runs)
        value_product(j, 0)
        return runs, go

    start = tuple(jnp.zeros((1, t), F32) for _ in heads)

    @pl.when(i == 0)
    def _():
        scores(0, 0, True)
        weights(0, start)
        value_product(0, 0)

    @pl.when(i > 0)
    def _():
        scores(i, 0, True)
        scores(i - 1, 1, False)
        runs = weights(1, weights(0, start))
        go = live(runs)
        value_product(i, 0)
        value_product(i - 1, 1)
        lax.while_loop(lambda c: (c[0] >= 0) & c[2],
                       lambda c: (c[0] - 1,) + tile(c[0], c[1]), (i - 2, runs, go))

    for hh in heads:
        o_ref[:, head_cols(hh)] = acc_ref[hh].T.astype(o_ref.dtype)


def sb_attention(q, kv, *, t=256, hg=4):
    s, d = q.shape
    ng = d // (hg * HEAD_DIM)
    w = hg * HEAD_DIM
    return pl.pallas_call(
        functools.partial(_sb_kernel, t=t, hg=hg),
        grid=(ng, s // t),
        in_specs=[pl.BlockSpec((t, w), lambda g, i: (i, g)),
                  pl.BlockSpec((s, w), lambda g, i: (0, g)),
                  pl.BlockSpec((s, w), lambda g, i: (0, ng + g))],
        out_specs=pl.BlockSpec((t, w), lambda g, i: (i, g)),
        out_shape=jax.ShapeDtypeStruct((s, d), BF16),
        scratch_shapes=[pltpu.VMEM((hg, s // t, HEAD_DIM, t), BF16),
                        pltpu.VMEM((hg, 2, t, t), F32),
                        pltpu.VMEM((hg, 2, t, t), BF16),
                        pltpu.VMEM((hg, 2, t, t), BF16),
                        pltpu.VMEM((hg, HEAD_DIM, t), F32)],
        compiler_params=_params("arbitrary", "arbitrary"),
        name="sb_attention")(q, kv, kv)


def _matmul_residual_kernel(order_ref, a_ref, w_ref, h_ref, o_ref, w16_ref):
    @pl.when(pl.program_id(0) == 0)
    def _():
        for p in range(w_ref.shape[0] // HEAD_DIM):
            src = pl.multiple_of(order_ref[p] * HEAD_DIM, HEAD_DIM)
            w16_ref[p * HEAD_DIM:(p + 1) * HEAD_DIM, :] = w_ref[pl.ds(src, HEAD_DIM), :].astype(BF16)

    o_ref[...] = h_ref[...] + jnp.dot(a_ref[...], w16_ref[...], preferred_element_type=F32)


def matmul_residual(a, w, layer, h, order, *, tm=512):
    s, k = a.shape
    n = w.shape[2]
    return pl.pallas_call(
        _matmul_residual_kernel,
        grid_spec=pltpu.PrefetchScalarGridSpec(
            num_scalar_prefetch=1, grid=(s // tm,),
            in_specs=[pl.BlockSpec((tm, k), lambda i, order: (i, 0)),
                      pl.BlockSpec((None, k, n), lambda i, order: (layer, 0, 0),
                                   pipeline_mode=pl.Buffered(1)),
                      pl.BlockSpec((tm, n), lambda i, order: (i, 0))],
            out_specs=pl.BlockSpec((tm, n), lambda i, order: (i, 0)),
            scratch_shapes=[pltpu.VMEM((k, n), BF16)]),
        out_shape=jax.ShapeDtypeStruct((s, n), F32),
        compiler_params=_params("arbitrary"),
        name="matmul_residual")(order, a, w, h)


def _ffn_step(h_ref, g_ref, weights, gout_ref, o_ref, xn_ref, norm_out):
    @pl.when(pl.program_id(1) == 0)
    def _():
        x = h_ref[...]
        xn_ref[...] = _rms_scale(x, g_ref[...]).astype(BF16)
        o_ref[...] = x

    wg, wu, wd = weights()
    xn = xn_ref[...]
    gate = jnp.dot(xn, wg, preferred_element_type=F32)
    up = jnp.dot(xn, wu, preferred_element_type=F32)
    act = (gate * jax.nn.sigmoid(gate) * up).astype(BF16)
    o_ref[...] += jnp.dot(act, wd, preferred_element_type=F32)

    if norm_out:
        @pl.when(pl.program_id(1) == pl.num_programs(1) - 1)
        def _():
            o_ref[...] = _rms_scale(o_ref[...], gout_ref[...])


def _ffn_first_kernel(h_ref, g_ref, wg_ref, wu_ref, wd_ref, gout_ref,
                      o_ref, wg_out_ref, wu_out_ref, wd_out_ref, xn_ref, *, norm_out):
    def weights():
        wg, wu, wd = (w[...].astype(BF16) for w in (wg_ref, wu_ref, wd_ref))
        wg_out_ref[...] = wg
        wu_out_ref[...] = wu
        wd_out_ref[...] = wd
        return wg, wu, wd

    _ffn_step(h_ref, g_ref, weights, gout_ref, o_ref, xn_ref, norm_out)


def _ffn_rest_kernel(h_ref, g_ref, wg_ref, wu_ref, wd_ref, gout_ref, first_ref, o_ref, xn_ref, *, norm_out):
    del first_ref
    _ffn_step(h_ref, g_ref, lambda: (wg_ref[...], wu_ref[...], wd_ref[...]), gout_ref, o_ref, xn_ref,
              norm_out)


def ffn(h, g, wg, wu, wd, layer, g_out, *, norm_out, tm=512, tf=512, tm_first=1024, tf_first=256):
    s, d = h.shape
    dff = wg.shape[2]
    vec = pl.BlockSpec((1, d), lambda i, f: (0, 0))
    skip = tm_first // tm

    def w_specs(lyr, t_f):
        return [pl.BlockSpec((None, d, t_f), lambda i, f: (lyr, 0, f)),
                pl.BlockSpec((None, d, t_f), lambda i, f: (lyr, 0, f)),
                pl.BlockSpec((None, t_f, d), lambda i, f: (lyr, f, 0))]

    first, wg16, wu16, wd16 = pl.pallas_call(
        functools.partial(_ffn_first_kernel, norm_out=norm_out), grid=(1, dff // tf_first),
        in_specs=[pl.BlockSpec((tm_first, d), lambda i, f: (0, 0), pipeline_mode=pl.Buffered(1)), vec,
                  *w_specs(layer, tf_first), vec],
        out_specs=[pl.BlockSpec((tm_first, d), lambda i, f: (0, 0)), *w_specs(0, tf_first)],
        out_shape=[jax.ShapeDtypeStruct((s, d), F32),
                   jax.ShapeDtypeStruct((1, d, dff), BF16), jax.ShapeDtypeStruct((1, d, dff), BF16),
                   jax.ShapeDtypeStruct((1, dff, d), BF16)],
        scratch_shapes=[pltpu.VMEM((tm_first, d), BF16)],
        compiler_params=_params("arbitrary", "arbitrary"),
        name="ffn_first")(h, g, wg, wu, wd, g_out)
    return pl.pallas_call(
        functools.partial(_ffn_rest_kernel, norm_out=norm_out), grid=((s - tm_first) // tm, dff // tf),
        in_specs=[pl.BlockSpec((tm, d), lambda i, f: (i + skip, 0)), vec, *w_specs(0, tf), vec,
                  pl.BlockSpec(memory_space=pl.ANY)],
        out_specs=pl.BlockSpec((tm, d), lambda i, f: (i + skip, 0)),
        out_shape=jax.ShapeDtypeStruct((s, d), F32),
        input_output_aliases={6: 0},
        scratch_shapes=[pltpu.VMEM((tm, d), BF16)],
        compiler_params=_params("parallel", "arbitrary"),
        name="ffn_rest")(h, g, wg16, wu16, wd16, g_out, first)


def kernel(x, attn_norm, ffn_norm, a_w_in, a_b_f, a_w_out, kv_norm, w_kv, b_w_q, b_w_out,
           ffn_w_gate, ffn_w_up, ffn_w_down, final_norm):
    b, s, d = x.shape
    assert b == 1 and d == N_HEADS * HEAD_DIM
    depth = attn_norm.shape[0]
    n_a = a_w_in.shape[0]
    h = x.reshape(s, d)
    w_in_t = jnp.swapaxes(a_w_in, 1, 2)
    fox_order = jnp.argsort(a_b_f, axis=1).astype(jnp.int32)
    w_qkv = cast_bf16_heads(w_in_t, fox_order, 3)
    w_f = jnp.take_along_axis(w_in_t[:, 3 * d:], fox_order[:, :, None], axis=1)
    w_f = jnp.pad(w_f, ((0, 0), (0, LANES - N_HEADS), (0, 0)))
    b_f = jnp.pad(jnp.take_along_axis(a_b_f, fox_order, axis=1), ((0, 0), (0, LANES - N_HEADS)))
    natural = jnp.arange(N_HEADS, dtype=jnp.int32)
    w_kv = w_kv[None]
    kv = None
    for layer in range(depth):
        g_attn = attn_norm[layer].reshape(1, d)
        if layer < n_a:
            qkv, f_logit = norm_matmul(h, g_attn, w_qkv, layer, w_f, transposed=True, tn=d,
                                       scaled_cols=d, col_scale=HEAD_DIM ** -0.5 * LOG2E)
            o = fox_attention(qkv, gate_cumsum(f_logit, b_f[layer:layer + 1]))
            h = matmul_residual(o, a_w_out, layer, h, fox_order[layer])
        else:
            j = layer - n_a
            q = norm_matmul_resident(h, g_attn, b_w_q, j, col_scale=HEAD_DIM ** -0.5 * LOG2E)
            o = sb_attention(q, kv)
            h = matmul_residual(o, b_w_out, j, h, natural)
        h = ffn(h, ffn_norm[layer].reshape(1, d), ffn_w_gate, ffn_w_up, ffn_w_down, layer,
                final_norm.reshape(1, d), norm_out=layer == depth - 1)
        if layer == n_a - 1:
            kv = norm_matmul(h, kv_norm.reshape(1, d), w_kv, 0)
    return h.reshape(b, s, d)
```

```python
import functools

import jax
import jax.numpy as jnp
from jax import lax
from jax.experimental import pallas as pl
from jax.experimental.pallas import tpu as pltpu

N_HEADS = 16
HEAD_DIM = 128
EPS = 1e-6
LOG2E = 1.4426950408889634
LANES = 128
MXU_WIDTH = 256
VMEM_LIMIT = 56 * 1024 * 1024

KEY_CHUNK = 128
QUERY_CHUNK = MXU_WIDTH
PROB_CHUNK = 32
WEIGHT_FLOOR_LOG2 = -154.0

F32 = jnp.float32
BF16 = jnp.bfloat16


def _params(*sem):
    return pltpu.CompilerParams(dimension_semantics=sem, vmem_limit_bytes=VMEM_LIMIT)


def _rms_scale(x, g):
    ms = jnp.mean(x * x, axis=-1, keepdims=True)
    return x * lax.rsqrt(ms + EPS) * g


def _cast_heads_kernel(order_ref, *refs):
    del order_ref
    *x_refs, o_ref = refs
    for u, x_ref in enumerate(x_refs):
        o_ref[u * HEAD_DIM:(u + 1) * HEAD_DIM, :] = x_ref[...].astype(o_ref.dtype)


def cast_bf16_heads(w, orders, n_sections, *, per_step=4):
    n_layers, _, c = w.shape
    n_heads = orders.shape[1]

    def source(u):
        def index(l, i, order):
            slot = i * per_step + u
            return l, (slot // n_heads) * n_heads + order[l * n_heads + slot % n_heads], 0
        return index

    return pl.pallas_call(
        _cast_heads_kernel,
        grid_spec=pltpu.PrefetchScalarGridSpec(
            num_scalar_prefetch=1, grid=(n_layers, n_sections * n_heads // per_step),
            in_specs=[pl.BlockSpec((None, HEAD_DIM, c), source(u)) for u in range(per_step)],
            out_specs=pl.BlockSpec((None, per_step * HEAD_DIM, c), lambda l, i, order: (l, i, 0))),
        out_shape=jax.ShapeDtypeStruct((n_layers, n_sections * n_heads * HEAD_DIM, c), BF16),
        compiler_params=_params("parallel", "parallel"),
        name="cast_bf16_heads")(orders.reshape(-1), *([w] * per_step))


def _scaled_tile(acc, n_scaled_tiles, col_scale):
    if n_scaled_tiles == 0:
        return acc
    return acc * jnp.where(pl.program_id(1) < n_scaled_tiles, col_scale, 1.0)


def _project(xn, w, transposed):
    if transposed:
        return lax.dot_general(xn, w, (((1,), (1,)), ((), ())), preferred_element_type=F32)
    return jnp.dot(xn, w, preferred_element_type=F32)


def _norm_matmul_kernel(x_ref, g_ref, w_ref, o_ref, xn_ref, *, n_scaled_tiles, col_scale, transposed):
    @pl.when(pl.program_id(1) == 0)
    def _():
        xn_ref[...] = _rms_scale(x_ref[...], g_ref[...]).astype(BF16)

    acc = _project(xn_ref[...], w_ref[...].astype(BF16), transposed)
    o_ref[...] = _scaled_tile(acc, n_scaled_tiles, col_scale).astype(o_ref.dtype)


def _norm_matmul_gate_kernel(x_ref, g_ref, w_ref, wf_ref, o_ref, f_ref, xn_ref,
                             *, n_scaled_tiles, col_scale, transposed):
    @pl.when(pl.program_id(1) == 0)
    def _():
        xn = _rms_scale(x_ref[...], g_ref[...]).astype(BF16)
        xn_ref[...] = xn
        f_ref[...] = _project(xn, wf_ref[...].astype(BF16), transposed)

    acc = _project(xn_ref[...], w_ref[...].astype(BF16), transposed)
    o_ref[...] = _scaled_tile(acc, n_scaled_tiles, col_scale).astype(o_ref.dtype)


def norm_matmul(x, g, w, layer, wf=None, *, transposed=False, scaled_cols=0, col_scale=1.0,
                tm=1024, tn=1024):
    s, d = x.shape
    n = w.shape[1] if transposed else w.shape[2]
    assert scaled_cols % tn == 0
    static = dict(n_scaled_tiles=scaled_cols // tn, col_scale=col_scale, transposed=transposed)
    grid = (s // tm, n // tn)
    x_spec = pl.BlockSpec((tm, d), lambda i, j: (i, 0))
    g_spec = pl.BlockSpec((1, d), lambda i, j: (0, 0))
    if transposed:
        w_spec = pl.BlockSpec((None, tn, d), lambda i, j: (layer, j, 0))
        wf_spec = pl.BlockSpec((None, LANES, d), lambda i, j: (layer, 0, 0))
    else:
        w_spec = pl.BlockSpec((None, d, tn), lambda i, j: (layer, 0, j))
        wf_spec = pl.BlockSpec((None, d, LANES), lambda i, j: (layer, 0, 0))
    o_spec = pl.BlockSpec((tm, tn), lambda i, j: (i, j))
    scratch = [pltpu.VMEM((tm, d), BF16)]
    if wf is None:
        return pl.pallas_call(
            functools.partial(_norm_matmul_kernel, **static), grid=grid,
            in_specs=[x_spec, g_spec, w_spec], out_specs=o_spec,
            out_shape=jax.ShapeDtypeStruct((s, n), BF16),
            scratch_shapes=scratch, compiler_params=_params("parallel", "arbitrary"),
            name="norm_matmul")(x, g, w)
    f_spec = pl.BlockSpec((tm, LANES), lambda i, j: (i, 0))
    return pl.pallas_call(
        functools.partial(_norm_matmul_gate_kernel, **static), grid=grid,
        in_specs=[x_spec, g_spec, w_spec, wf_spec], out_specs=[o_spec, f_spec],
        out_shape=[jax.ShapeDtypeStruct((s, n), BF16),
                   jax.ShapeDtypeStruct((s, LANES), F32)],
        scratch_shapes=scratch, compiler_params=_params("parallel", "arbitrary"),
        name="norm_matmul_gate")(x, g, w, wf)


def _norm_matmul_resident_kernel(x_ref, g_ref, w_ref, o_ref, w16_ref, *, col_scale):
    @pl.when(pl.program_id(0) == 0)
    def _():
        w16_ref[...] = w_ref[...].astype(BF16)

    xn = _rms_scale(x_ref[...], g_ref[...]).astype(BF16)
    acc = jnp.dot(xn, w16_ref[...], preferred_element_type=F32)
    o_ref[...] = (acc * col_scale).astype(o_ref.dtype)


def norm_matmul_resident(x, g, w, layer, *, col_scale, tm=512):
    s, d = x.shape
    n = w.shape[2]
    return pl.pallas_call(
        functools.partial(_norm_matmul_resident_kernel, col_scale=col_scale), grid=(s // tm,),
        in_specs=[pl.BlockSpec((tm, d), lambda i: (i, 0)),
                  pl.BlockSpec((1, d), lambda i: (0, 0)),
                  pl.BlockSpec((None, d, n), lambda i: (layer, 0, 0), pipeline_mode=pl.Buffered(1))],
        out_specs=pl.BlockSpec((tm, n), lambda i: (i, 0)),
        out_shape=jax.ShapeDtypeStruct((s, n), BF16),
        scratch_shapes=[pltpu.VMEM((d, n), BF16)],
        compiler_params=_params("arbitrary"), name="norm_matmul_resident")(x, g, w)


def _log_sigmoid(z):
    return jnp.minimum(z, 0.0) - jnp.log(1.0 + jnp.exp(-jnp.abs(z)))


def _split3(x):
    hi = x.astype(BF16)
    r = x - hi.astype(F32)
    mid = r.astype(BF16)
    lo = (r - mid.astype(F32)).astype(BF16)
    return hi, mid, lo


def _gate_cumsum_kernel(f_ref, b_ref, ccol_ref, carry_ref, *, tc):
    @pl.when(pl.program_id(0) == 0)
    def _():
        carry_ref[...] = jnp.zeros_like(carry_ref)

    log_f = _log_sigmoid(f_ref[...] + b_ref[...])
    row = lax.broadcasted_iota(jnp.int32, (tc, tc), 0)
    col = lax.broadcasted_iota(jnp.int32, (tc, tc), 1)
    tri = jnp.where(col <= row, 1.0, 0.0).astype(BF16)
    hi, mid, lo = _split3(log_f)
    c = (jnp.dot(tri, hi, preferred_element_type=F32)
         + jnp.dot(tri, mid, preferred_element_type=F32)
         + jnp.dot(tri, lo, preferred_element_type=F32)) + carry_ref[...]
    ccol_ref[...] = c
    carry_ref[...] = c[tc - 1:tc, :]


def gate_cumsum(f, b, *, tc=512):
    s = f.shape[0]
    return pl.pallas_call(
        functools.partial(_gate_cumsum_kernel, tc=tc), grid=(s // tc,),
        in_specs=[pl.BlockSpec((tc, LANES), lambda i: (i, 0)),
                  pl.BlockSpec((1, LANES), lambda i: (0, 0))],
        out_specs=pl.BlockSpec((tc, LANES), lambda i: (i, 0)),
        out_shape=jax.ShapeDtypeStruct((s, LANES), F32),
        scratch_shapes=[pltpu.VMEM((1, LANES), F32)],
        compiler_params=_params("arbitrary"), name="gate_cumsum")(f, b)


def _colmax8(x):
    return jnp.max(x.reshape(x.shape[0] // 8, 8, x.shape[1]), axis=0)


def _neg_abs(x):
    bits = lax.bitcast_convert_type(x, jnp.int32) | jnp.int32(-2 ** 31)
    return lax.bitcast_convert_type(bits, F32)


def _colsum8(x):
    return jnp.sum(x.reshape(x.shape[0] // 8, 8, x.shape[1]), axis=0)


def _fox_kernel(q_ref, k_ref, v_ref, ccol_ref, o_ref,
                vt_ref, csb_ref, kn_ref, s_ref, p_ref, acc_ref, lp_ref, qt_ref, *, t, hg):
    grp = pl.program_id(0)
    i = pl.program_id(1)
    n = i + 1
    heads = range(hg)

    def head_cols(hh):
        return slice(hh * HEAD_DIM, (hh + 1) * HEAD_DIM)

    @pl.when(i == 0)
    def _():
        for hh in heads:
            kn_ref[hh] = jnp.zeros((1, LANES), F32)

        def stage(c, carry):
            rows = pl.ds(pl.multiple_of(c * t, t), t)
            lane = lax.broadcasted_iota(jnp.int32, (t, LANES), 1)
            for hh in heads:
                vt_ref[hh, c] = v_ref[rows, head_cols(hh)].astype(F32).T.astype(BF16)
                cs = jnp.sum(jnp.where(lane == grp * hg + hh, ccol_ref[rows, :], 0.0),
                             axis=-1, keepdims=True)
                csb_ref[hh, rows, :] = jnp.broadcast_to(cs * LOG2E, (t, LANES))
                kf = k_ref[rows, head_cols(hh)].astype(F32)
                kn_ref[hh] = jnp.maximum(kn_ref[hh], jnp.max(jnp.sum(kf * kf, axis=-1, keepdims=True)))
            return carry
        lax.fori_loop(0, k_ref.shape[0] // t, stage, None)

    for hh in heads:
        qt_ref[hh] = q_ref[:, head_cols(hh)].astype(F32).T.astype(BF16)
    reach = [jnp.sqrt(jnp.sum(qt_ref[hh].astype(F32) ** 2, axis=0, keepdims=True) * kn_ref[hh][:, 0:1])
             for hh in heads]
    acc_ref[...] = jnp.zeros(acc_ref.shape, F32)
    lp_ref[...] = jnp.zeros(lp_ref.shape, F32)
    diff = (lax.broadcasted_iota(jnp.int32, (KEY_CHUNK, QUERY_CHUNK), 0)
            - lax.broadcasted_iota(jnp.int32, (KEY_CHUNK, QUERY_CHUNK), 1))

    def scores(j, slot, diagonal):
        mparts = [[jnp.full((8, QUERY_CHUNK), -jnp.inf, F32) for _ in range(t // QUERY_CHUNK)]
                  for _ in heads]
        for c in range(t // KEY_CHUNK):
            k0 = pl.multiple_of(j * t + c * KEY_CHUNK, KEY_CHUNK)
            rows = slice(c * KEY_CHUNK, (c + 1) * KEY_CHUNK)
            for g in range(t // QUERY_CHUNK):
                cols = slice(g * QUERY_CHUNK, (g + 1) * QUERY_CHUNK)
                shift = g * QUERY_CHUNK - c * KEY_CHUNK
                for hh in heads:
                    if diagonal and shift <= -QUERY_CHUNK:
                        s_ref[hh, slot, rows, cols] = jnp.full((KEY_CHUNK, QUERY_CHUNK), -jnp.inf, F32)
                        continue
                    bias = csb_ref[hh, pl.ds(k0, KEY_CHUNK), :]
                    bias = jnp.concatenate([bias] * (QUERY_CHUNK // LANES), axis=1)
                    kc = k_ref[pl.ds(k0, KEY_CHUNK), head_cols(hh)]
                    s = jnp.dot(kc, qt_ref[hh, :, cols], preferred_element_type=F32) - bias
                    if diagonal and shift < KEY_CHUNK - 1:
                        s = jnp.where(diff <= shift, s, -jnp.inf)
                    s_ref[hh, slot, rows, cols] = s
                    mparts[hh][g] = jnp.maximum(mparts[hh][g], _colmax8(s))
        return [jnp.concatenate(mparts[hh], axis=1) for hh in heads]

    def new_max(m_old, mpart):
        m_new = jnp.maximum(m_old, jnp.max(mpart, axis=0, keepdims=True))
        return m_new, jnp.exp2(m_old - m_new)

    def half(step, slot, carry, last=False):
        ms, alphas = carry
        j = i - step
        lsums = [jnp.zeros((8, t), F32) for _ in heads]
        for c in range(t // PROB_CHUNK):
            rows = slice(c * PROB_CHUNK, (c + 1) * PROB_CHUNK)
            for hh in heads:
                p = jnp.exp2(s_ref[hh, slot, rows, :] - ms[hh])
                p_ref[hh, rows, :] = p.astype(BF16)
                lsums[hh] = lsums[hh] + _colsum8(p)
        if not last:
            mparts = scores(j - 1, 1 - slot, False)
            new = [new_max(ms[hh], mparts[hh]) for hh in heads]
            carry = tuple(m for m, _ in new), tuple(a for _, a in new)
        for hh in heads:
            lp_ref[hh] = alphas[hh] * lp_ref[hh] + lsums[hh]
            acc_ref[hh] = alphas[hh] * acc_ref[hh] + jnp.dot(vt_ref[hh, j], p_ref[hh],
                                                            preferred_element_type=F32)
        return carry

    def tile_live(j, ms):
        last_key = jnp.maximum(j + 1, 1) * t - 1
        gaps = [reach[hh] - csb_ref[hh, pl.ds(last_key, 1), :][:, 0:1] - ms[hh] for hh in heads]
        return jnp.max(functools.reduce(jnp.maximum, gaps)) > WEIGHT_FLOOR_LOG2

    first = [new_max(jnp.full((1, t), -jnp.inf, F32), mp) for mp in scores(i, 0, True)]
    carry = tuple(m for m, _ in first), tuple(a for _, a in first)
    n_pairs = (n - 1) // 2

    def pair(c):
        mm, _, ms, alphas = c
        carry = half(2 * mm, 0, (ms, alphas))
        alive = tile_live(i - (2 * mm + 3), carry[0])
        return (mm + 1, alive) + half(2 * mm + 1, 1, carry)

    mm, alive, ms, alphas = lax.while_loop(
        lambda c: (c[0] < n_pairs) & c[1], pair, (jnp.int32(0), tile_live(i - 1, carry[0])) + carry)
    step = 2 * mm
    single = (step == n - 1) | jnp.logical_not(alive)

    @pl.when(single)
    def _():
        half(step, 0, (ms, alphas), last=True)

    @pl.when(jnp.logical_not(single))
    def _():
        half(step + 1, 1, half(step, 0, (ms, alphas)), last=True)

    for hh in heads:
        l = jnp.sum(lp_ref[hh], axis=0, keepdims=True)
        o_ref[:, head_cols(hh)] = (acc_ref[hh] / l).T.astype(o_ref.dtype)


def fox_attention(qkv, ccol, *, t=256, hg=4):
    s = qkv.shape[0]
    d = qkv.shape[1] // 3
    ng = d // (hg * HEAD_DIM)
    w = hg * HEAD_DIM
    return pl.pallas_call(
        functools.partial(_fox_kernel, t=t, hg=hg),
        grid=(ng, s // t),
        in_specs=[pl.BlockSpec((t, w), lambda g, i: (i, g)),
                  pl.BlockSpec((s, w), lambda g, i: (0, ng + g), pipeline_mode=pl.Buffered(1)),
                  pl.BlockSpec((s, w), lambda g, i: (0, 2 * ng + g), pipeline_mode=pl.Buffered(1)),
                  pl.BlockSpec((s, LANES), lambda g, i: (0, 0), pipeline_mode=pl.Buffered(1))],
        out_specs=pl.BlockSpec((t, w), lambda g, i: (i, g)),
        out_shape=jax.ShapeDtypeStruct((s, d), BF16),
        scratch_shapes=[pltpu.VMEM((hg, s // t, HEAD_DIM, t), BF16),
                        pltpu.VMEM((hg, s, LANES), F32),
                        pltpu.VMEM((hg, 1, LANES), F32),
                        pltpu.VMEM((hg, 2, t, t), F32),
                        pltpu.VMEM((hg, t, t), BF16),
                        pltpu.VMEM((hg, HEAD_DIM, t), F32), pltpu.VMEM((hg, 8, t), F32),
                        pltpu.VMEM((hg, HEAD_DIM, t), BF16)],
        compiler_params=_params("arbitrary", "arbitrary"),
        name="fox_attention")(qkv, qkv, qkv, ccol)


def _sb_kernel(q_ref, k_ref, v_ref, o_ref, vt_ref, lb_ref, lm_ref, w_ref, acc_ref, qt_ref, *, t, hg):
    i = pl.program_id(1)
    n_sub = t // MXU_WIDTH
    heads = range(hg)

    def head_cols(hh):
        return slice(hh * HEAD_DIM, (hh + 1) * HEAD_DIM)

    @pl.when(i == 0)
    def _():
        def stage(c, carry):
            rows = pl.ds(pl.multiple_of(c * t, t), t)
            for hh in heads:
                vt_ref[hh, c] = v_ref[rows, head_cols(hh)].astype(F32).T.astype(BF16)
            return carry
        lax.fori_loop(0, k_ref.shape[0] // t, stage, None)

    for hh in heads:
        qt_ref[hh] = q_ref[:, head_cols(hh)].astype(F32).T.astype(BF16)
    acc_ref[...] = jnp.zeros(acc_ref.shape, F32)
    diff = (lax.broadcasted_iota(jnp.int32, (KEY_CHUNK, QUERY_CHUNK), 0)
            - lax.broadcasted_iota(jnp.int32, (KEY_CHUNK, QUERY_CHUNK), 1))
    later = jnp.where(lax.broadcasted_iota(jnp.int32, (MXU_WIDTH, MXU_WIDTH), 1)
                      > lax.broadcasted_iota(jnp.int32, (MXU_WIDTH, MXU_WIDTH), 0),
                      1.0, 0.0).astype(BF16)

    def scores(j, slot, diagonal):
        for c in range(t // KEY_CHUNK):
            k0 = pl.multiple_of(j * t + c * KEY_CHUNK, KEY_CHUNK)
            rows = slice(c * KEY_CHUNK, (c + 1) * KEY_CHUNK)
            for g in range(t // QUERY_CHUNK):
                cols = slice(g * QUERY_CHUNK, (g + 1) * QUERY_CHUNK)
                shift = g * QUERY_CHUNK - c * KEY_CHUNK
                for hh in heads:
                    if diagonal and shift <= -(QUERY_CHUNK - 1):
                        lb_ref[hh, slot, rows, cols] = jnp.full((KEY_CHUNK, QUERY_CHUNK), -jnp.inf, F32)
                        lm_ref[hh, slot, rows, cols] = jnp.zeros((KEY_CHUNK, QUERY_CHUNK), BF16)
                        continue
                    kc = k_ref[pl.ds(k0, KEY_CHUNK), head_cols(hh)]
                    z2 = jnp.dot(kc, qt_ref[hh, :, cols], preferred_element_type=F32)
                    lb2 = jnp.minimum(z2, 0.0) - jnp.log(1.0 + jnp.exp2(_neg_abs(z2))) * LOG2E
                    l1m2 = lb2 - z2
                    if diagonal and shift < KEY_CHUNK:
                        visible = diff < shift
                        lb2 = jnp.where(visible, lb2, -jnp.inf)
                        l1m2 = jnp.where(visible, l1m2, 0.0)
                    lb_ref[hh, slot, rows, cols] = lb2
                    lm_ref[hh, slot, rows, cols] = l1m2.astype(BF16)

    def weights(slot, runs):
        runs = list(runs)
        for b in reversed(range(n_sub)):
            rows = slice(b * MXU_WIDTH, (b + 1) * MXU_WIDTH)
            totals = [[] for _ in heads]
            for g in range(t // QUERY_CHUNK):
                cols = slice(g * QUERY_CHUNK, (g + 1) * QUERY_CHUNK)
                for hh in heads:
                    lm = lm_ref[hh, slot, rows, cols]
                    within = jnp.dot(later, lm, preferred_element_type=F32)
                    w = jnp.exp2(lb_ref[hh, slot, rows, cols] + within + runs[hh][:, cols])
                    w_ref[hh, slot, rows, cols] = w.astype(BF16)
                    totals[hh].append(within[0:1, :] + lm[0:1, :].astype(F32))
            for hh in heads:
                runs[hh] = runs[hh] + jnp.concatenate(totals[hh], axis=1)
        return tuple(runs)

    def value_product(j, slot):
        for hh in heads:
            acc_ref[hh] += jnp.dot(vt_ref[hh, j], w_ref[hh, slot], preferred_element_type=F32)

    def live(runs):
        return jnp.max(functools.reduce(jnp.maximum, runs)) > WEIGHT_FLOOR_LOG2

    def tile(j, runs):
        scores(j, 0, False)
        runs = weights(0, runs)
        go = live(runs)
        value_product(j, 0)
        return runs, go

    start = tuple(jnp.zeros((1, t), F32) for _ in heads)

    @pl.when(i == 0)
    def _():
        scores(0, 0, True)
        weights(0, start)
        value_product(0, 0)

    @pl.when(i > 0)
    def _():
        scores(i, 0, True)
        scores(i - 1, 1, False)
        runs = weights(1, weights(0, start))
        go = live(runs)
        value_product(i, 0)
        value_product(i - 1, 1)
        lax.while_loop(lambda c: (c[0] >= 0) & c[2],
                       lambda c: (c[0] - 1,) + tile(c[0], c[1]), (i - 2, runs, go))

    for hh in heads:
        o_ref[:, head_cols(hh)] = acc_ref[hh].T.astype(o_ref.dtype)


def sb_attention(q, kv, *, t=256, hg=4):
    s, d = q.shape
    ng = d // (hg * HEAD_DIM)
    w = hg * HEAD_DIM
    return pl.pallas_call(
        functools.partial(_sb_kernel, t=t, hg=hg),
        grid=(ng, s // t),
        in_specs=[pl.BlockSpec((t, w), lambda g, i: (i, g)),
                  pl.BlockSpec((s, w), lambda g, i: (0, g)),
                  pl.BlockSpec((s, w), lambda g, i: (0, ng + g))],
        out_specs=pl.BlockSpec((t, w), lambda g, i: (i, g)),
        out_shape=jax.ShapeDtypeStruct((s, d), BF16),
        scratch_shapes=[pltpu.VMEM((hg, s // t, HEAD_DIM, t), BF16),
                        pltpu.VMEM((hg, 2, t, t), F32),
                        pltpu.VMEM((hg, 2, t, t), BF16),
                        pltpu.VMEM((hg, 2, t, t), BF16),
                        pltpu.VMEM((hg, HEAD_DIM, t), F32),
                        pltpu.VMEM((hg, HEAD_DIM, t), BF16)],
        compiler_params=_params("arbitrary", "arbitrary"),
        name="sb_attention")(q, kv, kv)


def _matmul_residual_kernel(order_ref, a_ref, w_ref, h_ref, o_ref, w16_ref):
    @pl.when(pl.program_id(0) == 0)
    def _():
        for p in range(w_ref.shape[0] // HEAD_DIM):
            src = pl.multiple_of(order_ref[p] * HEAD_DIM, HEAD_DIM)
            w16_ref[p * HEAD_DIM:(p + 1) * HEAD_DIM, :] = w_ref[pl.ds(src, HEAD_DIM), :].astype(BF16)

    o_ref[...] = h_ref[...] + jnp.dot(a_ref[...], w16_ref[...], preferred_element_type=F32)


def matmul_residual(a, w, layer, h, order, *, tm=512):
    s, k = a.shape
    n = w.shape[2]
    return pl.pallas_call(
        _matmul_residual_kernel,
        grid_spec=pltpu.PrefetchScalarGridSpec(
            num_scalar_prefetch=1, grid=(s // tm,),
            in_specs=[pl.BlockSpec((tm, k), lambda i, order: (i, 0)),
                      pl.BlockSpec((None, k, n), lambda i, order: (layer, 0, 0),
                                   pipeline_mode=pl.Buffered(1)),
                      pl.BlockSpec((tm, n), lambda i, order: (i, 0))],
            out_specs=pl.BlockSpec((tm, n), lambda i, order: (i, 0)),
            scratch_shapes=[pltpu.VMEM((k, n), BF16)]),
        out_shape=jax.ShapeDtypeStruct((s, n), F32),
        compiler_params=_params("arbitrary"),
        name="matmul_residual")(order, a, w, h)


def _ffn_step(h_ref, g_ref, weights, gout_ref, o_ref, xn_ref, norm_out):
    @pl.when(pl.program_id(1) == 0)
    def _():
        x = h_ref[...]
        xn_ref[...] = _rms_scale(x, g_ref[...]).astype(BF16)
        o_ref[...] = x

    wg, wu, wd = weights()
    xn = xn_ref[...]
    gate = jnp.dot(xn, wg, preferred_element_type=F32)
    up = jnp.dot(xn, wu, preferred_element_type=F32)
    act = (gate * jax.nn.sigmoid(gate) * up).astype(BF16)
    o_ref[...] += jnp.dot(act, wd, preferred_element_type=F32)

    if norm_out:
        @pl.when(pl.program_id(1) == pl.num_programs(1) - 1)
        def _():
            o_ref[...] = _rms_scale(o_ref[...], gout_ref[...])


def _ffn_first_kernel(h_ref, g_ref, wg_ref, wu_ref, wd_ref, gout_ref,
                      o_ref, wg_out_ref, wu_out_ref, wd_out_ref, xn_ref, *, norm_out):
    def weights():
        wg, wu, wd = (w[...].astype(BF16) for w in (wg_ref, wu_ref, wd_ref))
        wg_out_ref[...] = wg
        wu_out_ref[...] = wu
        wd_out_ref[...] = wd
        return wg, wu, wd

    _ffn_step(h_ref, g_ref, weights, gout_ref, o_ref, xn_ref, norm_out)


def _ffn_rest_kernel(h_ref, g_ref, wg_ref, wu_ref, wd_ref, gout_ref, first_ref, o_ref, xn_ref, *, norm_out):
    del first_ref
    _ffn_step(h_ref, g_ref, lambda: (wg_ref[...], wu_ref[...], wd_ref[...]), gout_ref, o_ref, xn_ref,
              norm_out)


def ffn(h, g, wg, wu, wd, layer, g_out, *, norm_out, tm=512, tf=512, tm_first=1024, tf_first=256):
    s, d = h.shape
    dff = wg.shape[2]
    vec = pl.BlockSpec((1, d), lambda i, f: (0, 0))
    skip = tm_first // tm

    def w_specs(lyr, t_f):
        return [pl.BlockSpec((None, d, t_f), lambda i, f: (lyr, 0, f)),
                pl.BlockSpec((None, d, t_f), lambda i, f: (lyr, 0, f)),
                pl.BlockSpec((None, t_f, d), lambda i, f: (lyr, f, 0))]

    first, wg16, wu16, wd16 = pl.pallas_call(
        functools.partial(_ffn_first_kernel, norm_out=norm_out), grid=(1, dff // tf_first),
        in_specs=[pl.BlockSpec((tm_first, d), lambda i, f: (0, 0), pipeline_mode=pl.Buffered(1)), vec,
                  *w_specs(layer, tf_first), vec],
        out_specs=[pl.BlockSpec((tm_first, d), lambda i, f: (0, 0)), *w_specs(0, tf_first)],
        out_shape=[jax.ShapeDtypeStruct((s, d), F32),
                   jax.ShapeDtypeStruct((1, d, dff), BF16), jax.ShapeDtypeStruct((1, d, dff), BF16),
                   jax.ShapeDtypeStruct((1, dff, d), BF16)],
        scratch_shapes=[pltpu.VMEM((tm_first, d), BF16)],
        compiler_params=_params("arbitrary", "arbitrary"),
        name="ffn_first")(h, g, wg, wu, wd, g_out)
    return pl.pallas_call(
        functools.partial(_ffn_rest_kernel, norm_out=norm_out), grid=((s - tm_first) // tm, dff // tf),
        in_specs=[pl.BlockSpec((tm, d), lambda i, f: (i + skip, 0)), vec, *w_specs(0, tf), vec,
                  pl.BlockSpec(memory_space=pl.ANY)],
        out_specs=pl.BlockSpec((tm, d), lambda i, f: (i + skip, 0)),
        out_shape=jax.ShapeDtypeStruct((s, d), F32),
        input_output_aliases={6: 0},
        scratch_shapes=[pltpu.VMEM((tm, d), BF16)],
        compiler_params=_params("parallel", "arbitrary"),
        name="ffn_rest")(h, g, wg16, wu16, wd16, g_out, first)


def kernel(x, attn_norm, ffn_norm, a_w_in, a_b_f, a_w_out, kv_norm, w_kv, b_w_q, b_w_out,
           ffn_w_gate, ffn_w_up, ffn_w_down, final_norm):
    b, s, d = x.shape
    assert b == 1 and d == N_HEADS * HEAD_DIM
    depth = attn_norm.shape[0]
    n_a = a_w_in.shape[0]
    h = x.reshape(s, d)
    w_in_t = jnp.swapaxes(a_w_in, 1, 2)
    fox_order = jnp.argsort(a_b_f, axis=1).astype(jnp.int32)
    w_qkv = cast_bf16_heads(w_in_t, fox_order, 3)
    w_f = jnp.take_along_axis(w_in_t[:, 3 * d:], fox_order[:, :, None], axis=1)
    w_f = jnp.pad(w_f, ((0, 0), (0, LANES - N_HEADS), (0, 0)))
    b_f = jnp.pad(jnp.take_along_axis(a_b_f, fox_order, axis=1), ((0, 0), (0, LANES - N_HEADS)))
    natural = jnp.arange(N_HEADS, dtype=jnp.int32)
    w_kv = w_kv[None]
    kv = None
    for layer in range(depth):
        g_attn = attn_norm[layer].reshape(1, d)
        if layer < n_a:
            qkv, f_logit = norm_matmul(h, g_attn, w_qkv, layer, w_f, transposed=True, tn=d,
                                       scaled_cols=d, col_scale=HEAD_DIM ** -0.5 * LOG2E)
            o = fox_attention(qkv, gate_cumsum(f_logit, b_f[layer:layer + 1]))
            h = matmul_residual(o, a_w_out, layer, h, fox_order[layer])
        else:
            j = layer - n_a
            q = norm_matmul_resident(h, g_attn, b_w_q, j, col_scale=HEAD_DIM ** -0.5 * LOG2E)
            o = sb_attention(q, kv)
            h = matmul_residual(o, b_w_out, j, h, natural)
        h = ffn(h, ffn_norm[layer].reshape(1, d), ffn_w_gate, ffn_w_up, ffn_w_down, layer,
                final_norm.reshape(1, d), norm_out=layer == depth - 1)
        if layer == n_a - 1:
            kv = norm_matmul(h, kv_norm.reshape(1, d), w_kv, 0)
    return h.reshape(b, s, d)
```
